```python
import math
import jax, jax.numpy as jnp
from jax import lax
import numpy as np

D_MODEL = 1024
BATCH = 8
SEQ = 2048
DEPTH = 2
DEC_BATCH = 32
DEC_SEQ = 4
PAST_LEN = 16384
PAGE_SIZE = 128

GLA_HEADS = 4
GLA_DK = 64
GLA_DV = 128
GLA_K = GLA_HEADS * GLA_DK
GLA_V = GLA_HEADS * GLA_DV
GLA_LORA = 16
GLA_GATE_NORM = 16.0
GLA_CHUNK = 64
RWKV_HEADS = 8
RWKV_HD = 64
RWKV_W = RWKV_HEADS * RWKV_HD
RWKV_LORA_W = 64
RWKV_LORA_A = 64
RWKV_SPLITS = [RWKV_W, RWKV_W, RWKV_W, RWKV_LORA_W, RWKV_LORA_A, RWKV_W]
RWKV_IN = sum(RWKV_SPLITS)
RWKV_OFFS = [int(v) for v in np.cumsum(RWKV_SPLITS)[:-1]]
RWKV_LN_EPS = 64e-5
NSA_HEADS = 8
NSA_KV_HEADS = 2
NSA_GROUP = NSA_HEADS // NSA_KV_HEADS
NSA_HD = 64
NSA_Q = NSA_HEADS * NSA_HD
NSA_KVW = NSA_KV_HEADS * NSA_HD
CMP_BLOCK = 32
SEL_BLOCK = 64
CMP_PER_SEL = SEL_BLOCK // CMP_BLOCK
N_SELECT = 16
WINDOW = 512
Q_BLOCK = 64
N_KV_KINDS = 4
FORCE_SCORE = 1e9
BRANCH_W = 512
N_BRANCH = 3
DN_ALPHA = (2 * DEPTH) ** 0.25
DN_BETA = (8 * DEPTH) ** -0.25
LN_EPS = 1e-5
NORM_EPS = 1e-6

IN_SPLITS = [GLA_K, GLA_K, GLA_V, GLA_LORA, GLA_V,
             RWKV_IN,
             NSA_Q, 6 * NSA_KVW, 3 * NSA_HEADS, NSA_Q,
             N_BRANCH * D_MODEL]
IN_WIDTH = sum(IN_SPLITS)
IN_OFFS = [int(v) for v in np.cumsum(IN_SPLITS)[:-1]]

kernel_name = "gla_rwkv7_nsa_parallel_deepnorm_step"


def _layernorm(x, g, b):
    xf = x.astype(jnp.float32)
    mu = jnp.mean(xf, -1, keepdims=True)
    var = jnp.mean(jnp.square(xf - mu), -1, keepdims=True)
    return ((xf - mu) * lax.rsqrt(var + LN_EPS) * g + b).astype(x.dtype)


def _masked_softmax(s, mask, axis):
    s = jnp.where(mask, s.astype(jnp.float32), -jnp.inf)
    m = jnp.max(s, axis=axis, keepdims=True)
    m = jnp.where(jnp.isfinite(m), m, 0.0)
    p = jnp.exp(s - m)
    den = jnp.sum(p, axis=axis, keepdims=True)
    return p / jnp.where(den > 0, den, 1.0)


def _alibi_slopes():
    h = np.arange(1, NSA_HEADS + 1, dtype=np.float32)
    return jnp.asarray(2.0 ** (-8.0 * h / NSA_HEADS), jnp.float32).reshape(NSA_KV_HEADS, NSA_GROUP)


def gla_mixer(q, k, v, a_low, g, a_up, a_bias, norm_g, s0):
    f32 = jnp.float32
    B, T, _ = q.shape
    C = GLA_CHUNK if T % GLA_CHUNK == 0 else T
    n = T // C

    def heads(t, d):
        return t.astype(f32).reshape(B, n, C, GLA_HEADS, d).transpose(1, 0, 3, 2, 4)

    log_a = jax.nn.log_sigmoid((a_low @ a_up + a_bias).astype(f32)) / GLA_GATE_NORM
    xs = (heads(q, GLA_DK) * GLA_DK ** -0.5, heads(k, GLA_DK), heads(v, GLA_DV), heads(log_a, GLA_DK))
    causal = jnp.tril(jnp.ones((C, C), bool))

    def chunk_step(S, inp):
        qc, kc, vc, lc = inp
        cum = jnp.cumsum(lc, axis=2)
        o_inter = jnp.einsum('bhcd,bhde->bhce', qc * jnp.exp(cum), S)
        rel = jnp.where(causal[:, :, None], cum[:, :, :, None, :] - cum[:, :, None, :, :], -jnp.inf)
        att = jnp.einsum('bhid,bhijd->bhij', qc, jnp.exp(rel) * kc[:, :, None, :, :])
        o = o_inter + jnp.einsum('bhij,bhje->bhie', att, vc)
        last = cum[:, :, -1, :]
        S = jnp.exp(last)[..., None] * S + jnp.einsum('bhjd,bhje->bhde', kc * jnp.exp(last[:, :, None, :] - cum), vc)
        return S, o

    S, o = lax.scan(chunk_step, s0.astype(f32), xs)
    o = o.transpose(1, 0, 3, 2, 4).reshape(B, T, GLA_HEADS, GLA_DV)
    o = o * lax.rsqrt(jnp.mean(o * o, -1, keepdims=True) + NORM_EPS)
    o = o.reshape(B, T, GLA_V) * norm_g * jax.nn.silu(g.astype(f32))
    return o.astype(q.dtype), S.astype(s0.dtype)


def rwkv_mixer(z, z_prev0, mu, w0, w_up, a0, a_up, k_k, k_a, r_k, ln_w, ln_b, s0):
    f32 = jnp.float32
    B, T, _ = z.shape
    zf = z.astype(f32)
    z_prev = jnp.concatenate([z_prev0[:, None].astype(f32), zf[:, :-1]], axis=1)
    zs = zf + (z_prev - zf) * mu
    r, k, v, wl, al, g = jnp.split(zs, RWKV_OFFS, axis=-1)
    w = -jax.nn.softplus(-(w0 + jnp.tanh(wl) @ w_up)) - 0.5
    decay = jnp.exp(-jnp.exp(w))
    a = jax.nn.sigmoid(a0 + al @ a_up)
    hs = lambda t: t.reshape(B, T, RWKV_HEADS, RWKV_HD)
    kk = hs(k * k_k)
    kk = kk * lax.rsqrt(jnp.sum(kk * kk, -1, keepdims=True) + NORM_EPS)
    k = k * (1.0 + (a - 1.0) * k_a)
    r, k, v, a, decay = hs(r), hs(k), hs(v), hs(a), hs(decay)

    def step(S, inp):
        r_t, w_t, k_t, v_t, kk_t, a_t = inp
        sa = jnp.einsum('bhij,bhj->bhi', S, -kk_t)
        S = S * w_t[:, :, None, :] + sa[..., None] * (kk_t * a_t)[:, :, None, :] + v_t[..., None] * k_t[:, :, None, :]
        return S, jnp.einsum('bhij,bhj->bhi', S, r_t)

    tm = lambda t: jnp.swapaxes(t, 0, 1)
    S, y = lax.scan(step, s0.astype(f32), (tm(r), tm(decay), tm(k), tm(v), tm(kk), tm(a)))
    y = tm(y)
    mean = jnp.mean(y, -1, keepdims=True)
    var = jnp.mean(jnp.square(y - mean), -1, keepdims=True)
    y = ((y - mean) * lax.rsqrt(var + RWKV_LN_EPS)).reshape(B, T, RWKV_W) * ln_w + ln_b
    bonus = jnp.sum(r * k * r_k, -1, keepdims=True) * v
    y = (y + bonus.reshape(B, T, RWKV_W)) * jax.nn.silu(g)
    return y.astype(z.dtype), S.astype(s0.dtype), z[:, -1]


def nsa_compress(k_full):
    B, L = k_full.shape[:2]
    kf = k_full.astype(jnp.float32).reshape(B, L // CMP_BLOCK, CMP_BLOCK, NSA_KV_HEADS, NSA_HD)
    return jnp.mean(kf, 2).astype(k_full.dtype)


def nsa_blocks(k_full):
    B, L = k_full.shape[:2]
    return k_full.reshape(B, L // SEL_BLOCK, SEL_BLOCK, NSA_KV_HEADS, NSA_HD).transpose(0, 3, 1, 2, 4)


def nsa_attend(q, t_pos, kc, vc, c_end, ks_b, vs_b, kw, vw, w_pos, gates, slopes):
    f32 = jnp.float32
    B = q.shape[0]
    NS = ks_b.shape[2]
    tq = t_pos[:, None]
    dist_c = (tq - c_end[None, :]).astype(f32)
    s_c = jnp.einsum('bqkgd,bnkd->bqkgn', q, kc).astype(f32) - slopes[None, None, :, :, None] * dist_c[None, :, None, None, :]
    p_c = _masked_softmax(s_c, (c_end[None, :] <= tq)[None, :, None, None, :], -1)
    o_c = jnp.einsum('bqkgn,bnkd->bqkgd', p_c.astype(vc.dtype), vc)
    imp = jnp.sum(p_c, 3).reshape(p_c.shape[0], p_c.shape[1], NSA_KV_HEADS, NS, CMP_PER_SEL).sum(-1)
    blk = jnp.arange(NS)
    forced = (blk[None, :] == tq // SEL_BLOCK) | (blk[None, :] == 0)
    valid = blk[None, :] * SEL_BLOCK <= tq
    score = jnp.where(forced[None, :, None, :], FORCE_SCORE, jnp.where(valid[None, :, None, :], imp, -FORCE_SCORE))
    _, idx = lax.top_k(score, min(N_SELECT, NS))
    bi = jnp.arange(B)[:, None, None, None]
    ki = jnp.arange(NSA_KV_HEADS)[None, None, :, None]
    ks = ks_b[bi, ki, idx]
    vs = vs_b[bi, ki, idx]
    pos_s = idx[..., None] * SEL_BLOCK + jnp.arange(SEL_BLOCK)
    dist_s = t_pos[None, :, None, None, None] - pos_s
    s_s = jnp.einsum('bqkgd,bqkjsd->bqkgjs', q, ks).astype(f32) - slopes[None, None, :, :, None, None] * dist_s[:, :, :, None].astype(f32)
    p_s = _masked_softmax(s_s, (dist_s >= 0)[:, :, :, None], (-2, -1))
    o_s = jnp.einsum('bqkgjs,bqkjsd->bqkgd', p_s.astype(vs.dtype), vs)
    dist_w = tq - w_pos[None, :]
    mask_w = (dist_w >= 0) & (dist_w <= WINDOW) & (w_pos[None, :] >= 0)
    s_w = jnp.einsum('bqkgd,bwkd->bqkgw', q, kw).astype(f32) - slopes[None, None, :, :, None] * dist_w[None, :, None, None, :].astype(f32)
    p_w = _masked_softmax(s_w, mask_w[None, :, None, None, :], -1)
    o_w = jnp.einsum('bqkgw,bwkd->bqkgd', p_w.astype(vw.dtype), vw)
    return gates[..., 0:1] * o_c + gates[..., 1:2] * o_s + gates[..., 2:3] * o_w


def nsa_mixer(qr, kvr, bgr, g, slopes, t0, kv_past, win_buf, win_len):
    B, T, _ = qr.shape
    q = qr.reshape(B, T, NSA_KV_HEADS, NSA_GROUP, NSA_HD) * NSA_HD ** -0.5
    kv = kvr.reshape(B, T, 6, NSA_KV_HEADS, NSA_HD)
    gates = jax.nn.sigmoid(bgr.reshape(B, T, NSA_KV_HEADS, NSA_GROUP, 3))
    new_rows = kv[:, :, :N_KV_KINDS]
    new_win = kv[:, :, N_KV_KINDS:]
    parts = ([] if kv_past is None else [kv_past]) + [new_rows]
    L = sum(p.shape[1] for p in parts)
    Lp = -(-L // SEL_BLOCK) * SEL_BLOCK
    if Lp > L:
        parts.append(jnp.zeros((B, Lp - L) + new_rows.shape[2:], new_rows.dtype))
    full = jnp.concatenate(parts, axis=1) if len(parts) > 1 else parts[0]
    kc = nsa_compress(full[:, :, 0])
    vc = nsa_compress(full[:, :, 1])
    c_end = jnp.arange(Lp // CMP_BLOCK) * CMP_BLOCK + (CMP_BLOCK - 1)
    ks_b = nsa_blocks(full[:, :, 2])
    vs_b = nsa_blocks(full[:, :, 3])
    if win_buf is None:
        nb = T // Q_BLOCK
        wpad = jnp.pad(new_win, ((0, 0), (WINDOW, 0), (0, 0), (0, 0), (0, 0)))

        def block_fn(i):
            s = i * Q_BLOCK
            wk = lax.dynamic_slice_in_dim(wpad, s, Q_BLOCK + WINDOW, axis=1)
            w_pos = s - WINDOW + jnp.arange(Q_BLOCK + WINDOW)
            return nsa_attend(lax.dynamic_slice_in_dim(q, s, Q_BLOCK, 1), t0 + s + jnp.arange(Q_BLOCK),
                              kc, vc, c_end, ks_b, vs_b, wk[:, :, 0], wk[:, :, 1], w_pos,
                              lax.dynamic_slice_in_dim(gates, s, Q_BLOCK, 1), slopes)

        o = lax.map(block_fn, jnp.arange(nb))
        o = o.transpose(1, 0, 2, 3, 4, 5).reshape(B, T, NSA_Q)
        win_state = jnp.pad(new_win, ((0, 0), (max(win_len - T, 0), 0), (0, 0), (0, 0), (0, 0)))[:, -win_len:]
    else:
        wb = win_buf.shape[1]
        win_all = jnp.concatenate([win_buf, new_win], axis=1)
        w_pos = t0 - wb + jnp.arange(wb + T)
        o = nsa_attend(q, t0 + jnp.arange(T), kc, vc, c_end, ks_b, vs_b,
                       win_all[:, :, 0], win_all[:, :, 1], w_pos, gates, slopes).reshape(B, T, NSA_Q)
        win_state = win_all[:, -wb:]
    return o * jax.nn.silu(g), new_rows, win_state


def trunk_layer(x, t0, kv_past, win_buf, gla_s0, rwkv_s0, shift0, win_len,
                w_in, b_in, gla_a_up, gla_a_bias, gla_norm, rwkv_mu, rwkv_w0, rwkv_w_up, rwkv_a0,
                rwkv_a_up, rwkv_k_k, rwkv_k_a, rwkv_r_k, rwkv_ln_w, rwkv_ln_b, w_br, w_out, ln_g, ln_b):
    B, T, _ = x.shape
    z = x @ w_in + b_in
    gq, gk, gv, ga, gg, rz, nq, nkv, nbg, ng, mg = jnp.split(z, IN_OFFS, axis=-1)
    o_gla, gla_s = gla_mixer(gq, gk, gv, ga, gg, gla_a_up, gla_a_bias, gla_norm, gla_s0)
    o_rwkv, rwkv_s, shift = rwkv_mixer(rz, shift0, rwkv_mu, rwkv_w0, rwkv_w_up, rwkv_a0, rwkv_a_up,
                                       rwkv_k_k, rwkv_k_a, rwkv_r_k, rwkv_ln_w, rwkv_ln_b, rwkv_s0)
    o_nsa, kv_rows, win_state = nsa_mixer(nq, nkv, nbg, ng, _alibi_slopes(), t0, kv_past, win_buf, win_len)
    br = jnp.stack([o_gla, o_rwkv, o_nsa.astype(o_gla.dtype)], axis=2)
    proj = jnp.einsum('btmc,mcd->btmd', br, w_br)
    gate = jax.nn.sigmoid(mg.reshape(B, T, N_BRANCH, D_MODEL))
    y = jnp.sum(gate * proj, axis=2) @ w_out
    x = _layernorm(DN_ALPHA * x + y, ln_g, ln_b)
    return x, kv_rows, win_state, gla_s, rwkv_s, shift


def setup_inputs(seed: int = 0) -> dict:
    key = jax.random.key(seed)
    ks = jax.random.split(key, 32)
    f32 = jnp.float32
    n_pages = PAST_LEN // PAGE_SIZE
    n_pool = (DEC_BATCH * n_pages * 5) // 4
    win_len = min(WINDOW, PAST_LEN)
    nrm = lambda k, shape, s: jax.random.normal(k, shape, f32) * s
    page_table = jax.random.permutation(ks[0], n_pool)[:DEC_BATCH * n_pages].reshape(DEC_BATCH, n_pages).astype(jnp.int32)
    return {
        "x_prompt": nrm(ks[1], (BATCH, SEQ, D_MODEL), 1.0),
        "x_sample": nrm(ks[2], (DEC_BATCH, DEC_SEQ, D_MODEL), 1.0),
        "cache_nsa_kv": nrm(ks[3], (DEPTH, n_pool, PAGE_SIZE, N_KV_KINDS, NSA_KV_HEADS, NSA_HD), 1.0),
        "state_nsa_win": nrm(ks[4], (DEPTH, DEC_BATCH, win_len, 2, NSA_KV_HEADS, NSA_HD), 1.0),
        "state_gla": nrm(ks[5], (DEPTH, DEC_BATCH, GLA_HEADS, GLA_DK, GLA_DV), 0.5),
        "state_rwkv": nrm(ks[6], (DEPTH, DEC_BATCH, RWKV_HEADS, RWKV_HD, RWKV_HD), 0.5),
        "state_rwkv_shift": nrm(ks[7], (DEPTH, DEC_BATCH, RWKV_IN), 1.0),
        "page_table": page_table,
        "w_in": nrm(ks[8], (DEPTH, D_MODEL, IN_WIDTH), D_MODEL ** -0.5),
        "b_in": nrm(ks[9], (DEPTH, IN_WIDTH), 0.01),
        "gla_a_up": nrm(ks[10], (DEPTH, GLA_LORA, GLA_K), GLA_LORA ** -0.5),
        "gla_a_bias": nrm(ks[11], (DEPTH, GLA_K), 0.1),
        "gla_norm": 1.0 + nrm(ks[12], (DEPTH, GLA_V), 0.01),
        "rwkv_mu": jax.random.uniform(ks[13], (DEPTH, RWKV_IN), f32),
        "rwkv_w0": jax.random.uniform(ks[14], (DEPTH, RWKV_W), f32, -3.0, 1.0),
        "rwkv_w_up": nrm(ks[15], (DEPTH, RWKV_LORA_W, RWKV_W), 0.5 * RWKV_LORA_W ** -0.5),
        "rwkv_a0": nrm(ks[16], (DEPTH, RWKV_W), 0.1),
        "rwkv_a_up": nrm(ks[17], (DEPTH, RWKV_LORA_A, RWKV_W), RWKV_LORA_A ** -0.5),
        "rwkv_k_k": 0.85 + nrm(ks[18], (DEPTH, RWKV_W), 0.01),
        "rwkv_k_a": 1.0 + nrm(ks[19], (DEPTH, RWKV_W), 0.01),
        "rwkv_r_k": nrm(ks[20], (DEPTH, RWKV_HEADS, RWKV_HD), 0.1),
        "rwkv_ln_w": 1.0 + nrm(ks[21], (DEPTH, RWKV_W), 0.01),
        "rwkv_ln_b": nrm(ks[22], (DEPTH, RWKV_W), 0.01),
        "w_br": nrm(ks[23], (DEPTH, N_BRANCH, BRANCH_W, D_MODEL), DN_BETA * BRANCH_W ** -0.5),
        "w_out": nrm(ks[24], (DEPTH, D_MODEL, D_MODEL), DN_BETA * D_MODEL ** -0.5),
        "ln_g": 1.0 + nrm(ks[25], (DEPTH, D_MODEL), 0.01),
        "ln_b": nrm(ks[26], (DEPTH, D_MODEL), 0.01),
    }


def reference(x_prompt, x_sample, cache_nsa_kv, state_nsa_win, state_gla, state_rwkv, state_rwkv_shift,
              page_table, w_in, b_in, gla_a_up, gla_a_bias, gla_norm, rwkv_mu, rwkv_w0, rwkv_w_up,
              rwkv_a0, rwkv_a_up, rwkv_k_k, rwkv_k_a, rwkv_r_k, rwkv_ln_w, rwkv_ln_b, w_br, w_out, ln_g, ln_b):
    n_pages = PAST_LEN // PAGE_SIZE
    win_len = min(WINDOW, PAST_LEN)
    bp = x_prompt.shape[0]
    bs = x_sample.shape[0]
    dt = x_prompt.dtype
    xp, xs = x_prompt, x_sample
    kv_p, kv_s, win_p, win_s, gla_p, gla_s, rwkv_p, rwkv_s, sh_p, sh_s = ([] for _ in range(10))
    weights = (w_in, b_in, gla_a_up, gla_a_bias, gla_norm, rwkv_mu, rwkv_w0, rwkv_w_up, rwkv_a0,
               rwkv_a_up, rwkv_k_k, rwkv_k_a, rwkv_r_k, rwkv_ln_w, rwkv_ln_b, w_br, w_out, ln_g, ln_b)
    for l in range(DEPTH):
        lw = [w[l] for w in weights]
        xp, a, b, c, d, e = trunk_layer(
            xp, 0, None, None,
            jnp.zeros((bp, GLA_HEADS, GLA_DK, GLA_DV), dt),
            jnp.zeros((bp, RWKV_HEADS, RWKV_HD, RWKV_HD), dt),
            jnp.zeros((bp, RWKV_IN), dt), win_len, *lw)
        kv_p.append(a); win_p.append(b); gla_p.append(c); rwkv_p.append(d); sh_p.append(e)
        past = cache_nsa_kv[l][page_table].reshape(bs, n_pages * PAGE_SIZE, N_KV_KINDS, NSA_KV_HEADS, NSA_HD)
        xs, a, b, c, d, e = trunk_layer(
            xs, PAST_LEN, past, state_nsa_win[l], state_gla[l], state_rwkv[l], state_rwkv_shift[l],
            win_len, *lw)
        kv_s.append(a); win_s.append(b); gla_s.append(c); rwkv_s.append(d); sh_s.append(e)
    return (xp, xs, jnp.stack(kv_p), jnp.stack(kv_s), jnp.stack(win_p), jnp.stack(win_s),
            jnp.stack(gla_p), jnp.stack(gla_s), jnp.stack(rwkv_p), jnp.stack(rwkv_s),
            jnp.stack(sh_p), jnp.stack(sh_s))
```

```python
import functools

import jax
import jax.numpy as jnp
from jax import lax
from jax.experimental import pallas as pl
from jax.experimental.pallas import tpu as pltpu

F32 = jnp.float32
BF16 = jnp.bfloat16

D_MODEL = 1024
DEPTH = 2
PAST_LEN = 16384
PAGE_SIZE = 128
N_PAGES = PAST_LEN // PAGE_SIZE

GLA_HEADS, GLA_DK, GLA_DV = 4, 64, 128
GLA_K, GLA_V, GLA_LORA = 256, 512, 16
GLA_GATE_NORM = 16.0
GLA_CHUNK = 64
GLA_SUB = 16

RWKV_HEADS, RWKV_HD, RWKV_W = 8, 64, 512
RWKV_IN = 2176
RWKV_LN_EPS = 64e-5
RWKV_CHUNK = 64

NSA_HEADS, NSA_KV_HEADS, NSA_GROUP, NSA_HD = 8, 2, 4, 64
CMP_BLOCK, SEL_BLOCK, N_SELECT, WINDOW = 32, 64, 16, 512
FORCE_SCORE = 1e9
NEG = -1e30

DN_ALPHA = (2 * DEPTH) ** 0.25
LN_EPS = 1e-5
NORM_EPS = 1e-6

ZG_W = 1664
ZN_W = 2048
ZM_W = 3072
SAMPLE_TPAD = 16
VMEM_LIMIT = 56 * 1024 * 1024


def _mm(a, b):
    return jnp.dot(a.astype(BF16), b.astype(BF16), preferred_element_type=F32)


def _mm_nt(a, b):
    return lax.dot_general(a.astype(BF16), b.astype(BF16), (((1,), (1,)), ((), ())), preferred_element_type=F32)


def _mm_tn(a, b):
    return lax.dot_general(a.astype(BF16), b.astype(BF16), (((0,), (0,)), ((), ())), preferred_element_type=F32)


def _mm_exact(a, b):
    return jnp.dot(a, b, preferred_element_type=F32, precision=lax.Precision.HIGHEST)


def _mm_split(a, b01):
    hi = a.astype(BF16)
    lo = (a - hi.astype(F32)).astype(BF16)
    return (jnp.dot(hi, b01, preferred_element_type=F32) + jnp.dot(lo, b01, preferred_element_type=F32))


def _softplus(x):
    return jnp.maximum(x, 0.0) + jnp.log1p(jnp.exp(-jnp.abs(x)))


def _sigmoid(x):
    return 1.0 / (1.0 + jnp.exp(-x))


def _silu(x):
    return x * _sigmoid(x)


def _iota(shape, dim):
    return lax.broadcasted_iota(jnp.int32, shape, dim)


def _cparams(sem):
    return pltpu.CompilerParams(dimension_semantics=sem, vmem_limit_bytes=VMEM_LIMIT)


def _proj_kernel(x_ref, w_ref, b_ref, o_ref):
    o_ref[...] = _mm(x_ref[...], w_ref[...]) + b_ref[...]


def _proj(x, w, b, tm):
    m, k = x.shape
    n = w.shape[1]
    return pl.pallas_call(
        _proj_kernel,
        grid=(m // tm,),
        in_specs=[pl.BlockSpec((tm, k), lambda i: (i, 0)),
                  pl.BlockSpec((k, n), lambda i: (0, 0)),
                  pl.BlockSpec((1, n), lambda i: (0, 0))],
        out_specs=pl.BlockSpec((tm, n), lambda i: (i, 0)),
        out_shape=jax.ShapeDtypeStruct((m, n), F32),
        compiler_params=_cparams(("parallel",)),
        name="proj",
    )(x, w, b)


def _gla_kernel(zg_ref, s0_ref, aup_ref, abias_ref, norm_ref, o_ref, sout_ref, st_scr, *, tb_rows, chunk, t_valid):
    tb = pl.program_id(1)
    c_rows = chunk
    sub = min(GLA_SUB, c_rows)
    nsub = c_rows // sub

    @pl.when(tb == 0)
    def _():
        for h in range(GLA_HEADS):
            st_scr[h] = s0_ref[0, h].T

    tri = (_iota((c_rows, c_rows), 1) <= _iota((c_rows, c_rows), 0)).astype(F32)
    ones_red = jnp.ones((GLA_DK, 128), BF16)
    lane_s = _iota((sub, 128), 1)
    row_s = _iota((sub, 128), 0)
    col_c = _iota((sub, c_rows), 1)

    def chunk_body(c, carry):
        r0 = pl.multiple_of(c * c_rows, c_rows)
        z = zg_ref[0, pl.ds(r0, c_rows), :]
        q = z[:, 0:256] * (GLA_DK ** -0.5)
        k = z[:, 256:512]
        v = z[:, 512:1024]
        g = z[:, 1024:1536]
        ga = z[:, 1536:1552]
        la = -_softplus(-(_mm(ga, aup_ref[...]) + abias_ref[...])) * (1.0 / GLA_GATE_NORM)
        if t_valid is not None:
            ok = (tb * tb_rows + r0 + _iota((c_rows, 1), 0)) < t_valid
            la = jnp.where(ok, la, 0.0)
            k = jnp.where(ok, k, 0.0)
            v = jnp.where(ok, v, 0.0)
        cum = _mm_exact(tri, la)
        for h in range(GLA_HEADS):
            qh = q[:, 64 * h:64 * h + 64]
            kh = k[:, 64 * h:64 * h + 64]
            ch = cum[:, 64 * h:64 * h + 64]
            vh = v[:, 128 * h:128 * h + 128]
            st = st_scr[h]
            o = _mm_nt(qh * jnp.exp(ch), st)
            att_rows = []
            for blk in range(nsub):
                sl = slice(blk * sub, (blk + 1) * sub)
                q_i, k_i, c_i = qh[sl], kh[sl], ch[sl]
                es = [q_i * k_i[j:j + 1] * jnp.exp(jnp.minimum(c_i - c_i[j:j + 1], 0.0)) for j in range(sub)]
                red = _mm(jnp.concatenate(es, axis=0), ones_red)
                a_i = jnp.zeros((sub, 128), F32)
                for j in range(sub):
                    a_i = a_i + jnp.where((lane_s == blk * sub + j) & (row_s >= j), red[j * sub:(j + 1) * sub], 0.0)
                a_i = a_i[:, 0:c_rows]
                if blk > 0:
                    b_i = ch[blk * sub - 1:blk * sub]
                    q_t = q_i * jnp.exp(c_i - b_i)
                    k_t = kh * jnp.exp(jnp.minimum(b_i - ch, 0.0))
                    a_i = a_i + jnp.where(col_c < blk * sub, _mm_nt(q_t, k_t), 0.0)
                att_rows.append(a_i)
            att = att_rows[0] if nsub == 1 else jnp.concatenate(att_rows, axis=0)
            o = o + _mm(att, vh)
            last = ch[c_rows - 1:c_rows]
            st_scr[h] = st * jnp.exp(last) + _mm_tn(vh, kh * jnp.exp(last - ch))
            o = o * lax.rsqrt(jnp.mean(o * o, axis=-1, keepdims=True) + NORM_EPS)
            o = o * norm_ref[:, 128 * h:128 * h + 128] * _silu(g[:, 128 * h:128 * h + 128])
            o_ref[0, pl.ds(r0, c_rows), 128 * h:128 * h + 128] = o
        return carry

    lax.fori_loop(0, tb_rows // c_rows, chunk_body, 0)

    @pl.when(tb == pl.num_programs(1) - 1)
    def _():
        for h in range(GLA_HEADS):
            sout_ref[0, h] = st_scr[h].T


def _gla(zg, s0, a_up, a_bias, norm_g, *, tb_rows, chunk, t_valid):
    b, t, _ = zg.shape
    kern = functools.partial(_gla_kernel, tb_rows=tb_rows, chunk=chunk, t_valid=t_valid)
    return pl.pallas_call(
        kern,
        grid=(b, t // tb_rows),
        in_specs=[pl.BlockSpec((1, tb_rows, ZG_W), lambda i, j: (i, j, 0)),
                  pl.BlockSpec((1, GLA_HEADS, GLA_DK, GLA_DV), lambda i, j: (i, 0, 0, 0)),
                  pl.BlockSpec((GLA_LORA, GLA_K), lambda i, j: (0, 0)),
                  pl.BlockSpec((1, GLA_K), lambda i, j: (0, 0)),
                  pl.BlockSpec((1, GLA_V), lambda i, j: (0, 0))],
        out_specs=[pl.BlockSpec((1, tb_rows, GLA_V), lambda i, j: (i, j, 0)),
                   pl.BlockSpec((1, GLA_HEADS, GLA_DK, GLA_DV), lambda i, j: (i, 0, 0, 0))],
        out_shape=[jax.ShapeDtypeStruct((b, t, GLA_V), F32),
                   jax.ShapeDtypeStruct((b, GLA_HEADS, GLA_DK, GLA_DV), F32)],
        scratch_shapes=[pltpu.VMEM((GLA_HEADS, GLA_DV, GLA_DK), F32)],
        compiler_params=_cparams(("parallel", "arbitrary")),
        name="gla",
    )(zg, s0, a_up, a_bias, norm_g)


def _rwkv_kernel(zr_ref, s0_ref, sh0_ref, seg_ref, mu_ref, w0_ref, wup_ref, a0_ref, aup_ref, kk_ref, ka_ref, rk_ref,
                 lnw_ref, lnb_ref, y_ref, sout_ref,
                 s_scr, prev_scr, lw_s, kk_s, kka_s, k2_s, r_s, v_s, y_s, *, tb_rows, chunk, t_valid):
    tb = pl.program_id(1)
    c_rows = chunk
    nh = RWKV_HEADS

    @pl.when(tb == 0)
    def _():
        s_scr[...] = s0_ref[0]
        prev_scr[...] = sh0_ref[0]

    z = zr_ref[0]
    rows = _iota((tb_rows, 1), 0)
    zp = jnp.where(rows == 0, prev_scr[...], pltpu.roll(z, 1, axis=0))
    prev_scr[...] = z[tb_rows - 1:tb_rows]
    zs = z + (zp - z) * mu_ref[...]
    r = zs[:, 0:512]
    k = zs[:, 512:1024]
    v = zs[:, 1024:1536]
    wl = zs[:, 1536:1600]
    al = zs[:, 1600:1664]
    w = -_softplus(-(w0_ref[...] + _mm(jnp.tanh(wl), wup_ref[...]))) - 0.5
    lw = -jnp.exp(w)
    a = _sigmoid(a0_ref[...] + _mm(al, aup_ref[...]))
    kk = k * kk_ref[...]
    kk = kk * lax.rsqrt(_mm_split(kk * kk, seg_ref[...]) + NORM_EPS)
    k2 = k * (1.0 + (a - 1.0) * ka_ref[...])
    kka = kk * a
    if t_valid is not None:
        ok = (tb * tb_rows + rows) < t_valid
        lw = jnp.where(ok, lw, 0.0)
        kka = jnp.where(ok, kka, 0.0)
        k2 = jnp.where(ok, k2, 0.0)
    lw_s[...] = lw
    kk_s[...] = kk
    kka_s[...] = kka
    k2_s[...] = k2
    r_s[...] = r
    v_s[...] = v

    ri = _iota((c_rows, c_rows), 0)
    ci = _iota((c_rows, c_rows), 1)
    tri = (ci <= ri).astype(F32)
    strict = ci < ri
    incl = ci <= ri
    n_dbl = max(1, (c_rows - 1).bit_length())

    def chunk_body(c, carry):
        r0 = pl.multiple_of(c * c_rows, c_rows)
        ds = pl.ds(r0, c_rows)
        lwc = lw_s[ds, :]
        cl = _mm_exact(tri, lwc)
        e_inv = jnp.exp(-cl)
        e_fwd = jnp.exp(cl)
        e_prev = jnp.exp(cl - lwc)
        e_end = jnp.exp(cl[c_rows - 1:c_rows] - cl)
        g_end = jnp.exp(cl[c_rows - 1:c_rows])
        kkc, kkac, k2c, rc, vc = kk_s[ds, :], kka_s[ds, :], k2_s[ds, :], r_s[ds, :], v_s[ds, :]
        for h in range(nh):
            hs = slice(64 * h, 64 * h + 64)
            a_t = -kkac[:, hs] * e_inv[:, hs]
            b_t = kkc[:, hs] * e_prev[:, hs]
            k_t = k2c[:, hs] * e_inv[:, hs]
            r_t = rc[:, hs] * e_fwd[:, hs]
            vh = vc[:, hs]
            ak = jnp.concatenate([a_t, k_t], axis=0)
            ba = _mm_nt(b_t, ak)
            ra = _mm_nt(r_t, ak)
            l_a = jnp.where(strict, ba[:, 0:c_rows], 0.0)
            l_k = jnp.where(strict, ba[:, c_rows:2 * c_rows], 0.0)
            m_a = jnp.where(incl, ra[:, 0:c_rows], 0.0)
            m_k = jnp.where(incl, ra[:, c_rows:2 * c_rows], 0.0)
            x = jnp.concatenate([b_t, _mm(l_k, vh)], axis=1)
            lp = l_a
            for step in range(n_dbl):
                x = x + _mm(lp, x)
                if step + 1 < n_dbl:
                    lp = _mm(lp, lp)
            s0 = s_scr[h]
            u = _mm_nt(x[:, 0:64], s0) + x[:, 64:128]
            y = _mm_nt(r_t, s0) + _mm(m_a, u) + _mm(m_k, vh)
            akg = jnp.concatenate([-kkac[:, hs] * e_end[:, hs], k2c[:, hs] * e_end[:, hs]], axis=0)
            uv = jnp.concatenate([u, vh], axis=0)
            s_scr[h] = s0 * g_end[:, hs] + _mm_tn(uv, akg)
            y_s[ds, hs] = y
        return carry

    lax.fori_loop(0, tb_rows // c_rows, chunk_body, 0)

    y = y_s[...]
    seg = seg_ref[...]
    mean = _mm_split(y, seg) * (1.0 / RWKV_HD)
    d = y - mean
    var = _mm_split(d * d, seg) * (1.0 / RWKV_HD)
    yn = d * lax.rsqrt(var + RWKV_LN_EPS) * lnw_ref[...] + lnb_ref[...]
    bonus = _mm_split(r * k2 * rk_ref[...], seg) * v
    y_ref[0] = (yn + bonus) * _silu(zs[:, 1664:2176])

    @pl.when(tb == pl.num_programs(1) - 1)
    def _():
        sout_ref[0] = s_scr[...]


def _rwkv(zr, s0, sh0, seg, params, *, tb_rows, chunk, t_valid):
    b, t, _ = zr.shape
    kern = functools.partial(_rwkv_kernel, tb_rows=tb_rows, chunk=chunk, t_valid=t_valid)
    full = lambda shp: pl.BlockSpec(shp, lambda i, j: (0,) * len(shp))
    mu, w0, w_up, a0, a_up, k_k, k_a, r_k, ln_w, ln_b = params
    return pl.pallas_call(
        kern,
        grid=(b, t // tb_rows),
        in_specs=[pl.BlockSpec((1, tb_rows, RWKV_IN), lambda i, j: (i, j, 0)),
                  pl.BlockSpec((1, RWKV_HEADS, RWKV_HD, RWKV_HD), lambda i, j: (i, 0, 0, 0)),
                  pl.BlockSpec((1, 1, RWKV_IN), lambda i, j: (i, 0, 0)),
                  full((RWKV_W, RWKV_W)), full((1, RWKV_IN)), full((1, RWKV_W)), full((64, RWKV_W)),
                  full((1, RWKV_W)), full((64, RWKV_W)), full((1, RWKV_W)), full((1, RWKV_W)), full((1, RWKV_W)),
                  full((1, RWKV_W)), full((1, RWKV_W))],
        out_specs=[pl.BlockSpec((1, tb_rows, RWKV_W), lambda i, j: (i, j, 0)),
                   pl.BlockSpec((1, RWKV_HEADS, RWKV_HD, RWKV_HD), lambda i, j: (i, 0, 0, 0))],
        out_shape=[jax.ShapeDtypeStruct((b, t, RWKV_W), F32),
                   jax.ShapeDtypeStruct((b, RWKV_HEADS, RWKV_HD, RWKV_HD), F32)],
        scratch_shapes=[pltpu.VMEM((RWKV_HEADS, RWKV_HD, RWKV_HD), F32), pltpu.VMEM((1, RWKV_IN), F32)]
        + [pltpu.VMEM((tb_rows, RWKV_W), F32) for _ in range(7)],
        compiler_params=_cparams(("parallel", "arbitrary")),
        name="rwkv",
    )(zr, s0, sh0, seg, mu, w0, w_up, a0, a_up, k_k, k_a, r_k, ln_w, ln_b)


def _slope(h):
    return 2.0 ** (-8.0 * (h + 1) / NSA_HEADS)


def _online_update(s, mask, v, m_ref, l_ref, acc_ref, idx):
    s = jnp.where(mask, s, NEG)
    m_old = m_ref[idx]
    m_new = jnp.maximum(m_old, jnp.max(s, axis=-1, keepdims=True))
    p = jnp.where(mask, jnp.exp(s - m_new), 0.0)
    alpha = jnp.exp(m_old - m_new)
    l_ref[idx] = alpha * l_ref[idx] + jnp.sum(p, axis=-1, keepdims=True)
    acc_ref[idx] = alpha * acc_ref[idx] + _mm(p, v)
    m_ref[idx] = m_new


def _online_init(m_ref, l_ref, acc_ref):
    m_ref[...] = jnp.full(m_ref.shape, NEG, F32)
    l_ref[...] = jnp.zeros(l_ref.shape, F32)
    acc_ref[...] = jnp.zeros(acc_ref.shape, F32)


def _online_result(l_ref, acc_ref, idx):
    l = l_ref[idx]
    return acc_ref[idx] / jnp.where(l > 0, l, 1.0)


def _nsa_prompt_kernel(q_ref, kv_ref, bg_ref, g_ref, o_ref, kc_s, vc_s, q_s, oc_s, m_s, l_s, acc_s, m2_s, l2_s, acc2_s,
                       *, t_len, tq):
    i = pl.program_id(1)
    nb = t_len // SEL_BLOCK
    tk = tq
    win_tiles = WINDOW // tk

    @pl.when(i == 0)
    def _():
        kcm = kv_ref[0, :, 0:128].reshape(nb, SEL_BLOCK, 128)
        vcm = kv_ref[0, :, 128:256].reshape(nb, SEL_BLOCK, 128)
        kc_s[0:nb] = jnp.sum(kcm[:, 0:CMP_BLOCK], axis=1) * (1.0 / CMP_BLOCK)
        kc_s[nb:2 * nb] = jnp.sum(kcm[:, CMP_BLOCK:SEL_BLOCK], axis=1) * (1.0 / CMP_BLOCK)
        vc_s[0:nb] = jnp.sum(vcm[:, 0:CMP_BLOCK], axis=1) * (1.0 / CMP_BLOCK)
        vc_s[nb:2 * nb] = jnp.sum(vcm[:, CMP_BLOCK:SEL_BLOCK], axis=1) * (1.0 / CMP_BLOCK)

    for h in range(NSA_HEADS):
        q_s[h] = (q_ref[0, :, 64 * h:64 * h + 64] * (NSA_HD ** -0.5)).astype(BF16)

    t_col = i * tq + _iota((tq, 1), 0)
    ccol = _iota((1, 2 * nb), 1)
    cend = jnp.where(ccol < nb, SEL_BLOCK * ccol + (CMP_BLOCK - 1), SEL_BLOCK * (ccol - nb) + (SEL_BLOCK - 1))
    dist_c = (t_col - cend).astype(F32)
    mask_c = cend <= t_col
    jcol = _iota((1, nb), 1)
    tblk = lax.shift_right_logical(t_col, 6)
    forced = (jcol == tblk) | (jcol == 0)
    valid = jcol <= tblk

    for kvh in range(NSA_KV_HEADS):
        ls = slice(64 * kvh, 64 * kvh + 64)
        kc = kc_s[:, ls]
        vc = vc_s[:, ls]
        imp_e = jnp.zeros((tq, nb), F32)
        imp_o = jnp.zeros((tq, nb), F32)
        for g in range(NSA_GROUP):
            h = NSA_GROUP * kvh + g
            s = _mm_nt(q_s[h], kc) - _slope(h) * dist_c
            s = jnp.where(mask_c, s, NEG)
            m = jnp.max(s, axis=-1, keepdims=True)
            p = jnp.where(mask_c, jnp.exp(s - m), 0.0)
            den = jnp.sum(p, axis=-1, keepdims=True)
            p = p / jnp.where(den > 0, den, 1.0)
            oc_s[h] = _mm(p, vc)
            imp_e = imp_e + p[:, 0:nb]
            imp_o = imp_o + p[:, nb:2 * nb]
        score = jnp.where(forced, FORCE_SCORE, jnp.where(valid, imp_e + imp_o, -FORCE_SCORE))
        rank = jnp.zeros((tq, nb), F32)
        for jp in range(nb):
            cj = score[:, jp:jp + 1]
            rank = rank + jnp.where((cj > score) | ((cj == score) & (jp < jcol)), 1.0, 0.0)
        sel = jnp.where(rank < N_SELECT, 1.0, 0.0).astype(BF16)

        _online_init(m_s, l_s, acc_s)
        _online_init(m2_s, l2_s, acc2_s)

        def sel_body(kt, carry):
            k0 = pl.multiple_of(kt * tk, tk)
            kpos = kt * tk + _iota((1, tk), 1)
            dist_i = t_col - kpos
            dist = dist_i.astype(F32)
            expand = jnp.where(_iota((nb, tk), 0) == lax.shift_right_logical(kpos, 6), 1.0, 0.0).astype(BF16)
            mask = (jnp.dot(sel, expand, preferred_element_type=F32) > 0.5) & (dist_i >= 0)
            kt_k = kv_ref[0, pl.ds(k0, tk), 256 + 64 * kvh:320 + 64 * kvh]
            kt_v = kv_ref[0, pl.ds(k0, tk), 384 + 64 * kvh:448 + 64 * kvh]
            for g in range(NSA_GROUP):
                h = NSA_GROUP * kvh + g
                s = _mm_nt(q_s[h], kt_k) - _slope(h) * dist
                _online_update(s, mask, kt_v, m_s, l_s, acc_s, g)
            return carry

        lax.fori_loop(0, i + 1, sel_body, 0)

        def win_body(kt, carry):
            k0 = pl.multiple_of(kt * tk, tk)
            kpos = kt * tk + _iota((1, tk), 1)
            dist_i = t_col - kpos
            dist = dist_i.astype(F32)
            mask = (dist_i >= 0) & (dist_i <= WINDOW)
            kt_k = kv_ref[0, pl.ds(k0, tk), 512 + 64 * kvh:576 + 64 * kvh]
            kt_v = kv_ref[0, pl.ds(k0, tk), 640 + 64 * kvh:704 + 64 * kvh]
            for g in range(NSA_GROUP):
                h = NSA_GROUP * kvh + g
                s = _mm_nt(q_s[h], kt_k) - _slope(h) * dist
                _online_update(s, mask, kt_v, m2_s, l2_s, acc2_s, g)
            return carry

        lax.fori_loop(jnp.maximum(i - win_tiles, 0), i + 1, win_body, 0)

        for g in range(NSA_GROUP):
            h = NSA_GROUP * kvh + g
            gates = _sigmoid(bg_ref[0, :, 3 * h:3 * h + 3])
            o = (gates[:, 0:1] * oc_s[h] + gates[:, 1:2] * _online_result(l_s, acc_s, g)
                 + gates[:, 2:3] * _online_result(l2_s, acc2_s, g))
            o_ref[0, :, 64 * h:64 * h + 64] = o * _silu(g_ref[0, :, 64 * h:64 * h + 64])


def _nsa_prompt(zn, *, tq):
    b, t, _ = zn.shape
    nb = t // SEL_BLOCK
    kern = functools.partial(_nsa_prompt_kernel, t_len=t, tq=tq)
    stat = lambda: [pltpu.VMEM((NSA_GROUP, tq, 1), F32), pltpu.VMEM((NSA_GROUP, tq, 1), F32),
                    pltpu.VMEM((NSA_GROUP, tq, NSA_HD), F32)]
    return pl.pallas_call(
        kern,
        grid=(b, t // tq),
        in_specs=[pl.BlockSpec((1, tq, 512), lambda i, j: (i, j, 2)),
                  pl.BlockSpec((1, t, 768), lambda i, j: (i, 0, 0)),
                  pl.BlockSpec((1, tq, 256), lambda i, j: (i, j, 3)),
                  pl.BlockSpec((1, tq, 512), lambda i, j: (i, j, 3))],
        out_specs=pl.BlockSpec((1, tq, 512), lambda i, j: (i, j, 0)),
        out_shape=jax.ShapeDtypeStruct((b, t, 512), F32),
        scratch_shapes=[pltpu.VMEM((2 * nb, 128), F32), pltpu.VMEM((2 * nb, 128), F32),
                        pltpu.VMEM((NSA_HEADS, tq, NSA_HD), BF16), pltpu.VMEM((NSA_HEADS, tq, NSA_HD), F32)]
        + stat() + stat(),
        compiler_params=_cparams(("parallel", "arbitrary")),
        name="nsa_prompt",
    )(zn, zn, zn, zn)


PAGES_PER_STEP = 8
N_PAGE_GROUPS = N_PAGES // PAGES_PER_STEP
N_PAST_SEL = PAST_LEN // SEL_BLOCK


def _nsa_sample_kernel(pt_ref, zn_ref, win_ref, *rest):
    pages = rest[:PAGES_PER_STEP]
    o_ref = rest[PAGES_PER_STEP]
    kce_s, kco_s, vce_s, vco_s, sel_s, oc_s, m_s, l_s, acc_s = rest[PAGES_PER_STEP + 1:]
    ph = pl.program_id(1)
    gi = pl.program_id(2)
    rq = SAMPLE_TPAD
    npast = N_PAST_SEL
    t_col = PAST_LEN + _iota((rq, 1), 0)
    keys_per_step = PAGES_PER_STEP * PAGE_SIZE

    def q_head(h):
        return zn_ref[0, :, 1024 + 64 * h:1088 + 64 * h] * (NSA_HD ** -0.5)

    @pl.when(ph == 0)
    def _pool():
        ek, ok_, ev, ov = [], [], [], []
        for pr in pages:
            xk = pr[0, :, 0:128].reshape(PAGE_SIZE // SEL_BLOCK, SEL_BLOCK, 128)
            xv = pr[0, :, 128:256].reshape(PAGE_SIZE // SEL_BLOCK, SEL_BLOCK, 128)
            ek.append(jnp.sum(xk[:, 0:CMP_BLOCK], axis=1) * (1.0 / CMP_BLOCK))
            ok_.append(jnp.sum(xk[:, CMP_BLOCK:SEL_BLOCK], axis=1) * (1.0 / CMP_BLOCK))
            ev.append(jnp.sum(xv[:, 0:CMP_BLOCK], axis=1) * (1.0 / CMP_BLOCK))
            ov.append(jnp.sum(xv[:, CMP_BLOCK:SEL_BLOCK], axis=1) * (1.0 / CMP_BLOCK))
        nrow = PAGES_PER_STEP * (PAGE_SIZE // SEL_BLOCK)
        dst = pl.ds(pl.multiple_of(gi * nrow, nrow), nrow)
        kce_s[dst, :] = jnp.concatenate(ek, axis=0)
        kco_s[dst, :] = jnp.concatenate(ok_, axis=0)
        vce_s[dst, :] = jnp.concatenate(ev, axis=0)
        vco_s[dst, :] = jnp.concatenate(ov, axis=0)

    @pl.when((ph == 0) & (gi == N_PAGE_GROUPS - 1))
    def _compressed():
        jrow = _iota((1, npast), 1)
        dist_e = (t_col - (SEL_BLOCK * jrow + (CMP_BLOCK - 1))).astype(F32)
        dist_o = (t_col - (SEL_BLOCK * jrow + (SEL_BLOCK - 1))).astype(F32)
        jp = _iota((npast, npast), 0)
        jj = _iota((npast, npast), 1)
        for kvh in range(NSA_KV_HEADS):
            ls = slice(64 * kvh, 64 * kvh + 64)
            kce, kco, vce, vco = kce_s[:, ls], kco_s[:, ls], vce_s[:, ls], vco_s[:, ls]
            imp_e = jnp.zeros((rq, npast), F32)
            imp_o = jnp.zeros((rq, npast), F32)
            for g in range(NSA_GROUP):
                h = NSA_GROUP * kvh + g
                qh = q_head(h)
                se = _mm_nt(qh, kce) - _slope(h) * dist_e
                so = _mm_nt(qh, kco) - _slope(h) * dist_o
                m = jnp.maximum(jnp.max(se, axis=-1, keepdims=True), jnp.max(so, axis=-1, keepdims=True))
                pe = jnp.exp(se - m)
                po = jnp.exp(so - m)
                den = jnp.sum(pe, axis=-1, keepdims=True) + jnp.sum(po, axis=-1, keepdims=True)
                pe = pe / den
                po = po / den
                oc_s[h] = _mm(pe, vce) + _mm(po, vco)
                imp_e = imp_e + pe
                imp_o = imp_o + po
            imp = imp_e + imp_o
            imp_t = jnp.concatenate([imp, jnp.zeros((128 - rq, npast), F32)], axis=0).T
            sel_rows = []
            for t in range(rq):
                col = imp_t[:, t:t + 1]
                row = imp[t:t + 1, :]
                beats = ((col > row) | ((col == row) & (jp < jj))) & (jp >= 1)
                rank = jnp.sum(jnp.where(beats, 1.0, 0.0), axis=0, keepdims=True)
                sel_rows.append(jnp.where((jrow == 0) | (rank < N_SELECT - 2), 1.0, 0.0))
            sel_s[kvh] = jnp.concatenate(sel_rows, axis=0).astype(BF16)

    @pl.when(ph == 1)
    def _selected():
        @pl.when(gi == 0)
        def _():
            _online_init(m_s, l_s, acc_s)

        k_all = jnp.concatenate([pr[0, :, 0:128] for pr in pages], axis=0)
        v_all = jnp.concatenate([pr[0, :, 128:256] for pr in pages], axis=0)
        kpos = gi * keys_per_step + _iota((1, keys_per_step), 1)
        dist = (t_col - kpos).astype(F32)
        expand = jnp.where(_iota((npast, keys_per_step), 0) == lax.shift_right_logical(kpos, 6), 1.0, 0.0).astype(BF16)
        for kvh in range(NSA_KV_HEADS):
            ls = slice(64 * kvh, 64 * kvh + 64)
            mask = jnp.dot(sel_s[kvh], expand, preferred_element_type=F32) > 0.5
            for g in range(NSA_GROUP):
                h = NSA_GROUP * kvh + g
                s = _mm_nt(q_head(h), k_all[:, ls]) - _slope(h) * dist
                _online_update(s, mask, v_all[:, ls], m_s, l_s, acc_s, h)

    @pl.when((ph == 1) & (gi == N_PAGE_GROUPS - 1))
    def _finish():
        new = zn_ref[0, :, 0:768]
        npos = PAST_LEN + _iota((1, rq), 1)
        dist_ni = t_col - npos
        dist_n = dist_ni.astype(F32)
        mask_n = dist_ni >= 0
        wpos = PAST_LEN - WINDOW + _iota((1, WINDOW), 1)
        dist_wi = t_col - wpos
        dist_w = dist_wi.astype(F32)
        mask_w = dist_wi <= WINDOW
        for kvh in range(NSA_KV_HEADS):
            ls = slice(64 * kvh, 64 * kvh + 64)
            wk = win_ref[0, :, 64 * kvh:64 * kvh + 64]
            wv = win_ref[0, :, 128 + 64 * kvh:192 + 64 * kvh]
            for g in range(NSA_GROUP):
                h = NSA_GROUP * kvh + g
                qh = q_head(h)
                sl = _slope(h)
                s = _mm_nt(qh, new[:, 256 + 64 * kvh:320 + 64 * kvh]) - sl * dist_n
                _online_update(s, mask_n, new[:, 384 + 64 * kvh:448 + 64 * kvh], m_s, l_s, acc_s, h)
                o_sel = _online_result(l_s, acc_s, h)
                s1 = jnp.where(mask_w, _mm_nt(qh, wk) - sl * dist_w, NEG)
                s2 = jnp.where(mask_n, _mm_nt(qh, new[:, 512 + 64 * kvh:576 + 64 * kvh]) - sl * dist_n, NEG)
                m = jnp.maximum(jnp.max(s1, axis=-1, keepdims=True), jnp.max(s2, axis=-1, keepdims=True))
                p1 = jnp.where(mask_w, jnp.exp(s1 - m), 0.0)
                p2 = jnp.where(mask_n, jnp.exp(s2 - m), 0.0)
                den = jnp.sum(p1, axis=-1, keepdims=True) + jnp.sum(p2, axis=-1, keepdims=True)
                o_win = (_mm(p1, wv) + _mm(p2, new[:, 640 + 64 * kvh:704 + 64 * kvh])) / jnp.where(den > 0, den, 1.0)
                gates = _sigmoid(zn_ref[0, :, 768 + 3 * h:771 + 3 * h])
                o = gates[:, 0:1] * oc_s[h] + gates[:, 1:2] * o_sel + gates[:, 2:3] * o_win
                o_ref[0, :, 64 * h:64 * h + 64] = o * _silu(zn_ref[0, :, 1536 + 64 * h:1600 + 64 * h])


def _nsa_sample(page_table, zn, win_state, cache, *, layer, n_pool):
    b = zn.shape[0]
    rq = SAMPLE_TPAD

    def page_map(kidx):
        return lambda i, ph, gi, pt: (layer * n_pool + pt[i, gi * PAGES_PER_STEP + kidx], 0, ph)

    grid_spec = pltpu.PrefetchScalarGridSpec(
        num_scalar_prefetch=1,
        grid=(b, 2, N_PAGE_GROUPS),
        in_specs=[pl.BlockSpec((1, rq, ZN_W), lambda i, ph, gi, pt: (i, 0, 0)),
                  pl.BlockSpec((1, WINDOW, 256), lambda i, ph, gi, pt: (layer * b + i, 0, 0))]
        + [pl.BlockSpec((1, PAGE_SIZE, 256), page_map(kidx)) for kidx in range(PAGES_PER_STEP)],
        out_specs=pl.BlockSpec((1, rq, 512), lambda i, ph, gi, pt: (i, 0, 0)),
        scratch_shapes=[pltpu.VMEM((N_PAST_SEL, 128), F32) for _ in range(4)]
        + [pltpu.VMEM((NSA_KV_HEADS, rq, N_PAST_SEL), BF16), pltpu.VMEM((NSA_HEADS, rq, NSA_HD), F32),
           pltpu.VMEM((NSA_HEADS, rq, 1), F32), pltpu.VMEM((NSA_HEADS, rq, 1), F32),
           pltpu.VMEM((NSA_HEADS, rq, NSA_HD), F32)],
    )
    return pl.pallas_call(
        _nsa_sample_kernel,
        grid_spec=grid_spec,
        out_shape=jax.ShapeDtypeStruct((b, rq, 512), F32),
        compiler_params=_cparams(("parallel", "arbitrary", "arbitrary")),
        name="nsa_sample",
    )(page_table, zn, win_state, *([cache] * PAGES_PER_STEP))


def _merge_kernel(x_ref, bg_ref, br_ref, bn_ref, mg_ref, wbr_ref, wout_ref, lng_ref, lnb_ref, o_ref):
    acc = _sigmoid(mg_ref[:, 0:1024]) * _mm(bg_ref[...], wbr_ref[0])
    acc = acc + _sigmoid(mg_ref[:, 1024:2048]) * _mm(br_ref[...], wbr_ref[1])
    acc = acc + _sigmoid(mg_ref[:, 2048:3072]) * _mm(bn_ref[...], wbr_ref[2])
    xf = DN_ALPHA * x_ref[...] + _mm(acc, wout_ref[...])
    mu = jnp.mean(xf, axis=-1, keepdims=True)
    d = xf - mu
    var = jnp.mean(d * d, axis=-1, keepdims=True)
    o_ref[...] = d * lax.rsqrt(var + LN_EPS) * lng_ref[...] + lnb_ref[...]


def _merge(x, o_gla, o_rwkv, o_nsa, zm, w_br, w_out, ln_g, ln_b, tm):
    m = x.shape[0]
    row = lambda n: pl.BlockSpec((tm, n), lambda i: (i, 0))
    return pl.pallas_call(
        _merge_kernel,
        grid=(m // tm,),
        in_specs=[row(D_MODEL), row(512), row(512), row(512), row(ZM_W),
                  pl.BlockSpec((3, 512, D_MODEL), lambda i: (0, 0, 0)),
                  pl.BlockSpec((D_MODEL, D_MODEL), lambda i: (0, 0)),
                  pl.BlockSpec((1, D_MODEL), lambda i: (0, 0)),
                  pl.BlockSpec((1, D_MODEL), lambda i: (0, 0))],
        out_specs=row(D_MODEL),
        out_shape=jax.ShapeDtypeStruct((m, D_MODEL), F32),
        compiler_params=_cparams(("parallel",)),
        name="merge",
    )(x, o_gla, o_rwkv, o_nsa, zm, w_br, w_out, ln_g, ln_b)


def _pack_weights(w_in, b_in):
    def pack(a):
        z = lambda n: jnp.zeros(a.shape[:-1] + (n,), a.dtype)
        gla = jnp.concatenate([a[..., 0:1024], a[..., 1040:1552], a[..., 1024:1040], z(ZG_W - 1552)], axis=-1)
        rwkv = a[..., 1552:3728]
        nsa = jnp.concatenate([a[..., 4240:5008], a[..., 5008:5032], z(1024 - 792), a[..., 3728:4240],
                               a[..., 5032:5544]], axis=-1)
        mg = a[..., 5544:8616]
        return gla, rwkv, nsa, mg
    ws = [w.astype(BF16) for w in pack(w_in)]
    bs = [b[:, None, :] for b in pack(b_in)]
    return ws, bs


def kernel(x_prompt, x_sample, cache_nsa_kv, state_nsa_win, state_gla, state_rwkv, state_rwkv_shift, page_table,
           w_in, b_in, gla_a_up, gla_a_bias, gla_norm, rwkv_mu, rwkv_w0, rwkv_w_up, rwkv_a0, rwkv_a_up, rwkv_k_k,
           rwkv_k_a, rwkv_r_k, rwkv_ln_w, rwkv_ln_b, w_br, w_out, ln_g, ln_b):
    bp, tp, _ = x_prompt.shape
    bs, ts, _ = x_sample.shape
    n_pool = cache_nsa_kv.shape[1]
    ws, bws = _pack_weights(w_in, b_in)
    w_br_b = w_br.astype(BF16)
    w_out_b = w_out.astype(BF16)
    seg = (jnp.arange(RWKV_W)[:, None] // RWKV_HD == jnp.arange(RWKV_W)[None, :] // RWKV_HD).astype(BF16)
    cache = cache_nsa_kv.reshape(DEPTH * n_pool, PAGE_SIZE, 512)
    win_state = state_nsa_win.reshape(DEPTH * bs, WINDOW, 256)
    row2 = lambda a: a.reshape(DEPTH, 1, -1)
    gla_a_bias2, gla_norm2 = row2(gla_a_bias), row2(gla_norm)
    r_par = [row2(rwkv_mu), row2(rwkv_w0), rwkv_w_up, row2(rwkv_a0), rwkv_a_up, row2(rwkv_k_k), row2(rwkv_k_a),
             row2(rwkv_r_k), row2(rwkv_ln_w), row2(rwkv_ln_b)]
    ln_g2, ln_b2 = row2(ln_g), row2(ln_b)

    xp = x_prompt.reshape(bp * tp, D_MODEL)
    xs = jnp.pad(x_sample, ((0, 0), (0, SAMPLE_TPAD - ts), (0, 0))).reshape(bs * SAMPLE_TPAD, D_MODEL)
    zeros_gla = jnp.zeros((bp, GLA_HEADS, GLA_DK, GLA_DV), F32)
    zeros_rwkv = jnp.zeros((bp, RWKV_HEADS, RWKV_HD, RWKV_HD), F32)
    zeros_shift = jnp.zeros((bp, 1, RWKV_IN), F32)

    outs = {k: [] for k in ("kv_p", "kv_s", "win_p", "win_s", "gla_p", "gla_s", "rwkv_p", "rwkv_s", "sh_p", "sh_s")}
    for l in range(DEPTH):
        rp = [p[l] for p in r_par]
        zg, zr, zn, zm = (_proj(xp, ws[i][l], bws[i][l], 512) for i in range(4))
        zn3 = zn.reshape(bp, tp, ZN_W)
        o_gla, gla_st = _gla(zg.reshape(bp, tp, ZG_W), zeros_gla, gla_a_up[l], gla_a_bias2[l], gla_norm2[l],
                             tb_rows=512, chunk=GLA_CHUNK, t_valid=None)
        zr3 = zr.reshape(bp, tp, RWKV_IN)
        o_rwkv, rwkv_st = _rwkv(zr3, zeros_rwkv, zeros_shift, seg, rp, tb_rows=512, chunk=RWKV_CHUNK, t_valid=None)
        o_nsa = _nsa_prompt(zn3, tq=256)
        xp = _merge(xp, o_gla.reshape(bp * tp, 512), o_rwkv.reshape(bp * tp, 512), o_nsa.reshape(bp * tp, 512), zm,
                    w_br_b[l], w_out_b[l], ln_g2[l], ln_b2[l], 512)
        outs["kv_p"].append(zn3[:, :, 0:512].reshape(bp, tp, 4, NSA_KV_HEADS, NSA_HD))
        outs["win_p"].append(zn3[:, tp - WINDOW:, 512:768].reshape(bp, WINDOW, 2, NSA_KV_HEADS, NSA_HD))
        outs["gla_p"].append(gla_st)
        outs["rwkv_p"].append(rwkv_st)
        outs["sh_p"].append(zr3[:, tp - 1, :])
        rows_s = bs * SAMPLE_TPAD
        zg, zr, zn, zm = (_proj(xs, ws[i][l], bws[i][l], rows_s) for i in range(4))
        zn3 = zn.reshape(bs, SAMPLE_TPAD, ZN_W)
        o_gla, gla_st = _gla(zg.reshape(bs, SAMPLE_TPAD, ZG_W), state_gla[l], gla_a_up[l], gla_a_bias2[l],
                             gla_norm2[l], tb_rows=SAMPLE_TPAD, chunk=SAMPLE_TPAD, t_valid=ts)
        zr3 = zr.reshape(bs, SAMPLE_TPAD, RWKV_IN)
        o_rwkv, rwkv_st = _rwkv(zr3, state_rwkv[l], state_rwkv_shift[l][:, None, :], seg, rp,
                                tb_rows=SAMPLE_TPAD, chunk=SAMPLE_TPAD, t_valid=ts)
        o_nsa = _nsa_sample(page_table, zn3, win_state, cache, layer=l, n_pool=n_pool)
        xs = _merge(xs, o_gla.reshape(rows_s, 512), o_rwkv.reshape(rows_s, 512), o_nsa.reshape(rows_s, 512), zm,
                    w_br_b[l], w_out_b[l], ln_g2[l], ln_b2[l], rows_s)
        outs["kv_s"].append(zn3[:, 0:ts, 0:512].reshape(bs, ts, 4, NSA_KV_HEADS, NSA_HD))
        new_win = zn3[:, 0:ts, 512:768].reshape(bs, ts, 2, NSA_KV_HEADS, NSA_HD)
        outs["win_s"].append(jnp.concatenate([state_nsa_win[l][:, ts:], new_win], axis=1))
        outs["gla_s"].append(gla_st)
        outs["rwkv_s"].append(rwkv_st)
        outs["sh_s"].append(zr3[:, ts - 1, :])

    st = lambda k: jnp.stack(outs[k])
    y_prompt = xp.reshape(bp, tp, D_MODEL)
    y_sample = xs.reshape(bs, SAMPLE_TPAD, D_MODEL)[:, 0:ts]
    return (y_prompt, y_sample, st("kv_p"), st("kv_s"), st("win_p"), st("win_s"), st("gla_p"), st("gla_s"),
            st("rwkv_p"), st("rwkv_s"), st("sh_p"), st("sh_s"))
```

```python
import functools

import jax
import jax.numpy as jnp
from jax import lax
from jax.experimental import pallas as pl
from jax.experimental.pallas import tpu as pltpu

F32 = jnp.float32
BF16 = jnp.bfloat16

D_MODEL = 1024
DEPTH = 2
PAST_LEN = 16384
PAGE_SIZE = 128
N_PAGES = PAST_LEN // PAGE_SIZE

GLA_HEADS, GLA_DK, GLA_DV = 4, 64, 128
GLA_K, GLA_V, GLA_LORA = 256, 512, 16
GLA_GATE_NORM = 16.0
GLA_CHUNK = 64
GLA_SUB = 16

RWKV_HEADS, RWKV_HD, RWKV_W = 8, 64, 512
RWKV_IN = 2176
RWKV_LN_EPS = 64e-5
RWKV_CHUNK = 64

NSA_HEADS, NSA_KV_HEADS, NSA_GROUP, NSA_HD = 8, 2, 4, 64
CMP_BLOCK, SEL_BLOCK, N_SELECT, WINDOW = 32, 64, 16, 512
FORCE_SCORE = 1e9
NEG = -1e30

DN_ALPHA = (2 * DEPTH) ** 0.25
LN_EPS = 1e-5
NORM_EPS = 1e-6

ZG_W = 1664
ZN_W = 2048
ZM_W = 3072
SAMPLE_TPAD = 16
VMEM_LIMIT = 56 * 1024 * 1024


def _mm(a, b):
    return jnp.dot(a.astype(BF16), b.astype(BF16), preferred_element_type=F32)


def _mm_nt(a, b):
    return lax.dot_general(a.astype(BF16), b.astype(BF16), (((1,), (1,)), ((), ())), preferred_element_type=F32)


def _mm_tn(a, b):
    return lax.dot_general(a.astype(BF16), b.astype(BF16), (((0,), (0,)), ((), ())), preferred_element_type=F32)


def _mm_exact(a, b):
    return jnp.dot(a, b, preferred_element_type=F32, precision=lax.Precision.HIGHEST)


def _mm_split(a, b01):
    hi = a.astype(BF16)
    lo = (a - hi.astype(F32)).astype(BF16)
    return (jnp.dot(hi, b01, preferred_element_type=F32) + jnp.dot(lo, b01, preferred_element_type=F32))


def _softplus(x):
    return jnp.maximum(x, 0.0) + jnp.log1p(jnp.exp(-jnp.abs(x)))


def _sigmoid(x):
    return 1.0 / (1.0 + jnp.exp(-x))


def _silu(x):
    return x * _sigmoid(x)


def _iota(shape, dim):
    return lax.broadcasted_iota(jnp.int32, shape, dim)


def _cparams(sem):
    return pltpu.CompilerParams(dimension_semantics=sem, vmem_limit_bytes=VMEM_LIMIT)


def _proj_kernel(x_ref, w_ref, b_ref, o_ref):
    o_ref[...] = _mm(x_ref[...], w_ref[...]) + b_ref[...]


def _proj(x, w, b, tm):
    m, k = x.shape
    n = w.shape[1]
    return pl.pallas_call(
        _proj_kernel,
        grid=(m // tm,),
        in_specs=[pl.BlockSpec((tm, k), lambda i: (i, 0)),
                  pl.BlockSpec((k, n), lambda i: (0, 0)),
                  pl.BlockSpec((1, n), lambda i: (0, 0))],
        out_specs=pl.BlockSpec((tm, n), lambda i: (i, 0)),
        out_shape=jax.ShapeDtypeStruct((m, n), F32),
        compiler_params=_cparams(("parallel",)),
        name="proj",
    )(x, w, b)


def _gla_kernel(zg_ref, s0_ref, aup_ref, abias_ref, norm_ref, o_ref, sout_ref, st_scr, *, tb_rows, chunk, t_valid):
    tb = pl.program_id(1)
    c_rows = chunk
    sub = min(GLA_SUB, c_rows)
    nsub = c_rows // sub

    @pl.when(tb == 0)
    def _():
        for h in range(GLA_HEADS):
            st_scr[h] = s0_ref[0, h].T

    tri = (_iota((c_rows, c_rows), 1) <= _iota((c_rows, c_rows), 0)).astype(F32)
    ones_red = jnp.ones((GLA_DK, 128), BF16)
    lane_s = _iota((sub, 128), 1)
    row_s = _iota((sub, 128), 0)
    col_c = _iota((sub, c_rows), 1)

    def chunk_body(c, carry):
        r0 = pl.multiple_of(c * c_rows, c_rows)
        z = zg_ref[0, pl.ds(r0, c_rows), :]
        q = z[:, 0:256] * (GLA_DK ** -0.5)
        k = z[:, 256:512]
        v = z[:, 512:1024]
        g = z[:, 1024:1536]
        ga = z[:, 1536:1552]
        la = -_softplus(-(_mm(ga, aup_ref[...]) + abias_ref[...])) * (1.0 / GLA_GATE_NORM)
        if t_valid is not None:
            ok = (tb * tb_rows + r0 + _iota((c_rows, 1), 0)) < t_valid
            la = jnp.where(ok, la, 0.0)
            k = jnp.where(ok, k, 0.0)
            v = jnp.where(ok, v, 0.0)
        cum = _mm_exact(tri, la)
        for h in range(GLA_HEADS):
            qh = q[:, 64 * h:64 * h + 64]
            kh = k[:, 64 * h:64 * h + 64]
            ch = cum[:, 64 * h:64 * h + 64]
            vh = v[:, 128 * h:128 * h + 128]
            st = st_scr[h]
            o = _mm_nt(qh * jnp.exp(ch), st)
            att_rows = []
            for blk in range(nsub):
                sl = slice(blk * sub, (blk + 1) * sub)
                q_i, k_i, c_i = qh[sl], kh[sl], ch[sl]
                es = [q_i * k_i[j:j + 1] * jnp.exp(jnp.minimum(c_i - c_i[j:j + 1], 0.0)) for j in range(sub)]
                red = _mm(jnp.concatenate(es, axis=0), ones_red)
                a_i = jnp.zeros((sub, 128), F32)
                for j in range(sub):
                    a_i = a_i + jnp.where((lane_s == blk * sub + j) & (row_s >= j), red[j * sub:(j + 1) * sub], 0.0)
                a_i = a_i[:, 0:c_rows]
                if blk > 0:
                    b_i = ch[blk * sub - 1:blk * sub]
                    q_t = q_i * jnp.exp(c_i - b_i)
                    k_t = kh * jnp.exp(jnp.minimum(b_i - ch, 0.0))
                    a_i = a_i + jnp.where(col_c < blk * sub, _mm_nt(q_t, k_t), 0.0)
                att_rows.append(a_i)
            att = att_rows[0] if nsub == 1 else jnp.concatenate(att_rows, axis=0)
            o = o + _mm(att, vh)
            last = ch[c_rows - 1:c_rows]
            st_scr[h] = st * jnp.exp(last) + _mm_tn(vh, kh * jnp.exp(last - ch))
            o = o * lax.rsqrt(jnp.mean(o * o, axis=-1, keepdims=True) + NORM_EPS)
            o = o * norm_ref[:, 128 * h:128 * h + 128] * _silu(g[:, 128 * h:128 * h + 128])
            o_ref[0, pl.ds(r0, c_rows), 128 * h:128 * h + 128] = o
        return carry

    lax.fori_loop(0, tb_rows // c_rows, chunk_body, 0)

    @pl.when(tb == pl.num_programs(1) - 1)
    def _():
        for h in range(GLA_HEADS):
            sout_ref[0, h] = st_scr[h].T


def _gla(zg, s0, a_up, a_bias, norm_g, *, tb_rows, chunk, t_valid):
    b, t, _ = zg.shape
    kern = functools.partial(_gla_kernel, tb_rows=tb_rows, chunk=chunk, t_valid=t_valid)
    return pl.pallas_call(
        kern,
        grid=(b, t // tb_rows),
        in_specs=[pl.BlockSpec((1, tb_rows, ZG_W), lambda i, j: (i, j, 0)),
                  pl.BlockSpec((1, GLA_HEADS, GLA_DK, GLA_DV), lambda i, j: (i, 0, 0, 0)),
                  pl.BlockSpec((GLA_LORA, GLA_K), lambda i, j: (0, 0)),
                  pl.BlockSpec((1, GLA_K), lambda i, j: (0, 0)),
                  pl.BlockSpec((1, GLA_V), lambda i, j: (0, 0))],
        out_specs=[pl.BlockSpec((1, tb_rows, GLA_V), lambda i, j: (i, j, 0)),
                   pl.BlockSpec((1, GLA_HEADS, GLA_DK, GLA_DV), lambda i, j: (i, 0, 0, 0))],
        out_shape=[jax.ShapeDtypeStruct((b, t, GLA_V), F32),
                   jax.ShapeDtypeStruct((b, GLA_HEADS, GLA_DK, GLA_DV), F32)],
        scratch_shapes=[pltpu.VMEM((GLA_HEADS, GLA_DV, GLA_DK), F32)],
        compiler_params=_cparams(("parallel", "arbitrary")),
        name="gla",
    )(zg, s0, a_up, a_bias, norm_g)


def _rwkv_kernel(zr_ref, s0_ref, sh0_ref, seg_ref, mu_ref, w0_ref, wup_ref, a0_ref, aup_ref, kk_ref, ka_ref, rk_ref,
                 lnw_ref, lnb_ref, y_ref, sout_ref,
                 s_scr, prev_scr, lw_s, kk_s, kka_s, k2_s, r_s, v_s, y_s, *, tb_rows, chunk, t_valid):
    tb = pl.program_id(1)
    c_rows = chunk
    nh = RWKV_HEADS

    @pl.when(tb == 0)
    def _():
        s_scr[...] = s0_ref[0]
        prev_scr[...] = sh0_ref[0]

    z = zr_ref[0]
    rows = _iota((tb_rows, 1), 0)
    zp = jnp.where(rows == 0, prev_scr[...], pltpu.roll(z, 1, axis=0))
    prev_scr[...] = z[tb_rows - 1:tb_rows]
    zs = z + (zp - z) * mu_ref[...]
    r = zs[:, 0:512]
    k = zs[:, 512:1024]
    v = zs[:, 1024:1536]
    wl = zs[:, 1536:1600]
    al = zs[:, 1600:1664]
    w = -_softplus(-(w0_ref[...] + _mm(jnp.tanh(wl), wup_ref[...]))) - 0.5
    lw = -jnp.exp(w)
    a = _sigmoid(a0_ref[...] + _mm(al, aup_ref[...]))
    kk = k * kk_ref[...]
    kk = kk * lax.rsqrt(_mm_split(kk * kk, seg_ref[...]) + NORM_EPS)
    k2 = k * (1.0 + (a - 1.0) * ka_ref[...])
    kka = kk * a
    if t_valid is not None:
        ok = (tb * tb_rows + rows) < t_valid
        lw = jnp.where(ok, lw, 0.0)
        kka = jnp.where(ok, kka, 0.0)
        k2 = jnp.where(ok, k2, 0.0)
    lw_s[...] = lw
    kk_s[...] = kk
    kka_s[...] = kka
    k2_s[...] = k2
    r_s[...] = r
    v_s[...] = v

    ri = _iota((c_rows, c_rows), 0)
    ci = _iota((c_rows, c_rows), 1)
    tri = (ci <= ri).astype(F32)
    strict = ci < ri
    incl = ci <= ri
    n_dbl = max(1, (c_rows - 1).bit_length())

    def chunk_body(c, carry):
        r0 = pl.multiple_of(c * c_rows, c_rows)
        ds = pl.ds(r0, c_rows)
        lwc = lw_s[ds, :]
        cl = _mm_exact(tri, lwc)
        e_inv = jnp.exp(-cl)
        e_fwd = jnp.exp(cl)
        e_prev = jnp.exp(cl - lwc)
        e_end = jnp.exp(cl[c_rows - 1:c_rows] - cl)
        g_end = jnp.exp(cl[c_rows - 1:c_rows])
        kkc, kkac, k2c, rc, vc = kk_s[ds, :], kka_s[ds, :], k2_s[ds, :], r_s[ds, :], v_s[ds, :]
        heads = range(nh)
        hsl = [slice(64 * h, 64 * h + 64) for h in heads]
        a_t = [-kkac[:, hs] * e_inv[:, hs] for hs in hsl]
        b_t = [kkc[:, hs] * e_prev[:, hs] for hs in hsl]
        k_t = [k2c[:, hs] * e_inv[:, hs] for hs in hsl]
        r_t = [rc[:, hs] * e_fwd[:, hs] for hs in hsl]
        vh = [vc[:, hs] for hs in hsl]
        ak = [jnp.concatenate([a_t[h], k_t[h]], axis=0).astype(BF16) for h in heads]
        ba = [_mm_nt(b_t[h], ak[h]) for h in heads]
        ra = [_mm_nt(r_t[h], ak[h]) for h in heads]
        s0 = [s_scr[h] for h in heads]
        rs = [_mm_nt(r_t[h], s0[h]) for h in heads]
        l_k = [jnp.where(strict, ba[h][:, c_rows:2 * c_rows], 0.0) for h in heads]
        lkv = [_mm(l_k[h], vh[h]) for h in heads]
        x = [jnp.concatenate([b_t[h], lkv[h]], axis=1) for h in heads]
        lp = [jnp.where(strict, ba[h][:, 0:c_rows], 0.0).astype(BF16) for h in heads]
        for step in range(n_dbl):
            x = [x[h] + _mm(lp[h], x[h]) for h in heads]
            if step + 1 < n_dbl:
                lp = [_mm(lp[h], lp[h]).astype(BF16) for h in heads]
        u = [_mm_nt(x[h][:, 0:64], s0[h]) + x[h][:, 64:128] for h in heads]
        m_a = [jnp.where(incl, ra[h][:, 0:c_rows], 0.0) for h in heads]
        m_k = [jnp.where(incl, ra[h][:, c_rows:2 * c_rows], 0.0) for h in heads]
        y = [rs[h] + _mm(m_a[h], u[h]) + _mm(m_k[h], vh[h]) for h in heads]
        akg = [jnp.concatenate([-kkac[:, hs] * e_end[:, hs], k2c[:, hs] * e_end[:, hs]], axis=0) for hs in hsl]
        s_new = [s0[h] * g_end[:, hsl[h]] + _mm_tn(jnp.concatenate([u[h], vh[h]], axis=0), akg[h]) for h in heads]
        for h in heads:
            s_scr[h] = s_new[h]
            y_s[ds, hsl[h]] = y[h]
        return carry

    lax.fori_loop(0, tb_rows // c_rows, chunk_body, 0)

    y = y_s[...]
    seg = seg_ref[...]
    mean = _mm_split(y, seg) * (1.0 / RWKV_HD)
    d = y - mean
    var = _mm_split(d * d, seg) * (1.0 / RWKV_HD)
    yn = d * lax.rsqrt(var + RWKV_LN_EPS) * lnw_ref[...] + lnb_ref[...]
    bonus = _mm_split(r * k2 * rk_ref[...], seg) * v
    y_ref[0] = (yn + bonus) * _silu(zs[:, 1664:2176])

    @pl.when(tb == pl.num_programs(1) - 1)
    def _():
        sout_ref[0] = s_scr[...]


def _rwkv(zr, s0, sh0, seg, params, *, tb_rows, chunk, t_valid):
    b, t, _ = zr.shape
    kern = functools.partial(_rwkv_kernel, tb_rows=tb_rows, chunk=chunk, t_valid=t_valid)
    full = lambda shp: pl.BlockSpec(shp, lambda i, j: (0,) * len(shp))
    mu, w0, w_up, a0, a_up, k_k, k_a, r_k, ln_w, ln_b = params
    return pl.pallas_call(
        kern,
        grid=(b, t // tb_rows),
        in_specs=[pl.BlockSpec((1, tb_rows, RWKV_IN), lambda i, j: (i, j, 0)),
                  pl.BlockSpec((1, RWKV_HEADS, RWKV_HD, RWKV_HD), lambda i, j: (i, 0, 0, 0)),
                  pl.BlockSpec((1, 1, RWKV_IN), lambda i, j: (i, 0, 0)),
                  full((RWKV_W, RWKV_W)), full((1, RWKV_IN)), full((1, RWKV_W)), full((64, RWKV_W)),
                  full((1, RWKV_W)), full((64, RWKV_W)), full((1, RWKV_W)), full((1, RWKV_W)), full((1, RWKV_W)),
                  full((1, RWKV_W)), full((1, RWKV_W))],
        out_specs=[pl.BlockSpec((1, tb_rows, RWKV_W), lambda i, j: (i, j, 0)),
                   pl.BlockSpec((1, RWKV_HEADS, RWKV_HD, RWKV_HD), lambda i, j: (i, 0, 0, 0))],
        out_shape=[jax.ShapeDtypeStruct((b, t, RWKV_W), F32),
                   jax.ShapeDtypeStruct((b, RWKV_HEADS, RWKV_HD, RWKV_HD), F32)],
        scratch_shapes=[pltpu.VMEM((RWKV_HEADS, RWKV_HD, RWKV_HD), F32), pltpu.VMEM((1, RWKV_IN), F32)]
        + [pltpu.VMEM((tb_rows, RWKV_W), F32) for _ in range(7)],
        compiler_params=_cparams(("parallel", "arbitrary")),
        name="rwkv",
    )(zr, s0, sh0, seg, mu, w0, w_up, a0, a_up, k_k, k_a, r_k, ln_w, ln_b)


def _slope(h):
    return 2.0 ** (-8.0 * (h + 1) / NSA_HEADS)


def _online_update(s, mask, v, m_ref, l_ref, acc_ref, idx):
    s = jnp.where(mask, s, NEG)
    m_old = m_ref[idx]
    m_new = jnp.maximum(m_old, jnp.max(s, axis=-1, keepdims=True))
    p = jnp.where(mask, jnp.exp(s - m_new), 0.0)
    alpha = jnp.exp(m_old - m_new)
    l_ref[idx] = alpha * l_ref[idx] + jnp.sum(p, axis=-1, keepdims=True)
    acc_ref[idx] = alpha * acc_ref[idx] + _mm(p, v)
    m_ref[idx] = m_new


def _online_init(m_ref, l_ref, acc_ref):
    m_ref[...] = jnp.full(m_ref.shape, NEG, F32)
    l_ref[...] = jnp.zeros(l_ref.shape, F32)
    acc_ref[...] = jnp.zeros(acc_ref.shape, F32)


def _online_result(l_ref, acc_ref, idx):
    l = l_ref[idx]
    return acc_ref[idx] / jnp.where(l > 0, l, 1.0)


def _nsa_prompt_kernel(q_ref, kv_ref, bg_ref, g_ref, o_ref, kc_s, vc_s, q_s, oc_s, m_s, l_s, acc_s, m2_s, l2_s, acc2_s,
                       *, t_len, tq):
    i = pl.program_id(1)
    nb = t_len // SEL_BLOCK
    tk = tq
    win_tiles = WINDOW // tk

    @pl.when(i == 0)
    def _():
        kcm = kv_ref[0, :, 0:128].reshape(nb, SEL_BLOCK, 128)
        vcm = kv_ref[0, :, 128:256].reshape(nb, SEL_BLOCK, 128)
        kc_s[0:nb] = jnp.sum(kcm[:, 0:CMP_BLOCK], axis=1) * (1.0 / CMP_BLOCK)
        kc_s[nb:2 * nb] = jnp.sum(kcm[:, CMP_BLOCK:SEL_BLOCK], axis=1) * (1.0 / CMP_BLOCK)
        vc_s[0:nb] = jnp.sum(vcm[:, 0:CMP_BLOCK], axis=1) * (1.0 / CMP_BLOCK)
        vc_s[nb:2 * nb] = jnp.sum(vcm[:, CMP_BLOCK:SEL_BLOCK], axis=1) * (1.0 / CMP_BLOCK)

    for h in range(NSA_HEADS):
        q_s[h] = (q_ref[0, :, 64 * h:64 * h + 64] * (NSA_HD ** -0.5)).astype(BF16)

    t_col = i * tq + _iota((tq, 1), 0)
    ccol = _iota((1, 2 * nb), 1)
    cend = jnp.where(ccol < nb, SEL_BLOCK * ccol + (CMP_BLOCK - 1), SEL_BLOCK * (ccol - nb) + (SEL_BLOCK - 1))
    dist_c = (t_col - cend).astype(F32)
    mask_c = cend <= t_col
    jcol = _iota((1, nb), 1)
    tblk = lax.shift_right_logical(t_col, 6)
    forced = (jcol == tblk) | (jcol == 0)
    valid = jcol <= tblk

    for kvh in range(NSA_KV_HEADS):
        ls = slice(64 * kvh, 64 * kvh + 64)
        kc = kc_s[:, ls]
        vc = vc_s[:, ls]
        imp_e = jnp.zeros((tq, nb), F32)
        imp_o = jnp.zeros((tq, nb), F32)
        for g in range(NSA_GROUP):
            h = NSA_GROUP * kvh + g
            s = _mm_nt(q_s[h], kc) - _slope(h) * dist_c
            s = jnp.where(mask_c, s, NEG)
            m = jnp.max(s, axis=-1, keepdims=True)
            p = jnp.where(mask_c, jnp.exp(s - m), 0.0)
            den = jnp.sum(p, axis=-1, keepdims=True)
            p = p / jnp.where(den > 0, den, 1.0)
            oc_s[h] = _mm(p, vc)
            imp_e = imp_e + p[:, 0:nb]
            imp_o = imp_o + p[:, nb:2 * nb]
        score = jnp.where(forced, FORCE_SCORE, jnp.where(valid, imp_e + imp_o, -FORCE_SCORE))
        rank = jnp.zeros((tq, nb), F32)
        for jp in range(nb):
            cj = score[:, jp:jp + 1]
            rank = rank + jnp.where((cj > score) | ((cj == score) & (jp < jcol)), 1.0, 0.0)
        sel = jnp.where(rank < N_SELECT, 1.0, 0.0).astype(BF16)

        _online_init(m_s, l_s, acc_s)
        _online_init(m2_s, l2_s, acc2_s)

        def sel_body(kt, carry):
            k0 = pl.multiple_of(kt * tk, tk)
            kpos = kt * tk + _iota((1, tk), 1)
            dist_i = t_col - kpos
            dist = dist_i.astype(F32)
            expand = jnp.where(_iota((nb, tk), 0) == lax.shift_right_logical(kpos, 6), 1.0, 0.0).astype(BF16)
            mask = (jnp.dot(sel, expand, preferred_element_type=F32) > 0.5) & (dist_i >= 0)
            kt_k = kv_ref[0, pl.ds(k0, tk), 256 + 64 * kvh:320 + 64 * kvh]
            kt_v = kv_ref[0, pl.ds(k0, tk), 384 + 64 * kvh:448 + 64 * kvh]
            for g in range(NSA_GROUP):
                h = NSA_GROUP * kvh + g
                s = _mm_nt(q_s[h], kt_k) - _slope(h) * dist
                _online_update(s, mask, kt_v, m_s, l_s, acc_s, g)
            return carry

        lax.fori_loop(0, i + 1, sel_body, 0)

        def win_body(kt, carry):
            k0 = pl.multiple_of(kt * tk, tk)
            kpos = kt * tk + _iota((1, tk), 1)
            dist_i = t_col - kpos
            dist = dist_i.astype(F32)
            mask = (dist_i >= 0) & (dist_i <= WINDOW)
            kt_k = kv_ref[0, pl.ds(k0, tk), 512 + 64 * kvh:576 + 64 * kvh]
            kt_v = kv_ref[0, pl.ds(k0, tk), 640 + 64 * kvh:704 + 64 * kvh]
            for g in range(NSA_GROUP):
                h = NSA_GROUP * kvh + g
                s = _mm_nt(q_s[h], kt_k) - _slope(h) * dist
                _online_update(s, mask, kt_v, m2_s, l2_s, acc2_s, g)
            return carry

        lax.fori_loop(jnp.maximum(i - win_tiles, 0), i + 1, win_body, 0)

        for g in range(NSA_GROUP):
            h = NSA_GROUP * kvh + g
            gates = _sigmoid(bg_ref[0, :, 3 * h:3 * h + 3])
            o = (gates[:, 0:1] * oc_s[h] + gates[:, 1:2] * _online_result(l_s, acc_s, g)
                 + gates[:, 2:3] * _online_result(l2_s, acc2_s, g))
            o_ref[0, :, 64 * h:64 * h + 64] = o * _silu(g_ref[0, :, 64 * h:64 * h + 64])


def _nsa_prompt(zn, *, tq):
    b, t, _ = zn.shape
    nb = t // SEL_BLOCK
    kern = functools.partial(_nsa_prompt_kernel, t_len=t, tq=tq)
    stat = lambda: [pltpu.VMEM((NSA_GROUP, tq, 1), F32), pltpu.VMEM((NSA_GROUP, tq, 1), F32),
                    pltpu.VMEM((NSA_GROUP, tq, NSA_HD), F32)]
    return pl.pallas_call(
        kern,
        grid=(b, t // tq),
        in_specs=[pl.BlockSpec((1, tq, 512), lambda i, j: (i, j, 2)),
                  pl.BlockSpec((1, t, 768), lambda i, j: (i, 0, 0)),
                  pl.BlockSpec((1, tq, 256), lambda i, j: (i, j, 3)),
                  pl.BlockSpec((1, tq, 512), lambda i, j: (i, j, 3))],
        out_specs=pl.BlockSpec((1, tq, 512), lambda i, j: (i, j, 0)),
        out_shape=jax.ShapeDtypeStruct((b, t, 512), F32),
        scratch_shapes=[pltpu.VMEM((2 * nb, 128), F32), pltpu.VMEM((2 * nb, 128), F32),
                        pltpu.VMEM((NSA_HEADS, tq, NSA_HD), BF16), pltpu.VMEM((NSA_HEADS, tq, NSA_HD), F32)]
        + stat() + stat(),
        compiler_params=_cparams(("parallel", "arbitrary")),
        name="nsa_prompt",
    )(zn, zn, zn, zn)


PAGES_PER_STEP = 8
N_PAGE_GROUPS = N_PAGES // PAGES_PER_STEP
N_PAST_SEL = PAST_LEN // SEL_BLOCK


def _nsa_sample_kernel(pt_ref, zn_ref, win_ref, *rest):
    pages = rest[:PAGES_PER_STEP]
    o_ref = rest[PAGES_PER_STEP]
    kce_s, kco_s, vce_s, vco_s, sel_s, oc_s, m_s, l_s, acc_s = rest[PAGES_PER_STEP + 1:]
    ph = pl.program_id(1)
    gi = pl.program_id(2)
    rq = SAMPLE_TPAD
    npast = N_PAST_SEL
    t_col = PAST_LEN + _iota((rq, 1), 0)
    keys_per_step = PAGES_PER_STEP * PAGE_SIZE

    def q_head(h):
        return zn_ref[0, :, 1024 + 64 * h:1088 + 64 * h] * (NSA_HD ** -0.5)

    @pl.when(ph == 0)
    def _pool():
        ek, ok_, ev, ov = [], [], [], []
        for pr in pages:
            xk = pr[0, :, 0:128].reshape(PAGE_SIZE // SEL_BLOCK, SEL_BLOCK, 128)
            xv = pr[0, :, 128:256].reshape(PAGE_SIZE // SEL_BLOCK, SEL_BLOCK, 128)
            ek.append(jnp.sum(xk[:, 0:CMP_BLOCK], axis=1) * (1.0 / CMP_BLOCK))
            ok_.append(jnp.sum(xk[:, CMP_BLOCK:SEL_BLOCK], axis=1) * (1.0 / CMP_BLOCK))
            ev.append(jnp.sum(xv[:, 0:CMP_BLOCK], axis=1) * (1.0 / CMP_BLOCK))
            ov.append(jnp.sum(xv[:, CMP_BLOCK:SEL_BLOCK], axis=1) * (1.0 / CMP_BLOCK))
        nrow = PAGES_PER_STEP * (PAGE_SIZE // SEL_BLOCK)
        dst = pl.ds(pl.multiple_of(gi * nrow, nrow), nrow)
        kce_s[dst, :] = jnp.concatenate(ek, axis=0)
        kco_s[dst, :] = jnp.concatenate(ok_, axis=0)
        vce_s[dst, :] = jnp.concatenate(ev, axis=0)
        vco_s[dst, :] = jnp.concatenate(ov, axis=0)

    @pl.when((ph == 0) & (gi == N_PAGE_GROUPS - 1))
    def _compressed():
        jrow = _iota((1, npast), 1)
        dist_e = (t_col - (SEL_BLOCK * jrow + (CMP_BLOCK - 1))).astype(F32)
        dist_o = (t_col - (SEL_BLOCK * jrow + (SEL_BLOCK - 1))).astype(F32)
        jp = _iota((npast, npast), 0)
        jj = _iota((npast, npast), 1)
        for kvh in range(NSA_KV_HEADS):
            ls = slice(64 * kvh, 64 * kvh + 64)
            kce, kco, vce, vco = kce_s[:, ls], kco_s[:, ls], vce_s[:, ls], vco_s[:, ls]
            imp_e = jnp.zeros((rq, npast), F32)
            imp_o = jnp.zeros((rq, npast), F32)
            for g in range(NSA_GROUP):
                h = NSA_GROUP * kvh + g
                qh = q_head(h)
                se = _mm_nt(qh, kce) - _slope(h) * dist_e
                so = _mm_nt(qh, kco) - _slope(h) * dist_o
                m = jnp.maximum(jnp.max(se, axis=-1, keepdims=True), jnp.max(so, axis=-1, keepdims=True))
                pe = jnp.exp(se - m)
                po = jnp.exp(so - m)
                den = jnp.sum(pe, axis=-1, keepdims=True) + jnp.sum(po, axis=-1, keepdims=True)
                pe = pe / den
                po = po / den
                oc_s[h] = _mm(pe, vce) + _mm(po, vco)
                imp_e = imp_e + pe
                imp_o = imp_o + po
            imp = imp_e + imp_o
            imp_t = jnp.concatenate([imp, jnp.zeros((128 - rq, npast), F32)], axis=0).T
            sel_rows = []
            for t in range(rq):
                col = imp_t[:, t:t + 1]
                row = imp[t:t + 1, :]
                beats = ((col > row) | ((col == row) & (jp < jj))) & (jp >= 1)
                rank = jnp.sum(jnp.where(beats, 1.0, 0.0), axis=0, keepdims=True)
                sel_rows.append(jnp.where((jrow == 0) | (rank < N_SELECT - 2), 1.0, 0.0))
            sel_s[kvh] = jnp.concatenate(sel_rows, axis=0).astype(BF16)

    @pl.when(ph == 1)
    def _selected():
        @pl.when(gi == 0)
        def _():
            _online_init(m_s, l_s, acc_s)

        k_all = jnp.concatenate([pr[0, :, 0:128] for pr in pages], axis=0)
        v_all = jnp.concatenate([pr[0, :, 128:256] for pr in pages], axis=0)
        kpos = gi * keys_per_step + _iota((1, keys_per_step), 1)
        dist = (t_col - kpos).astype(F32)
        expand = jnp.where(_iota((npast, keys_per_step), 0) == lax.shift_right_logical(kpos, 6), 1.0, 0.0).astype(BF16)
        for kvh in range(NSA_KV_HEADS):
            ls = slice(64 * kvh, 64 * kvh + 64)
            mask = jnp.dot(sel_s[kvh], expand, preferred_element_type=F32) > 0.5
            for g in range(NSA_GROUP):
                h = NSA_GROUP * kvh + g
                s = _mm_nt(q_head(h), k_all[:, ls]) - _slope(h) * dist
                _online_update(s, mask, v_all[:, ls], m_s, l_s, acc_s, h)

    @pl.when((ph == 1) & (gi == N_PAGE_GROUPS - 1))
    def _finish():
        new = zn_ref[0, :, 0:768]
        npos = PAST_LEN + _iota((1, rq), 1)
        dist_ni = t_col - npos
        dist_n = dist_ni.astype(F32)
        mask_n = dist_ni >= 0
        wpos = PAST_LEN - WINDOW + _iota((1, WINDOW), 1)
        dist_wi = t_col - wpos
        dist_w = dist_wi.astype(F32)
        mask_w = dist_wi <= WINDOW
        for kvh in range(NSA_KV_HEADS):
            ls = slice(64 * kvh, 64 * kvh + 64)
            wk = win_ref[0, :, 64 * kvh:64 * kvh + 64]
            wv = win_ref[0, :, 128 + 64 * kvh:192 + 64 * kvh]
            for g in range(NSA_GROUP):
                h = NSA_GROUP * kvh + g
                qh = q_head(h)
                sl = _slope(h)
                s = _mm_nt(qh, new[:, 256 + 64 * kvh:320 + 64 * kvh]) - sl * dist_n
                _online_update(s, mask_n, new[:, 384 + 64 * kvh:448 + 64 * kvh], m_s, l_s, acc_s, h)
                o_sel = _online_result(l_s, acc_s, h)
                s1 = jnp.where(mask_w, _mm_nt(qh, wk) - sl * dist_w, NEG)
                s2 = jnp.where(mask_n, _mm_nt(qh, new[:, 512 + 64 * kvh:576 + 64 * kvh]) - sl * dist_n, NEG)
                m = jnp.maximum(jnp.max(s1, axis=-1, keepdims=True), jnp.max(s2, axis=-1, keepdims=True))
                p1 = jnp.where(mask_w, jnp.exp(s1 - m), 0.0)
                p2 = jnp.where(mask_n, jnp.exp(s2 - m), 0.0)
                den = jnp.sum(p1, axis=-1, keepdims=True) + jnp.sum(p2, axis=-1, keepdims=True)
                o_win = (_mm(p1, wv) + _mm(p2, new[:, 640 + 64 * kvh:704 + 64 * kvh])) / jnp.where(den > 0, den, 1.0)
                gates = _sigmoid(zn_ref[0, :, 768 + 3 * h:771 + 3 * h])
                o = gates[:, 0:1] * oc_s[h] + gates[:, 1:2] * o_sel + gates[:, 2:3] * o_win
                o_ref[0, :, 64 * h:64 * h + 64] = o * _silu(zn_ref[0, :, 1536 + 64 * h:1600 + 64 * h])


def _nsa_sample(page_table, zn, win_state, cache, *, layer, n_pool):
    b = zn.shape[0]
    rq = SAMPLE_TPAD

    def page_map(kidx):
        return lambda i, ph, gi, pt: (layer * n_pool + pt[i, gi * PAGES_PER_STEP + kidx], 0, ph)

    grid_spec = pltpu.PrefetchScalarGridSpec(
        num_scalar_prefetch=1,
        grid=(b, 2, N_PAGE_GROUPS),
        in_specs=[pl.BlockSpec((1, rq, ZN_W), lambda i, ph, gi, pt: (i, 0, 0)),
                  pl.BlockSpec((1, WINDOW, 256), lambda i, ph, gi, pt: (layer * b + i, 0, 0))]
        + [pl.BlockSpec((1, PAGE_SIZE, 256), page_map(kidx)) for kidx in range(PAGES_PER_STEP)],
        out_specs=pl.BlockSpec((1, rq, 512), lambda i, ph, gi, pt: (i, 0, 0)),
        scratch_shapes=[pltpu.VMEM((N_PAST_SEL, 128), F32) for _ in range(4)]
        + [pltpu.VMEM((NSA_KV_HEADS, rq, N_PAST_SEL), BF16), pltpu.VMEM((NSA_HEADS, rq, NSA_HD), F32),
           pltpu.VMEM((NSA_HEADS, rq, 1), F32), pltpu.VMEM((NSA_HEADS, rq, 1), F32),
           pltpu.VMEM((NSA_HEADS, rq, NSA_HD), F32)],
    )
    return pl.pallas_call(
        _nsa_sample_kernel,
        grid_spec=grid_spec,
        out_shape=jax.ShapeDtypeStruct((b, rq, 512), F32),
        compiler_params=_cparams(("parallel", "arbitrary", "arbitrary")),
        name="nsa_sample",
    )(page_table, zn, win_state, *([cache] * PAGES_PER_STEP))


def _mm_split3(a, b01):
    hi = a.astype(BF16)
    r1 = a - hi.astype(F32)
    mid = r1.astype(BF16)
    lo = (r1 - mid.astype(F32)).astype(BF16)
    return (jnp.dot(hi, b01, preferred_element_type=F32) + jnp.dot(mid, b01, preferred_element_type=F32)
            + jnp.dot(lo, b01, preferred_element_type=F32))


def _flash_init_t(m_ref, l_ref, acc_ref):
    m_ref[...] = jnp.full(m_ref.shape, NEG, F32)
    l_ref[...] = jnp.zeros(l_ref.shape, F32)
    acc_ref[...] = jnp.zeros(acc_ref.shape, F32)


def _flash_step_t(s_t, mask, v, m_ref, l_ref, acc_ref, guard):
    s_t = jnp.where(mask, s_t, NEG)
    m_old = m_ref[...]
    m_new = jnp.maximum(m_old, jnp.max(s_t, axis=0, keepdims=True))
    p = jnp.exp(s_t - m_new)
    if guard:
        p = jnp.where(mask, p, 0.0)
    alpha = jnp.exp(m_old - m_new)
    l_ref[...] = alpha * l_ref[...] + jnp.sum(p, axis=0, keepdims=True)
    acc_ref[...] = alpha * acc_ref[...] + _mm_tn(v, p)
    m_ref[...] = m_new


def _flash_result_t(l_ref, acc_ref):
    l = l_ref[...]
    return acc_ref[...] / jnp.where(l > 0, l, 1.0)


def _nsa_prompt_t_kernel(q_ref, kv_ref, bg_ref, g_ref, o_ref, kc_s, vc_s, ks_s, vs_s, kw_s, vw_s, qa_s, sel_s, oc_s,
                         m_s, l_s, acc_s, m2_s, l2_s, acc2_s, *, t_len, tq):
    i = pl.program_id(1)
    nb = t_len // SEL_BLOCK
    tk = tq
    ncol = NSA_GROUP * tq
    win_tiles = WINDOW // tk

    def aug_lanes(lane, pos):
        return jnp.where(lane == 64, lax.shift_right_logical(pos, 7).astype(F32),
                         jnp.where(lane == 65, (pos & 127).astype(F32), 0.0))

    def head_lanes(x, kvh):
        return x if kvh == 0 else pltpu.roll(x, 64, axis=1)

    @pl.when(i == 0)
    def _():
        lane = _iota((t_len, 128), 1)
        aug = aug_lanes(lane, _iota((t_len, 128), 0))
        for kcol, vcol, k_dst, v_dst in ((256, 384, ks_s, vs_s), (512, 640, kw_s, vw_s)):
            kf = kv_ref[0, :, kcol:kcol + 128]
            vf = kv_ref[0, :, vcol:vcol + 128]
            for kvh in range(NSA_KV_HEADS):
                k_dst[kvh] = jnp.where(lane < 64, head_lanes(kf, kvh), aug).astype(BF16)
                v_dst[kvh] = jnp.where(lane < 64, head_lanes(vf, kvh), 0.0).astype(BF16)
        kcm = kv_ref[0, :, 0:128].reshape(nb, SEL_BLOCK, 128)
        vcm = kv_ref[0, :, 128:256].reshape(nb, SEL_BLOCK, 128)
        inv = 1.0 / CMP_BLOCK
        kc = jnp.concatenate([jnp.sum(kcm[:, 0:CMP_BLOCK], axis=1) * inv,
                              jnp.sum(kcm[:, CMP_BLOCK:SEL_BLOCK], axis=1) * inv], axis=0)
        vc = jnp.concatenate([jnp.sum(vcm[:, 0:CMP_BLOCK], axis=1) * inv,
                              jnp.sum(vcm[:, CMP_BLOCK:SEL_BLOCK], axis=1) * inv], axis=0)
        lane_c = _iota((2 * nb, 128), 1)
        r_c = _iota((2 * nb, 128), 0)
        cend = jnp.where(r_c < nb, SEL_BLOCK * r_c + (CMP_BLOCK - 1), SEL_BLOCK * (r_c - nb) + (SEL_BLOCK - 1))
        aug_c = aug_lanes(lane_c, cend)
        for kvh in range(NSA_KV_HEADS):
            kc_s[kvh] = jnp.where(lane_c < 64, head_lanes(kc, kvh), aug_c).astype(BF16)
            vc_s[kvh] = jnp.where(lane_c < 64, head_lanes(vc, kvh), 0.0).astype(BF16)

    lane_q = _iota((tq, 128), 1)
    for h in range(NSA_HEADS):
        qb = q_ref[0, :, 128 * (h // 2):128 * (h // 2) + 128]
        if h % 2:
            qb = pltpu.roll(qb, 64, axis=1)
        sl = _slope(h)
        qa = jnp.where(lane_q < 64, qb * (NSA_HD ** -0.5),
                       jnp.where(lane_q == 64, 128.0 * sl, jnp.where(lane_q == 65, sl, 0.0)))
        qa_s[h // NSA_GROUP, (h % NSA_GROUP) * tq:(h % NSA_GROUP + 1) * tq, :] = qa.astype(BF16)

    t_row = i * tq + (_iota((1, ncol), 1) & (tq - 1))
    t_row1 = i * tq + _iota((1, tq), 1)
    r_c1 = _iota((2 * nb, 1), 0)
    cend_col = jnp.where(r_c1 < nb, SEL_BLOCK * r_c1 + (CMP_BLOCK - 1), SEL_BLOCK * (r_c1 - nb) + (SEL_BLOCK - 1))
    mask_c = cend_col <= t_row
    jrow = _iota((nb, 1), 0)
    tblk = lax.shift_right_logical(t_row1, 6)
    forced = (jrow == tblk) | (jrow == 0)
    valid = jrow <= tblk
    bg_t = bg_ref[0].T

    for kvh in range(NSA_KV_HEADS):
        qa = qa_s[kvh]
        s_c = jnp.where(mask_c, _mm_nt(kc_s[kvh], qa), NEG)
        m = jnp.max(s_c, axis=0, keepdims=True)
        p = jnp.where(mask_c, jnp.exp(s_c - m), 0.0)
        den = jnp.sum(p, axis=0, keepdims=True)
        p = p / jnp.where(den > 0, den, 1.0)
        oc_s[...] = _mm_tn(vc_s[kvh], p)
        imp_e = p[0:nb, 0:tq]
        imp_o = p[nb:2 * nb, 0:tq]
        for g in range(1, NSA_GROUP):
            imp_e = imp_e + p[0:nb, g * tq:(g + 1) * tq]
            imp_o = imp_o + p[nb:2 * nb, g * tq:(g + 1) * tq]
        score = jnp.where(forced, FORCE_SCORE, jnp.where(valid, imp_e + imp_o, -FORCE_SCORE))
        rank = jnp.zeros((nb, tq), F32)
        for jp in range(nb):
            rj = score[jp:jp + 1, :]
            rank = rank + jnp.where((rj > score) | ((rj == score) & (jp < jrow)), 1.0, 0.0)
        sel = jnp.where(rank < N_SELECT, 1.0, 0.0).astype(BF16)
        sel_s[...] = jnp.concatenate([sel] * NSA_GROUP, axis=1)

        _flash_init_t(m_s, l_s, acc_s)
        _flash_init_t(m2_s, l2_s, acc2_s)

        def sel_body(kt, carry):
            k0 = pl.multiple_of(kt * tk, tk)
            pos_col = kt * tk + _iota((tk, 1), 0)
            onehot = jnp.where(_iota((tk, nb), 1) == lax.shift_right_logical(pos_col, 6), 1.0, 0.0).astype(BF16)
            mask = (jnp.dot(onehot, sel_s[...], preferred_element_type=F32) > 0.5) & (pos_col <= t_row)
            s_t = _mm_nt(ks_s[kvh, pl.ds(k0, tk), :], qa_s[kvh])
            _flash_step_t(s_t, mask, vs_s[kvh, pl.ds(k0, tk), :], m_s, l_s, acc_s, False)
            return carry

        lax.fori_loop(0, i + 1, sel_body, 0)

        def win_body(kt, carry):
            k0 = pl.multiple_of(kt * tk, tk)
            pos_col = kt * tk + _iota((tk, 1), 0)
            dist = t_row - pos_col
            mask = (dist >= 0) & (dist <= WINDOW)
            s_t = _mm_nt(kw_s[kvh, pl.ds(k0, tk), :], qa_s[kvh])
            _flash_step_t(s_t, mask, vw_s[kvh, pl.ds(k0, tk), :], m2_s, l2_s, acc2_s, False)
            return carry

        lax.fori_loop(jnp.maximum(i - win_tiles, 0), i + 1, win_body, 0)

        o_sel = _flash_result_t(l_s, acc_s)
        o_win = _flash_result_t(l2_s, acc2_s)
        o_cmp = oc_s[...]
        for g in range(NSA_GROUP):
            h = NSA_GROUP * kvh + g
            cs = slice(g * tq, (g + 1) * tq)
            gates = _sigmoid(bg_t[3 * h:3 * h + 3, :])
            comb = gates[0:1] * o_cmp[:, cs] + gates[1:2] * o_sel[:, cs] + gates[2:3] * o_win[:, cs]
            o_ref[0, :, 64 * h:64 * h + 64] = comb.T[:, 0:64] * _silu(g_ref[0, :, 64 * h:64 * h + 64])


def _nsa_prompt_t(zn, *, tq):
    b, t, _ = zn.shape
    nb = t // SEL_BLOCK
    ncol = NSA_GROUP * tq
    kern = functools.partial(_nsa_prompt_t_kernel, t_len=t, tq=tq)
    stat = lambda: [pltpu.VMEM((1, ncol), F32), pltpu.VMEM((1, ncol), F32), pltpu.VMEM((128, ncol), F32)]
    kvbuf = lambda n: pltpu.VMEM((NSA_KV_HEADS, n, 128), BF16)
    return pl.pallas_call(
        kern,
        grid=(b, t // tq),
        in_specs=[pl.BlockSpec((1, tq, 512), lambda i, j: (i, j, 2)),
                  pl.BlockSpec((1, t, 768), lambda i, j: (i, 0, 0)),
                  pl.BlockSpec((1, tq, 256), lambda i, j: (i, j, 3)),
                  pl.BlockSpec((1, tq, 512), lambda i, j: (i, j, 3))],
        out_specs=pl.BlockSpec((1, tq, 512), lambda i, j: (i, j, 0)),
        out_shape=jax.ShapeDtypeStruct((b, t, 512), F32),
        scratch_shapes=[kvbuf(2 * nb), kvbuf(2 * nb), kvbuf(t), kvbuf(t), kvbuf(t), kvbuf(t), kvbuf(ncol),
                        pltpu.VMEM((nb, ncol), BF16), pltpu.VMEM((128, ncol), F32)] + stat() + stat(),
        compiler_params=_cparams(("parallel", "arbitrary")),
        name="nsa_prompt",
    )(zn, zn, zn, zn)


SAMPLE_COLS = NSA_HEADS * SAMPLE_TPAD


def _nsa_sample_t_kernel(pt_ref, zn_ref, win_ref, e0_ref, *rest):
    pages = rest[:PAGES_PER_STEP]
    o_ref = rest[PAGES_PER_STEP]
    kce_s, kco_s, vce_s, vco_s, qt_s, sel_s, oc_s, m_s, l_s, acc_s = rest[PAGES_PER_STEP + 1:]
    ph = pl.program_id(1)
    gi = pl.program_id(2)
    rq = SAMPLE_TPAD
    npast = N_PAST_SEL
    keys_per_step = PAGES_PER_STEP * PAGE_SIZE
    blocks_per_step = keys_per_step // SEL_BLOCK
    col = _iota((1, SAMPLE_COLS), 1)
    t_row = PAST_LEN + (col & (rq - 1))
    hcol = lax.shift_right_logical(col, 4)
    slope_row = jnp.zeros((1, SAMPLE_COLS), F32)
    for h in range(NSA_HEADS):
        slope_row = jnp.where(hcol == h, _slope(h), slope_row)

    def scores(keys, pos_col):
        return _mm_nt(keys, qt_s[...]) - slope_row * (t_row - pos_col).astype(F32)

    @pl.when((ph == 0) & (gi == 0))
    def _queries():
        zero = jnp.zeros((rq, NSA_HD), F32)
        rows = []
        for h in range(NSA_HEADS):
            qh = zn_ref[0, :, 1024 + 64 * h:1088 + 64 * h] * (NSA_HD ** -0.5)
            rows.append(jnp.concatenate([qh, zero] if h < NSA_GROUP else [zero, qh], axis=1))
        qt_s[...] = jnp.concatenate(rows, axis=0).astype(BF16)

    @pl.when(ph == 0)
    def _pool():
        ek, ok_, ev, ov = [], [], [], []
        inv = 1.0 / CMP_BLOCK
        for pr in pages:
            xk = pr[0, :, 0:128].reshape(PAGE_SIZE // SEL_BLOCK, SEL_BLOCK, 128)
            xv = pr[0, :, 128:256].reshape(PAGE_SIZE // SEL_BLOCK, SEL_BLOCK, 128)
            ek.append(jnp.sum(xk[:, 0:CMP_BLOCK], axis=1) * inv)
            ok_.append(jnp.sum(xk[:, CMP_BLOCK:SEL_BLOCK], axis=1) * inv)
            ev.append(jnp.sum(xv[:, 0:CMP_BLOCK], axis=1) * inv)
            ov.append(jnp.sum(xv[:, CMP_BLOCK:SEL_BLOCK], axis=1) * inv)
        dst = pl.ds(pl.multiple_of(gi * blocks_per_step, blocks_per_step), blocks_per_step)
        kce_s[dst, :] = jnp.concatenate(ek, axis=0)
        kco_s[dst, :] = jnp.concatenate(ok_, axis=0)
        vce_s[dst, :] = jnp.concatenate(ev, axis=0)
        vco_s[dst, :] = jnp.concatenate(ov, axis=0)

    @pl.when((ph == 0) & (gi == N_PAGE_GROUPS - 1))
    def _compressed():
        jcol = _iota((npast, 1), 0)
        se = scores(kce_s[...], SEL_BLOCK * jcol + (CMP_BLOCK - 1))
        so = scores(kco_s[...], SEL_BLOCK * jcol + (SEL_BLOCK - 1))
        m = jnp.maximum(jnp.max(se, axis=0, keepdims=True), jnp.max(so, axis=0, keepdims=True))
        pe = jnp.exp(se - m)
        po = jnp.exp(so - m)
        den = jnp.sum(pe, axis=0, keepdims=True) + jnp.sum(po, axis=0, keepdims=True)
        pe = pe / den
        po = po / den
        oc_s[...] = _mm_tn(vce_s[...], pe) + _mm_tn(vco_s[...], po)
        ca = _iota((SAMPLE_COLS, SAMPLE_COLS), 0)
        cb = _iota((SAMPLE_COLS, SAMPLE_COLS), 1)
        same = ((lax.shift_right_logical(ca, 6) == lax.shift_right_logical(cb, 6))
                & ((ca & (rq - 1)) == (cb & (rq - 1))))
        gsum = jnp.where(same, 1.0, 0.0).astype(BF16)
        imp_t = _mm_split3(pe, gsum) + _mm_split3(po, gsum)
        imp = imp_t.T
        jp = _iota((npast, npast), 0)
        jj = _iota((npast, npast), 1)
        jrow = _iota((1, npast), 1)
        groups = []
        for kvh in range(NSA_KV_HEADS):
            sel_rows = []
            for t in range(rq):
                c = kvh * NSA_GROUP * rq + t
                colv = imp_t[:, c:c + 1]
                rowv = imp[c:c + 1, :]
                beats = ((colv > rowv) | ((colv == rowv) & (jp < jj))) & (jp >= 1)
                rank = jnp.sum(jnp.where(beats, 1.0, 0.0), axis=0, keepdims=True)
                sel_rows.append(jnp.where((jrow == 0) | (rank < N_SELECT - 2), 1.0, 0.0))
            groups += [jnp.concatenate(sel_rows, axis=0)] * NSA_GROUP
        sel_s[...] = jnp.concatenate(groups, axis=0).T.astype(BF16)

    @pl.when(ph == 1)
    def _selected():
        @pl.when(gi == 0)
        def _():
            _flash_init_t(m_s, l_s, acc_s)

        k_all = jnp.concatenate([pr[0, :, 0:128] for pr in pages], axis=0)
        v_all = jnp.concatenate([pr[0, :, 128:256] for pr in pages], axis=0)
        pos_col = gi * keys_per_step + _iota((keys_per_step, 1), 0)
        sel_rows = sel_s[pl.ds(pl.multiple_of(gi * blocks_per_step, blocks_per_step), blocks_per_step), :]
        mask = jnp.dot(e0_ref[...], sel_rows, preferred_element_type=F32) > 0.5
        _flash_step_t(scores(k_all, pos_col), mask, v_all, m_s, l_s, acc_s, False)

    @pl.when((ph == 1) & (gi == N_PAGE_GROUPS - 1))
    def _finish():
        new = zn_ref[0, :, 0:768]
        npos_col = PAST_LEN + _iota((rq, 1), 0)
        mask_n = npos_col <= t_row
        _flash_step_t(scores(new[:, 256:384], npos_col), mask_n, new[:, 384:512], m_s, l_s, acc_s, False)
        o_sel = _flash_result_t(l_s, acc_s).T
        wpos_col = PAST_LEN - WINDOW + _iota((WINDOW, 1), 0)
        mask_w = (t_row - wpos_col) <= WINDOW
        s1 = jnp.where(mask_w, scores(win_ref[0, :, 0:128], wpos_col), NEG)
        s2 = jnp.where(mask_n, scores(new[:, 512:640], npos_col), NEG)
        m = jnp.maximum(jnp.max(s1, axis=0, keepdims=True), jnp.max(s2, axis=0, keepdims=True))
        p1 = jnp.where(mask_w, jnp.exp(s1 - m), 0.0)
        p2 = jnp.where(mask_n, jnp.exp(s2 - m), 0.0)
        den = jnp.sum(p1, axis=0, keepdims=True) + jnp.sum(p2, axis=0, keepdims=True)
        o_win = ((_mm_tn(win_ref[0, :, 128:256], p1) + _mm_tn(new[:, 640:768], p2)) / jnp.where(den > 0, den, 1.0)).T
        o_cmp = oc_s[...].T
        for h in range(NSA_HEADS):
            rs = slice(h * rq, (h + 1) * rq)
            ls = slice(64 * (h // NSA_GROUP), 64 * (h // NSA_GROUP) + 64)
            gates = _sigmoid(zn_ref[0, :, 768 + 3 * h:771 + 3 * h])
            o = gates[:, 0:1] * o_cmp[rs, ls] + gates[:, 1:2] * o_sel[rs, ls] + gates[:, 2:3] * o_win[rs, ls]
            o_ref[0, :, 64 * h:64 * h + 64] = o * _silu(zn_ref[0, :, 1536 + 64 * h:1600 + 64 * h])


def _nsa_sample_t(page_table, zn, win_state, cache, *, layer, n_pool):
    b = zn.shape[0]
    rq = SAMPLE_TPAD
    keys_per_step = PAGES_PER_STEP * PAGE_SIZE
    blocks_per_step = keys_per_step // SEL_BLOCK
    e0 = (jnp.arange(keys_per_step)[:, None] // SEL_BLOCK == jnp.arange(blocks_per_step)[None, :]).astype(BF16)

    def page_map(kidx):
        return lambda i, ph, gi, pt: (layer * n_pool + pt[i, gi * PAGES_PER_STEP + kidx], 0, ph)

    sq = lambda dt: pltpu.VMEM((SAMPLE_COLS, SAMPLE_COLS), dt)
    grid_spec = pltpu.PrefetchScalarGridSpec(
        num_scalar_prefetch=1,
        grid=(b, 2, N_PAGE_GROUPS),
        in_specs=[pl.BlockSpec((1, rq, ZN_W), lambda i, ph, gi, pt: (i, 0, 0)),
                  pl.BlockSpec((1, WINDOW, 256), lambda i, ph, gi, pt: (layer * b + i, 0, 0)),
                  pl.BlockSpec((keys_per_step, blocks_per_step), lambda i, ph, gi, pt: (0, 0))]
        + [pl.BlockSpec((1, PAGE_SIZE, 256), page_map(kidx)) for kidx in range(PAGES_PER_STEP)],
        out_specs=pl.BlockSpec((1, rq, 512), lambda i, ph, gi, pt: (i, 0, 0)),
        scratch_shapes=[pltpu.VMEM((N_PAST_SEL, 128), F32) for _ in range(4)]
        + [sq(BF16), pltpu.VMEM((N_PAST_SEL, SAMPLE_COLS), BF16), sq(F32),
           pltpu.VMEM((1, SAMPLE_COLS), F32), pltpu.VMEM((1, SAMPLE_COLS), F32), sq(F32)],
    )
    return pl.pallas_call(
        _nsa_sample_t_kernel,
        grid_spec=grid_spec,
        out_shape=jax.ShapeDtypeStruct((b, rq, 512), F32),
        compiler_params=_cparams(("parallel", "arbitrary", "arbitrary")),
        name="nsa_sample",
    )(page_table, zn, win_state, e0, *([cache] * PAGES_PER_STEP))


def _merge_kernel(x_ref, bg_ref, br_ref, bn_ref, mg_ref, wbr_ref, wout_ref, lng_ref, lnb_ref, o_ref):
    acc = _sigmoid(mg_ref[:, 0:1024]) * _mm(bg_ref[...], wbr_ref[0])
    acc = acc + _sigmoid(mg_ref[:, 1024:2048]) * _mm(br_ref[...], wbr_ref[1])
    acc = acc + _sigmoid(mg_ref[:, 2048:3072]) * _mm(bn_ref[...], wbr_ref[2])
    xf = DN_ALPHA * x_ref[...] + _mm(acc, wout_ref[...])
    mu = jnp.mean(xf, axis=-1, keepdims=True)
    d = xf - mu
    var = jnp.mean(d * d, axis=-1, keepdims=True)
    o_ref[...] = d * lax.rsqrt(var + LN_EPS) * lng_ref[...] + lnb_ref[...]


def _merge(x, o_gla, o_rwkv, o_nsa, zm, w_br, w_out, ln_g, ln_b, tm):
    m = x.shape[0]
    row = lambda n: pl.BlockSpec((tm, n), lambda i: (i, 0))
    return pl.pallas_call(
        _merge_kernel,
        grid=(m // tm,),
        in_specs=[row(D_MODEL), row(512), row(512), row(512), row(ZM_W),
                  pl.BlockSpec((3, 512, D_MODEL), lambda i: (0, 0, 0)),
                  pl.BlockSpec((D_MODEL, D_MODEL), lambda i: (0, 0)),
                  pl.BlockSpec((1, D_MODEL), lambda i: (0, 0)),
                  pl.BlockSpec((1, D_MODEL), lambda i: (0, 0))],
        out_specs=row(D_MODEL),
        out_shape=jax.ShapeDtypeStruct((m, D_MODEL), F32),
        compiler_params=_cparams(("parallel",)),
        name="merge",
    )(x, o_gla, o_rwkv, o_nsa, zm, w_br, w_out, ln_g, ln_b)


def _pack_weights(w_in, b_in):
    def pack(a):
        z = lambda n: jnp.zeros(a.shape[:-1] + (n,), a.dtype)
        gla = jnp.concatenate([a[..., 0:1024], a[..., 1040:1552], a[..., 1024:1040], z(ZG_W - 1552)], axis=-1)
        rwkv = a[..., 1552:3728]
        nsa = jnp.concatenate([a[..., 4240:5008], a[..., 5008:5032], z(1024 - 792), a[..., 3728:4240],
                               a[..., 5032:5544]], axis=-1)
        mg = a[..., 5544:8616]
        return gla, rwkv, nsa, mg
    ws = [w.astype(BF16) for w in pack(w_in)]
    bs = [b[:, None, :] for b in pack(b_in)]
    return ws, bs


def kernel(x_prompt, x_sample, cache_nsa_kv, state_nsa_win, state_gla, state_rwkv, state_rwkv_shift, page_table,
           w_in, b_in, gla_a_up, gla_a_bias, gla_norm, rwkv_mu, rwkv_w0, rwkv_w_up, rwkv_a0, rwkv_a_up, rwkv_k_k,
           rwkv_k_a, rwkv_r_k, rwkv_ln_w, rwkv_ln_b, w_br, w_out, ln_g, ln_b):
    bp, tp, _ = x_prompt.shape
    bs, ts, _ = x_sample.shape
    n_pool = cache_nsa_kv.shape[1]
    ws, bws = _pack_weights(w_in, b_in)
    w_br_b = w_br.astype(BF16)
    w_out_b = w_out.astype(BF16)
    seg = (jnp.arange(RWKV_W)[:, None] // RWKV_HD == jnp.arange(RWKV_W)[None, :] // RWKV_HD).astype(BF16)
    cache = cache_nsa_kv.reshape(DEPTH * n_pool, PAGE_SIZE, 512)
    win_state = state_nsa_win.reshape(DEPTH * bs, WINDOW, 256)
    row2 = lambda a: a.reshape(DEPTH, 1, -1)
    gla_a_bias2, gla_norm2 = row2(gla_a_bias), row2(gla_norm)
    r_par = [row2(rwkv_mu), row2(rwkv_w0), rwkv_w_up, row2(rwkv_a0), rwkv_a_up, row2(rwkv_k_k), row2(rwkv_k_a),
             row2(rwkv_r_k), row2(rwkv_ln_w), row2(rwkv_ln_b)]
    ln_g2, ln_b2 = row2(ln_g), row2(ln_b)

    xp = x_prompt.reshape(bp * tp, D_MODEL)
    xs = jnp.pad(x_sample, ((0, 0), (0, SAMPLE_TPAD - ts), (0, 0))).reshape(bs * SAMPLE_TPAD, D_MODEL)
    zeros_gla = jnp.zeros((bp, GLA_HEADS, GLA_DK, GLA_DV), F32)
    zeros_rwkv = jnp.zeros((bp, RWKV_HEADS, RWKV_HD, RWKV_HD), F32)
    zeros_shift = jnp.zeros((bp, 1, RWKV_IN), F32)

    outs = {k: [] for k in ("kv_p", "kv_s", "win_p", "win_s", "gla_p", "gla_s", "rwkv_p", "rwkv_s", "sh_p", "sh_s")}
    for l in range(DEPTH):
        rp = [p[l] for p in r_par]
        zg, zr, zn, zm = (_proj(xp, ws[i][l], bws[i][l], 512) for i in range(4))
        zn3 = zn.reshape(bp, tp, ZN_W)
        o_gla, gla_st = _gla(zg.reshape(bp, tp, ZG_W), zeros_gla, gla_a_up[l], gla_a_bias2[l], gla_norm2[l],
                             tb_rows=512, chunk=GLA_CHUNK, t_valid=None)
        zr3 = zr.reshape(bp, tp, RWKV_IN)
        o_rwkv, rwkv_st = _rwkv(zr3, zeros_rwkv, zeros_shift, seg, rp, tb_rows=512, chunk=RWKV_CHUNK, t_valid=None)
        o_nsa = _nsa_prompt_t(zn3, tq=256)
        xp = _merge(xp, o_gla.reshape(bp * tp, 512), o_rwkv.reshape(bp * tp, 512), o_nsa.reshape(bp * tp, 512), zm,
                    w_br_b[l], w_out_b[l], ln_g2[l], ln_b2[l], 512)
        outs["kv_p"].append(zn3[:, :, 0:512].reshape(bp, tp, 4, NSA_KV_HEADS, NSA_HD))
        outs["win_p"].append(zn3[:, tp - WINDOW:, 512:768].reshape(bp, WINDOW, 2, NSA_KV_HEADS, NSA_HD))
        outs["gla_p"].append(gla_st)
        outs["rwkv_p"].append(rwkv_st)
        outs["sh_p"].append(zr3[:, tp - 1, :])
        rows_s = bs * SAMPLE_TPAD
        zg, zr, zn, zm = (_proj(xs, ws[i][l], bws[i][l], rows_s) for i in range(4))
        zn3 = zn.reshape(bs, SAMPLE_TPAD, ZN_W)
        o_gla, gla_st = _gla(zg.reshape(bs, SAMPLE_TPAD, ZG_W), state_gla[l], gla_a_up[l], gla_a_bias2[l],
                             gla_norm2[l], tb_rows=SAMPLE_TPAD, chunk=SAMPLE_TPAD, t_valid=ts)
        zr3 = zr.reshape(bs, SAMPLE_TPAD, RWKV_IN)
        o_rwkv, rwkv_st = _rwkv(zr3, state_rwkv[l], state_rwkv_shift[l][:, None, :], seg, rp,
                                tb_rows=SAMPLE_TPAD, chunk=SAMPLE_TPAD, t_valid=ts)
        o_nsa = _nsa_sample_t(page_table, zn3, win_state, cache, layer=l, n_pool=n_pool)
        xs = _merge(xs, o_gla.reshape(rows_s, 512), o_rwkv.reshape(rows_s, 512), o_nsa.reshape(rows_s, 512), zm,
                    w_br_b[l], w_out_b[l], ln_g2[l], ln_b2[l], rows_s)
        outs["kv_s"].append(zn3[:, 0:ts, 0:512].reshape(bs, ts, 4, NSA_KV_HEADS, NSA_HD))
        new_win = zn3[:, 0:ts, 512:768].reshape(bs, ts, 2, NSA_KV_HEADS, NSA_HD)
        outs["win_s"].append(jnp.concatenate([state_nsa_win[l][:, ts:], new_win], axis=1))
        outs["gla_s"].append(gla_st)
        outs["rwkv_s"].append(rwkv_st)
        outs["sh_s"].append(zr3[:, ts - 1, :])

    st = lambda k: jnp.stack(outs[k])
    y_prompt = xp.reshape(bp, tp, D_MODEL)
    y_sample = xs.reshape(bs, SAMPLE_TPAD, D_MODEL)[:, 0:ts]
    return (y_prompt, y_sample, st("kv_p"), st("kv_s"), st("win_p"), st("win_s"), st("gla_p"), st("gla_s"),
            st("rwkv_p"), st("rwkv_s"), st("sh_p"), st("sh_s"))
```

```python
import functools

import jax
import jax.numpy as jnp
from jax import lax
from jax.experimental import pallas as pl
from jax.experimental.pallas import tpu as pltpu

F32 = jnp.float32
BF16 = jnp.bfloat16

D_MODEL = 1024
DEPTH = 2
PAST_LEN = 16384
PAGE_SIZE = 128
N_PAGES = PAST_LEN // PAGE_SIZE

GLA_HEADS, GLA_DK, GLA_DV = 4, 64, 128
GLA_K, GLA_V, GLA_LORA = 256, 512, 16
GLA_GATE_NORM = 16.0
GLA_CHUNK = 64
GLA_SUB = 16

RWKV_HEADS, RWKV_HD, RWKV_W = 8, 64, 512
RWKV_IN = 2176
RWKV_LN_EPS = 64e-5
RWKV_CHUNK = 64

NSA_HEADS, NSA_KV_HEADS, NSA_GROUP, NSA_HD = 8, 2, 4, 64
CMP_BLOCK, SEL_BLOCK, N_SELECT, WINDOW = 32, 64, 16, 512
FORCE_SCORE = 1e9
NEG = -1e30

DN_ALPHA = (2 * DEPTH) ** 0.25
LN_EPS = 1e-5
NORM_EPS = 1e-6

ZG_W = 1664
ZN_W = 2048
ZM_W = 3072
SAMPLE_TPAD = 16
VMEM_LIMIT = 56 * 1024 * 1024


def _mm(a, b):
    return jnp.dot(a.astype(BF16), b.astype(BF16), preferred_element_type=F32)


def _mm_nt(a, b):
    return lax.dot_general(a.astype(BF16), b.astype(BF16), (((1,), (1,)), ((), ())), preferred_element_type=F32)


def _mm_tn(a, b):
    return lax.dot_general(a.astype(BF16), b.astype(BF16), (((0,), (0,)), ((), ())), preferred_element_type=F32)


def _mm_exact(a, b):
    return jnp.dot(a, b, preferred_element_type=F32, precision=lax.Precision.HIGHEST)


def _mm_split(a, b01):
    hi = a.astype(BF16)
    lo = (a - hi.astype(F32)).astype(BF16)
    return (jnp.dot(hi, b01, preferred_element_type=F32) + jnp.dot(lo, b01, preferred_element_type=F32))


def _softplus(x):
    return jnp.maximum(x, 0.0) + jnp.log1p(jnp.exp(-jnp.abs(x)))


def _sigmoid(x):
    return 1.0 / (1.0 + jnp.exp(-x))


def _silu(x):
    return x * _sigmoid(x)


def _iota(shape, dim):
    return lax.broadcasted_iota(jnp.int32, shape, dim)


def _cparams(sem):
    return pltpu.CompilerParams(dimension_semantics=sem, vmem_limit_bytes=VMEM_LIMIT)


def _proj_kernel(x_ref, w_ref, b_ref, o_ref):
    o_ref[...] = _mm(x_ref[...], w_ref[...]) + b_ref[...]


def _proj(x, w, b, tm):
    m, k = x.shape
    n = w.shape[1]
    return pl.pallas_call(
        _proj_kernel,
        grid=(m // tm,),
        in_specs=[pl.BlockSpec((tm, k), lambda i: (i, 0)),
                  pl.BlockSpec((k, n), lambda i: (0, 0)),
                  pl.BlockSpec((1, n), lambda i: (0, 0))],
        out_specs=pl.BlockSpec((tm, n), lambda i: (i, 0)),
        out_shape=jax.ShapeDtypeStruct((m, n), F32),
        compiler_params=_cparams(("parallel",)),
        name="proj",
    )(x, w, b)


def _gla_kernel(zg_ref, s0_ref, aup_ref, abias_ref, norm_ref, o_ref, sout_ref, st_scr, *, tb_rows, chunk, t_valid):
    tb = pl.program_id(1)
    c_rows = chunk
    sub = min(GLA_SUB, c_rows)
    nsub = c_rows // sub

    @pl.when(tb == 0)
    def _():
        for h in range(GLA_HEADS):
            st_scr[h] = s0_ref[0, h].T

    tri = (_iota((c_rows, c_rows), 1) <= _iota((c_rows, c_rows), 0)).astype(F32)
    ones_red = jnp.ones((GLA_DK, 128), BF16)
    lane_s = _iota((sub, 128), 1)
    row_s = _iota((sub, 128), 0)
    col_c = _iota((sub, c_rows), 1)

    def chunk_body(c, carry):
        r0 = pl.multiple_of(c * c_rows, c_rows)
        z = zg_ref[0, pl.ds(r0, c_rows), :]
        q = z[:, 0:256] * (GLA_DK ** -0.5)
        k = z[:, 256:512]
        v = z[:, 512:1024]
        g = z[:, 1024:1536]
        ga = z[:, 1536:1552]
        la = -_softplus(-(_mm(ga, aup_ref[...]) + abias_ref[...])) * (1.0 / GLA_GATE_NORM)
        if t_valid is not None:
            ok = (tb * tb_rows + r0 + _iota((c_rows, 1), 0)) < t_valid
            la = jnp.where(ok, la, 0.0)
            k = jnp.where(ok, k, 0.0)
            v = jnp.where(ok, v, 0.0)
        cum = _mm_exact(tri, la)
        heads = range(GLA_HEADS)
        pairs = [(h, blk) for h in heads for blk in range(nsub)]
        qh = [q[:, 64 * h:64 * h + 64] for h in heads]
        kh = [k[:, 64 * h:64 * h + 64] for h in heads]
        ch = [cum[:, 64 * h:64 * h + 64] for h in heads]
        vh = [v[:, 128 * h:128 * h + 128] for h in heads]
        st = [st_scr[h] for h in heads]
        o_in = [_mm_nt(qh[h] * jnp.exp(ch[h]), st[h]) for h in heads]
        red, off = {}, {}
        for h, blk in pairs:
            sl = slice(blk * sub, (blk + 1) * sub)
            q_i, k_i, c_i = qh[h][sl], kh[h][sl], ch[h][sl]
            es = [q_i * k_i[j:j + 1] * jnp.exp(jnp.minimum(c_i - c_i[j:j + 1], 0.0)) for j in range(sub)]
            red[h, blk] = _mm(jnp.concatenate(es, axis=0), ones_red)
            if blk > 0:
                b_i = ch[h][blk * sub - 1:blk * sub]
                q_t = q_i * jnp.exp(c_i - b_i)
                k_t = kh[h] * jnp.exp(jnp.minimum(b_i - ch[h], 0.0))
                off[h, blk] = _mm_nt(q_t, k_t)
        att = []
        for h in heads:
            att_rows = []
            for blk in range(nsub):
                a_i = jnp.zeros((sub, 128), F32)
                for j in range(sub):
                    a_i = a_i + jnp.where((lane_s == blk * sub + j) & (row_s >= j),
                                          red[h, blk][j * sub:(j + 1) * sub], 0.0)
                a_i = a_i[:, 0:c_rows]
                if blk > 0:
                    a_i = a_i + jnp.where(col_c < blk * sub, off[h, blk], 0.0)
                att_rows.append(a_i)
            att.append(att_rows[0] if nsub == 1 else jnp.concatenate(att_rows, axis=0))
        o = [o_in[h] + _mm(att[h], vh[h]) for h in heads]
        last = [ch[h][c_rows - 1:c_rows] for h in heads]
        st_new = [st[h] * jnp.exp(last[h]) + _mm_tn(vh[h], kh[h] * jnp.exp(last[h] - ch[h])) for h in heads]
        for h in heads:
            st_scr[h] = st_new[h]
            oh = o[h] * lax.rsqrt(jnp.mean(o[h] * o[h], axis=-1, keepdims=True) + NORM_EPS)
            oh = oh * norm_ref[:, 128 * h:128 * h + 128] * _silu(g[:, 128 * h:128 * h + 128])
            o_ref[0, pl.ds(r0, c_rows), 128 * h:128 * h + 128] = oh
        return carry

    lax.fori_loop(0, tb_rows // c_rows, chunk_body, 0)

    @pl.when(tb == pl.num_programs(1) - 1)
    def _():
        for h in range(GLA_HEADS):
            sout_ref[0, h] = st_scr[h].T


def _gla(zg, s0, a_up, a_bias, norm_g, *, tb_rows, chunk, t_valid):
    b, t, _ = zg.shape
    kern = functools.partial(_gla_kernel, tb_rows=tb_rows, chunk=chunk, t_valid=t_valid)
    return pl.pallas_call(
        kern,
        grid=(b, t // tb_rows),
        in_specs=[pl.BlockSpec((1, tb_rows, ZG_W), lambda i, j: (i, j, 0)),
                  pl.BlockSpec((1, GLA_HEADS, GLA_DK, GLA_DV), lambda i, j: (i, 0, 0, 0)),
                  pl.BlockSpec((GLA_LORA, GLA_K), lambda i, j: (0, 0)),
                  pl.BlockSpec((1, GLA_K), lambda i, j: (0, 0)),
                  pl.BlockSpec((1, GLA_V), lambda i, j: (0, 0))],
        out_specs=[pl.BlockSpec((1, tb_rows, GLA_V), lambda i, j: (i, j, 0)),
                   pl.BlockSpec((1, GLA_HEADS, GLA_DK, GLA_DV), lambda i, j: (i, 0, 0, 0))],
        out_shape=[jax.ShapeDtypeStruct((b, t, GLA_V), F32),
                   jax.ShapeDtypeStruct((b, GLA_HEADS, GLA_DK, GLA_DV), F32)],
        scratch_shapes=[pltpu.VMEM((GLA_HEADS, GLA_DV, GLA_DK), F32)],
        compiler_params=_cparams(("parallel", "arbitrary")),
        name="gla",
    )(zg, s0, a_up, a_bias, norm_g)


def _rwkv_kernel(zr_ref, s0_ref, sh0_ref, seg_ref, mu_ref, w0_ref, wup_ref, a0_ref, aup_ref, kk_ref, ka_ref, rk_ref,
                 lnw_ref, lnb_ref, y_ref, sout_ref,
                 s_scr, prev_scr, lw_s, kk_s, kka_s, k2_s, r_s, v_s, y_s, *, tb_rows, chunk, t_valid):
    tb = pl.program_id(1)
    c_rows = chunk
    nh = RWKV_HEADS

    @pl.when(tb == 0)
    def _():
        s_scr[...] = s0_ref[0]
        prev_scr[...] = sh0_ref[0]

    z = zr_ref[0]
    rows = _iota((tb_rows, 1), 0)
    zp = jnp.where(rows == 0, prev_scr[...], pltpu.roll(z, 1, axis=0))
    prev_scr[...] = z[tb_rows - 1:tb_rows]
    zs = z + (zp - z) * mu_ref[...]
    r = zs[:, 0:512]
    k = zs[:, 512:1024]
    v = zs[:, 1024:1536]
    wl = zs[:, 1536:1600]
    al = zs[:, 1600:1664]
    w = -_softplus(-(w0_ref[...] + _mm(jnp.tanh(wl), wup_ref[...]))) - 0.5
    lw = -jnp.exp(w)
    a = _sigmoid(a0_ref[...] + _mm(al, aup_ref[...]))
    kk = k * kk_ref[...]
    kk = kk * lax.rsqrt(_mm_split(kk * kk, seg_ref[...]) + NORM_EPS)
    k2 = k * (1.0 + (a - 1.0) * ka_ref[...])
    kka = kk * a
    if t_valid is not None:
        ok = (tb * tb_rows + rows) < t_valid
        lw = jnp.where(ok, lw, 0.0)
        kka = jnp.where(ok, kka, 0.0)
        k2 = jnp.where(ok, k2, 0.0)
    lw_s[...] = lw
    kk_s[...] = kk
    kka_s[...] = kka
    k2_s[...] = k2
    r_s[...] = r
    v_s[...] = v

    ri = _iota((c_rows, c_rows), 0)
    ci = _iota((c_rows, c_rows), 1)
    tri = (ci <= ri).astype(F32)
    strict = ci < ri
    incl = ci <= ri
    n_dbl = max(1, (c_rows - 1).bit_length())

    def chunk_body(c, carry):
        r0 = pl.multiple_of(c * c_rows, c_rows)
        ds = pl.ds(r0, c_rows)
        lwc = lw_s[ds, :]
        cl = _mm_exact(tri, lwc)
        e_inv = jnp.exp(-cl)
        e_fwd = jnp.exp(cl)
        e_prev = jnp.exp(cl - lwc)
        e_end = jnp.exp(cl[c_rows - 1:c_rows] - cl)
        g_end = jnp.exp(cl[c_rows - 1:c_rows])
        kkc, kkac, k2c, rc, vc = kk_s[ds, :], kka_s[ds, :], k2_s[ds, :], r_s[ds, :], v_s[ds, :]
        heads = range(nh)
        hsl = [slice(64 * h, 64 * h + 64) for h in heads]
        a_t = [-kkac[:, hs] * e_inv[:, hs] for hs in hsl]
        b_t = [kkc[:, hs] * e_prev[:, hs] for hs in hsl]
        k_t = [k2c[:, hs] * e_inv[:, hs] for hs in hsl]
        r_t = [rc[:, hs] * e_fwd[:, hs] for hs in hsl]
        vh = [vc[:, hs] for hs in hsl]
        ak = [jnp.concatenate([a_t[h], k_t[h]], axis=0).astype(BF16) for h in heads]
        ba = [_mm_nt(b_t[h], ak[h]) for h in heads]
        ra = [_mm_nt(r_t[h], ak[h]) for h in heads]
        s0 = [s_scr[h] for h in heads]
        rs = [_mm_nt(r_t[h], s0[h]) for h in heads]
        l_k = [jnp.where(strict, ba[h][:, c_rows:2 * c_rows], 0.0) for h in heads]
        lkv = [_mm(l_k[h], vh[h]) for h in heads]
        x = [jnp.concatenate([b_t[h], lkv[h]], axis=1) for h in heads]
        lp = [jnp.where(strict, ba[h][:, 0:c_rows], 0.0).astype(BF16) for h in heads]
        for step in range(n_dbl):
            x = [x[h] + _mm(lp[h], x[h]) for h in heads]
            if step + 1 < n_dbl:
                lp = [_mm(lp[h], lp[h]).astype(BF16) for h in heads]
        u = [_mm_nt(x[h][:, 0:64], s0[h]) + x[h][:, 64:128] for h in heads]
        m_a = [jnp.where(incl, ra[h][:, 0:c_rows], 0.0) for h in heads]
        m_k = [jnp.where(incl, ra[h][:, c_rows:2 * c_rows], 0.0) for h in heads]
        y = [rs[h] + _mm(m_a[h], u[h]) + _mm(m_k[h], vh[h]) for h in heads]
        akg = [jnp.concatenate([-kkac[:, hs] * e_end[:, hs], k2c[:, hs] * e_end[:, hs]], axis=0) for hs in hsl]
        s_new = [s0[h] * g_end[:, hsl[h]] + _mm_tn(jnp.concatenate([u[h], vh[h]], axis=0), akg[h]) for h in heads]
        for h in heads:
            s_scr[h] = s_new[h]
            y_s[ds, hsl[h]] = y[h]
        return carry

    lax.fori_loop(0, tb_rows // c_rows, chunk_body, 0)

    y = y_s[...]
    seg = seg_ref[...]
    mean = _mm_split(y, seg) * (1.0 / RWKV_HD)
    d = y - mean
    var = _mm_split(d * d, seg) * (1.0 / RWKV_HD)
    yn = d * lax.rsqrt(var + RWKV_LN_EPS) * lnw_ref[...] + lnb_ref[...]
    bonus = _mm_split(r * k2 * rk_ref[...], seg) * v
    y_ref[0] = (yn + bonus) * _silu(zs[:, 1664:2176])

    @pl.when(tb == pl.num_programs(1) - 1)
    def _():
        sout_ref[0] = s_scr[...]


def _rwkv(zr, s0, sh0, seg, params, *, tb_rows, chunk, t_valid):
    b, t, _ = zr.shape
    kern = functools.partial(_rwkv_kernel, tb_rows=tb_rows, chunk=chunk, t_valid=t_valid)
    full = lambda shp: pl.BlockSpec(shp, lambda i, j: (0,) * len(shp))
    mu, w0, w_up, a0, a_up, k_k, k_a, r_k, ln_w, ln_b = params
    return pl.pallas_call(
        kern,
        grid=(b, t // tb_rows),
        in_specs=[pl.BlockSpec((1, tb_rows, RWKV_IN), lambda i, j: (i, j, 0)),
                  pl.BlockSpec((1, RWKV_HEADS, RWKV_HD, RWKV_HD), lambda i, j: (i, 0, 0, 0)),
                  pl.BlockSpec((1, 1, RWKV_IN), lambda i, j: (i, 0, 0)),
                  full((RWKV_W, RWKV_W)), full((1, RWKV_IN)), full((1, RWKV_W)), full((64, RWKV_W)),
                  full((1, RWKV_W)), full((64, RWKV_W)), full((1, RWKV_W)), full((1, RWKV_W)), full((1, RWKV_W)),
                  full((1, RWKV_W)), full((1, RWKV_W))],
        out_specs=[pl.BlockSpec((1, tb_rows, RWKV_W), lambda i, j: (i, j, 0)),
                   pl.BlockSpec((1, RWKV_HEADS, RWKV_HD, RWKV_HD), lambda i, j: (i, 0, 0, 0))],
        out_shape=[jax.ShapeDtypeStruct((b, t, RWKV_W), F32),
                   jax.ShapeDtypeStruct((b, RWKV_HEADS, RWKV_HD, RWKV_HD), F32)],
        scratch_shapes=[pltpu.VMEM((RWKV_HEADS, RWKV_HD, RWKV_HD), F32), pltpu.VMEM((1, RWKV_IN), F32)]
        + [pltpu.VMEM((tb_rows, RWKV_W), F32) for _ in range(7)],
        compiler_params=_cparams(("parallel", "arbitrary")),
        name="rwkv",
    )(zr, s0, sh0, seg, mu, w0, w_up, a0, a_up, k_k, k_a, r_k, ln_w, ln_b)


def _slope(h):
    return 2.0 ** (-8.0 * (h + 1) / NSA_HEADS)


def _online_update(s, mask, v, m_ref, l_ref, acc_ref, idx):
    s = jnp.where(mask, s, NEG)
    m_old = m_ref[idx]
    m_new = jnp.maximum(m_old, jnp.max(s, axis=-1, keepdims=True))
    p = jnp.where(mask, jnp.exp(s - m_new), 0.0)
    alpha = jnp.exp(m_old - m_new)
    l_ref[idx] = alpha * l_ref[idx] + jnp.sum(p, axis=-1, keepdims=True)
    acc_ref[idx] = alpha * acc_ref[idx] + _mm(p, v)
    m_ref[idx] = m_new


def _online_init(m_ref, l_ref, acc_ref):
    m_ref[...] = jnp.full(m_ref.shape, NEG, F32)
    l_ref[...] = jnp.zeros(l_ref.shape, F32)
    acc_ref[...] = jnp.zeros(acc_ref.shape, F32)


def _online_result(l_ref, acc_ref, idx):
    l = l_ref[idx]
    return acc_ref[idx] / jnp.where(l > 0, l, 1.0)


def _nsa_prompt_kernel(q_ref, kv_ref, bg_ref, g_ref, o_ref, kc_s, vc_s, q_s, oc_s, m_s, l_s, acc_s, m2_s, l2_s, acc2_s,
                       *, t_len, tq):
    i = pl.program_id(1)
    nb = t_len // SEL_BLOCK
    tk = tq
    win_tiles = WINDOW // tk

    @pl.when(i == 0)
    def _():
        kcm = kv_ref[0, :, 0:128].reshape(nb, SEL_BLOCK, 128)
        vcm = kv_ref[0, :, 128:256].reshape(nb, SEL_BLOCK, 128)
        kc_s[0:nb] = jnp.sum(kcm[:, 0:CMP_BLOCK], axis=1) * (1.0 / CMP_BLOCK)
        kc_s[nb:2 * nb] = jnp.sum(kcm[:, CMP_BLOCK:SEL_BLOCK], axis=1) * (1.0 / CMP_BLOCK)
        vc_s[0:nb] = jnp.sum(vcm[:, 0:CMP_BLOCK], axis=1) * (1.0 / CMP_BLOCK)
        vc_s[nb:2 * nb] = jnp.sum(vcm[:, CMP_BLOCK:SEL_BLOCK], axis=1) * (1.0 / CMP_BLOCK)

    for h in range(NSA_HEADS):
        q_s[h] = (q_ref[0, :, 64 * h:64 * h + 64] * (NSA_HD ** -0.5)).astype(BF16)

    t_col = i * tq + _iota((tq, 1), 0)
    ccol = _iota((1, 2 * nb), 1)
    cend = jnp.where(ccol < nb, SEL_BLOCK * ccol + (CMP_BLOCK - 1), SEL_BLOCK * (ccol - nb) + (SEL_BLOCK - 1))
    dist_c = (t_col - cend).astype(F32)
    mask_c = cend <= t_col
    jcol = _iota((1, nb), 1)
    tblk = lax.shift_right_logical(t_col, 6)
    forced = (jcol == tblk) | (jcol == 0)
    valid = jcol <= tblk

    for kvh in range(NSA_KV_HEADS):
        ls = slice(64 * kvh, 64 * kvh + 64)
        kc = kc_s[:, ls]
        vc = vc_s[:, ls]
        imp_e = jnp.zeros((tq, nb), F32)
        imp_o = jnp.zeros((tq, nb), F32)
        for g in range(NSA_GROUP):
            h = NSA_GROUP * kvh + g
            s = _mm_nt(q_s[h], kc) - _slope(h) * dist_c
            s = jnp.where(mask_c, s, NEG)
            m = jnp.max(s, axis=-1, keepdims=True)
            p = jnp.where(mask_c, jnp.exp(s - m), 0.0)
            den = jnp.sum(p, axis=-1, keepdims=True)
            p = p / jnp.where(den > 0, den, 1.0)
            oc_s[h] = _mm(p, vc)
            imp_e = imp_e + p[:, 0:nb]
            imp_o = imp_o + p[:, nb:2 * nb]
        score = jnp.where(forced, FORCE_SCORE, jnp.where(valid, imp_e + imp_o, -FORCE_SCORE))
        rank = jnp.zeros((tq, nb), F32)
        for jp in range(nb):
            cj = score[:, jp:jp + 1]
            rank = rank + jnp.where((cj > score) | ((cj == score) & (jp < jcol)), 1.0, 0.0)
        sel = jnp.where(rank < N_SELECT, 1.0, 0.0).astype(BF16)

        _online_init(m_s, l_s, acc_s)
        _online_init(m2_s, l2_s, acc2_s)

        def sel_body(kt, carry):
            k0 = pl.multiple_of(kt * tk, tk)
            kpos = kt * tk + _iota((1, tk), 1)
            dist_i = t_col - kpos
            dist = dist_i.astype(F32)
            expand = jnp.where(_iota((nb, tk), 0) == lax.shift_right_logical(kpos, 6), 1.0, 0.0).astype(BF16)
            mask = (jnp.dot(sel, expand, preferred_element_type=F32) > 0.5) & (dist_i >= 0)
            kt_k = kv_ref[0, pl.ds(k0, tk), 256 + 64 * kvh:320 + 64 * kvh]
            kt_v = kv_ref[0, pl.ds(k0, tk), 384 + 64 * kvh:448 + 64 * kvh]
            for g in range(NSA_GROUP):
                h = NSA_GROUP * kvh + g
                s = _mm_nt(q_s[h], kt_k) - _slope(h) * dist
                _online_update(s, mask, kt_v, m_s, l_s, acc_s, g)
            return carry

        lax.fori_loop(0, i + 1, sel_body, 0)

        def win_body(kt, carry):
            k0 = pl.multiple_of(kt * tk, tk)
            kpos = kt * tk + _iota((1, tk), 1)
            dist_i = t_col - kpos
            dist = dist_i.astype(F32)
            mask = (dist_i >= 0) & (dist_i <= WINDOW)
            kt_k = kv_ref[0, pl.ds(k0, tk), 512 + 64 * kvh:576 + 64 * kvh]
            kt_v = kv_ref[0, pl.ds(k0, tk), 640 + 64 * kvh:704 + 64 * kvh]
            for g in range(NSA_GROUP):
                h = NSA_GROUP * kvh + g
                s = _mm_nt(q_s[h], kt_k) - _slope(h) * dist
                _online_update(s, mask, kt_v, m2_s, l2_s, acc2_s, g)
            return carry

        lax.fori_loop(jnp.maximum(i - win_tiles, 0), i + 1, win_body, 0)

        for g in range(NSA_GROUP):
            h = NSA_GROUP * kvh + g
            gates = _sigmoid(bg_ref[0, :, 3 * h:3 * h + 3])
            o = (gates[:, 0:1] * oc_s[h] + gates[:, 1:2] * _online_result(l_s, acc_s, g)
                 + gates[:, 2:3] * _online_result(l2_s, acc2_s, g))
            o_ref[0, :, 64 * h:64 * h + 64] = o * _silu(g_ref[0, :, 64 * h:64 * h + 64])


def _nsa_prompt(zn, *, tq):
    b, t, _ = zn.shape
    nb = t // SEL_BLOCK
    kern = functools.partial(_nsa_prompt_kernel, t_len=t, tq=tq)
    stat = lambda: [pltpu.VMEM((NSA_GROUP, tq, 1), F32), pltpu.VMEM((NSA_GROUP, tq, 1), F32),
                    pltpu.VMEM((NSA_GROUP, tq, NSA_HD), F32)]
    return pl.pallas_call(
        kern,
        grid=(b, t // tq),
        in_specs=[pl.BlockSpec((1, tq, 512), lambda i, j: (i, j, 2)),
                  pl.BlockSpec((1, t, 768), lambda i, j: (i, 0, 0)),
                  pl.BlockSpec((1, tq, 256), lambda i, j: (i, j, 3)),
                  pl.BlockSpec((1, tq, 512), lambda i, j: (i, j, 3))],
        out_specs=pl.BlockSpec((1, tq, 512), lambda i, j: (i, j, 0)),
        out_shape=jax.ShapeDtypeStruct((b, t, 512), F32),
        scratch_shapes=[pltpu.VMEM((2 * nb, 128), F32), pltpu.VMEM((2 * nb, 128), F32),
                        pltpu.VMEM((NSA_HEADS, tq, NSA_HD), BF16), pltpu.VMEM((NSA_HEADS, tq, NSA_HD), F32)]
        + stat() + stat(),
        compiler_params=_cparams(("parallel", "arbitrary")),
        name="nsa_prompt",
    )(zn, zn, zn, zn)


PAGES_PER_STEP = 8
N_PAGE_GROUPS = N_PAGES // PAGES_PER_STEP
N_PAST_SEL = PAST_LEN // SEL_BLOCK


def _nsa_sample_kernel(pt_ref, zn_ref, win_ref, *rest):
    pages = rest[:PAGES_PER_STEP]
    o_ref = rest[PAGES_PER_STEP]
    kce_s, kco_s, vce_s, vco_s, sel_s, oc_s, m_s, l_s, acc_s = rest[PAGES_PER_STEP + 1:]
    ph = pl.program_id(1)
    gi = pl.program_id(2)
    rq = SAMPLE_TPAD
    npast = N_PAST_SEL
    t_col = PAST_LEN + _iota((rq, 1), 0)
    keys_per_step = PAGES_PER_STEP * PAGE_SIZE

    def q_head(h):
        return zn_ref[0, :, 1024 + 64 * h:1088 + 64 * h] * (NSA_HD ** -0.5)

    @pl.when(ph == 0)
    def _pool():
        ek, ok_, ev, ov = [], [], [], []
        for pr in pages:
            xk = pr[0, :, 0:128].reshape(PAGE_SIZE // SEL_BLOCK, SEL_BLOCK, 128)
            xv = pr[0, :, 128:256].reshape(PAGE_SIZE // SEL_BLOCK, SEL_BLOCK, 128)
            ek.append(jnp.sum(xk[:, 0:CMP_BLOCK], axis=1) * (1.0 / CMP_BLOCK))
            ok_.append(jnp.sum(xk[:, CMP_BLOCK:SEL_BLOCK], axis=1) * (1.0 / CMP_BLOCK))
            ev.append(jnp.sum(xv[:, 0:CMP_BLOCK], axis=1) * (1.0 / CMP_BLOCK))
            ov.append(jnp.sum(xv[:, CMP_BLOCK:SEL_BLOCK], axis=1) * (1.0 / CMP_BLOCK))
        nrow = PAGES_PER_STEP * (PAGE_SIZE // SEL_BLOCK)
        dst = pl.ds(pl.multiple_of(gi * nrow, nrow), nrow)
        kce_s[dst, :] = jnp.concatenate(ek, axis=0)
        kco_s[dst, :] = jnp.concatenate(ok_, axis=0)
        vce_s[dst, :] = jnp.concatenate(ev, axis=0)
        vco_s[dst, :] = jnp.concatenate(ov, axis=0)

    @pl.when((ph == 0) & (gi == N_PAGE_GROUPS - 1))
    def _compressed():
        jrow = _iota((1, npast), 1)
        dist_e = (t_col - (SEL_BLOCK * jrow + (CMP_BLOCK - 1))).astype(F32)
        dist_o = (t_col - (SEL_BLOCK * jrow + (SEL_BLOCK - 1))).astype(F32)
        jp = _iota((npast, npast), 0)
        jj = _iota((npast, npast), 1)
        for kvh in range(NSA_KV_HEADS):
            ls = slice(64 * kvh, 64 * kvh + 64)
            kce, kco, vce, vco = kce_s[:, ls], kco_s[:, ls], vce_s[:, ls], vco_s[:, ls]
            imp_e = jnp.zeros((rq, npast), F32)
            imp_o = jnp.zeros((rq, npast), F32)
            for g in range(NSA_GROUP):
                h = NSA_GROUP * kvh + g
                qh = q_head(h)
                se = _mm_nt(qh, kce) - _slope(h) * dist_e
                so = _mm_nt(qh, kco) - _slope(h) * dist_o
                m = jnp.maximum(jnp.max(se, axis=-1, keepdims=True), jnp.max(so, axis=-1, keepdims=True))
                pe = jnp.exp(se - m)
                po = jnp.exp(so - m)
                den = jnp.sum(pe, axis=-1, keepdims=True) + jnp.sum(po, axis=-1, keepdims=True)
                pe = pe / den
                po = po / den
                oc_s[h] = _mm(pe, vce) + _mm(po, vco)
                imp_e = imp_e + pe
                imp_o = imp_o + po
            imp = imp_e + imp_o
            imp_t = jnp.concatenate([imp, jnp.zeros((128 - rq, npast), F32)], axis=0).T
            sel_rows = []
            for t in range(rq):
                col = imp_t[:, t:t + 1]
                row = imp[t:t + 1, :]
                beats = ((col > row) | ((col == row) & (jp < jj))) & (jp >= 1)
                rank = jnp.sum(jnp.where(beats, 1.0, 0.0), axis=0, keepdims=True)
                sel_rows.append(jnp.where((jrow == 0) | (rank < N_SELECT - 2), 1.0, 0.0))
            sel_s[kvh] = jnp.concatenate(sel_rows, axis=0).astype(BF16)

    @pl.when(ph == 1)
    def _selected():
        @pl.when(gi == 0)
        def _():
            _online_init(m_s, l_s, acc_s)

        k_all = jnp.concatenate([pr[0, :, 0:128] for pr in pages], axis=0)
        v_all = jnp.concatenate([pr[0, :, 128:256] for pr in pages], axis=0)
        kpos = gi * keys_per_step + _iota((1, keys_per_step), 1)
        dist = (t_col - kpos).astype(F32)
        expand = jnp.where(_iota((npast, keys_per_step), 0) == lax.shift_right_logical(kpos, 6), 1.0, 0.0).astype(BF16)
        for kvh in range(NSA_KV_HEADS):
            ls = slice(64 * kvh, 64 * kvh + 64)
            mask = jnp.dot(sel_s[kvh], expand, preferred_element_type=F32) > 0.5
            for g in range(NSA_GROUP):
                h = NSA_GROUP * kvh + g
                s = _mm_nt(q_head(h), k_all[:, ls]) - _slope(h) * dist
                _online_update(s, mask, v_all[:, ls], m_s, l_s, acc_s, h)

    @pl.when((ph == 1) & (gi == N_PAGE_GROUPS - 1))
    def _finish():
        new = zn_ref[0, :, 0:768]
        npos = PAST_LEN + _iota((1, rq), 1)
        dist_ni = t_col - npos
        dist_n = dist_ni.astype(F32)
        mask_n = dist_ni >= 0
        wpos = PAST_LEN - WINDOW + _iota((1, WINDOW), 1)
        dist_wi = t_col - wpos
        dist_w = dist_wi.astype(F32)
        mask_w = dist_wi <= WINDOW
        for kvh in range(NSA_KV_HEADS):
            ls = slice(64 * kvh, 64 * kvh + 64)
            wk = win_ref[0, :, 64 * kvh:64 * kvh + 64]
            wv = win_ref[0, :, 128 + 64 * kvh:192 + 64 * kvh]
            for g in range(NSA_GROUP):
                h = NSA_GROUP * kvh + g
                qh = q_head(h)
                sl = _slope(h)
                s = _mm_nt(qh, new[:, 256 + 64 * kvh:320 + 64 * kvh]) - sl * dist_n
                _online_update(s, mask_n, new[:, 384 + 64 * kvh:448 + 64 * kvh], m_s, l_s, acc_s, h)
                o_sel = _online_result(l_s, acc_s, h)
                s1 = jnp.where(mask_w, _mm_nt(qh, wk) - sl * dist_w, NEG)
                s2 = jnp.where(mask_n, _mm_nt(qh, new[:, 512 + 64 * kvh:576 + 64 * kvh]) - sl * dist_n, NEG)
                m = jnp.maximum(jnp.max(s1, axis=-1, keepdims=True), jnp.max(s2, axis=-1, keepdims=True))
                p1 = jnp.where(mask_w, jnp.exp(s1 - m), 0.0)
                p2 = jnp.where(mask_n, jnp.exp(s2 - m), 0.0)
                den = jnp.sum(p1, axis=-1, keepdims=True) + jnp.sum(p2, axis=-1, keepdims=True)
                o_win = (_mm(p1, wv) + _mm(p2, new[:, 640 + 64 * kvh:704 + 64 * kvh])) / jnp.where(den > 0, den, 1.0)
                gates = _sigmoid(zn_ref[0, :, 768 + 3 * h:771 + 3 * h])
                o = gates[:, 0:1] * oc_s[h] + gates[:, 1:2] * o_sel + gates[:, 2:3] * o_win
                o_ref[0, :, 64 * h:64 * h + 64] = o * _silu(zn_ref[0, :, 1536 + 64 * h:1600 + 64 * h])


def _nsa_sample(page_table, zn, win_state, cache, *, layer, n_pool):
    b = zn.shape[0]
    rq = SAMPLE_TPAD

    def page_map(kidx):
        return lambda i, ph, gi, pt: (layer * n_pool + pt[i, gi * PAGES_PER_STEP + kidx], 0, ph)

    grid_spec = pltpu.PrefetchScalarGridSpec(
        num_scalar_prefetch=1,
        grid=(b, 2, N_PAGE_GROUPS),
        in_specs=[pl.BlockSpec((1, rq, ZN_W), lambda i, ph, gi, pt: (i, 0, 0)),
                  pl.BlockSpec((1, WINDOW, 256), lambda i, ph, gi, pt: (layer * b + i, 0, 0))]
        + [pl.BlockSpec((1, PAGE_SIZE, 256), page_map(kidx)) for kidx in range(PAGES_PER_STEP)],
        out_specs=pl.BlockSpec((1, rq, 512), lambda i, ph, gi, pt: (i, 0, 0)),
        scratch_shapes=[pltpu.VMEM((N_PAST_SEL, 128), F32) for _ in range(4)]
        + [pltpu.VMEM((NSA_KV_HEADS, rq, N_PAST_SEL), BF16), pltpu.VMEM((NSA_HEADS, rq, NSA_HD), F32),
           pltpu.VMEM((NSA_HEADS, rq, 1), F32), pltpu.VMEM((NSA_HEADS, rq, 1), F32),
           pltpu.VMEM((NSA_HEADS, rq, NSA_HD), F32)],
    )
    return pl.pallas_call(
        _nsa_sample_kernel,
        grid_spec=grid_spec,
        out_shape=jax.ShapeDtypeStruct((b, rq, 512), F32),
        compiler_params=_cparams(("parallel", "arbitrary", "arbitrary")),
        name="nsa_sample",
    )(page_table, zn, win_state, *([cache] * PAGES_PER_STEP))


def _mm_split3(a, b01):
    hi = a.astype(BF16)
    r1 = a - hi.astype(F32)
    mid = r1.astype(BF16)
    lo = (r1 - mid.astype(F32)).astype(BF16)
    return (jnp.dot(hi, b01, preferred_element_type=F32) + jnp.dot(mid, b01, preferred_element_type=F32)
            + jnp.dot(lo, b01, preferred_element_type=F32))


def _flash_init_t(m_ref, l_ref, acc_ref):
    m_ref[...] = jnp.full(m_ref.shape, NEG, F32)
    l_ref[...] = jnp.zeros(l_ref.shape, F32)
    acc_ref[...] = jnp.zeros(acc_ref.shape, F32)


def _flash_step_t(s_t, mask, v, m_ref, l_ref, acc_ref, guard):
    s_t = jnp.where(mask, s_t, NEG)
    m_old = m_ref[...]
    m_new = jnp.maximum(m_old, jnp.max(s_t, axis=0, keepdims=True))
    p = jnp.exp(s_t - m_new)
    if guard:
        p = jnp.where(mask, p, 0.0)
    alpha = jnp.exp(m_old - m_new)
    l_ref[...] = alpha * l_ref[...] + jnp.sum(p, axis=0, keepdims=True)
    acc_ref[...] = alpha * acc_ref[...] + _mm_tn(v, p)
    m_ref[...] = m_new


def _flash_result_t(l_ref, acc_ref):
    l = l_ref[...]
    return acc_ref[...] / jnp.where(l > 0, l, 1.0)


def _nsa_prompt_t_kernel(q_ref, kv_ref, bg_ref, g_ref, o_ref, kc_s, vc_s, ks_s, vs_s, kw_s, vw_s, qa_s, sel_s, oc_s,
                         m_s, l_s, acc_s, m2_s, l2_s, acc2_s, *, t_len, tq):
    i = pl.program_id(1)
    nb = t_len // SEL_BLOCK
    tk = tq
    ncol = NSA_GROUP * tq
    win_tiles = WINDOW // tk

    def aug_lanes(lane, pos):
        return jnp.where(lane == 64, lax.shift_right_logical(pos, 7).astype(F32),
                         jnp.where(lane == 65, (pos & 127).astype(F32), 0.0))

    def head_lanes(x, kvh):
        return x if kvh == 0 else pltpu.roll(x, 64, axis=1)

    @pl.when(i == 0)
    def _():
        lane = _iota((t_len, 128), 1)
        aug = aug_lanes(lane, _iota((t_len, 128), 0))
        for kcol, vcol, k_dst, v_dst in ((256, 384, ks_s, vs_s), (512, 640, kw_s, vw_s)):
            kf = kv_ref[0, :, kcol:kcol + 128]
            vf = kv_ref[0, :, vcol:vcol + 128]
            for kvh in range(NSA_KV_HEADS):
                k_dst[kvh] = jnp.where(lane < 64, head_lanes(kf, kvh), aug).astype(BF16)
                v_dst[kvh] = jnp.where(lane < 64, head_lanes(vf, kvh), 0.0).astype(BF16)
        kcm = kv_ref[0, :, 0:128].reshape(nb, SEL_BLOCK, 128)
        vcm = kv_ref[0, :, 128:256].reshape(nb, SEL_BLOCK, 128)
        inv = 1.0 / CMP_BLOCK
        kc = jnp.concatenate([jnp.sum(kcm[:, 0:CMP_BLOCK], axis=1) * inv,
                              jnp.sum(kcm[:, CMP_BLOCK:SEL_BLOCK], axis=1) * inv], axis=0)
        vc = jnp.concatenate([jnp.sum(vcm[:, 0:CMP_BLOCK], axis=1) * inv,
                              jnp.sum(vcm[:, CMP_BLOCK:SEL_BLOCK], axis=1) * inv], axis=0)
        lane_c = _iota((2 * nb, 128), 1)
        r_c = _iota((2 * nb, 128), 0)
        cend = jnp.where(r_c < nb, SEL_BLOCK * r_c + (CMP_BLOCK - 1), SEL_BLOCK * (r_c - nb) + (SEL_BLOCK - 1))
        aug_c = aug_lanes(lane_c, cend)
        for kvh in range(NSA_KV_HEADS):
            kc_s[kvh] = jnp.where(lane_c < 64, head_lanes(kc, kvh), aug_c).astype(BF16)
            vc_s[kvh] = jnp.where(lane_c < 64, head_lanes(vc, kvh), 0.0).astype(BF16)

    lane_q = _iota((tq, 128), 1)
    for h in range(NSA_HEADS):
        qb = q_ref[0, :, 128 * (h // 2):128 * (h // 2) + 128]
        if h % 2:
            qb = pltpu.roll(qb, 64, axis=1)
        sl = _slope(h)
        qa = jnp.where(lane_q < 64, qb * (NSA_HD ** -0.5),
                       jnp.where(lane_q == 64, 128.0 * sl, jnp.where(lane_q == 65, sl, 0.0)))
        qa_s[h // NSA_GROUP, (h % NSA_GROUP) * tq:(h % NSA_GROUP + 1) * tq, :] = qa.astype(BF16)

    t_row = i * tq + (_iota((1, ncol), 1) & (tq - 1))
    t_row1 = i * tq + _iota((1, tq), 1)
    r_c1 = _iota((2 * nb, 1), 0)
    cend_col = jnp.where(r_c1 < nb, SEL_BLOCK * r_c1 + (CMP_BLOCK - 1), SEL_BLOCK * (r_c1 - nb) + (SEL_BLOCK - 1))
    mask_c = cend_col <= t_row
    jrow = _iota((nb, 1), 0)
    tblk = lax.shift_right_logical(t_row1, 6)
    forced = (jrow == tblk) | (jrow == 0)
    valid = jrow <= tblk
    bg_t = bg_ref[0].T

    for kvh in range(NSA_KV_HEADS):
        qa = qa_s[kvh]
        s_c = jnp.where(mask_c, _mm_nt(kc_s[kvh], qa), NEG)
        m = jnp.max(s_c, axis=0, keepdims=True)
        p = jnp.where(mask_c, jnp.exp(s_c - m), 0.0)
        den = jnp.sum(p, axis=0, keepdims=True)
        p = p / jnp.where(den > 0, den, 1.0)
        oc_s[...] = _mm_tn(vc_s[kvh], p)
        imp_e = p[0:nb, 0:tq]
        imp_o = p[nb:2 * nb, 0:tq]
        for g in range(1, NSA_GROUP):
            imp_e = imp_e + p[0:nb, g * tq:(g + 1) * tq]
            imp_o = imp_o + p[nb:2 * nb, g * tq:(g + 1) * tq]
        score = jnp.where(forced, FORCE_SCORE, jnp.where(valid, imp_e + imp_o, -FORCE_SCORE))
        rank = jnp.zeros((nb, tq), F32)
        for jp in range(nb):
            rj = score[jp:jp + 1, :]
            rank = rank + jnp.where((rj > score) | ((rj == score) & (jp < jrow)), 1.0, 0.0)
        sel = jnp.where(rank < N_SELECT, 1.0, 0.0).astype(BF16)
        sel_s[...] = jnp.concatenate([sel] * NSA_GROUP, axis=1)

        _flash_init_t(m_s, l_s, acc_s)
        _flash_init_t(m2_s, l2_s, acc2_s)

        def sel_body(kt, carry):
            k0 = pl.multiple_of(kt * tk, tk)
            pos_col = kt * tk + _iota((tk, 1), 0)
            onehot = jnp.where(_iota((tk, nb), 1) == lax.shift_right_logical(pos_col, 6), 1.0, 0.0).astype(BF16)
            mask = (jnp.dot(onehot, sel_s[...], preferred_element_type=F32) > 0.5) & (pos_col <= t_row)
            s_t = _mm_nt(ks_s[kvh, pl.ds(k0, tk), :], qa_s[kvh])
            _flash_step_t(s_t, mask, vs_s[kvh, pl.ds(k0, tk), :], m_s, l_s, acc_s, False)
            return carry

        lax.fori_loop(0, i + 1, sel_body, 0)

        def win_body(kt, carry):
            k0 = pl.multiple_of(kt * tk, tk)
            pos_col = kt * tk + _iota((tk, 1), 0)
            dist = t_row - pos_col
            mask = (dist >= 0) & (dist <= WINDOW)
            s_t = _mm_nt(kw_s[kvh, pl.ds(k0, tk), :], qa_s[kvh])
            _flash_step_t(s_t, mask, vw_s[kvh, pl.ds(k0, tk), :], m2_s, l2_s, acc2_s, False)
            return carry

        lax.fori_loop(jnp.maximum(i - win_tiles, 0), i + 1, win_body, 0)

        o_sel = _flash_result_t(l_s, acc_s)
        o_win = _flash_result_t(l2_s, acc2_s)
        o_cmp = oc_s[...]
        for g in range(NSA_GROUP):
            h = NSA_GROUP * kvh + g
            cs = slice(g * tq, (g + 1) * tq)
            gates = _sigmoid(bg_t[3 * h:3 * h + 3, :])
            comb = gates[0:1] * o_cmp[:, cs] + gates[1:2] * o_sel[:, cs] + gates[2:3] * o_win[:, cs]
            o_ref[0, :, 64 * h:64 * h + 64] = comb.T[:, 0:64] * _silu(g_ref[0, :, 64 * h:64 * h + 64])


def _nsa_prompt_t(zn, *, tq):
    b, t, _ = zn.shape
    nb = t // SEL_BLOCK
    ncol = NSA_GROUP * tq
    kern = functools.partial(_nsa_prompt_t_kernel, t_len=t, tq=tq)
    stat = lambda: [pltpu.VMEM((1, ncol), F32), pltpu.VMEM((1, ncol), F32), pltpu.VMEM((128, ncol), F32)]
    kvbuf = lambda n: pltpu.VMEM((NSA_KV_HEADS, n, 128), BF16)
    return pl.pallas_call(
        kern,
        grid=(b, t // tq),
        in_specs=[pl.BlockSpec((1, tq, 512), lambda i, j: (i, j, 2)),
                  pl.BlockSpec((1, t, 768), lambda i, j: (i, 0, 0)),
                  pl.BlockSpec((1, tq, 256), lambda i, j: (i, j, 3)),
                  pl.BlockSpec((1, tq, 512), lambda i, j: (i, j, 3))],
        out_specs=pl.BlockSpec((1, tq, 512), lambda i, j: (i, j, 0)),
        out_shape=jax.ShapeDtypeStruct((b, t, 512), F32),
        scratch_shapes=[kvbuf(2 * nb), kvbuf(2 * nb), kvbuf(t), kvbuf(t), kvbuf(t), kvbuf(t), kvbuf(ncol),
                        pltpu.VMEM((nb, ncol), BF16), pltpu.VMEM((128, ncol), F32)] + stat() + stat(),
        compiler_params=_cparams(("parallel", "arbitrary")),
        name="nsa_prompt",
    )(zn, zn, zn, zn)


SAMPLE_COLS = NSA_HEADS * SAMPLE_TPAD
PAGE_ROWS_PER_TOKEN = 4 * NSA_KV_HEADS


def _nsa_sample_t_kernel(pt_ref, zn_ref, win_ref, e0_ref, *rest):
    pages = rest[:PAGES_PER_STEP]
    o_ref = rest[PAGES_PER_STEP]
    kce_s, kco_s, vce_s, vco_s, qt_s, sel_s, oc_s, m_s, l_s, acc_s = rest[PAGES_PER_STEP + 1:]
    ph = pl.program_id(1)
    gi = pl.program_id(2)
    rq = SAMPLE_TPAD
    npast = N_PAST_SEL
    keys_per_step = PAGES_PER_STEP * PAGE_SIZE
    blocks_per_step = keys_per_step // SEL_BLOCK
    col = _iota((1, SAMPLE_COLS), 1)
    t_row = PAST_LEN + (col & (rq - 1))
    hcol = lax.shift_right_logical(col, 4)
    slope_row = jnp.zeros((1, SAMPLE_COLS), F32)
    for h in range(NSA_HEADS):
        slope_row = jnp.where(hcol == h, _slope(h), slope_row)

    def scores(keys, pos_col):
        return _mm_nt(keys, qt_s[...]) - slope_row * (t_row - pos_col).astype(F32)

    def page_kind(pr, kind):
        page = pr.at[0]
        rows = [page[pl.ds(2 * kind + kv, PAGE_SIZE, stride=PAGE_ROWS_PER_TOKEN), :] for kv in range(NSA_KV_HEADS)]
        return jnp.concatenate(rows, axis=1)

    @pl.when((ph == 0) & (gi == 0))
    def _queries():
        zero = jnp.zeros((rq, NSA_HD), F32)
        rows = []
        for h in range(NSA_HEADS):
            qh = zn_ref[0, :, 1024 + 64 * h:1088 + 64 * h] * (NSA_HD ** -0.5)
            rows.append(jnp.concatenate([qh, zero] if h < NSA_GROUP else [zero, qh], axis=1))
        qt_s[...] = jnp.concatenate(rows, axis=0).astype(BF16)

    @pl.when(ph == 0)
    def _pool():
        ek, ok_, ev, ov = [], [], [], []
        inv = 1.0 / CMP_BLOCK
        for pr in pages:
            xk = page_kind(pr, 0).reshape(PAGE_SIZE // SEL_BLOCK, SEL_BLOCK, 128)
            xv = page_kind(pr, 1).reshape(PAGE_SIZE // SEL_BLOCK, SEL_BLOCK, 128)
            ek.append(jnp.sum(xk[:, 0:CMP_BLOCK], axis=1) * inv)
            ok_.append(jnp.sum(xk[:, CMP_BLOCK:SEL_BLOCK], axis=1) * inv)
            ev.append(jnp.sum(xv[:, 0:CMP_BLOCK], axis=1) * inv)
            ov.append(jnp.sum(xv[:, CMP_BLOCK:SEL_BLOCK], axis=1) * inv)
        dst = pl.ds(pl.multiple_of(gi * blocks_per_step, blocks_per_step), blocks_per_step)
        kce_s[dst, :] = jnp.concatenate(ek, axis=0)
        kco_s[dst, :] = jnp.concatenate(ok_, axis=0)
        vce_s[dst, :] = jnp.concatenate(ev, axis=0)
        vco_s[dst, :] = jnp.concatenate(ov, axis=0)

    @pl.when((ph == 0) & (gi == N_PAGE_GROUPS - 1))
    def _compressed():
        jcol = _iota((npast, 1), 0)
        se = scores(kce_s[...], SEL_BLOCK * jcol + (CMP_BLOCK - 1))
        so = scores(kco_s[...], SEL_BLOCK * jcol + (SEL_BLOCK - 1))
        m = jnp.maximum(jnp.max(se, axis=0, keepdims=True), jnp.max(so, axis=0, keepdims=True))
        pe = jnp.exp(se - m)
        po = jnp.exp(so - m)
        den = jnp.sum(pe, axis=0, keepdims=True) + jnp.sum(po, axis=0, keepdims=True)
        pe = pe / den
        po = po / den
        oc_s[...] = _mm_tn(vce_s[...], pe) + _mm_tn(vco_s[...], po)
        ca = _iota((SAMPLE_COLS, SAMPLE_COLS), 0)
        cb = _iota((SAMPLE_COLS, SAMPLE_COLS), 1)
        same = ((lax.shift_right_logical(ca, 6) == lax.shift_right_logical(cb, 6))
                & ((ca & (rq - 1)) == (cb & (rq - 1))))
        gsum = jnp.where(same, 1.0, 0.0).astype(BF16)
        imp_t = _mm_split3(pe, gsum) + _mm_split3(po, gsum)
        imp = imp_t.T
        jp = _iota((npast, npast), 0)
        jj = _iota((npast, npast), 1)
        jrow = _iota((1, npast), 1)
        groups = []
        for kvh in range(NSA_KV_HEADS):
            sel_rows = []
            for t in range(rq):
                c = kvh * NSA_GROUP * rq + t
                colv = imp_t[:, c:c + 1]
                rowv = imp[c:c + 1, :]
                beats = ((colv > rowv) | ((colv == rowv) & (jp < jj))) & (jp >= 1)
                rank = jnp.sum(jnp.where(beats, 1.0, 0.0), axis=0, keepdims=True)
                sel_rows.append(jnp.where((jrow == 0) | (rank < N_SELECT - 2), 1.0, 0.0))
            groups += [jnp.concatenate(sel_rows, axis=0)] * NSA_GROUP
        sel_s[...] = jnp.concatenate(groups, axis=0).T.astype(BF16)

    @pl.when(ph == 1)
    def _selected():
        @pl.when(gi == 0)
        def _():
            _flash_init_t(m_s, l_s, acc_s)

        k_all = jnp.concatenate([page_kind(pr, 2) for pr in pages], axis=0)
        v_all = jnp.concatenate([page_kind(pr, 3) for pr in pages], axis=0)
        pos_col = gi * keys_per_step + _iota((keys_per_step, 1), 0)
        sel_rows = sel_s[pl.ds(pl.multiple_of(gi * blocks_per_step, blocks_per_step), blocks_per_step), :]
        mask = jnp.dot(e0_ref[...], sel_rows, preferred_element_type=F32) > 0.5
        _flash_step_t(scores(k_all, pos_col), mask, v_all, m_s, l_s, acc_s, False)

    @pl.when((ph == 1) & (gi == N_PAGE_GROUPS - 1))
    def _finish():
        new = zn_ref[0, :, 0:768]
        npos_col = PAST_LEN + _iota((rq, 1), 0)
        mask_n = npos_col <= t_row
        _flash_step_t(scores(new[:, 256:384], npos_col), mask_n, new[:, 384:512], m_s, l_s, acc_s, False)
        o_sel = _flash_result_t(l_s, acc_s).T
        wpos_col = PAST_LEN - WINDOW + _iota((WINDOW, 1), 0)
        mask_w = (t_row - wpos_col) <= WINDOW
        s1 = jnp.where(mask_w, scores(win_ref[0, :, 0:128], wpos_col), NEG)
        s2 = jnp.where(mask_n, scores(new[:, 512:640], npos_col), NEG)
        m = jnp.maximum(jnp.max(s1, axis=0, keepdims=True), jnp.max(s2, axis=0, keepdims=True))
        p1 = jnp.where(mask_w, jnp.exp(s1 - m), 0.0)
        p2 = jnp.where(mask_n, jnp.exp(s2 - m), 0.0)
        den = jnp.sum(p1, axis=0, keepdims=True) + jnp.sum(p2, axis=0, keepdims=True)
        o_win = ((_mm_tn(win_ref[0, :, 128:256], p1) + _mm_tn(new[:, 640:768], p2)) / jnp.where(den > 0, den, 1.0)).T
        o_cmp = oc_s[...].T
        for h in range(NSA_HEADS):
            rs = slice(h * rq, (h + 1) * rq)
            ls = slice(64 * (h // NSA_GROUP), 64 * (h // NSA_GROUP) + 64)
            gates = _sigmoid(zn_ref[0, :, 768 + 3 * h:771 + 3 * h])
            o = gates[:, 0:1] * o_cmp[rs, ls] + gates[:, 1:2] * o_sel[rs, ls] + gates[:, 2:3] * o_win[rs, ls]
            o_ref[0, :, 64 * h:64 * h + 64] = o * _silu(zn_ref[0, :, 1536 + 64 * h:1600 + 64 * h])


def _nsa_sample_t(page_table, zn, win_state, cache, *, layer, n_pool):
    b = zn.shape[0]
    rq = SAMPLE_TPAD
    keys_per_step = PAGES_PER_STEP * PAGE_SIZE
    blocks_per_step = keys_per_step // SEL_BLOCK
    e0 = (jnp.arange(keys_per_step)[:, None] // SEL_BLOCK == jnp.arange(blocks_per_step)[None, :]).astype(BF16)

    def page_map(kidx):
        return lambda i, ph, gi, pt: (layer * n_pool + pt[i, gi * PAGES_PER_STEP + kidx], 0, 0)

    sq = lambda dt: pltpu.VMEM((SAMPLE_COLS, SAMPLE_COLS), dt)
    grid_spec = pltpu.PrefetchScalarGridSpec(
        num_scalar_prefetch=1,
        grid=(b, 2, N_PAGE_GROUPS),
        in_specs=[pl.BlockSpec((1, rq, ZN_W), lambda i, ph, gi, pt: (i, 0, 0)),
                  pl.BlockSpec((1, WINDOW, 256), lambda i, ph, gi, pt: (layer * b + i, 0, 0)),
                  pl.BlockSpec((keys_per_step, blocks_per_step), lambda i, ph, gi, pt: (0, 0))]
        + [pl.BlockSpec((1, PAGE_SIZE * PAGE_ROWS_PER_TOKEN, NSA_HD), page_map(kidx)) for kidx in range(PAGES_PER_STEP)],
        out_specs=pl.BlockSpec((1, rq, 512), lambda i, ph, gi, pt: (i, 0, 0)),
        scratch_shapes=[pltpu.VMEM((N_PAST_SEL, 128), F32) for _ in range(4)]
        + [sq(BF16), pltpu.VMEM((N_PAST_SEL, SAMPLE_COLS), BF16), sq(F32),
           pltpu.VMEM((1, SAMPLE_COLS), F32), pltpu.VMEM((1, SAMPLE_COLS), F32), sq(F32)],
    )
    return pl.pallas_call(
        _nsa_sample_t_kernel,
        grid_spec=grid_spec,
        out_shape=jax.ShapeDtypeStruct((b, rq, 512), F32),
        compiler_params=_cparams(("parallel", "arbitrary", "arbitrary")),
        name="nsa_sample",
    )(page_table, zn, win_state, e0, *([cache] * PAGES_PER_STEP))


def _merge_kernel(x_ref, bg_ref, br_ref, bn_ref, mg_ref, wbr_ref, wout_ref, lng_ref, lnb_ref, o_ref):
    acc = _sigmoid(mg_ref[:, 0:1024]) * _mm(bg_ref[...], wbr_ref[0])
    acc = acc + _sigmoid(mg_ref[:, 1024:2048]) * _mm(br_ref[...], wbr_ref[1])
    acc = acc + _sigmoid(mg_ref[:, 2048:3072]) * _mm(bn_ref[...], wbr_ref[2])
    xf = DN_ALPHA * x_ref[...] + _mm(acc, wout_ref[...])
    mu = jnp.mean(xf, axis=-1, keepdims=True)
    d = xf - mu
    var = jnp.mean(d * d, axis=-1, keepdims=True)
    o_ref[...] = d * lax.rsqrt(var + LN_EPS) * lng_ref[...] + lnb_ref[...]


def _merge(x, o_gla, o_rwkv, o_nsa, zm, w_br, w_out, ln_g, ln_b, tm):
    m = x.shape[0]
    row = lambda n: pl.BlockSpec((tm, n), lambda i: (i, 0))
    return pl.pallas_call(
        _merge_kernel,
        grid=(m // tm,),
        in_specs=[row(D_MODEL), row(512), row(512), row(512), row(ZM_W),
                  pl.BlockSpec((3, 512, D_MODEL), lambda i: (0, 0, 0)),
                  pl.BlockSpec((D_MODEL, D_MODEL), lambda i: (0, 0)),
                  pl.BlockSpec((1, D_MODEL), lambda i: (0, 0)),
                  pl.BlockSpec((1, D_MODEL), lambda i: (0, 0))],
        out_specs=row(D_MODEL),
        out_shape=jax.ShapeDtypeStruct((m, D_MODEL), F32),
        compiler_params=_cparams(("parallel",)),
        name="merge",
    )(x, o_gla, o_rwkv, o_nsa, zm, w_br, w_out, ln_g, ln_b)


def _pack_weights(w_in, b_in):
    def pack(a):
        z = lambda n: jnp.zeros(a.shape[:-1] + (n,), a.dtype)
        gla = jnp.concatenate([a[..., 0:1024], a[..., 1040:1552], a[..., 1024:1040], z(ZG_W - 1552)], axis=-1)
        rwkv = a[..., 1552:3728]
        nsa = jnp.concatenate([a[..., 4240:5008], a[..., 5008:5032], z(1024 - 792), a[..., 3728:4240],
                               a[..., 5032:5544]], axis=-1)
        mg = a[..., 5544:8616]
        return gla, rwkv, nsa, mg
    ws = [w.astype(BF16) for w in pack(w_in)]
    bs = [b[:, None, :] for b in pack(b_in)]
    return ws, bs


def kernel(x_prompt, x_sample, cache_nsa_kv, state_nsa_win, state_gla, state_rwkv, state_rwkv_shift, page_table,
           w_in, b_in, gla_a_up, gla_a_bias, gla_norm, rwkv_mu, rwkv_w0, rwkv_w_up, rwkv_a0, rwkv_a_up, rwkv_k_k,
           rwkv_k_a, rwkv_r_k, rwkv_ln_w, rwkv_ln_b, w_br, w_out, ln_g, ln_b):
    bp, tp, _ = x_prompt.shape
    bs, ts, _ = x_sample.shape
    n_pool = cache_nsa_kv.shape[1]
    ws, bws = _pack_weights(w_in, b_in)
    w_br_b = w_br.astype(BF16)
    w_out_b = w_out.astype(BF16)
    seg = (jnp.arange(RWKV_W)[:, None] // RWKV_HD == jnp.arange(RWKV_W)[None, :] // RWKV_HD).astype(BF16)
    cache = cache_nsa_kv.reshape(DEPTH * n_pool, PAGE_SIZE * PAGE_ROWS_PER_TOKEN, NSA_HD)
    win_state = state_nsa_win.reshape(DEPTH * bs, WINDOW, 256)
    row2 = lambda a: a.reshape(DEPTH, 1, -1)
    gla_a_bias2, gla_norm2 = row2(gla_a_bias), row2(gla_norm)
    r_par = [row2(rwkv_mu), row2(rwkv_w0), rwkv_w_up, row2(rwkv_a0), rwkv_a_up, row2(rwkv_k_k), row2(rwkv_k_a),
             row2(rwkv_r_k), row2(rwkv_ln_w), row2(rwkv_ln_b)]
    ln_g2, ln_b2 = row2(ln_g), row2(ln_b)

    xp = x_prompt.reshape(bp * tp, D_MODEL)
    xs = jnp.pad(x_sample, ((0, 0), (0, SAMPLE_TPAD - ts), (0, 0))).reshape(bs * SAMPLE_TPAD, D_MODEL)
    zeros_gla = jnp.zeros((bp, GLA_HEADS, GLA_DK, GLA_DV), F32)
    zeros_rwkv = jnp.zeros((bp, RWKV_HEADS, RWKV_HD, RWKV_HD), F32)
    zeros_shift = jnp.zeros((bp, 1, RWKV_IN), F32)

    outs = {k: [] for k in ("kv_p", "kv_s", "win_p", "win_s", "gla_p", "gla_s", "rwkv_p", "rwkv_s", "sh_p", "sh_s")}
    for l in range(DEPTH):
        rp = [p[l] for p in r_par]
        zg, zr, zn, zm = (_proj(xp, ws[i][l], bws[i][l], 512) for i in range(4))
        zn3 = zn.reshape(bp, tp, ZN_W)
        o_gla, gla_st = _gla(zg.reshape(bp, tp, ZG_W), zeros_gla, gla_a_up[l], gla_a_bias2[l], gla_norm2[l],
                             tb_rows=512, chunk=GLA_CHUNK, t_valid=None)
        zr3 = zr.reshape(bp, tp, RWKV_IN)
        o_rwkv, rwkv_st = _rwkv(zr3, zeros_rwkv, zeros_shift, seg, rp, tb_rows=512, chunk=RWKV_CHUNK, t_valid=None)
        o_nsa = _nsa_prompt_t(zn3, tq=256)
        xp = _merge(xp, o_gla.reshape(bp * tp, 512), o_rwkv.reshape(bp * tp, 512), o_nsa.reshape(bp * tp, 512), zm,
                    w_br_b[l], w_out_b[l], ln_g2[l], ln_b2[l], 512)
        outs["kv_p"].append(zn3[:, :, 0:512].reshape(bp, tp, 4, NSA_KV_HEADS, NSA_HD))
        outs["win_p"].append(zn3[:, tp - WINDOW:, 512:768].reshape(bp, WINDOW, 2, NSA_KV_HEADS, NSA_HD))
        outs["gla_p"].append(gla_st)
        outs["rwkv_p"].append(rwkv_st)
        outs["sh_p"].append(zr3[:, tp - 1, :])
        rows_s = bs * SAMPLE_TPAD
        zg, zr, zn, zm = (_proj(xs, ws[i][l], bws[i][l], rows_s) for i in range(4))
        zn3 = zn.reshape(bs, SAMPLE_TPAD, ZN_W)
        o_gla, gla_st = _gla(zg.reshape(bs, SAMPLE_TPAD, ZG_W), state_gla[l], gla_a_up[l], gla_a_bias2[l],
                             gla_norm2[l], tb_rows=SAMPLE_TPAD, chunk=SAMPLE_TPAD, t_valid=ts)
        zr3 = zr.reshape(bs, SAMPLE_TPAD, RWKV_IN)
        o_rwkv, rwkv_st = _rwkv(zr3, state_rwkv[l], state_rwkv_shift[l][:, None, :], seg, rp,
                                tb_rows=SAMPLE_TPAD, chunk=SAMPLE_TPAD, t_valid=ts)
        o_nsa = _nsa_sample_t(page_table, zn3, win_state, cache, layer=l, n_pool=n_pool)
        xs = _merge(xs, o_gla.reshape(rows_s, 512), o_rwkv.reshape(rows_s, 512), o_nsa.reshape(rows_s, 512), zm,
                    w_br_b[l], w_out_b[l], ln_g2[l], ln_b2[l], rows_s)
        outs["kv_s"].append(zn3[:, 0:ts, 0:512].reshape(bs, ts, 4, NSA_KV_HEADS, NSA_HD))
        new_win = zn3[:, 0:ts, 512:768].reshape(bs, ts, 2, NSA_KV_HEADS, NSA_HD)
        outs["win_s"].append(jnp.concatenate([state_nsa_win[l][:, ts:], new_win], axis=1))
        outs["gla_s"].append(gla_st)
        outs["rwkv_s"].append(rwkv_st)
        outs["sh_s"].append(zr3[:, ts - 1, :])

    st = lambda k: jnp.stack(outs[k])
    y_prompt = xp.reshape(bp, tp, D_MODEL)
    y_sample = xs.reshape(bs, SAMPLE_TPAD, D_MODEL)[:, 0:ts]
    return (y_prompt, y_sample, st("kv_p"), st("kv_s"), st("win_p"), st("win_s"), st("gla_p"), st("gla_s"),
            st("rwkv_p"), st("rwkv_s"), st("sh_p"), st("sh_s"))
```

```python
import functools

import jax
import jax.numpy as jnp
from jax import lax
from jax.experimental import pallas as pl
from jax.experimental.pallas import tpu as pltpu

F32 = jnp.float32
BF16 = jnp.bfloat16

D_MODEL = 1024
DEPTH = 2
PAST_LEN = 16384
PAGE_SIZE = 128
N_PAGES = PAST_LEN // PAGE_SIZE

GLA_HEADS, GLA_DK, GLA_DV = 4, 64, 128
GLA_K, GLA_V, GLA_LORA = 256, 512, 16
GLA_GATE_NORM = 16.0
GLA_CHUNK = 64
GLA_SUB = 16

RWKV_HEADS, RWKV_HD, RWKV_W = 8, 64, 512
RWKV_IN = 2176
RWKV_LN_EPS = 64e-5
RWKV_CHUNK = 64

NSA_HEADS, NSA_KV_HEADS, NSA_GROUP, NSA_HD = 8, 2, 4, 64
CMP_BLOCK, SEL_BLOCK, N_SELECT, WINDOW = 32, 64, 16, 512
FORCE_SCORE = 1e9
NEG = -1e30

DN_ALPHA = (2 * DEPTH) ** 0.25
LN_EPS = 1e-5
NORM_EPS = 1e-6

ZG_W = 1664
ZN_W = 2048
ZM_W = 3072
SAMPLE_TPAD = 16
VMEM_LIMIT = 56 * 1024 * 1024


def _mm(a, b):
    return jnp.dot(a.astype(BF16), b.astype(BF16), preferred_element_type=F32)


def _mm_nt(a, b):
    return lax.dot_general(a.astype(BF16), b.astype(BF16), (((1,), (1,)), ((), ())), preferred_element_type=F32)


def _mm_tn(a, b):
    return lax.dot_general(a.astype(BF16), b.astype(BF16), (((0,), (0,)), ((), ())), preferred_element_type=F32)


def _mm_exact(a, b):
    return jnp.dot(a, b, preferred_element_type=F32, precision=lax.Precision.HIGHEST)


def _split2(a):
    hi = a.astype(BF16)
    return hi, (a - hi.astype(F32)).astype(BF16)


def _mm_split(a, b01):
    hi, lo = _split2(a)
    return jnp.dot(hi, b01, preferred_element_type=F32) + jnp.dot(lo, b01, preferred_element_type=F32)


def _softplus(x):
    return jnp.maximum(x, 0.0) + jnp.log1p(jnp.exp(-jnp.abs(x)))


def _sigmoid(x):
    return 1.0 / (1.0 + jnp.exp(-x))


def _silu(x):
    return x * _sigmoid(x)


def _iota(shape, dim):
    return lax.broadcasted_iota(jnp.int32, shape, dim)


def _cparams(sem):
    return pltpu.CompilerParams(dimension_semantics=sem, vmem_limit_bytes=VMEM_LIMIT)


def _proj_kernel(x_ref, w_ref, b_ref, o_ref):
    o_ref[...] = _mm(x_ref[...], w_ref[...]) + b_ref[...]


def _proj(x, w, b, tm):
    m, k = x.shape
    n = w.shape[1]
    return pl.pallas_call(
        _proj_kernel,
        grid=(m // tm,),
        in_specs=[pl.BlockSpec((tm, k), lambda i: (i, 0)),
                  pl.BlockSpec((k, n), lambda i: (0, 0)),
                  pl.BlockSpec((1, n), lambda i: (0, 0))],
        out_specs=pl.BlockSpec((tm, n), lambda i: (i, 0)),
        out_shape=jax.ShapeDtypeStruct((m, n), F32),
        compiler_params=_cparams(("parallel",)),
        name="proj",
    )(x, w, b)


def _gla_kernel(zg_ref, s0_ref, aup_ref, abias_ref, norm_ref, o_ref, sout_ref, st_scr, *, tb_rows, chunk, t_valid):
    tb = pl.program_id(1)
    c_rows = chunk
    sub = min(GLA_SUB, c_rows)
    nsub = c_rows // sub

    @pl.when(tb == 0)
    def _():
        for h in range(GLA_HEADS):
            st_scr[h] = s0_ref[0, h].T

    tri = (_iota((c_rows, c_rows), 1) <= _iota((c_rows, c_rows), 0)).astype(F32)
    ones_red = jnp.ones((GLA_DK, 128), BF16)
    lane_s = _iota((sub, 128), 1)
    row_s = _iota((sub, 128), 0)
    col_c = _iota((sub, c_rows), 1)

    def chunk_body(c, carry):
        r0 = pl.multiple_of(c * c_rows, c_rows)
        z = zg_ref[0, pl.ds(r0, c_rows), :]
        q = z[:, 0:256] * (GLA_DK ** -0.5)
        k = z[:, 256:512]
        v = z[:, 512:1024]
        g = z[:, 1024:1536]
        ga = z[:, 1536:1552]
        la = -_softplus(-(_mm(ga, aup_ref[...]) + abias_ref[...])) * (1.0 / GLA_GATE_NORM)
        if t_valid is not None:
            ok = (tb * tb_rows + r0 + _iota((c_rows, 1), 0)) < t_valid
            la = jnp.where(ok, la, 0.0)
            k = jnp.where(ok, k, 0.0)
            v = jnp.where(ok, v, 0.0)
        cum = _mm_exact(tri, la)
        heads = range(GLA_HEADS)
        pairs = [(h, blk) for h in heads for blk in range(nsub)]
        qh = [q[:, 64 * h:64 * h + 64] for h in heads]
        kh = [k[:, 64 * h:64 * h + 64] for h in heads]
        ch = [cum[:, 64 * h:64 * h + 64] for h in heads]
        vh = [v[:, 128 * h:128 * h + 128] for h in heads]
        st = [st_scr[h] for h in heads]
        o_in = [_mm_nt(qh[h] * jnp.exp(ch[h]), st[h]) for h in heads]
        red, off = {}, {}
        for h, blk in pairs:
            sl = slice(blk * sub, (blk + 1) * sub)
            q_i, k_i, c_i = qh[h][sl], kh[h][sl], ch[h][sl]
            es = [q_i * k_i[j:j + 1] * jnp.exp(jnp.minimum(c_i - c_i[j:j + 1], 0.0)) for j in range(sub)]
            red[h, blk] = _mm(jnp.concatenate(es, axis=0), ones_red)
            if blk > 0:
                b_i = ch[h][blk * sub - 1:blk * sub]
                q_t = q_i * jnp.exp(c_i - b_i)
                k_t = kh[h] * jnp.exp(jnp.minimum(b_i - ch[h], 0.0))
                off[h, blk] = _mm_nt(q_t, k_t)
        att = []
        for h in heads:
            att_rows = []
            for blk in range(nsub):
                a_i = jnp.zeros((sub, 128), F32)
                for j in range(sub):
                    a_i = a_i + jnp.where((lane_s == blk * sub + j) & (row_s >= j),
                                          red[h, blk][j * sub:(j + 1) * sub], 0.0)
                a_i = a_i[:, 0:c_rows]
                if blk > 0:
                    a_i = a_i + jnp.where(col_c < blk * sub, off[h, blk], 0.0)
                att_rows.append(a_i)
            att.append(att_rows[0] if nsub == 1 else jnp.concatenate(att_rows, axis=0))
        o = [o_in[h] + _mm(att[h], vh[h]) for h in heads]
        last = [ch[h][c_rows - 1:c_rows] for h in heads]
        st_new = [st[h] * jnp.exp(last[h]) + _mm_tn(vh[h], kh[h] * jnp.exp(last[h] - ch[h])) for h in heads]
        for h in heads:
            st_scr[h] = st_new[h]
            oh = o[h] * lax.rsqrt(jnp.mean(o[h] * o[h], axis=-1, keepdims=True) + NORM_EPS)
            oh = oh * norm_ref[:, 128 * h:128 * h + 128] * _silu(g[:, 128 * h:128 * h + 128])
            o_ref[0, pl.ds(r0, c_rows), 128 * h:128 * h + 128] = oh
        return carry

    lax.fori_loop(0, tb_rows // c_rows, chunk_body, 0)

    @pl.when(tb == pl.num_programs(1) - 1)
    def _():
        for h in range(GLA_HEADS):
            sout_ref[0, h] = st_scr[h].T


def _gla(zg, s0, a_up, a_bias, norm_g, *, tb_rows, chunk, t_valid):
    b, t, _ = zg.shape
    kern = functools.partial(_gla_kernel, tb_rows=tb_rows, chunk=chunk, t_valid=t_valid)
    return pl.pallas_call(
        kern,
        grid=(b, t // tb_rows),
        in_specs=[pl.BlockSpec((1, tb_rows, ZG_W), lambda i, j: (i, j, 0)),
                  pl.BlockSpec((1, GLA_HEADS, GLA_DK, GLA_DV), lambda i, j: (i, 0, 0, 0)),
                  pl.BlockSpec((GLA_LORA, GLA_K), lambda i, j: (0, 0)),
                  pl.BlockSpec((1, GLA_K), lambda i, j: (0, 0)),
                  pl.BlockSpec((1, GLA_V), lambda i, j: (0, 0))],
        out_specs=[pl.BlockSpec((1, tb_rows, GLA_V), lambda i, j: (i, j, 0)),
                   pl.BlockSpec((1, GLA_HEADS, GLA_DK, GLA_DV), lambda i, j: (i, 0, 0, 0))],
        out_shape=[jax.ShapeDtypeStruct((b, t, GLA_V), F32),
                   jax.ShapeDtypeStruct((b, GLA_HEADS, GLA_DK, GLA_DV), F32)],
        scratch_shapes=[pltpu.VMEM((GLA_HEADS, GLA_DV, GLA_DK), F32)],
        compiler_params=_cparams(("parallel", "arbitrary")),
        name="gla",
    )(zg, s0, a_up, a_bias, norm_g)


def _rwkv_kernel(zr_ref, s0_ref, sh0_ref, seg_ref, mu_ref, w0_ref, wup_ref, a0_ref, aup_ref, kk_ref, ka_ref, rk_ref,
                 lnw_ref, lnb_ref, y_ref, sout_ref,
                 s_scr, prev_scr, lw_s, kk_s, kka_s, k2_s, r_s, v_s, y_s, *, tb_rows, chunk, t_valid):
    tb = pl.program_id(1)
    c_rows = chunk
    nh = RWKV_HEADS

    @pl.when(tb == 0)
    def _():
        s_scr[...] = s0_ref[0]
        prev_scr[...] = sh0_ref[0]

    z = zr_ref[0]
    rows = _iota((tb_rows, 1), 0)
    zp = jnp.where(rows == 0, prev_scr[...], pltpu.roll(z, 1, axis=0))
    prev_scr[...] = z[tb_rows - 1:tb_rows]
    zs = z + (zp - z) * mu_ref[...]
    r = zs[:, 0:512]
    k = zs[:, 512:1024]
    v = zs[:, 1024:1536]
    wl = zs[:, 1536:1600]
    al = zs[:, 1600:1664]
    w = -_softplus(-(w0_ref[...] + _mm(jnp.tanh(wl), wup_ref[...]))) - 0.5
    lw = -jnp.exp(w)
    a = _sigmoid(a0_ref[...] + _mm(al, aup_ref[...]))
    kk = k * kk_ref[...]
    kk = kk * lax.rsqrt(_mm_split(kk * kk, seg_ref[...]) + NORM_EPS)
    k2 = k * (1.0 + (a - 1.0) * ka_ref[...])
    kka = kk * a
    if t_valid is not None:
        ok = (tb * tb_rows + rows) < t_valid
        lw = jnp.where(ok, lw, 0.0)
        kka = jnp.where(ok, kka, 0.0)
        k2 = jnp.where(ok, k2, 0.0)
    lw_s[...] = lw
    kk_s[...] = kk
    kka_s[...] = kka
    k2_s[...] = k2
    r_s[...] = r
    v_s[...] = v

    ri = _iota((c_rows, c_rows), 0)
    ci = _iota((c_rows, c_rows), 1)
    tri = (ci <= ri).astype(F32)
    strict = ci < ri
    incl = ci <= ri
    n_dbl = max(1, (c_rows - 1).bit_length())

    def chunk_body(c, carry):
        r0 = pl.multiple_of(c * c_rows, c_rows)
        ds = pl.ds(r0, c_rows)
        lwc = lw_s[ds, :]
        cl = _mm_exact(tri, lwc)
        e_inv = jnp.exp(-cl)
        e_fwd = jnp.exp(cl)
        e_prev = jnp.exp(cl - lwc)
        e_end = jnp.exp(cl[c_rows - 1:c_rows] - cl)
        g_end = jnp.exp(cl[c_rows - 1:c_rows])
        kkc, kkac, k2c, rc, vc = kk_s[ds, :], kka_s[ds, :], k2_s[ds, :], r_s[ds, :], v_s[ds, :]
        heads = range(nh)
        hsl = [slice(64 * h, 64 * h + 64) for h in heads]
        a_t = [-kkac[:, hs] * e_inv[:, hs] for hs in hsl]
        b_t = [kkc[:, hs] * e_prev[:, hs] for hs in hsl]
        k_t = [k2c[:, hs] * e_inv[:, hs] for hs in hsl]
        r_t = [rc[:, hs] * e_fwd[:, hs] for hs in hsl]
        vh = [vc[:, hs] for hs in hsl]
        ak = [jnp.concatenate([a_t[h], k_t[h]], axis=0).astype(BF16) for h in heads]
        ba = [_mm_nt(b_t[h], ak[h]) for h in heads]
        ra = [_mm_nt(r_t[h], ak[h]) for h in heads]
        s0 = [s_scr[h] for h in heads]
        rs = [_mm_nt(r_t[h], s0[h]) for h in heads]
        l_k = [jnp.where(strict, ba[h][:, c_rows:2 * c_rows], 0.0) for h in heads]
        lkv = [_mm(l_k[h], vh[h]) for h in heads]
        x = [jnp.concatenate([b_t[h], lkv[h]], axis=1) for h in heads]
        lp = [jnp.where(strict, ba[h][:, 0:c_rows], 0.0).astype(BF16) for h in heads]
        for step in range(n_dbl):
            x = [x[h] + _mm(lp[h], x[h]) for h in heads]
            if step + 1 < n_dbl:
                lp = [_mm(lp[h], lp[h]).astype(BF16) for h in heads]
        u = [_mm_nt(x[h][:, 0:64], s0[h]) + x[h][:, 64:128] for h in heads]
        m_a = [jnp.where(incl, ra[h][:, 0:c_rows], 0.0) for h in heads]
        m_k = [jnp.where(incl, ra[h][:, c_rows:2 * c_rows], 0.0) for h in heads]
        y = [rs[h] + _mm(m_a[h], u[h]) + _mm(m_k[h], vh[h]) for h in heads]
        akg = [jnp.concatenate([-kkac[:, hs] * e_end[:, hs], k2c[:, hs] * e_end[:, hs]], axis=0) for hs in hsl]
        s_new = [s0[h] * g_end[:, hsl[h]] + _mm_tn(jnp.concatenate([u[h], vh[h]], axis=0), akg[h]) for h in heads]
        for h in heads:
            s_scr[h] = s_new[h]
            y_s[ds, hsl[h]] = y[h]
        return carry

    lax.fori_loop(0, tb_rows // c_rows, chunk_body, 0)

    y = y_s[...]
    seg = seg_ref[...]
    mean = _mm_split(y, seg) * (1.0 / RWKV_HD)
    d = y - mean
    var = _mm_split(d * d, seg) * (1.0 / RWKV_HD)
    yn = d * lax.rsqrt(var + RWKV_LN_EPS) * lnw_ref[...] + lnb_ref[...]
    bonus = _mm_split(r * k2 * rk_ref[...], seg) * v
    y_ref[0] = (yn + bonus) * _silu(zs[:, 1664:2176])

    @pl.when(tb == pl.num_programs(1) - 1)
    def _():
        sout_ref[0] = s_scr[...]


def _rwkv(zr, s0, sh0, seg, params, *, tb_rows, chunk, t_valid):
    b, t, _ = zr.shape
    kern = functools.partial(_rwkv_kernel, tb_rows=tb_rows, chunk=chunk, t_valid=t_valid)
    full = lambda shp: pl.BlockSpec(shp, lambda i, j: (0,) * len(shp))
    mu, w0, w_up, a0, a_up, k_k, k_a, r_k, ln_w, ln_b = params
    return pl.pallas_call(
        kern,
        grid=(b, t // tb_rows),
        in_specs=[pl.BlockSpec((1, tb_rows, RWKV_IN), lambda i, j: (i, j, 0)),
                  pl.BlockSpec((1, RWKV_HEADS, RWKV_HD, RWKV_HD), lambda i, j: (i, 0, 0, 0)),
                  pl.BlockSpec((1, 1, RWKV_IN), lambda i, j: (i, 0, 0)),
                  full((RWKV_W, RWKV_W)), full((1, RWKV_IN)), full((1, RWKV_W)), full((64, RWKV_W)),
                  full((1, RWKV_W)), full((64, RWKV_W)), full((1, RWKV_W)), full((1, RWKV_W)), full((1, RWKV_W)),
                  full((1, RWKV_W)), full((1, RWKV_W))],
        out_specs=[pl.BlockSpec((1, tb_rows, RWKV_W), lambda i, j: (i, j, 0)),
                   pl.BlockSpec((1, RWKV_HEADS, RWKV_HD, RWKV_HD), lambda i, j: (i, 0, 0, 0))],
        out_shape=[jax.ShapeDtypeStruct((b, t, RWKV_W), F32),
                   jax.ShapeDtypeStruct((b, RWKV_HEADS, RWKV_HD, RWKV_HD), F32)],
        scratch_shapes=[pltpu.VMEM((RWKV_HEADS, RWKV_HD, RWKV_HD), F32), pltpu.VMEM((1, RWKV_IN), F32)]
        + [pltpu.VMEM((tb_rows, RWKV_W), F32) for _ in range(7)],
        compiler_params=_cparams(("parallel", "arbitrary")),
        name="rwkv",
    )(zr, s0, sh0, seg, mu, w0, w_up, a0, a_up, k_k, k_a, r_k, ln_w, ln_b)


def _slope(h):
    return 2.0 ** (-8.0 * (h + 1) / NSA_HEADS)


def _flash_init(m_ref, l_ref, acc_ref):
    m_ref[...] = jnp.full(m_ref.shape, NEG, F32)
    l_ref[...] = jnp.zeros(l_ref.shape, F32)
    acc_ref[...] = jnp.zeros(acc_ref.shape, F32)


def _flash_step_t(s_t, mask, v, m_ref, l_ref, acc_ref):
    s_t = jnp.where(mask, s_t, NEG)
    m_old = m_ref[...]
    m_new = jnp.maximum(m_old, jnp.max(s_t, axis=0, keepdims=True))
    p = jnp.exp(s_t - m_new)
    alpha = jnp.exp(m_old - m_new)
    l_ref[...] = alpha * l_ref[...] + jnp.sum(p, axis=0, keepdims=True)
    acc_ref[...] = alpha * acc_ref[...] + _mm_tn(v, p)
    m_ref[...] = m_new


def _flash_result(l_ref, acc_ref):
    l = l_ref[...]
    return acc_ref[...] / jnp.where(l > 0, l, 1.0)


def _nsa_prompt_kernel(q_ref, kv_ref, bg_ref, g_ref, o_ref, kc_s, vc_s, ks_s, vs_s, kw_s, vw_s, qa_s, sel_s, oc_s,
                       m_s, l_s, acc_s, m2_s, l2_s, acc2_s, *, t_len, tq):
    i = pl.program_id(1)
    nb = t_len // SEL_BLOCK
    tk = tq
    ncol = NSA_GROUP * tq
    win_tiles = WINDOW // tk

    def aug_lanes(lane, pos):
        return jnp.where(lane == 64, lax.shift_right_logical(pos, 7).astype(F32),
                         jnp.where(lane == 65, (pos & 127).astype(F32), 0.0))

    def head_lanes(x, kvh):
        return x if kvh == 0 else pltpu.roll(x, 64, axis=1)

    @pl.when(i == 0)
    def _():
        lane = _iota((t_len, 128), 1)
        aug = aug_lanes(lane, _iota((t_len, 128), 0))
        for kcol, vcol, k_dst, v_dst in ((256, 384, ks_s, vs_s), (512, 640, kw_s, vw_s)):
            kf = kv_ref[0, :, kcol:kcol + 128]
            vf = kv_ref[0, :, vcol:vcol + 128]
            for kvh in range(NSA_KV_HEADS):
                k_dst[kvh] = jnp.where(lane < 64, head_lanes(kf, kvh), aug).astype(BF16)
                v_dst[kvh] = jnp.where(lane < 64, head_lanes(vf, kvh), 0.0).astype(BF16)
        kcm = kv_ref[0, :, 0:128].reshape(nb, SEL_BLOCK, 128)
        vcm = kv_ref[0, :, 128:256].reshape(nb, SEL_BLOCK, 128)
        inv = 1.0 / CMP_BLOCK
        kc = jnp.concatenate([jnp.sum(kcm[:, 0:CMP_BLOCK], axis=1) * inv,
                              jnp.sum(kcm[:, CMP_BLOCK:SEL_BLOCK], axis=1) * inv], axis=0)
        vc = jnp.concatenate([jnp.sum(vcm[:, 0:CMP_BLOCK], axis=1) * inv,
                              jnp.sum(vcm[:, CMP_BLOCK:SEL_BLOCK], axis=1) * inv], axis=0)
        lane_c = _iota((2 * nb, 128), 1)
        r_c = _iota((2 * nb, 128), 0)
        cend = jnp.where(r_c < nb, SEL_BLOCK * r_c + (CMP_BLOCK - 1), SEL_BLOCK * (r_c - nb) + (SEL_BLOCK - 1))
        aug_c = aug_lanes(lane_c, cend)
        for kvh in range(NSA_KV_HEADS):
            kc_s[kvh] = jnp.where(lane_c < 64, head_lanes(kc, kvh), aug_c).astype(BF16)
            vc_s[kvh] = jnp.where(lane_c < 64, head_lanes(vc, kvh), 0.0).astype(BF16)

    lane_q = _iota((tq, 128), 1)
    for h in range(NSA_HEADS):
        qb = q_ref[0, :, 128 * (h // 2):128 * (h // 2) + 128]
        if h % 2:
            qb = pltpu.roll(qb, 64, axis=1)
        sl = _slope(h)
        qa = jnp.where(lane_q < 64, qb * (NSA_HD ** -0.5),
                       jnp.where(lane_q == 64, 128.0 * sl, jnp.where(lane_q == 65, sl, 0.0)))
        qa_s[h // NSA_GROUP, (h % NSA_GROUP) * tq:(h % NSA_GROUP + 1) * tq, :] = qa.astype(BF16)

    t_row = i * tq + (_iota((1, ncol), 1) & (tq - 1))
    t_row1 = i * tq + _iota((1, tq), 1)
    r_c1 = _iota((2 * nb, 1), 0)
    cend_col = jnp.where(r_c1 < nb, SEL_BLOCK * r_c1 + (CMP_BLOCK - 1), SEL_BLOCK * (r_c1 - nb) + (SEL_BLOCK - 1))
    mask_c = cend_col <= t_row
    jrow = _iota((nb, 1), 0)
    tblk = lax.shift_right_logical(t_row1, 6)
    forced = (jrow == tblk) | (jrow == 0)
    valid = jrow <= tblk
    bg_t = bg_ref[0].T

    for kvh in range(NSA_KV_HEADS):
        qa = qa_s[kvh]
        s_c = jnp.where(mask_c, _mm_nt(kc_s[kvh], qa), NEG)
        m = jnp.max(s_c, axis=0, keepdims=True)
        p = jnp.where(mask_c, jnp.exp(s_c - m), 0.0)
        den = jnp.sum(p, axis=0, keepdims=True)
        p = p / jnp.where(den > 0, den, 1.0)
        oc_s[...] = _mm_tn(vc_s[kvh], p)
        imp_e = p[0:nb, 0:tq]
        imp_o = p[nb:2 * nb, 0:tq]
        for g in range(1, NSA_GROUP):
            imp_e = imp_e + p[0:nb, g * tq:(g + 1) * tq]
            imp_o = imp_o + p[nb:2 * nb, g * tq:(g + 1) * tq]
        score = jnp.where(forced, FORCE_SCORE, jnp.where(valid, imp_e + imp_o, -FORCE_SCORE))
        rank = jnp.zeros((nb, tq), F32)
        for jp in range(nb):
            rj = score[jp:jp + 1, :]
            rank = rank + jnp.where((rj > score) | ((rj == score) & (jp < jrow)), 1.0, 0.0)
        sel = jnp.where(rank < N_SELECT, 1.0, 0.0).astype(BF16)
        sel_s[...] = jnp.concatenate([sel] * NSA_GROUP, axis=1)

        _flash_init(m_s, l_s, acc_s)
        _flash_init(m2_s, l2_s, acc2_s)

        def sel_body(kt, carry):
            k0 = pl.multiple_of(kt * tk, tk)
            pos_col = kt * tk + _iota((tk, 1), 0)
            onehot = jnp.where(_iota((tk, nb), 1) == lax.shift_right_logical(pos_col, 6), 1.0, 0.0).astype(BF16)
            mask = (jnp.dot(onehot, sel_s[...], preferred_element_type=F32) > 0.5) & (pos_col <= t_row)
            s_t = _mm_nt(ks_s[kvh, pl.ds(k0, tk), :], qa_s[kvh])
            _flash_step_t(s_t, mask, vs_s[kvh, pl.ds(k0, tk), :], m_s, l_s, acc_s)
            return carry

        lax.fori_loop(0, i + 1, sel_body, 0)

        def win_body(kt, carry):
            k0 = pl.multiple_of(kt * tk, tk)
            pos_col = kt * tk + _iota((tk, 1), 0)
            dist = t_row - pos_col
            mask = (dist >= 0) & (dist <= WINDOW)
            s_t = _mm_nt(kw_s[kvh, pl.ds(k0, tk), :], qa_s[kvh])
            _flash_step_t(s_t, mask, vw_s[kvh, pl.ds(k0, tk), :], m2_s, l2_s, acc2_s)
            return carry

        lax.fori_loop(jnp.maximum(i - win_tiles, 0), i + 1, win_body, 0)

        o_sel = _flash_result(l_s, acc_s)
        o_win = _flash_result(l2_s, acc2_s)
        o_cmp = oc_s[...]
        for g in range(NSA_GROUP):
            h = NSA_GROUP * kvh + g
            cs = slice(g * tq, (g + 1) * tq)
            gates = _sigmoid(bg_t[3 * h:3 * h + 3, :])
            comb = gates[0:1] * o_cmp[:, cs] + gates[1:2] * o_sel[:, cs] + gates[2:3] * o_win[:, cs]
            o_ref[0, :, 64 * h:64 * h + 64] = comb.T[:, 0:64] * _silu(g_ref[0, :, 64 * h:64 * h + 64])


def _nsa_prompt(zn, *, tq):
    b, t, _ = zn.shape
    nb = t // SEL_BLOCK
    ncol = NSA_GROUP * tq
    kern = functools.partial(_nsa_prompt_kernel, t_len=t, tq=tq)
    stat = lambda: [pltpu.VMEM((1, ncol), F32), pltpu.VMEM((1, ncol), F32), pltpu.VMEM((128, ncol), F32)]
    kvbuf = lambda n: pltpu.VMEM((NSA_KV_HEADS, n, 128), BF16)
    return pl.pallas_call(
        kern,
        grid=(b, t // tq),
        in_specs=[pl.BlockSpec((1, tq, 512), lambda i, j: (i, j, 2)),
                  pl.BlockSpec((1, t, 768), lambda i, j: (i, 0, 0)),
                  pl.BlockSpec((1, tq, 256), lambda i, j: (i, j, 3)),
                  pl.BlockSpec((1, tq, 512), lambda i, j: (i, j, 3))],
        out_specs=pl.BlockSpec((1, tq, 512), lambda i, j: (i, j, 0)),
        out_shape=jax.ShapeDtypeStruct((b, t, 512), F32),
        scratch_shapes=[kvbuf(2 * nb), kvbuf(2 * nb), kvbuf(t), kvbuf(t), kvbuf(t), kvbuf(t), kvbuf(ncol),
                        pltpu.VMEM((nb, ncol), BF16), pltpu.VMEM((128, ncol), F32)] + stat() + stat(),
        compiler_params=_cparams(("parallel", "arbitrary")),
        name="nsa_prompt",
    )(zn, zn, zn, zn)


PAGES_PER_STEP = 16
N_PAGE_GROUPS = N_PAGES // PAGES_PER_STEP
KEYS_PER_STEP = PAGES_PER_STEP * PAGE_SIZE
SEL_PER_STEP = KEYS_PER_STEP // SEL_BLOCK
N_PAST_SEL = PAST_LEN // SEL_BLOCK
SAMPLE_COLS = NSA_HEADS * SAMPLE_TPAD
HALF_PAGE_ROWS = 2 * NSA_KV_HEADS * NSA_HD


def _flash_step_r(s, mask, pv, m_ref, l_ref, acc_ref):
    s = jnp.where(mask, s, NEG)
    m_old = m_ref[...]
    m_new = jnp.maximum(m_old, jnp.max(s, axis=-1, keepdims=True))
    p = jnp.exp(s - m_new)
    alpha = jnp.exp(m_old - m_new)
    l_ref[...] = alpha * l_ref[...] + jnp.sum(p, axis=-1, keepdims=True)
    acc_ref[...] = alpha * acc_ref[...] + pv(p)
    m_ref[...] = m_new


def _nsa_sample_kernel(pt_ref, zn_ref, win_ref, pool_ref, e0_ref, *rest):
    pages = rest[:PAGES_PER_STEP]
    o_ref = rest[PAGES_PER_STEP]
    kce_s, kco_s, vce_s, vco_s, qt_s, sel_s, oc_s, m_s, l_s, acc_s = rest[PAGES_PER_STEP + 1:]
    ph = pl.program_id(1)
    gi = pl.program_id(2)
    rq = SAMPLE_TPAD
    npast = N_PAST_SEL
    col = _iota((SAMPLE_COLS, 1), 0)
    t_col = PAST_LEN + (col & (rq - 1))
    hcol = lax.shift_right_logical(col, 4)
    slope_col = jnp.zeros((SAMPLE_COLS, 1), F32)
    for h in range(NSA_HEADS):
        slope_col = jnp.where(hcol == h, _slope(h), slope_col)

    def bias(pos_row):
        return slope_col * (t_col - pos_row).astype(F32)

    def keys_t(lo):
        return jnp.concatenate([pg[0, lo:lo + 128, :] for pg in pages], axis=1)

    @pl.when((ph == 0) & (gi == 0))
    def _queries():
        zero = jnp.zeros((rq, NSA_HD), F32)
        rows = []
        for h in range(NSA_HEADS):
            qh = zn_ref[0, :, 1024 + 64 * h:1088 + 64 * h] * (NSA_HD ** -0.5)
            rows.append(jnp.concatenate([qh, zero] if h < NSA_GROUP else [zero, qh], axis=1))
        qt_s[...] = jnp.concatenate(rows, axis=0).astype(BF16)

    @pl.when(ph == 0)
    def _pool():
        dn = (((1,), (1,)), ((), ()))
        dst_e = pl.ds(pl.multiple_of(gi * SEL_PER_STEP, SEL_PER_STEP), SEL_PER_STEP)
        for lo, even_s, odd_s in ((0, kce_s, kco_s), (128, vce_s, vco_s)):
            hi_part, lo_part = _split2(keys_t(lo))
            pooled = (lax.dot_general(pool_ref[...], hi_part, dn, preferred_element_type=F32)
                      + lax.dot_general(pool_ref[...], lo_part, dn, preferred_element_type=F32)) * (1.0 / CMP_BLOCK)
            even_s[dst_e, :] = pooled[0:SEL_PER_STEP]
            odd_s[dst_e, :] = pooled[SEL_PER_STEP:2 * SEL_PER_STEP]

    @pl.when((ph == 0) & (gi == N_PAGE_GROUPS - 1))
    def _compressed():
        jrow = _iota((1, npast), 1)
        qt = qt_s[...]
        se = _mm_nt(qt, kce_s[...]) - bias(SEL_BLOCK * jrow + (CMP_BLOCK - 1))
        so = _mm_nt(qt, kco_s[...]) - bias(SEL_BLOCK * jrow + (SEL_BLOCK - 1))
        m = jnp.maximum(jnp.max(se, axis=-1, keepdims=True), jnp.max(so, axis=-1, keepdims=True))
        pe = jnp.exp(se - m)
        po = jnp.exp(so - m)
        den = jnp.sum(pe, axis=-1, keepdims=True) + jnp.sum(po, axis=-1, keepdims=True)
        pe = pe / den
        po = po / den
        oc_s[...] = _mm(pe, vce_s[...]) + _mm(po, vco_s[...])
        jp = _iota((npast, npast), 0)
        jj = _iota((npast, npast), 1)
        groups = []
        for kvh in range(NSA_KV_HEADS):
            base = kvh * NSA_GROUP * rq
            imp_e = pe[base:base + rq]
            imp_o = po[base:base + rq]
            for g in range(1, NSA_GROUP):
                imp_e = imp_e + pe[base + g * rq:base + (g + 1) * rq]
                imp_o = imp_o + po[base + g * rq:base + (g + 1) * rq]
            imp = imp_e + imp_o
            imp_t = jnp.concatenate([imp, jnp.zeros((128 - rq, npast), F32)], axis=0).T
            sel_rows = []
            for t in range(rq):
                colv = imp_t[:, t:t + 1]
                rowv = imp[t:t + 1, :]
                beats = ((colv > rowv) | ((colv == rowv) & (jp < jj))) & (jp >= 1)
                rank = jnp.sum(jnp.where(beats, 1.0, 0.0), axis=0, keepdims=True)
                sel_rows.append(jnp.where((jrow == 0) | (rank < N_SELECT - 2), 1.0, 0.0))
            groups += [jnp.concatenate(sel_rows, axis=0)] * NSA_GROUP
        sel_s[...] = jnp.concatenate(groups, axis=0).T.astype(BF16)

    @pl.when(ph == 1)
    def _selected():
        @pl.when(gi == 0)
        def _():
            _flash_init(m_s, l_s, acc_s)

        k_t = keys_t(0)
        v_t = keys_t(128)
        pos_row = gi * KEYS_PER_STEP + _iota((1, KEYS_PER_STEP), 1)
        s = _mm(qt_s[...], k_t) - bias(pos_row)
        sel_rows = sel_s[pl.ds(pl.multiple_of(gi * SEL_PER_STEP, SEL_PER_STEP), SEL_PER_STEP), :]
        mask = _mm_tn(sel_rows, e0_ref[...]) > 0.5
        _flash_step_r(s, mask, lambda p: _mm_nt(p, v_t), m_s, l_s, acc_s)

    @pl.when((ph == 1) & (gi == N_PAGE_GROUPS - 1))
    def _finish():
        qt = qt_s[...]
        new = zn_ref[0, :, 0:768]
        npos_row = PAST_LEN + _iota((1, rq), 1)
        mask_n = npos_row <= t_col
        s = _mm_nt(qt, new[:, 256:384]) - bias(npos_row)
        _flash_step_r(s, mask_n, lambda p: _mm(p, new[:, 384:512]), m_s, l_s, acc_s)
        l = l_s[...]
        o_sel = acc_s[...] / jnp.where(l > 0, l, 1.0)
        wpos_row = PAST_LEN - WINDOW + _iota((1, WINDOW), 1)
        mask_w = (t_col - wpos_row) <= WINDOW
        s1 = jnp.where(mask_w, _mm(qt, win_ref[0, 0:128, :]) - bias(wpos_row), NEG)
        s2 = jnp.where(mask_n, _mm_nt(qt, new[:, 512:640]) - bias(npos_row), NEG)
        m = jnp.maximum(jnp.max(s1, axis=-1, keepdims=True), jnp.max(s2, axis=-1, keepdims=True))
        p1 = jnp.where(mask_w, jnp.exp(s1 - m), 0.0)
        p2 = jnp.where(mask_n, jnp.exp(s2 - m), 0.0)
        den = jnp.sum(p1, axis=-1, keepdims=True) + jnp.sum(p2, axis=-1, keepdims=True)
        o_win = (_mm_nt(p1, win_ref[0, 128:256, :]) + _mm(p2, new[:, 640:768])) / jnp.where(den > 0, den, 1.0)
        o_cmp = oc_s[...]
        for h in range(NSA_HEADS):
            rs = slice(h * rq, (h + 1) * rq)
            ls = slice(64 * (h // NSA_GROUP), 64 * (h // NSA_GROUP) + 64)
            gates = _sigmoid(zn_ref[0, :, 768 + 3 * h:771 + 3 * h])
            o = gates[:, 0:1] * o_cmp[rs, ls] + gates[:, 1:2] * o_sel[rs, ls] + gates[:, 2:3] * o_win[rs, ls]
            o_ref[0, :, 64 * h:64 * h + 64] = o * _silu(zn_ref[0, :, 1536 + 64 * h:1600 + 64 * h])


def _nsa_sample(page_table, zn, win_t, cache_t, *, layer, n_pool):
    b = zn.shape[0]
    rq = SAMPLE_TPAD
    tok = jnp.arange(KEYS_PER_STEP)
    blk = jnp.arange(SEL_PER_STEP)
    in_blk = tok[None, :] // SEL_BLOCK == blk[:, None]
    first_half = (tok[None, :] % SEL_BLOCK) < CMP_BLOCK
    pool = jnp.concatenate([in_blk & first_half, in_blk & ~first_half], axis=0).astype(BF16)
    e0 = in_blk.astype(BF16)

    def page_map(kidx):
        return lambda i, ph, gi, pt: (layer * n_pool + pt[i, gi * PAGES_PER_STEP + kidx], ph, 0)

    const = lambda a: pl.BlockSpec(a.shape, lambda i, ph, gi, pt: (0, 0))
    sq = lambda dt: pltpu.VMEM((SAMPLE_COLS, SAMPLE_COLS), dt)
    grid_spec = pltpu.PrefetchScalarGridSpec(
        num_scalar_prefetch=1,
        grid=(b, 2, N_PAGE_GROUPS),
        in_specs=[pl.BlockSpec((1, rq, ZN_W), lambda i, ph, gi, pt: (i, 0, 0)),
                  pl.BlockSpec((1, HALF_PAGE_ROWS, WINDOW), lambda i, ph, gi, pt: (layer * b + i, 0, 0)),
                  const(pool), const(e0)]
        + [pl.BlockSpec((1, HALF_PAGE_ROWS, PAGE_SIZE), page_map(kidx)) for kidx in range(PAGES_PER_STEP)],
        out_specs=pl.BlockSpec((1, rq, 512), lambda i, ph, gi, pt: (i, 0, 0)),
        scratch_shapes=[pltpu.VMEM((N_PAST_SEL, 128), F32) for _ in range(4)]
        + [sq(BF16), pltpu.VMEM((N_PAST_SEL, SAMPLE_COLS), BF16), sq(F32),
           pltpu.VMEM((SAMPLE_COLS, 1), F32), pltpu.VMEM((SAMPLE_COLS, 1), F32), sq(F32)],
    )
    return pl.pallas_call(
        _nsa_sample_kernel,
        grid_spec=grid_spec,
        out_shape=jax.ShapeDtypeStruct((b, rq, 512), F32),
        compiler_params=_cparams(("parallel", "arbitrary", "arbitrary")),
        name="nsa_sample",
    )(page_table, zn, win_t, pool, e0, *([cache_t] * PAGES_PER_STEP))


def _merge_kernel(x_ref, bg_ref, br_ref, bn_ref, mg_ref, wbr_ref, wout_ref, lng_ref, lnb_ref, o_ref):
    acc = _sigmoid(mg_ref[:, 0:1024]) * _mm(bg_ref[...], wbr_ref[0])
    acc = acc + _sigmoid(mg_ref[:, 1024:2048]) * _mm(br_ref[...], wbr_ref[1])
    acc = acc + _sigmoid(mg_ref[:, 2048:3072]) * _mm(bn_ref[...], wbr_ref[2])
    xf = DN_ALPHA * x_ref[...] + _mm(acc, wout_ref[...])
    mu = jnp.mean(xf, axis=-1, keepdims=True)
    d = xf - mu
    var = jnp.mean(d * d, axis=-1, keepdims=True)
    o_ref[...] = d * lax.rsqrt(var + LN_EPS) * lng_ref[...] + lnb_ref[...]


def _merge(x, o_gla, o_rwkv, o_nsa, zm, w_br, w_out, ln_g, ln_b, tm):
    m = x.shape[0]
    row = lambda n: pl.BlockSpec((tm, n), lambda i: (i, 0))
    return pl.pallas_call(
        _merge_kernel,
        grid=(m // tm,),
        in_specs=[row(D_MODEL), row(512), row(512), row(512), row(ZM_W),
                  pl.BlockSpec((3, 512, D_MODEL), lambda i: (0, 0, 0)),
                  pl.BlockSpec((D_MODEL, D_MODEL), lambda i: (0, 0)),
                  pl.BlockSpec((1, D_MODEL), lambda i: (0, 0)),
                  pl.BlockSpec((1, D_MODEL), lambda i: (0, 0))],
        out_specs=row(D_MODEL),
        out_shape=jax.ShapeDtypeStruct((m, D_MODEL), F32),
        compiler_params=_cparams(("parallel",)),
        name="merge",
    )(x, o_gla, o_rwkv, o_nsa, zm, w_br, w_out, ln_g, ln_b)


def _pack_weights(w_in, b_in):
    def pack(a):
        z = lambda n: jnp.zeros(a.shape[:-1] + (n,), a.dtype)
        gla = jnp.concatenate([a[..., 0:1024], a[..., 1040:1552], a[..., 1024:1040], z(ZG_W - 1552)], axis=-1)
        rwkv = a[..., 1552:3728]
        nsa = jnp.concatenate([a[..., 4240:5008], a[..., 5008:5032], z(1024 - 792), a[..., 3728:4240],
                               a[..., 5032:5544]], axis=-1)
        mg = a[..., 5544:8616]
        return gla, rwkv, nsa, mg
    ws = [w.astype(BF16) for w in pack(w_in)]
    bs = [b[:, None, :] for b in pack(b_in)]
    return ws, bs


def kernel(x_prompt, x_sample, cache_nsa_kv, state_nsa_win, state_gla, state_rwkv, state_rwkv_shift, page_table,
           w_in, b_in, gla_a_up, gla_a_bias, gla_norm, rwkv_mu, rwkv_w0, rwkv_w_up, rwkv_a0, rwkv_a_up, rwkv_k_k,
           rwkv_k_a, rwkv_r_k, rwkv_ln_w, rwkv_ln_b, w_br, w_out, ln_g, ln_b):
    bp, tp, _ = x_prompt.shape
    bs, ts, _ = x_sample.shape
    n_pool = cache_nsa_kv.shape[1]
    ws, bws = _pack_weights(w_in, b_in)
    w_br_b = w_br.astype(BF16)
    w_out_b = w_out.astype(BF16)
    seg = (jnp.arange(RWKV_W)[:, None] // RWKV_HD == jnp.arange(RWKV_W)[None, :] // RWKV_HD).astype(BF16)
    cache_t = jnp.transpose(cache_nsa_kv, (0, 1, 3, 4, 5, 2)).reshape(DEPTH * n_pool, 8 * NSA_HD, PAGE_SIZE)
    win_t = jnp.transpose(state_nsa_win, (0, 1, 3, 4, 5, 2)).reshape(DEPTH * bs, HALF_PAGE_ROWS, WINDOW)
    row2 = lambda a: a.reshape(DEPTH, 1, -1)
    gla_a_bias2, gla_norm2 = row2(gla_a_bias), row2(gla_norm)
    r_par = [row2(rwkv_mu), row2(rwkv_w0), rwkv_w_up, row2(rwkv_a0), rwkv_a_up, row2(rwkv_k_k), row2(rwkv_k_a),
             row2(rwkv_r_k), row2(rwkv_ln_w), row2(rwkv_ln_b)]
    ln_g2, ln_b2 = row2(ln_g), row2(ln_b)

    xp = x_prompt.reshape(bp * tp, D_MODEL)
    xs = jnp.pad(x_sample, ((0, 0), (0, SAMPLE_TPAD - ts), (0, 0))).reshape(bs * SAMPLE_TPAD, D_MODEL)
    zeros_gla = jnp.zeros((bp, GLA_HEADS, GLA_DK, GLA_DV), F32)
    zeros_rwkv = jnp.zeros((bp, RWKV_HEADS, RWKV_HD, RWKV_HD), F32)
    zeros_shift = jnp.zeros((bp, 1, RWKV_IN), F32)

    outs = {k: [] for k in ("kv_p", "kv_s", "win_p", "win_s", "gla_p", "gla_s", "rwkv_p", "rwkv_s", "sh_p", "sh_s")}
    for l in range(DEPTH):
        rp = [p[l] for p in r_par]
        zg, zr, zn, zm = (_proj(xp, ws[i][l], bws[i][l], 512) for i in range(4))
        zn3 = zn.reshape(bp, tp, ZN_W)
        o_gla, gla_st = _gla(zg.reshape(bp, tp, ZG_W), zeros_gla, gla_a_up[l], gla_a_bias2[l], gla_norm2[l],
                             tb_rows=512, chunk=GLA_CHUNK, t_valid=None)
        zr3 = zr.reshape(bp, tp, RWKV_IN)
        o_rwkv, rwkv_st = _rwkv(zr3, zeros_rwkv, zeros_shift, seg, rp, tb_rows=512, chunk=RWKV_CHUNK, t_valid=None)
        o_nsa = _nsa_prompt(zn3, tq=256)
        xp = _merge(xp, o_gla.reshape(bp * tp, 512), o_rwkv.reshape(bp * tp, 512), o_nsa.reshape(bp * tp, 512), zm,
                    w_br_b[l], w_out_b[l], ln_g2[l], ln_b2[l], 512)
        outs["kv_p"].append(zn3[:, :, 0:512].reshape(bp, tp, 4, NSA_KV_HEADS, NSA_HD))
        outs["win_p"].append(zn3[:, tp - WINDOW:, 512:768].reshape(bp, WINDOW, 2, NSA_KV_HEADS, NSA_HD))
        outs["gla_p"].append(gla_st)
        outs["rwkv_p"].append(rwkv_st)
        outs["sh_p"].append(zr3[:, tp - 1, :])
        rows_s = bs * SAMPLE_TPAD
        zg, zr, zn, zm = (_proj(xs, ws[i][l], bws[i][l], rows_s) for i in range(4))
        zn3 = zn.reshape(bs, SAMPLE_TPAD, ZN_W)
        o_gla, gla_st = _gla(zg.reshape(bs, SAMPLE_TPAD, ZG_W), state_gla[l], gla_a_up[l], gla_a_bias2[l],
                             gla_norm2[l], tb_rows=SAMPLE_TPAD, chunk=SAMPLE_TPAD, t_valid=ts)
        zr3 = zr.reshape(bs, SAMPLE_TPAD, RWKV_IN)
        o_rwkv, rwkv_st = _rwkv(zr3, state_rwkv[l], state_rwkv_shift[l][:, None, :], seg, rp,
                                tb_rows=SAMPLE_TPAD, chunk=SAMPLE_TPAD, t_valid=ts)
        o_nsa = _nsa_sample(page_table, zn3, win_t, cache_t, layer=l, n_pool=n_pool)
        xs = _merge(xs, o_gla.reshape(rows_s, 512), o_rwkv.reshape(rows_s, 512), o_nsa.reshape(rows_s, 512), zm,
                    w_br_b[l], w_out_b[l], ln_g2[l], ln_b2[l], rows_s)
        outs["kv_s"].append(zn3[:, 0:ts, 0:512].reshape(bs, ts, 4, NSA_KV_HEADS, NSA_HD))
        new_win = zn3[:, 0:ts, 512:768].reshape(bs, ts, 2, NSA_KV_HEADS, NSA_HD)
        outs["win_s"].append(jnp.concatenate([state_nsa_win[l][:, ts:], new_win], axis=1))
        outs["gla_s"].append(gla_st)
        outs["rwkv_s"].append(rwkv_st)
        outs["sh_s"].append(zr3[:, ts - 1, :])

    st = lambda k: jnp.stack(outs[k])
    y_prompt = xp.reshape(bp, tp, D_MODEL)
    y_sample = xs.reshape(bs, SAMPLE_TPAD, D_MODEL)[:, 0:ts]
    return (y_prompt, y_sample, st("kv_p"), st("kv_s"), st("win_p"), st("win_s"), st("gla_p"), st("gla_s"),
            st("rwkv_p"), st("rwkv_s"), st("sh_p"), st("sh_s"))
```

```python
import functools

import jax
import jax.numpy as jnp
from jax import lax
from jax.experimental import pallas as pl
from jax.experimental.pallas import tpu as pltpu

F32 = jnp.float32
BF16 = jnp.bfloat16

D_MODEL = 1024
DEPTH = 2
PAST_LEN = 16384
PAGE_SIZE = 128
N_PAGES = PAST_LEN // PAGE_SIZE

GLA_HEADS, GLA_DK, GLA_DV = 4, 64, 128
GLA_K, GLA_V, GLA_LORA = 256, 512, 16
GLA_GATE_NORM = 16.0
GLA_CHUNK = 64
GLA_SUB = 16

RWKV_HEADS, RWKV_HD, RWKV_W = 8, 64, 512
RWKV_IN = 2176
RWKV_LN_EPS = 64e-5
RWKV_CHUNK = 64

NSA_HEADS, NSA_KV_HEADS, NSA_GROUP, NSA_HD = 8, 2, 4, 64
CMP_BLOCK, SEL_BLOCK, N_SELECT, WINDOW = 32, 64, 16, 512
FORCE_SCORE = 1e9
NEG = -1e30

DN_ALPHA = (2 * DEPTH) ** 0.25
LN_EPS = 1e-5
NORM_EPS = 1e-6

ZG_W = 1664
ZN_W = 2048
ZM_W = 3072
SAMPLE_TPAD = 16
VMEM_LIMIT = 56 * 1024 * 1024


def _mm(a, b):
    return jnp.dot(a.astype(BF16), b.astype(BF16), preferred_element_type=F32)


def _mm_nt(a, b):
    return lax.dot_general(a.astype(BF16), b.astype(BF16), (((1,), (1,)), ((), ())), preferred_element_type=F32)


def _mm_tn(a, b):
    return lax.dot_general(a.astype(BF16), b.astype(BF16), (((0,), (0,)), ((), ())), preferred_element_type=F32)


def _mm_exact(a, b):
    return jnp.dot(a, b, preferred_element_type=F32, precision=lax.Precision.HIGHEST)


def _split2(a):
    hi = a.astype(BF16)
    return hi, (a - hi.astype(F32)).astype(BF16)


def _mm_split(a, b01):
    hi, lo = _split2(a)
    return jnp.dot(hi, b01, preferred_element_type=F32) + jnp.dot(lo, b01, preferred_element_type=F32)


def _softplus(x):
    return jnp.maximum(x, 0.0) + jnp.log(1.0 + jnp.exp(-jnp.abs(x)))


def _sigmoid(x):
    return 0.5 * jnp.tanh(0.5 * x) + 0.5


def _silu(x):
    return x * _sigmoid(x)


def _iota(shape, dim):
    return lax.broadcasted_iota(jnp.int32, shape, dim)


def _cparams(sem):
    return pltpu.CompilerParams(dimension_semantics=sem, vmem_limit_bytes=VMEM_LIMIT)


def _proj_kernel(x_ref, w_ref, b_ref, o_ref):
    o_ref[...] = _mm(x_ref[...], w_ref[...]) + b_ref[...]


def _proj(x, w, b, tm):
    m, k = x.shape
    n = w.shape[1]
    return pl.pallas_call(
        _proj_kernel,
        grid=(m // tm,),
        in_specs=[pl.BlockSpec((tm, k), lambda i: (i, 0)),
                  pl.BlockSpec((k, n), lambda i: (0, 0)),
                  pl.BlockSpec((1, n), lambda i: (0, 0))],
        out_specs=pl.BlockSpec((tm, n), lambda i: (i, 0)),
        out_shape=jax.ShapeDtypeStruct((m, n), F32),
        compiler_params=_cparams(("parallel",)),
        name="proj",
    )(x, w, b)


def _gla_kernel(zg_ref, s0_ref, aup_ref, abias_ref, norm_ref, o_ref, sout_ref, st_scr, *, tb_rows, chunk, t_valid):
    tb = pl.program_id(1)
    c_rows = chunk
    sub = min(GLA_SUB, c_rows)
    nsub = c_rows // sub

    @pl.when(tb == 0)
    def _():
        for h in range(GLA_HEADS):
            st_scr[h] = s0_ref[0, h].T

    tri = (_iota((c_rows, c_rows), 1) <= _iota((c_rows, c_rows), 0)).astype(F32)
    ones_red = jnp.ones((GLA_DK, 128), BF16)
    lane_s = _iota((sub, 128), 1)
    row_s = _iota((sub, 128), 0)
    col_c = _iota((sub, c_rows), 1)

    def chunk_body(c, carry):
        r0 = pl.multiple_of(c * c_rows, c_rows)
        z = zg_ref[0, pl.ds(r0, c_rows), :]
        q = z[:, 0:256] * (GLA_DK ** -0.5)
        k = z[:, 256:512]
        v = z[:, 512:1024]
        g = z[:, 1024:1536]
        ga = z[:, 1536:1552]
        la = -_softplus(-(_mm(ga, aup_ref[...]) + abias_ref[...])) * (1.0 / GLA_GATE_NORM)
        if t_valid is not None:
            ok = (tb * tb_rows + r0 + _iota((c_rows, 1), 0)) < t_valid
            la = jnp.where(ok, la, 0.0)
            k = jnp.where(ok, k, 0.0)
            v = jnp.where(ok, v, 0.0)
        cum = _mm_exact(tri, la)
        heads = range(GLA_HEADS)
        pairs = [(h, blk) for h in heads for blk in range(nsub)]
        qh = [q[:, 64 * h:64 * h + 64] for h in heads]
        kh = [k[:, 64 * h:64 * h + 64] for h in heads]
        ch = [cum[:, 64 * h:64 * h + 64] for h in heads]
        vh = [v[:, 128 * h:128 * h + 128] for h in heads]
        st = [st_scr[h] for h in heads]
        o_in = [_mm_nt(qh[h] * jnp.exp(ch[h]), st[h]) for h in heads]
        red, off = {}, {}
        for h, blk in pairs:
            sl = slice(blk * sub, (blk + 1) * sub)
            q_i, k_i, c_i = qh[h][sl], kh[h][sl], ch[h][sl]
            es = [q_i * k_i[j:j + 1] * jnp.exp(jnp.minimum(c_i - c_i[j:j + 1], 0.0)) for j in range(sub)]
            red[h, blk] = _mm(jnp.concatenate(es, axis=0), ones_red)
            if blk > 0:
                b_i = ch[h][blk * sub - 1:blk * sub]
                q_t = q_i * jnp.exp(c_i - b_i)
                k_t = kh[h] * jnp.exp(jnp.minimum(b_i - ch[h], 0.0))
                off[h, blk] = _mm_nt(q_t, k_t)
        att = []
        for h in heads:
            att_rows = []
            for blk in range(nsub):
                a_i = jnp.zeros((sub, 128), F32)
                for j in range(sub):
                    a_i = a_i + jnp.where((lane_s == blk * sub + j) & (row_s >= j),
                                          red[h, blk][j * sub:(j + 1) * sub], 0.0)
                a_i = a_i[:, 0:c_rows]
                if blk > 0:
                    a_i = a_i + jnp.where(col_c < blk * sub, off[h, blk], 0.0)
                att_rows.append(a_i)
            att.append(att_rows[0] if nsub == 1 else jnp.concatenate(att_rows, axis=0))
        o = [o_in[h] + _mm(att[h], vh[h]) for h in heads]
        last = [ch[h][c_rows - 1:c_rows] for h in heads]
        st_new = [st[h] * jnp.exp(last[h]) + _mm_tn(vh[h], kh[h] * jnp.exp(last[h] - ch[h])) for h in heads]
        for h in heads:
            st_scr[h] = st_new[h]
            oh = o[h] * lax.rsqrt(jnp.mean(o[h] * o[h], axis=-1, keepdims=True) + NORM_EPS)
            oh = oh * norm_ref[:, 128 * h:128 * h + 128] * _silu(g[:, 128 * h:128 * h + 128])
            o_ref[0, pl.ds(r0, c_rows), 128 * h:128 * h + 128] = oh
        return carry

    lax.fori_loop(0, tb_rows // c_rows, chunk_body, 0)

    @pl.when(tb == pl.num_programs(1) - 1)
    def _():
        for h in range(GLA_HEADS):
            sout_ref[0, h] = st_scr[h].T


def _gla(zg, s0, a_up, a_bias, norm_g, *, tb_rows, chunk, t_valid):
    b, t, _ = zg.shape
    kern = functools.partial(_gla_kernel, tb_rows=tb_rows, chunk=chunk, t_valid=t_valid)
    return pl.pallas_call(
        kern,
        grid=(b, t // tb_rows),
        in_specs=[pl.BlockSpec((1, tb_rows, ZG_W), lambda i, j: (i, j, 0)),
                  pl.BlockSpec((1, GLA_HEADS, GLA_DK, GLA_DV), lambda i, j: (i, 0, 0, 0)),
                  pl.BlockSpec((GLA_LORA, GLA_K), lambda i, j: (0, 0)),
                  pl.BlockSpec((1, GLA_K), lambda i, j: (0, 0)),
                  pl.BlockSpec((1, GLA_V), lambda i, j: (0, 0))],
        out_specs=[pl.BlockSpec((1, tb_rows, GLA_V), lambda i, j: (i, j, 0)),
                   pl.BlockSpec((1, GLA_HEADS, GLA_DK, GLA_DV), lambda i, j: (i, 0, 0, 0))],
        out_shape=[jax.ShapeDtypeStruct((b, t, GLA_V), F32),
                   jax.ShapeDtypeStruct((b, GLA_HEADS, GLA_DK, GLA_DV), F32)],
        scratch_shapes=[pltpu.VMEM((GLA_HEADS, GLA_DV, GLA_DK), F32)],
        compiler_params=_cparams(("parallel", "arbitrary")),
        name="gla",
    )(zg, s0, a_up, a_bias, norm_g)


def _rwkv_kernel(zr_ref, s0_ref, sh0_ref, seg_ref, mu_ref, w0_ref, wup_ref, a0_ref, aup_ref, kk_ref, ka_ref, rk_ref,
                 lnw_ref, lnb_ref, y_ref, sout_ref,
                 s_scr, prev_scr, lw_s, kk_s, kka_s, k2_s, r_s, v_s, y_s, *, tb_rows, chunk, t_valid):
    tb = pl.program_id(1)
    c_rows = chunk
    nh = RWKV_HEADS

    @pl.when(tb == 0)
    def _():
        s_scr[...] = s0_ref[0]
        prev_scr[...] = sh0_ref[0]

    z = zr_ref[0]
    rows = _iota((tb_rows, 1), 0)
    zp = jnp.where(rows == 0, prev_scr[...], pltpu.roll(z, 1, axis=0))
    prev_scr[...] = z[tb_rows - 1:tb_rows]
    zs = z + (zp - z) * mu_ref[...]
    r = zs[:, 0:512]
    k = zs[:, 512:1024]
    v = zs[:, 1024:1536]
    wl = zs[:, 1536:1600]
    al = zs[:, 1600:1664]
    w = -_softplus(-(w0_ref[...] + _mm(jnp.tanh(wl), wup_ref[...]))) - 0.5
    lw = -jnp.exp(w)
    a = _sigmoid(a0_ref[...] + _mm(al, aup_ref[...]))
    kk = k * kk_ref[...]
    kk = kk * lax.rsqrt(_mm_split(kk * kk, seg_ref[...]) + NORM_EPS)
    k2 = k * (1.0 + (a - 1.0) * ka_ref[...])
    kka = kk * a
    if t_valid is not None:
        ok = (tb * tb_rows + rows) < t_valid
        lw = jnp.where(ok, lw, 0.0)
        kka = jnp.where(ok, kka, 0.0)
        k2 = jnp.where(ok, k2, 0.0)
    lw_s[...] = lw
    kk_s[...] = kk
    kka_s[...] = kka
    k2_s[...] = k2
    r_s[...] = r
    v_s[...] = v

    ri = _iota((c_rows, c_rows), 0)
    ci = _iota((c_rows, c_rows), 1)
    tri = (ci <= ri).astype(F32)
    strict = ci < ri
    incl = ci <= ri
    n_dbl = max(1, (c_rows - 1).bit_length())

    def chunk_body(c, carry):
        r0 = pl.multiple_of(c * c_rows, c_rows)
        ds = pl.ds(r0, c_rows)
        lwc = lw_s[ds, :]
        cl = _mm_exact(tri, lwc)
        e_inv = jnp.exp(-cl)
        e_fwd = jnp.exp(cl)
        e_prev = jnp.exp(cl - lwc)
        e_end = jnp.exp(cl[c_rows - 1:c_rows] - cl)
        g_end = jnp.exp(cl[c_rows - 1:c_rows])
        kkc, kkac, k2c, rc, vc = kk_s[ds, :], kka_s[ds, :], k2_s[ds, :], r_s[ds, :], v_s[ds, :]
        heads = range(nh)
        hsl = [slice(64 * h, 64 * h + 64) for h in heads]
        a_t = [-kkac[:, hs] * e_inv[:, hs] for hs in hsl]
        b_t = [kkc[:, hs] * e_prev[:, hs] for hs in hsl]
        k_t = [k2c[:, hs] * e_inv[:, hs] for hs in hsl]
        r_t = [rc[:, hs] * e_fwd[:, hs] for hs in hsl]
        vh = [vc[:, hs] for hs in hsl]
        cc = c_rows
        ak = [jnp.concatenate([a_t[h], k_t[h]], axis=0).astype(BF16) for h in heads]
        bra = [_mm_nt(jnp.concatenate([b_t[h], r_t[h]], axis=0), ak[h]) for h in heads]
        s0 = [s_scr[h] for h in heads]
        l_k = [jnp.where(strict, bra[h][0:cc, cc:2 * cc], 0.0) for h in heads]
        m_k = [jnp.where(incl, bra[h][cc:2 * cc, cc:2 * cc], 0.0) for h in heads]
        lmv = [_mm(jnp.concatenate([l_k[h], m_k[h]], axis=0), vh[h]) for h in heads]
        x = [jnp.concatenate([b_t[h], lmv[h][0:cc]], axis=1) for h in heads]
        lp = [jnp.where(strict, bra[h][0:cc, 0:cc], 0.0).astype(BF16) for h in heads]
        for step in range(n_dbl):
            if step + 1 < n_dbl:
                new = [_mm(lp[h], jnp.concatenate([x[h].astype(BF16), lp[h]], axis=1)) for h in heads]
                x = [x[h] + new[h][:, 0:128] for h in heads]
                lp = [new[h][:, 128:128 + cc].astype(BF16) for h in heads]
            else:
                x = [x[h] + _mm(lp[h], x[h]) for h in heads]
        prs = [_mm_nt(jnp.concatenate([x[h][:, 0:64], r_t[h]], axis=0), s0[h]) for h in heads]
        u = [prs[h][0:cc] + x[h][:, 64:128] for h in heads]
        m_a = [jnp.where(incl, bra[h][cc:2 * cc, 0:cc], 0.0) for h in heads]
        y = [prs[h][cc:2 * cc] + _mm(m_a[h], u[h]) + lmv[h][cc:2 * cc] for h in heads]
        akg = [jnp.concatenate([-kkac[:, hs] * e_end[:, hs], k2c[:, hs] * e_end[:, hs]], axis=0) for hs in hsl]
        s_new = [s0[h] * g_end[:, hsl[h]] + _mm_tn(jnp.concatenate([u[h], vh[h]], axis=0), akg[h]) for h in heads]
        for h in heads:
            s_scr[h] = s_new[h]
            y_s[ds, hsl[h]] = y[h]
        return carry

    lax.fori_loop(0, tb_rows // c_rows, chunk_body, 0)

    y = y_s[...]
    seg = seg_ref[...]
    mean = _mm_split(y, seg) * (1.0 / RWKV_HD)
    d = y - mean
    var = _mm_split(d * d, seg) * (1.0 / RWKV_HD)
    yn = d * lax.rsqrt(var + RWKV_LN_EPS) * lnw_ref[...] + lnb_ref[...]
    bonus = _mm_split(r * k2 * rk_ref[...], seg) * v
    y_ref[0] = (yn + bonus) * _silu(zs[:, 1664:2176])

    @pl.when(tb == pl.num_programs(1) - 1)
    def _():
        sout_ref[0] = s_scr[...]


def _rwkv(zr, s0, sh0, seg, params, *, tb_rows, chunk, t_valid):
    b, t, _ = zr.shape
    kern = functools.partial(_rwkv_kernel, tb_rows=tb_rows, chunk=chunk, t_valid=t_valid)
    full = lambda shp: pl.BlockSpec(shp, lambda i, j: (0,) * len(shp))
    mu, w0, w_up, a0, a_up, k_k, k_a, r_k, ln_w, ln_b = params
    return pl.pallas_call(
        kern,
        grid=(b, t // tb_rows),
        in_specs=[pl.BlockSpec((1, tb_rows, RWKV_IN), lambda i, j: (i, j, 0)),
                  pl.BlockSpec((1, RWKV_HEADS, RWKV_HD, RWKV_HD), lambda i, j: (i, 0, 0, 0)),
                  pl.BlockSpec((1, 1, RWKV_IN), lambda i, j: (i, 0, 0)),
                  full((RWKV_W, RWKV_W)), full((1, RWKV_IN)), full((1, RWKV_W)), full((64, RWKV_W)),
                  full((1, RWKV_W)), full((64, RWKV_W)), full((1, RWKV_W)), full((1, RWKV_W)), full((1, RWKV_W)),
                  full((1, RWKV_W)), full((1, RWKV_W))],
        out_specs=[pl.BlockSpec((1, tb_rows, RWKV_W), lambda i, j: (i, j, 0)),
                   pl.BlockSpec((1, RWKV_HEADS, RWKV_HD, RWKV_HD), lambda i, j: (i, 0, 0, 0))],
        out_shape=[jax.ShapeDtypeStruct((b, t, RWKV_W), F32),
                   jax.ShapeDtypeStruct((b, RWKV_HEADS, RWKV_HD, RWKV_HD), F32)],
        scratch_shapes=[pltpu.VMEM((RWKV_HEADS, RWKV_HD, RWKV_HD), F32), pltpu.VMEM((1, RWKV_IN), F32)]
        + [pltpu.VMEM((tb_rows, RWKV_W), F32) for _ in range(7)],
        compiler_params=_cparams(("parallel", "arbitrary")),
        name="rwkv",
    )(zr, s0, sh0, seg, mu, w0, w_up, a0, a_up, k_k, k_a, r_k, ln_w, ln_b)


def _slope(h):
    return 2.0 ** (-8.0 * (h + 1) / NSA_HEADS)


def _flash_init(m_ref, l_ref, acc_ref):
    m_ref[...] = jnp.full(m_ref.shape, NEG, F32)
    l_ref[...] = jnp.zeros(l_ref.shape, F32)
    acc_ref[...] = jnp.zeros(acc_ref.shape, F32)


def _flash_step_t(s_t, v, m_ref, l_ref, acc_ref):
    m_old = m_ref[...]
    m_new = jnp.maximum(m_old, jnp.max(s_t, axis=0, keepdims=True))
    p = jnp.exp(s_t - m_new)
    alpha = jnp.exp(m_old - m_new)
    l_ref[...] = alpha * l_ref[...] + jnp.sum(p, axis=0, keepdims=True)
    acc_ref[...] = alpha * acc_ref[...] + _mm_tn(v, p)
    m_ref[...] = m_new


def _flash_steps_t(s_list, v_list, m_ref, l_ref, acc_ref):
    ks = range(len(s_list))
    m_old = [m_ref[k] for k in ks]
    m_new = [jnp.maximum(m_old[k], jnp.max(s_list[k], axis=0, keepdims=True)) for k in ks]
    p = [jnp.exp(s_list[k] - m_new[k]) for k in ks]
    alpha = [jnp.exp(m_old[k] - m_new[k]) for k in ks]
    pv = [_mm_tn(v_list[k], p[k]) for k in ks]
    for k in ks:
        l_ref[k] = alpha[k] * l_ref[k] + jnp.sum(p[k], axis=0, keepdims=True)
        acc_ref[k] = alpha[k] * acc_ref[k] + pv[k]
        m_ref[k] = m_new[k]


def _flash_result(l_ref, acc_ref, k):
    l = l_ref[k]
    return acc_ref[k] / jnp.where(l > 0, l, 1.0)


SEL_LANE0 = 66
BLOCK_PENALTY = -(2.0 ** 100)


def _nsa_prompt_kernel(q_ref, kv_ref, bg_ref, g_ref, o_ref, kc_s, vc_s, ks_s, vs_s, kw_s, vw_s, qa_s, qs_s, oc_s,
                       m_s, l_s, acc_s, m2_s, l2_s, acc2_s, *, t_len, tq):
    i = pl.program_id(1)
    nb = t_len // SEL_BLOCK
    tk = tq
    ncol = NSA_GROUP * tq
    win_tiles = WINDOW // tk

    def aug_lanes(lane, pos):
        return jnp.where(lane == 64, lax.shift_right_logical(pos, 7).astype(F32),
                         jnp.where(lane == 65, (pos & 127).astype(F32), 0.0))

    def head_lanes(x, kvh):
        return x if kvh == 0 else pltpu.roll(x, 64, axis=1)

    @pl.when(i == 0)
    def _():
        lane = _iota((t_len, 128), 1)
        pos = _iota((t_len, 128), 0)
        aug = aug_lanes(lane, pos)
        aug_sel = jnp.where(lane - SEL_LANE0 == lax.shift_right_logical(pos, 6), 1.0, aug)
        for kcol, vcol, k_dst, v_dst, k_aug in ((256, 384, ks_s, vs_s, aug_sel), (512, 640, kw_s, vw_s, aug)):
            kf = kv_ref[0, :, kcol:kcol + 128]
            vf = kv_ref[0, :, vcol:vcol + 128]
            for kvh in range(NSA_KV_HEADS):
                k_dst[kvh] = jnp.where(lane < 64, head_lanes(kf, kvh), k_aug).astype(BF16)
                v_dst[kvh] = jnp.where(lane < 64, head_lanes(vf, kvh), 0.0).astype(BF16)
        kcm = kv_ref[0, :, 0:128].reshape(nb, SEL_BLOCK, 128)
        vcm = kv_ref[0, :, 128:256].reshape(nb, SEL_BLOCK, 128)
        inv = 1.0 / CMP_BLOCK
        kc = jnp.concatenate([jnp.sum(kcm[:, 0:CMP_BLOCK], axis=1) * inv,
                              jnp.sum(kcm[:, CMP_BLOCK:SEL_BLOCK], axis=1) * inv], axis=0)
        vc = jnp.concatenate([jnp.sum(vcm[:, 0:CMP_BLOCK], axis=1) * inv,
                              jnp.sum(vcm[:, CMP_BLOCK:SEL_BLOCK], axis=1) * inv], axis=0)
        lane_c = _iota((2 * nb, 128), 1)
        r_c = _iota((2 * nb, 128), 0)
        cend = jnp.where(r_c < nb, SEL_BLOCK * r_c + (CMP_BLOCK - 1), SEL_BLOCK * (r_c - nb) + (SEL_BLOCK - 1))
        aug_c = aug_lanes(lane_c, cend)
        for kvh in range(NSA_KV_HEADS):
            kc_s[kvh] = jnp.where(lane_c < 64, head_lanes(kc, kvh), aug_c).astype(BF16)
            vc_s[kvh] = jnp.where(lane_c < 64, head_lanes(vc, kvh), 0.0).astype(BF16)

    lane_q = _iota((tq, 128), 1)
    for h in range(NSA_HEADS):
        qb = q_ref[0, :, 128 * (h // 2):128 * (h // 2) + 128]
        if h % 2:
            qb = pltpu.roll(qb, 64, axis=1)
        sl = _slope(h)
        qa = jnp.where(lane_q < 64, qb * (NSA_HD ** -0.5),
                       jnp.where(lane_q == 64, 128.0 * sl, jnp.where(lane_q == 65, sl, 0.0)))
        qa_s[h // NSA_GROUP, (h % NSA_GROUP) * tq:(h % NSA_GROUP + 1) * tq, :] = qa.astype(BF16)

    t_row = i * tq + (_iota((1, ncol), 1) & (tq - 1))
    t_row1 = i * tq + _iota((1, tq), 1)
    r_c1 = _iota((2 * nb, 1), 0)
    cend_col = jnp.where(r_c1 < nb, SEL_BLOCK * r_c1 + (CMP_BLOCK - 1), SEL_BLOCK * (r_c1 - nb) + (SEL_BLOCK - 1))
    mask_c = cend_col <= t_row
    jrow = _iota((nb, 1), 0)
    tblk = lax.shift_right_logical(t_row1, 6)
    forced = (jrow == tblk) | (jrow == 0)
    valid = jrow <= tblk
    bg_t = bg_ref[0].T

    for kvh in range(NSA_KV_HEADS):
        qa = qa_s[kvh]
        s_c = jnp.where(mask_c, _mm_nt(kc_s[kvh], qa), NEG)
        m = jnp.max(s_c, axis=0, keepdims=True)
        p = jnp.where(mask_c, jnp.exp(s_c - m), 0.0)
        den = jnp.sum(p, axis=0, keepdims=True)
        p = p / jnp.where(den > 0, den, 1.0)
        oc_s[kvh] = _mm_tn(vc_s[kvh], p)
        imp_e = p[0:nb, 0:tq]
        imp_o = p[nb:2 * nb, 0:tq]
        for g in range(1, NSA_GROUP):
            imp_e = imp_e + p[0:nb, g * tq:(g + 1) * tq]
            imp_o = imp_o + p[nb:2 * nb, g * tq:(g + 1) * tq]
        score = jnp.where(forced, FORCE_SCORE, jnp.where(valid, imp_e + imp_o, -FORCE_SCORE))
        rank = jnp.zeros((nb, tq), F32)
        for jp in range(nb):
            rj = score[jp:jp + 1, :]
            rank = rank + jnp.where((rj > score) | ((rj == score) & (jp < jrow)), 1.0, 0.0)
        sel = jnp.where((rank < N_SELECT) & valid, 1.0, 0.0)
        sel_q = jnp.concatenate([sel, jnp.zeros((128 - nb, tq), F32)], axis=0).T
        sel_q = pltpu.roll(sel_q, SEL_LANE0, axis=1)
        pen = jnp.where((lane_q >= SEL_LANE0) & (lane_q < SEL_LANE0 + nb) & (sel_q < 0.5), BLOCK_PENALTY, 0.0)
        for g in range(NSA_GROUP):
            cs = slice(g * tq, (g + 1) * tq)
            qs_s[kvh, cs, :] = (qa_s[kvh, cs, :].astype(F32) + pen).astype(BF16)

    _flash_init(m_s, l_s, acc_s)
    _flash_init(m2_s, l2_s, acc2_s)
    kvs = range(NSA_KV_HEADS)

    def sel_step(kt, diagonal):
        k0 = pl.multiple_of(kt * tk, tk)
        s_t = [_mm_nt(ks_s[kvh, pl.ds(k0, tk), :], qs_s[kvh]) for kvh in kvs]
        if diagonal:
            causal = kt * tk + _iota((tk, 1), 0) <= t_row
            s_t = [jnp.where(causal, s, NEG) for s in s_t]
        _flash_steps_t(s_t, [vs_s[kvh, pl.ds(k0, tk), :] for kvh in kvs], m_s, l_s, acc_s)

    def sel_body(kt, carry):
        sel_step(kt, False)
        return carry

    lax.fori_loop(0, i, sel_body, 0)
    sel_step(i, True)

    def win_step(kt, back):
        k0 = pl.multiple_of(kt * tk, tk)
        s_t = [_mm_nt(kw_s[kvh, pl.ds(k0, tk), :], qa_s[kvh]) for kvh in kvs]
        dist = t_row - (kt * tk + _iota((tk, 1), 0))
        if back == 0:
            s_t = [jnp.where(dist >= 0, s, NEG) for s in s_t]
        elif back == win_tiles:
            s_t = [jnp.where(dist <= WINDOW, s, NEG) for s in s_t]
        _flash_steps_t(s_t, [vw_s[kvh, pl.ds(k0, tk), :] for kvh in kvs], m2_s, l2_s, acc2_s)

    for back in range(win_tiles, 0, -1):
        @pl.when(i >= back)
        def _(back=back):
            win_step(i - back, back)
    win_step(i, 0)

    for kvh in kvs:
        o_sel = _flash_result(l_s, acc_s, kvh)
        o_win = _flash_result(l2_s, acc2_s, kvh)
        o_cmp = oc_s[kvh]
        for g in range(NSA_GROUP):
            h = NSA_GROUP * kvh + g
            cs = slice(g * tq, (g + 1) * tq)
            gates = _sigmoid(bg_t[3 * h:3 * h + 3, :])
            comb = gates[0:1] * o_cmp[:, cs] + gates[1:2] * o_sel[:, cs] + gates[2:3] * o_win[:, cs]
            o_ref[0, :, 64 * h:64 * h + 64] = comb.T[:, 0:64] * _silu(g_ref[0, :, 64 * h:64 * h + 64])


def _nsa_prompt(zn, *, tq):
    b, t, _ = zn.shape
    nb = t // SEL_BLOCK
    ncol = NSA_GROUP * tq
    kern = functools.partial(_nsa_prompt_kernel, t_len=t, tq=tq)
    kv_f32 = lambda n: pltpu.VMEM((NSA_KV_HEADS, n, ncol), F32)
    stat = lambda: [kv_f32(1), kv_f32(1), kv_f32(128)]
    kvbuf = lambda n: pltpu.VMEM((NSA_KV_HEADS, n, 128), BF16)
    return pl.pallas_call(
        kern,
        grid=(b, t // tq),
        in_specs=[pl.BlockSpec((1, tq, 512), lambda i, j: (i, j, 2)),
                  pl.BlockSpec((1, t, 768), lambda i, j: (i, 0, 0)),
                  pl.BlockSpec((1, tq, 256), lambda i, j: (i, j, 3)),
                  pl.BlockSpec((1, tq, 512), lambda i, j: (i, j, 3))],
        out_specs=pl.BlockSpec((1, tq, 512), lambda i, j: (i, j, 0)),
        out_shape=jax.ShapeDtypeStruct((b, t, 512), F32),
        scratch_shapes=[kvbuf(2 * nb), kvbuf(2 * nb), kvbuf(t), kvbuf(t), kvbuf(t), kvbuf(t), kvbuf(ncol),
                        kvbuf(ncol), kv_f32(128)] + stat() + stat(),
        compiler_params=_cparams(("parallel", "arbitrary")),
        name="nsa_prompt",
    )(zn, zn, zn, zn)


PAGES_PER_STEP = 16
N_PAGE_GROUPS = N_PAGES // PAGES_PER_STEP
KEYS_PER_STEP = PAGES_PER_STEP * PAGE_SIZE
SEL_PER_STEP = KEYS_PER_STEP // SEL_BLOCK
N_PAST_SEL = PAST_LEN // SEL_BLOCK
SAMPLE_COLS = NSA_HEADS * SAMPLE_TPAD
HALF_PAGE_ROWS = 2 * NSA_KV_HEADS * NSA_HD


def _flash_step_r(s, mask, pv, m_ref, l_ref, acc_ref):
    s = jnp.where(mask, s, NEG)
    m_old = m_ref[...]
    m_new = jnp.maximum(m_old, jnp.max(s, axis=-1, keepdims=True))
    p = jnp.exp(s - m_new)
    alpha = jnp.exp(m_old - m_new)
    l_ref[...] = alpha * l_ref[...] + jnp.sum(p, axis=-1, keepdims=True)
    acc_ref[...] = alpha * acc_ref[...] + pv(p)
    m_ref[...] = m_new


def _nsa_sample_kernel(pt_ref, zn_ref, win_ref, pool_ref, e0_ref, *rest):
    pages = rest[:PAGES_PER_STEP]
    o_ref = rest[PAGES_PER_STEP]
    kce_s, kco_s, vce_s, vco_s, qt_s, sel_s, oc_s, m_s, l_s, acc_s = rest[PAGES_PER_STEP + 1:]
    ph = pl.program_id(1)
    gi = pl.program_id(2)
    rq = SAMPLE_TPAD
    npast = N_PAST_SEL
    col = _iota((SAMPLE_COLS, 1), 0)
    t_col = PAST_LEN + (col & (rq - 1))
    hcol = lax.shift_right_logical(col, 4)
    slope_col = jnp.zeros((SAMPLE_COLS, 1), F32)
    for h in range(NSA_HEADS):
        slope_col = jnp.where(hcol == h, _slope(h), slope_col)

    def bias(pos_row):
        return slope_col * (t_col - pos_row).astype(F32)

    def keys_t(lo):
        return jnp.concatenate([pg[0, lo:lo + 128, :] for pg in pages], axis=1)

    @pl.when((ph == 0) & (gi == 0))
    def _queries():
        zero = jnp.zeros((rq, NSA_HD), F32)
        rows = []
        for h in range(NSA_HEADS):
            qh = zn_ref[0, :, 1024 + 64 * h:1088 + 64 * h] * (NSA_HD ** -0.5)
            rows.append(jnp.concatenate([qh, zero] if h < NSA_GROUP else [zero, qh], axis=1))
        qt_s[...] = jnp.concatenate(rows, axis=0).astype(BF16)

    @pl.when(ph == 0)
    def _pool():
        dn = (((1,), (1,)), ((), ()))
        dst_e = pl.ds(pl.multiple_of(gi * SEL_PER_STEP, SEL_PER_STEP), SEL_PER_STEP)
        for lo, even_s, odd_s in ((0, kce_s, kco_s), (128, vce_s, vco_s)):
            hi_part, lo_part = _split2(keys_t(lo))
            pooled = (lax.dot_general(pool_ref[...], hi_part, dn, preferred_element_type=F32)
                      + lax.dot_general(pool_ref[...], lo_part, dn, preferred_element_type=F32)) * (1.0 / CMP_BLOCK)
            even_s[dst_e, :] = pooled[0:SEL_PER_STEP]
            odd_s[dst_e, :] = pooled[SEL_PER_STEP:2 * SEL_PER_STEP]

    @pl.when((ph == 0) & (gi == N_PAGE_GROUPS - 1))
    def _compressed():
        jrow = _iota((1, npast), 1)
        qt = qt_s[...]
        se = _mm_nt(qt, kce_s[...]) - bias(SEL_BLOCK * jrow + (CMP_BLOCK - 1))
        so = _mm_nt(qt, kco_s[...]) - bias(SEL_BLOCK * jrow + (SEL_BLOCK - 1))
        m = jnp.maximum(jnp.max(se, axis=-1, keepdims=True), jnp.max(so, axis=-1, keepdims=True))
        pe = jnp.exp(se - m)
        po = jnp.exp(so - m)
        den = jnp.sum(pe, axis=-1, keepdims=True) + jnp.sum(po, axis=-1, keepdims=True)
        pe = pe / den
        po = po / den
        oc_s[...] = _mm(pe, vce_s[...]) + _mm(po, vco_s[...])
        jp = _iota((npast, npast), 0)
        jj = _iota((npast, npast), 1)
        groups = []
        for kvh in range(NSA_KV_HEADS):
            base = kvh * NSA_GROUP * rq
            imp_e = pe[base:base + rq]
            imp_o = po[base:base + rq]
            for g in range(1, NSA_GROUP):
                imp_e = imp_e + pe[base + g * rq:base + (g + 1) * rq]
                imp_o = imp_o + po[base + g * rq:base + (g + 1) * rq]
            imp = imp_e + imp_o
            imp_t = jnp.concatenate([imp, jnp.zeros((128 - rq, npast), F32)], axis=0).T
            sel_rows = []
            for t in range(rq):
                colv = imp_t[:, t:t + 1]
                rowv = imp[t:t + 1, :]
                beats = ((colv > rowv) | ((colv == rowv) & (jp < jj))) & (jp >= 1)
                rank = jnp.sum(jnp.where(beats, 1.0, 0.0), axis=0, keepdims=True)
                sel_rows.append(jnp.where((jrow == 0) | (rank < N_SELECT - 2), 1.0, 0.0))
            groups += [jnp.concatenate(sel_rows, axis=0)] * NSA_GROUP
        sel_s[...] = jnp.concatenate(groups, axis=0).T.astype(BF16)

    @pl.when(ph == 1)
    def _selected():
        @pl.when(gi == 0)
        def _():
            _flash_init(m_s, l_s, acc_s)

        k_t = keys_t(0)
        v_t = keys_t(128)
        pos_row = gi * KEYS_PER_STEP + _iota((1, KEYS_PER_STEP), 1)
        s = _mm(qt_s[...], k_t) - bias(pos_row)
        sel_rows = sel_s[pl.ds(pl.multiple_of(gi * SEL_PER_STEP, SEL_PER_STEP), SEL_PER_STEP), :]
        mask = _mm_tn(sel_rows, e0_ref[...]) > 0.5
        _flash_step_r(s, mask, lambda p: _mm_nt(p, v_t), m_s, l_s, acc_s)

    @pl.when((ph == 1) & (gi == N_PAGE_GROUPS - 1))
    def _finish():
        qt = qt_s[...]
        new = zn_ref[0, :, 0:768]
        npos_row = PAST_LEN + _iota((1, rq), 1)
        mask_n = npos_row <= t_col
        s = _mm_nt(qt, new[:, 256:384]) - bias(npos_row)
        _flash_step_r(s, mask_n, lambda p: _mm(p, new[:, 384:512]), m_s, l_s, acc_s)
        l = l_s[...]
        o_sel = acc_s[...] / jnp.where(l > 0, l, 1.0)
        wpos_row = PAST_LEN - WINDOW + _iota((1, WINDOW), 1)
        mask_w = (t_col - wpos_row) <= WINDOW
        s1 = jnp.where(mask_w, _mm(qt, win_ref[0, 0:128, :]) - bias(wpos_row), NEG)
        s2 = jnp.where(mask_n, _mm_nt(qt, new[:, 512:640]) - bias(npos_row), NEG)
        m = jnp.maximum(jnp.max(s1, axis=-1, keepdims=True), jnp.max(s2, axis=-1, keepdims=True))
        p1 = jnp.where(mask_w, jnp.exp(s1 - m), 0.0)
        p2 = jnp.where(mask_n, jnp.exp(s2 - m), 0.0)
        den = jnp.sum(p1, axis=-1, keepdims=True) + jnp.sum(p2, axis=-1, keepdims=True)
        o_win = (_mm_nt(p1, win_ref[0, 128:256, :]) + _mm(p2, new[:, 640:768])) / jnp.where(den > 0, den, 1.0)
        o_cmp = oc_s[...]
        for h in range(NSA_HEADS):
            rs = slice(h * rq, (h + 1) * rq)
            ls = slice(64 * (h // NSA_GROUP), 64 * (h // NSA_GROUP) + 64)
            gates = _sigmoid(zn_ref[0, :, 768 + 3 * h:771 + 3 * h])
            o = gates[:, 0:1] * o_cmp[rs, ls] + gates[:, 1:2] * o_sel[rs, ls] + gates[:, 2:3] * o_win[rs, ls]
            o_ref[0, :, 64 * h:64 * h + 64] = o * _silu(zn_ref[0, :, 1536 + 64 * h:1600 + 64 * h])


def _nsa_sample(page_table, zn, win_t, cache_t, *, layer, n_pool):
    b = zn.shape[0]
    rq = SAMPLE_TPAD
    tok = jnp.arange(KEYS_PER_STEP)
    blk = jnp.arange(SEL_PER_STEP)
    in_blk = tok[None, :] // SEL_BLOCK == blk[:, None]
    first_half = (tok[None, :] % SEL_BLOCK) < CMP_BLOCK
    pool = jnp.concatenate([in_blk & first_half, in_blk & ~first_half], axis=0).astype(BF16)
    e0 = in_blk.astype(BF16)

    def page_map(kidx):
        return lambda i, ph, gi, pt: (layer * n_pool + pt[i, gi * PAGES_PER_STEP + kidx], ph, 0)

    const = lambda a: pl.BlockSpec(a.shape, lambda i, ph, gi, pt: (0, 0))
    sq = lambda dt: pltpu.VMEM((SAMPLE_COLS, SAMPLE_COLS), dt)
    grid_spec = pltpu.PrefetchScalarGridSpec(
        num_scalar_prefetch=1,
        grid=(b, 2, N_PAGE_GROUPS),
        in_specs=[pl.BlockSpec((1, rq, ZN_W), lambda i, ph, gi, pt: (i, 0, 0)),
                  pl.BlockSpec((1, HALF_PAGE_ROWS, WINDOW), lambda i, ph, gi, pt: (layer * b + i, 0, 0)),
                  const(pool), const(e0)]
        + [pl.BlockSpec((1, HALF_PAGE_ROWS, PAGE_SIZE), page_map(kidx)) for kidx in range(PAGES_PER_STEP)],
        out_specs=pl.BlockSpec((1, rq, 512), lambda i, ph, gi, pt: (i, 0, 0)),
        scratch_shapes=[pltpu.VMEM((N_PAST_SEL, 128), F32) for _ in range(4)]
        + [sq(BF16), pltpu.VMEM((N_PAST_SEL, SAMPLE_COLS), BF16), sq(F32),
           pltpu.VMEM((SAMPLE_COLS, 1), F32), pltpu.VMEM((SAMPLE_COLS, 1), F32), sq(F32)],
    )
    return pl.pallas_call(
        _nsa_sample_kernel,
        grid_spec=grid_spec,
        out_shape=jax.ShapeDtypeStruct((b, rq, 512), F32),
        compiler_params=_cparams(("parallel", "arbitrary", "arbitrary")),
        name="nsa_sample",
    )(page_table, zn, win_t, pool, e0, *([cache_t] * PAGES_PER_STEP))


def _merge_kernel(x_ref, bg_ref, br_ref, bn_ref, mg_ref, wbr_ref, wout_ref, lng_ref, lnb_ref, o_ref):
    acc = _sigmoid(mg_ref[:, 0:1024]) * _mm(bg_ref[...], wbr_ref[0])
    acc = acc + _sigmoid(mg_ref[:, 1024:2048]) * _mm(br_ref[...], wbr_ref[1])
    acc = acc + _sigmoid(mg_ref[:, 2048:3072]) * _mm(bn_ref[...], wbr_ref[2])
    xf = DN_ALPHA * x_ref[...] + _mm(acc, wout_ref[...])
    mu = jnp.mean(xf, axis=-1, keepdims=True)
    d = xf - mu
    var = jnp.mean(d * d, axis=-1, keepdims=True)
    o_ref[...] = d * lax.rsqrt(var + LN_EPS) * lng_ref[...] + lnb_ref[...]


def _merge(x, o_gla, o_rwkv, o_nsa, zm, w_br, w_out, ln_g, ln_b, tm):
    m = x.shape[0]
    row = lambda n: pl.BlockSpec((tm, n), lambda i: (i, 0))
    return pl.pallas_call(
        _merge_kernel,
        grid=(m // tm,),
        in_specs=[row(D_MODEL), row(512), row(512), row(512), row(ZM_W),
                  pl.BlockSpec((3, 512, D_MODEL), lambda i: (0, 0, 0)),
                  pl.BlockSpec((D_MODEL, D_MODEL), lambda i: (0, 0)),
                  pl.BlockSpec((1, D_MODEL), lambda i: (0, 0)),
                  pl.BlockSpec((1, D_MODEL), lambda i: (0, 0))],
        out_specs=row(D_MODEL),
        out_shape=jax.ShapeDtypeStruct((m, D_MODEL), F32),
        compiler_params=_cparams(("parallel",)),
        name="merge",
    )(x, o_gla, o_rwkv, o_nsa, zm, w_br, w_out, ln_g, ln_b)


def _pack_weights(w_in, b_in):
    def pack(a):
        z = lambda n: jnp.zeros(a.shape[:-1] + (n,), a.dtype)
        gla = jnp.concatenate([a[..., 0:1024], a[..., 1040:1552], a[..., 1024:1040], z(ZG_W - 1552)], axis=-1)
        rwkv = a[..., 1552:3728]
        nsa = jnp.concatenate([a[..., 4240:5008], a[..., 5008:5032], z(1024 - 792), a[..., 3728:4240],
                               a[..., 5032:5544]], axis=-1)
        mg = a[..., 5544:8616]
        return gla, rwkv, nsa, mg
    ws = [w.astype(BF16) for w in pack(w_in)]
    bs = [b[:, None, :] for b in pack(b_in)]
    return ws, bs


def kernel(x_prompt, x_sample, cache_nsa_kv, state_nsa_win, state_gla, state_rwkv, state_rwkv_shift, page_table,
           w_in, b_in, gla_a_up, gla_a_bias, gla_norm, rwkv_mu, rwkv_w0, rwkv_w_up, rwkv_a0, rwkv_a_up, rwkv_k_k,
           rwkv_k_a, rwkv_r_k, rwkv_ln_w, rwkv_ln_b, w_br, w_out, ln_g, ln_b):
    bp, tp, _ = x_prompt.shape
    bs, ts, _ = x_sample.shape
    n_pool = cache_nsa_kv.shape[1]
    ws, bws = _pack_weights(w_in, b_in)
    w_br_b = w_br.astype(BF16)
    w_out_b = w_out.astype(BF16)
    seg = (jnp.arange(RWKV_W)[:, None] // RWKV_HD == jnp.arange(RWKV_W)[None, :] // RWKV_HD).astype(BF16)
    cache_t = jnp.transpose(cache_nsa_kv, (0, 1, 3, 4, 5, 2)).reshape(DEPTH * n_pool, 8 * NSA_HD, PAGE_SIZE)
    win_t = jnp.transpose(state_nsa_win, (0, 1, 3, 4, 5, 2)).reshape(DEPTH * bs, HALF_PAGE_ROWS, WINDOW)
    row2 = lambda a: a.reshape(DEPTH, 1, -1)
    gla_a_bias2, gla_norm2 = row2(gla_a_bias), row2(gla_norm)
    r_par = [row2(rwkv_mu), row2(rwkv_w0), rwkv_w_up, row2(rwkv_a0), rwkv_a_up, row2(rwkv_k_k), row2(rwkv_k_a),
             row2(rwkv_r_k), row2(rwkv_ln_w), row2(rwkv_ln_b)]
    ln_g2, ln_b2 = row2(ln_g), row2(ln_b)

    xp = x_prompt.reshape(bp * tp, D_MODEL)
    xs = jnp.pad(x_sample, ((0, 0), (0, SAMPLE_TPAD - ts), (0, 0))).reshape(bs * SAMPLE_TPAD, D_MODEL)
    zeros_gla = jnp.zeros((bp, GLA_HEADS, GLA_DK, GLA_DV), F32)
    zeros_rwkv = jnp.zeros((bp, RWKV_HEADS, RWKV_HD, RWKV_HD), F32)
    zeros_shift = jnp.zeros((bp, 1, RWKV_IN), F32)

    outs = {k: [] for k in ("kv_p", "kv_s", "win_p", "win_s", "gla_p", "gla_s", "rwkv_p", "rwkv_s", "sh_p", "sh_s")}
    for l in range(DEPTH):
        rp = [p[l] for p in r_par]
        zg, zr, zn, zm = (_proj(xp, ws[i][l], bws[i][l], 512) for i in range(4))
        zn3 = zn.reshape(bp, tp, ZN_W)
        o_gla, gla_st = _gla(zg.reshape(bp, tp, ZG_W), zeros_gla, gla_a_up[l], gla_a_bias2[l], gla_norm2[l],
                             tb_rows=512, chunk=GLA_CHUNK, t_valid=None)
        zr3 = zr.reshape(bp, tp, RWKV_IN)
        o_rwkv, rwkv_st = _rwkv(zr3, zeros_rwkv, zeros_shift, seg, rp, tb_rows=512, chunk=RWKV_CHUNK, t_valid=None)
        o_nsa = _nsa_prompt(zn3, tq=256)
        xp = _merge(xp, o_gla.reshape(bp * tp, 512), o_rwkv.reshape(bp * tp, 512), o_nsa.reshape(bp * tp, 512), zm,
                    w_br_b[l], w_out_b[l], ln_g2[l], ln_b2[l], 512)
        outs["kv_p"].append(zn3[:, :, 0:512].reshape(bp, tp, 4, NSA_KV_HEADS, NSA_HD))
        outs["win_p"].append(zn3[:, tp - WINDOW:, 512:768].reshape(bp, WINDOW, 2, NSA_KV_HEADS, NSA_HD))
        outs["gla_p"].append(gla_st)
        outs["rwkv_p"].append(rwkv_st)
        outs["sh_p"].append(zr3[:, tp - 1, :])
        rows_s = bs * SAMPLE_TPAD
        zg, zr, zn, zm = (_proj(xs, ws[i][l], bws[i][l], rows_s) for i in range(4))
        zn3 = zn.reshape(bs, SAMPLE_TPAD, ZN_W)
        o_gla, gla_st = _gla(zg.reshape(bs, SAMPLE_TPAD, ZG_W), state_gla[l], gla_a_up[l], gla_a_bias2[l],
                             gla_norm2[l], tb_rows=SAMPLE_TPAD, chunk=SAMPLE_TPAD, t_valid=ts)
        zr3 = zr.reshape(bs, SAMPLE_TPAD, RWKV_IN)
        o_rwkv, rwkv_st = _rwkv(zr3, state_rwkv[l], state_rwkv_shift[l][:, None, :], seg, rp,
                                tb_rows=SAMPLE_TPAD, chunk=SAMPLE_TPAD, t_valid=ts)
        o_nsa = _nsa_sample(page_table, zn3, win_t, cache_t, layer=l, n_pool=n_pool)
        xs = _merge(xs, o_gla.reshape(rows_s, 512), o_rwkv.reshape(rows_s, 512), o_nsa.reshape(rows_s, 512), zm,
                    w_br_b[l], w_out_b[l], ln_g2[l], ln_b2[l], rows_s)
        outs["kv_s"].append(zn3[:, 0:ts, 0:512].reshape(bs, ts, 4, NSA_KV_HEADS, NSA_HD))
        new_win = zn3[:, 0:ts, 512:768].reshape(bs, ts, 2, NSA_KV_HEADS, NSA_HD)
        outs["win_s"].append(jnp.concatenate([state_nsa_win[l][:, ts:], new_win], axis=1))
        outs["gla_s"].append(gla_st)
        outs["rwkv_s"].append(rwkv_st)
        outs["sh_s"].append(zr3[:, ts - 1, :])

    st = lambda k: jnp.stack(outs[k])
    y_prompt = xp.reshape(bp, tp, D_MODEL)
    y_sample = xs.reshape(bs, SAMPLE_TPAD, D_MODEL)[:, 0:ts]
    return (y_prompt, y_sample, st("kv_p"), st("kv_s"), st("win_p"), st("win_s"), st("gla_p"), st("gla_s"),
            st("rwkv_p"), st("rwkv_s"), st("sh_p"), st("sh_s"))
```

```python
import functools

import jax
import jax.numpy as jnp
from jax import lax
from jax.experimental import pallas as pl
from jax.experimental.pallas import tpu as pltpu

F32 = jnp.float32
BF16 = jnp.bfloat16

D_MODEL = 1024
DEPTH = 2
PAST_LEN = 16384
PAGE_SIZE = 128
N_PAGES = PAST_LEN // PAGE_SIZE

GLA_HEADS, GLA_DK, GLA_DV = 4, 64, 128
GLA_K, GLA_V, GLA_LORA = 256, 512, 16
GLA_GATE_NORM = 16.0
GLA_CHUNK = 64
GLA_SUB = 16

RWKV_HEADS, RWKV_HD, RWKV_W = 8, 64, 512
RWKV_IN = 2176
RWKV_LN_EPS = 64e-5
RWKV_CHUNK = 64

NSA_HEADS, NSA_KV_HEADS, NSA_GROUP, NSA_HD = 8, 2, 4, 64
CMP_BLOCK, SEL_BLOCK, N_SELECT, WINDOW = 32, 64, 16, 512
FORCE_SCORE = 1e9
NEG = -1e30

DN_ALPHA = (2 * DEPTH) ** 0.25
LN_EPS = 1e-5
NORM_EPS = 1e-6

ZG_W = 1664
ZN_W = 2048
ZM_W = 3072
SAMPLE_TPAD = 16
VMEM_LIMIT = 56 * 1024 * 1024


def _mm(a, b):
    return jnp.dot(a.astype(BF16), b.astype(BF16), preferred_element_type=F32)


def _mm_nt(a, b):
    return lax.dot_general(a.astype(BF16), b.astype(BF16), (((1,), (1,)), ((), ())), preferred_element_type=F32)


def _mm_tn(a, b):
    return lax.dot_general(a.astype(BF16), b.astype(BF16), (((0,), (0,)), ((), ())), preferred_element_type=F32)


def _mm_exact(a, b):
    return jnp.dot(a, b, preferred_element_type=F32, precision=lax.Precision.HIGHEST)


def _split2(a):
    hi = a.astype(BF16)
    return hi, (a - hi.astype(F32)).astype(BF16)


def _mm_split(a, b01):
    hi, lo = _split2(a)
    return jnp.dot(hi, b01, preferred_element_type=F32) + jnp.dot(lo, b01, preferred_element_type=F32)


def _softplus(x):
    return jnp.maximum(x, 0.0) + jnp.log(1.0 + jnp.exp(-jnp.abs(x)))


def _sigmoid(x):
    return 0.5 * jnp.tanh(0.5 * x) + 0.5


def _silu(x):
    return x * _sigmoid(x)


def _iota(shape, dim):
    return lax.broadcasted_iota(jnp.int32, shape, dim)


def _cparams(sem):
    return pltpu.CompilerParams(dimension_semantics=sem, vmem_limit_bytes=VMEM_LIMIT)


def _proj_kernel(x_ref, w_ref, b_ref, o_ref):
    o_ref[...] = _mm(x_ref[...], w_ref[...]) + b_ref[...]


def _proj(x, w, b, tm):
    m, k = x.shape
    n = w.shape[1]
    return pl.pallas_call(
        _proj_kernel,
        grid=(m // tm,),
        in_specs=[pl.BlockSpec((tm, k), lambda i: (i, 0)),
                  pl.BlockSpec((k, n), lambda i: (0, 0)),
                  pl.BlockSpec((1, n), lambda i: (0, 0))],
        out_specs=pl.BlockSpec((tm, n), lambda i: (i, 0)),
        out_shape=jax.ShapeDtypeStruct((m, n), F32),
        compiler_params=_cparams(("parallel",)),
        name="proj",
    )(x, w, b)


def _gla_kernel(zg_ref, s0_ref, aup_ref, abias_ref, norm_ref, o_ref, sout_ref, st_scr, *, tb_rows, chunk, t_valid):
    tb = pl.program_id(1)
    c_rows = chunk
    sub = min(GLA_SUB, c_rows)
    nsub = c_rows // sub

    @pl.when(tb == 0)
    def _():
        for h in range(GLA_HEADS):
            st_scr[h] = s0_ref[0, h].T

    tri = (_iota((c_rows, c_rows), 1) <= _iota((c_rows, c_rows), 0)).astype(F32)
    ones_red = jnp.ones((GLA_DK, 128), BF16)
    lane_s = _iota((sub, 128), 1)
    row_s = _iota((sub, 128), 0)
    col_c = _iota((sub, c_rows), 1)

    def chunk_body(c, carry):
        r0 = pl.multiple_of(c * c_rows, c_rows)
        z = zg_ref[0, pl.ds(r0, c_rows), :]
        q = z[:, 0:256] * (GLA_DK ** -0.5)
        k = z[:, 256:512]
        v = z[:, 512:1024]
        g = z[:, 1024:1536]
        ga = z[:, 1536:1552]
        la = -_softplus(-(_mm(ga, aup_ref[...]) + abias_ref[...])) * (1.0 / GLA_GATE_NORM)
        if t_valid is not None:
            ok = (tb * tb_rows + r0 + _iota((c_rows, 1), 0)) < t_valid
            la = jnp.where(ok, la, 0.0)
            k = jnp.where(ok, k, 0.0)
            v = jnp.where(ok, v, 0.0)
        cum = _mm_exact(tri, la)
        heads = range(GLA_HEADS)
        pairs = [(h, blk) for h in heads for blk in range(nsub)]
        qh = [q[:, 64 * h:64 * h + 64] for h in heads]
        kh = [k[:, 64 * h:64 * h + 64] for h in heads]
        ch = [cum[:, 64 * h:64 * h + 64] for h in heads]
        vh = [v[:, 128 * h:128 * h + 128] for h in heads]
        st = [st_scr[h] for h in heads]
        o_in = [_mm_nt(qh[h] * jnp.exp(ch[h]), st[h]) for h in heads]
        red, off = {}, {}
        for h, blk in pairs:
            sl = slice(blk * sub, (blk + 1) * sub)
            q_i, k_i, c_i = qh[h][sl], kh[h][sl], ch[h][sl]
            es = [q_i * k_i[j:j + 1] * jnp.exp(jnp.minimum(c_i - c_i[j:j + 1], 0.0)) for j in range(sub)]
            red[h, blk] = _mm(jnp.concatenate(es, axis=0), ones_red)
            if blk > 0:
                b_i = ch[h][blk * sub - 1:blk * sub]
                q_t = q_i * jnp.exp(c_i - b_i)
                k_t = kh[h] * jnp.exp(jnp.minimum(b_i - ch[h], 0.0))
                off[h, blk] = _mm_nt(q_t, k_t)
        att = []
        for h in heads:
            att_rows = []
            for blk in range(nsub):
                a_i = jnp.zeros((sub, 128), F32)
                for j in range(sub):
                    a_i = a_i + jnp.where((lane_s == blk * sub + j) & (row_s >= j),
                                          red[h, blk][j * sub:(j + 1) * sub], 0.0)
                a_i = a_i[:, 0:c_rows]
                if blk > 0:
                    a_i = a_i + jnp.where(col_c < blk * sub, off[h, blk], 0.0)
                att_rows.append(a_i)
            att.append(att_rows[0] if nsub == 1 else jnp.concatenate(att_rows, axis=0))
        o = [o_in[h] + _mm(att[h], vh[h]) for h in heads]
        last = [ch[h][c_rows - 1:c_rows] for h in heads]
        st_new = [st[h] * jnp.exp(last[h]) + _mm_tn(vh[h], kh[h] * jnp.exp(last[h] - ch[h])) for h in heads]
        for h in heads:
            st_scr[h] = st_new[h]
            oh = o[h] * lax.rsqrt(jnp.mean(o[h] * o[h], axis=-1, keepdims=True) + NORM_EPS)
            oh = oh * norm_ref[:, 128 * h:128 * h + 128] * _silu(g[:, 128 * h:128 * h + 128])
            o_ref[0, pl.ds(r0, c_rows), 128 * h:128 * h + 128] = oh
        return carry

    lax.fori_loop(0, tb_rows // c_rows, chunk_body, 0)

    @pl.when(tb == pl.num_programs(1) - 1)
    def _():
        for h in range(GLA_HEADS):
            sout_ref[0, h] = st_scr[h].T


def _gla(zg, s0, a_up, a_bias, norm_g, *, tb_rows, chunk, t_valid):
    b, t, _ = zg.shape
    kern = functools.partial(_gla_kernel, tb_rows=tb_rows, chunk=chunk, t_valid=t_valid)
    return pl.pallas_call(
        kern,
        grid=(b, t // tb_rows),
        in_specs=[pl.BlockSpec((1, tb_rows, ZG_W), lambda i, j: (i, j, 0)),
                  pl.BlockSpec((1, GLA_HEADS, GLA_DK, GLA_DV), lambda i, j: (i, 0, 0, 0)),
                  pl.BlockSpec((GLA_LORA, GLA_K), lambda i, j: (0, 0)),
                  pl.BlockSpec((1, GLA_K), lambda i, j: (0, 0)),
                  pl.BlockSpec((1, GLA_V), lambda i, j: (0, 0))],
        out_specs=[pl.BlockSpec((1, tb_rows, GLA_V), lambda i, j: (i, j, 0)),
                   pl.BlockSpec((1, GLA_HEADS, GLA_DK, GLA_DV), lambda i, j: (i, 0, 0, 0))],
        out_shape=[jax.ShapeDtypeStruct((b, t, GLA_V), F32),
                   jax.ShapeDtypeStruct((b, GLA_HEADS, GLA_DK, GLA_DV), F32)],
        scratch_shapes=[pltpu.VMEM((GLA_HEADS, GLA_DV, GLA_DK), F32)],
        compiler_params=_cparams(("parallel", "arbitrary")),
        name="gla",
    )(zg, s0, a_up, a_bias, norm_g)


def _rwkv_kernel(zr_ref, s0_ref, sh0_ref, seg_ref, mu_ref, w0_ref, wup_ref, a0_ref, aup_ref, kk_ref, ka_ref, rk_ref,
                 lnw_ref, lnb_ref, y_ref, sout_ref,
                 s_scr, prev_scr, lw_s, kk_s, kka_s, k2_s, r_s, v_s, y_s, *, tb_rows, chunk, t_valid):
    tb = pl.program_id(1)
    c_rows = chunk
    nh = RWKV_HEADS

    @pl.when(tb == 0)
    def _():
        s_scr[...] = s0_ref[0]
        prev_scr[...] = sh0_ref[0]

    z = zr_ref[0]
    rows = _iota((tb_rows, 1), 0)
    zp = jnp.where(rows == 0, prev_scr[...], pltpu.roll(z, 1, axis=0))
    prev_scr[...] = z[tb_rows - 1:tb_rows]
    zs = z + (zp - z) * mu_ref[...]
    r = zs[:, 0:512]
    k = zs[:, 512:1024]
    v = zs[:, 1024:1536]
    wl = zs[:, 1536:1600]
    al = zs[:, 1600:1664]
    w = -_softplus(-(w0_ref[...] + _mm(jnp.tanh(wl), wup_ref[...]))) - 0.5
    lw = -jnp.exp(w)
    a = _sigmoid(a0_ref[...] + _mm(al, aup_ref[...]))
    kk = k * kk_ref[...]
    kk = kk * lax.rsqrt(_mm_split(kk * kk, seg_ref[...]) + NORM_EPS)
    k2 = k * (1.0 + (a - 1.0) * ka_ref[...])
    kka = kk * a
    if t_valid is not None:
        ok = (tb * tb_rows + rows) < t_valid
        lw = jnp.where(ok, lw, 0.0)
        kka = jnp.where(ok, kka, 0.0)
        k2 = jnp.where(ok, k2, 0.0)
    lw_s[...] = lw
    kk_s[...] = kk
    kka_s[...] = kka
    k2_s[...] = k2
    r_s[...] = r
    v_s[...] = v

    ri = _iota((c_rows, c_rows), 0)
    ci = _iota((c_rows, c_rows), 1)
    tri = (ci <= ri).astype(F32)
    strict = ci < ri
    incl = ci <= ri
    n_dbl = max(1, (c_rows - 1).bit_length())

    def chunk_body(c, carry):
        r0 = pl.multiple_of(c * c_rows, c_rows)
        ds = pl.ds(r0, c_rows)
        lwc = lw_s[ds, :]
        cl = _mm_exact(tri, lwc)
        e_inv = jnp.exp(-cl)
        e_fwd = jnp.exp(cl)
        e_prev = jnp.exp(cl - lwc)
        e_end = jnp.exp(cl[c_rows - 1:c_rows] - cl)
        g_end = jnp.exp(cl[c_rows - 1:c_rows])
        kkc, kkac, k2c, rc, vc = kk_s[ds, :], kka_s[ds, :], k2_s[ds, :], r_s[ds, :], v_s[ds, :]
        heads = range(nh)
        hsl = [slice(64 * h, 64 * h + 64) for h in heads]
        a_t = [-kkac[:, hs] * e_inv[:, hs] for hs in hsl]
        b_t = [kkc[:, hs] * e_prev[:, hs] for hs in hsl]
        k_t = [k2c[:, hs] * e_inv[:, hs] for hs in hsl]
        r_t = [rc[:, hs] * e_fwd[:, hs] for hs in hsl]
        vh = [vc[:, hs] for hs in hsl]
        cc = c_rows
        ak = [jnp.concatenate([a_t[h], k_t[h]], axis=0).astype(BF16) for h in heads]
        bra = [_mm_nt(jnp.concatenate([b_t[h], r_t[h]], axis=0), ak[h]) for h in heads]
        s0 = [s_scr[h] for h in heads]
        l_k = [jnp.where(strict, bra[h][0:cc, cc:2 * cc], 0.0) for h in heads]
        m_k = [jnp.where(incl, bra[h][cc:2 * cc, cc:2 * cc], 0.0) for h in heads]
        lmv = [_mm(jnp.concatenate([l_k[h], m_k[h]], axis=0), vh[h]) for h in heads]
        x = [jnp.concatenate([b_t[h], lmv[h][0:cc]], axis=1) for h in heads]
        lp = [jnp.where(strict, bra[h][0:cc, 0:cc], 0.0).astype(BF16) for h in heads]
        for step in range(n_dbl):
            if step + 1 < n_dbl:
                new = [_mm(lp[h], jnp.concatenate([x[h].astype(BF16), lp[h]], axis=1)) for h in heads]
                x = [x[h] + new[h][:, 0:128] for h in heads]
                lp = [new[h][:, 128:128 + cc].astype(BF16) for h in heads]
            else:
                x = [x[h] + _mm(lp[h], x[h]) for h in heads]
        prs = [_mm_nt(jnp.concatenate([x[h][:, 0:64], r_t[h]], axis=0), s0[h]) for h in heads]
        u = [prs[h][0:cc] + x[h][:, 64:128] for h in heads]
        m_a = [jnp.where(incl, bra[h][cc:2 * cc, 0:cc], 0.0) for h in heads]
        y = [prs[h][cc:2 * cc] + _mm(m_a[h], u[h]) + lmv[h][cc:2 * cc] for h in heads]
        akg = [jnp.concatenate([-kkac[:, hs] * e_end[:, hs], k2c[:, hs] * e_end[:, hs]], axis=0) for hs in hsl]
        s_new = [s0[h] * g_end[:, hsl[h]] + _mm_tn(jnp.concatenate([u[h], vh[h]], axis=0), akg[h]) for h in heads]
        for h in heads:
            s_scr[h] = s_new[h]
            y_s[ds, hsl[h]] = y[h]
        return carry

    lax.fori_loop(0, tb_rows // c_rows, chunk_body, 0)

    y = y_s[...]
    seg = seg_ref[...]
    mean = _mm_split(y, seg) * (1.0 / RWKV_HD)
    d = y - mean
    var = _mm_split(d * d, seg) * (1.0 / RWKV_HD)
    yn = d * lax.rsqrt(var + RWKV_LN_EPS) * lnw_ref[...] + lnb_ref[...]
    bonus = _mm_split(r * k2 * rk_ref[...], seg) * v
    y_ref[0] = (yn + bonus) * _silu(zs[:, 1664:2176])

    @pl.when(tb == pl.num_programs(1) - 1)
    def _():
        sout_ref[0] = s_scr[...]


def _rwkv(zr, s0, sh0, seg, params, *, tb_rows, chunk, t_valid):
    b, t, _ = zr.shape
    kern = functools.partial(_rwkv_kernel, tb_rows=tb_rows, chunk=chunk, t_valid=t_valid)
    full = lambda shp: pl.BlockSpec(shp, lambda i, j: (0,) * len(shp))
    mu, w0, w_up, a0, a_up, k_k, k_a, r_k, ln_w, ln_b = params
    return pl.pallas_call(
        kern,
        grid=(b, t // tb_rows),
        in_specs=[pl.BlockSpec((1, tb_rows, RWKV_IN), lambda i, j: (i, j, 0)),
                  pl.BlockSpec((1, RWKV_HEADS, RWKV_HD, RWKV_HD), lambda i, j: (i, 0, 0, 0)),
                  pl.BlockSpec((1, 1, RWKV_IN), lambda i, j: (i, 0, 0)),
                  full((RWKV_W, RWKV_W)), full((1, RWKV_IN)), full((1, RWKV_W)), full((64, RWKV_W)),
                  full((1, RWKV_W)), full((64, RWKV_W)), full((1, RWKV_W)), full((1, RWKV_W)), full((1, RWKV_W)),
                  full((1, RWKV_W)), full((1, RWKV_W))],
        out_specs=[pl.BlockSpec((1, tb_rows, RWKV_W), lambda i, j: (i, j, 0)),
                   pl.BlockSpec((1, RWKV_HEADS, RWKV_HD, RWKV_HD), lambda i, j: (i, 0, 0, 0))],
        out_shape=[jax.ShapeDtypeStruct((b, t, RWKV_W), F32),
                   jax.ShapeDtypeStruct((b, RWKV_HEADS, RWKV_HD, RWKV_HD), F32)],
        scratch_shapes=[pltpu.VMEM((RWKV_HEADS, RWKV_HD, RWKV_HD), F32), pltpu.VMEM((1, RWKV_IN), F32)]
        + [pltpu.VMEM((tb_rows, RWKV_W), F32) for _ in range(7)],
        compiler_params=_cparams(("parallel", "arbitrary")),
        name="rwkv",
    )(zr, s0, sh0, seg, mu, w0, w_up, a0, a_up, k_k, k_a, r_k, ln_w, ln_b)


def _slope(h):
    return 2.0 ** (-8.0 * (h + 1) / NSA_HEADS)


def _flash_init(m_ref, l_ref, acc_ref):
    m_ref[...] = jnp.full(m_ref.shape, NEG, F32)
    l_ref[...] = jnp.zeros(l_ref.shape, F32)
    acc_ref[...] = jnp.zeros(acc_ref.shape, F32)


def _flash_step_t(s_t, v, m_ref, l_ref, acc_ref):
    m_old = m_ref[...]
    m_new = jnp.maximum(m_old, jnp.max(s_t, axis=0, keepdims=True))
    p = jnp.exp(s_t - m_new)
    alpha = jnp.exp(m_old - m_new)
    l_ref[...] = alpha * l_ref[...] + jnp.sum(p, axis=0, keepdims=True)
    acc_ref[...] = alpha * acc_ref[...] + _mm_tn(v, p)
    m_ref[...] = m_new


def _flash_steps_t(s_list, v_list, m_ref, l_ref, acc_ref):
    ks = range(len(s_list))
    m_old = [m_ref[k] for k in ks]
    m_new = [jnp.maximum(m_old[k], jnp.max(s_list[k], axis=0, keepdims=True)) for k in ks]
    p = [jnp.exp(s_list[k] - m_new[k]) for k in ks]
    alpha = [jnp.exp(m_old[k] - m_new[k]) for k in ks]
    pv = [_mm_tn(v_list[k], p[k]) for k in ks]
    for k in ks:
        l_ref[k] = alpha[k] * l_ref[k] + jnp.sum(p[k], axis=0, keepdims=True)
        acc_ref[k] = alpha[k] * acc_ref[k] + pv[k]
        m_ref[k] = m_new[k]


def _flash_result(l_ref, acc_ref, k):
    l = l_ref[k]
    return acc_ref[k] / jnp.where(l > 0, l, 1.0)


SEL_LANE0 = 66
BLOCK_PENALTY = -(2.0 ** 100)


def _nsa_prompt_kernel(q_ref, kv_ref, bg_ref, g_ref, o_ref, kc_s, vc_s, ks_s, vs_s, kw_s, vw_s, qa_s, qs_s, oc_s,
                       m_s, l_s, acc_s, m2_s, l2_s, acc2_s, *, t_len, tq):
    i = pl.program_id(1)
    nb = t_len // SEL_BLOCK
    tk = tq
    ncol = NSA_GROUP * tq
    win_tiles = WINDOW // tk

    def aug_lanes(lane, pos):
        return jnp.where(lane == 64, lax.shift_right_logical(pos, 7).astype(F32),
                         jnp.where(lane == 65, (pos & 127).astype(F32), 0.0))

    def head_lanes(x, kvh):
        return x if kvh == 0 else pltpu.roll(x, 64, axis=1)

    @pl.when(i == 0)
    def _():
        lane = _iota((t_len, 128), 1)
        pos = _iota((t_len, 128), 0)
        aug = aug_lanes(lane, pos)
        aug_sel = jnp.where(lane - SEL_LANE0 == lax.shift_right_logical(pos, 6), 1.0, aug)
        for kcol, vcol, k_dst, v_dst, k_aug in ((256, 384, ks_s, vs_s, aug_sel), (512, 640, kw_s, vw_s, aug)):
            kf = kv_ref[0, :, kcol:kcol + 128]
            vf = kv_ref[0, :, vcol:vcol + 128]
            for kvh in range(NSA_KV_HEADS):
                k_dst[kvh] = jnp.where(lane < 64, head_lanes(kf, kvh), k_aug).astype(BF16)
                v_dst[kvh] = jnp.where(lane < 64, head_lanes(vf, kvh), 0.0).astype(BF16)
        kcm = kv_ref[0, :, 0:128].reshape(nb, SEL_BLOCK, 128)
        vcm = kv_ref[0, :, 128:256].reshape(nb, SEL_BLOCK, 128)
        inv = 1.0 / CMP_BLOCK
        kc = jnp.concatenate([jnp.sum(kcm[:, 0:CMP_BLOCK], axis=1) * inv,
                              jnp.sum(kcm[:, CMP_BLOCK:SEL_BLOCK], axis=1) * inv], axis=0)
        vc = jnp.concatenate([jnp.sum(vcm[:, 0:CMP_BLOCK], axis=1) * inv,
                              jnp.sum(vcm[:, CMP_BLOCK:SEL_BLOCK], axis=1) * inv], axis=0)
        lane_c = _iota((2 * nb, 128), 1)
        r_c = _iota((2 * nb, 128), 0)
        cend = jnp.where(r_c < nb, SEL_BLOCK * r_c + (CMP_BLOCK - 1), SEL_BLOCK * (r_c - nb) + (SEL_BLOCK - 1))
        aug_c = aug_lanes(lane_c, cend)
        for kvh in range(NSA_KV_HEADS):
            kc_s[kvh] = jnp.where(lane_c < 64, head_lanes(kc, kvh), aug_c).astype(BF16)
            vc_s[kvh] = jnp.where(lane_c < 64, head_lanes(vc, kvh), 0.0).astype(BF16)

    lane_q = _iota((tq, 128), 1)
    for h in range(NSA_HEADS):
        qb = q_ref[0, :, 128 * (h // 2):128 * (h // 2) + 128]
        if h % 2:
            qb = pltpu.roll(qb, 64, axis=1)
        sl = _slope(h)
        qa = jnp.where(lane_q < 64, qb * (NSA_HD ** -0.5),
                       jnp.where(lane_q == 64, 128.0 * sl, jnp.where(lane_q == 65, sl, 0.0)))
        qa_s[h // NSA_GROUP, (h % NSA_GROUP) * tq:(h % NSA_GROUP + 1) * tq, :] = qa.astype(BF16)

    t_row = i * tq + (_iota((1, ncol), 1) & (tq - 1))
    t_row1 = i * tq + _iota((1, tq), 1)
    r_c1 = _iota((2 * nb, 1), 0)
    cend_col = jnp.where(r_c1 < nb, SEL_BLOCK * r_c1 + (CMP_BLOCK - 1), SEL_BLOCK * (r_c1 - nb) + (SEL_BLOCK - 1))
    mask_c = cend_col <= t_row
    jrow = _iota((nb, 1), 0)
    tblk = lax.shift_right_logical(t_row1, 6)
    forced = (jrow == tblk) | (jrow == 0)
    valid = jrow <= tblk
    bg_t = bg_ref[0].T

    for kvh in range(NSA_KV_HEADS):
        qa = qa_s[kvh]
        s_c = jnp.where(mask_c, _mm_nt(kc_s[kvh], qa), NEG)
        m = jnp.max(s_c, axis=0, keepdims=True)
        p = jnp.where(mask_c, jnp.exp(s_c - m), 0.0)
        den = jnp.sum(p, axis=0, keepdims=True)
        p = p / jnp.where(den > 0, den, 1.0)
        oc_s[kvh] = _mm_tn(vc_s[kvh], p)
        imp_e = p[0:nb, 0:tq]
        imp_o = p[nb:2 * nb, 0:tq]
        for g in range(1, NSA_GROUP):
            imp_e = imp_e + p[0:nb, g * tq:(g + 1) * tq]
            imp_o = imp_o + p[nb:2 * nb, g * tq:(g + 1) * tq]
        score = jnp.where(forced, FORCE_SCORE, jnp.where(valid, imp_e + imp_o, -FORCE_SCORE))
        rank = jnp.zeros((nb, tq), F32)
        for jp in range(nb):
            rj = score[jp:jp + 1, :]
            rank = rank + jnp.where((rj > score) | ((rj == score) & (jp < jrow)), 1.0, 0.0)
        sel = jnp.where((rank < N_SELECT) & valid, 1.0, 0.0)
        sel_q = jnp.concatenate([sel, jnp.zeros((128 - nb, tq), F32)], axis=0).T
        sel_q = pltpu.roll(sel_q, SEL_LANE0, axis=1)
        pen = jnp.where((lane_q >= SEL_LANE0) & (lane_q < SEL_LANE0 + nb) & (sel_q < 0.5), BLOCK_PENALTY, 0.0)
        for g in range(NSA_GROUP):
            cs = slice(g * tq, (g + 1) * tq)
            qs_s[kvh, cs, :] = (qa_s[kvh, cs, :].astype(F32) + pen).astype(BF16)

    _flash_init(m_s, l_s, acc_s)
    _flash_init(m2_s, l2_s, acc2_s)
    kvs = range(NSA_KV_HEADS)

    def sel_step(kt, diagonal):
        k0 = pl.multiple_of(kt * tk, tk)
        s_t = [_mm_nt(ks_s[kvh, pl.ds(k0, tk), :], qs_s[kvh]) for kvh in kvs]
        if diagonal:
            causal = kt * tk + _iota((tk, 1), 0) <= t_row
            s_t = [jnp.where(causal, s, NEG) for s in s_t]
        _flash_steps_t(s_t, [vs_s[kvh, pl.ds(k0, tk), :] for kvh in kvs], m_s, l_s, acc_s)

    def sel_body(kt, carry):
        sel_step(kt, False)
        return carry

    lax.fori_loop(0, i, sel_body, 0)
    sel_step(i, True)

    def win_step(kt, back):
        k0 = pl.multiple_of(kt * tk, tk)
        s_t = [_mm_nt(kw_s[kvh, pl.ds(k0, tk), :], qa_s[kvh]) for kvh in kvs]
        dist = t_row - (kt * tk + _iota((tk, 1), 0))
        if back == 0:
            s_t = [jnp.where(dist >= 0, s, NEG) for s in s_t]
        elif back == win_tiles:
            s_t = [jnp.where(dist <= WINDOW, s, NEG) for s in s_t]
        _flash_steps_t(s_t, [vw_s[kvh, pl.ds(k0, tk), :] for kvh in kvs], m2_s, l2_s, acc2_s)

    for back in range(win_tiles, 0, -1):
        @pl.when(i >= back)
        def _(back=back):
            win_step(i - back, back)
    win_step(i, 0)

    for kvh in kvs:
        o_sel = _flash_result(l_s, acc_s, kvh)
        o_win = _flash_result(l2_s, acc2_s, kvh)
        o_cmp = oc_s[kvh]
        for g in range(NSA_GROUP):
            h = NSA_GROUP * kvh + g
            cs = slice(g * tq, (g + 1) * tq)
            gates = _sigmoid(bg_t[3 * h:3 * h + 3, :])
            comb = gates[0:1] * o_cmp[:, cs] + gates[1:2] * o_sel[:, cs] + gates[2:3] * o_win[:, cs]
            o_ref[0, :, 64 * h:64 * h + 64] = comb.T[:, 0:64] * _silu(g_ref[0, :, 64 * h:64 * h + 64])


def _nsa_prompt(zn, *, tq):
    b, t, _ = zn.shape
    nb = t // SEL_BLOCK
    ncol = NSA_GROUP * tq
    kern = functools.partial(_nsa_prompt_kernel, t_len=t, tq=tq)
    kv_f32 = lambda n: pltpu.VMEM((NSA_KV_HEADS, n, ncol), F32)
    stat = lambda: [kv_f32(1), kv_f32(1), kv_f32(128)]
    kvbuf = lambda n: pltpu.VMEM((NSA_KV_HEADS, n, 128), BF16)
    return pl.pallas_call(
        kern,
        grid=(b, t // tq),
        in_specs=[pl.BlockSpec((1, tq, 512), lambda i, j: (i, j, 2)),
                  pl.BlockSpec((1, t, 768), lambda i, j: (i, 0, 0)),
                  pl.BlockSpec((1, tq, 256), lambda i, j: (i, j, 3)),
                  pl.BlockSpec((1, tq, 512), lambda i, j: (i, j, 3))],
        out_specs=pl.BlockSpec((1, tq, 512), lambda i, j: (i, j, 0)),
        out_shape=jax.ShapeDtypeStruct((b, t, 512), F32),
        scratch_shapes=[kvbuf(2 * nb), kvbuf(2 * nb), kvbuf(t), kvbuf(t), kvbuf(t), kvbuf(t), kvbuf(ncol),
                        kvbuf(ncol), kv_f32(128)] + stat() + stat(),
        compiler_params=_cparams(("parallel", "arbitrary")),
        name="nsa_prompt",
    )(zn, zn, zn, zn)


PAGES_PER_STEP = 16
N_PAGE_GROUPS = N_PAGES // PAGES_PER_STEP
KEYS_PER_STEP = PAGES_PER_STEP * PAGE_SIZE
SEL_PER_STEP = KEYS_PER_STEP // SEL_BLOCK
N_PAST_SEL = PAST_LEN // SEL_BLOCK
HALF_PAGE_ROWS = 2 * NSA_KV_HEADS * NSA_HD


def _flash_step_r(s, mask, pv, m_ref, l_ref, acc_ref):
    s = jnp.where(mask, s, NEG)
    m_old = m_ref[...]
    m_new = jnp.maximum(m_old, jnp.max(s, axis=-1, keepdims=True))
    p = jnp.exp(s - m_new)
    alpha = jnp.exp(m_old - m_new)
    l_ref[...] = alpha * l_ref[...] + jnp.sum(p, axis=-1, keepdims=True)
    acc_ref[...] = alpha * acc_ref[...] + pv(p)
    m_ref[...] = m_new


def _nsa_sample_kernel(pt_ref, zn_ref, win_ref, pool_ref, e0_ref, *rest, n_tok):
    pages = rest[:PAGES_PER_STEP]
    o_ref = rest[PAGES_PER_STEP]
    kce_s, kco_s, vce_s, vco_s, qt_s, sel_s, oc_s, m_s, l_s, acc_s = rest[PAGES_PER_STEP + 1:]
    ph = pl.program_id(1)
    gi = pl.program_id(2)
    rq = n_tok
    ncol = NSA_HEADS * rq
    npast = N_PAST_SEL
    col = _iota((ncol, 1), 0)
    t_col = PAST_LEN + (col & (rq - 1))
    hcol = lax.shift_right_logical(col, rq.bit_length() - 1)
    slope_col = jnp.zeros((ncol, 1), F32)
    for h in range(NSA_HEADS):
        slope_col = jnp.where(hcol == h, _slope(h), slope_col)

    def bias(pos_row):
        return slope_col * (t_col - pos_row).astype(F32)

    def keys_t(lo):
        return jnp.concatenate([pg[0, lo:lo + 128, :] for pg in pages], axis=1)

    @pl.when((ph == 0) & (gi == 0))
    def _queries():
        zero = jnp.zeros((rq, NSA_HD), F32)
        rows = []
        for h in range(NSA_HEADS):
            qh = zn_ref[0, 0:rq, 1024 + 64 * h:1088 + 64 * h] * (NSA_HD ** -0.5)
            rows.append(jnp.concatenate([qh, zero] if h < NSA_GROUP else [zero, qh], axis=1))
        qt_s[...] = jnp.concatenate(rows, axis=0).astype(BF16)

    @pl.when(ph == 0)
    def _pool():
        dn = (((1,), (1,)), ((), ()))
        dst_e = pl.ds(pl.multiple_of(gi * SEL_PER_STEP, SEL_PER_STEP), SEL_PER_STEP)
        for lo, even_s, odd_s in ((0, kce_s, kco_s), (128, vce_s, vco_s)):
            hi_part, lo_part = _split2(keys_t(lo))
            pooled = (lax.dot_general(pool_ref[...], hi_part, dn, preferred_element_type=F32)
                      + lax.dot_general(pool_ref[...], lo_part, dn, preferred_element_type=F32)) * (1.0 / CMP_BLOCK)
            even_s[dst_e, :] = pooled[0:SEL_PER_STEP]
            odd_s[dst_e, :] = pooled[SEL_PER_STEP:2 * SEL_PER_STEP]

    @pl.when((ph == 0) & (gi == N_PAGE_GROUPS - 1))
    def _compressed():
        jrow = _iota((1, npast), 1)
        qt = qt_s[...]
        se = _mm_nt(qt, kce_s[...]) - bias(SEL_BLOCK * jrow + (CMP_BLOCK - 1))
        so = _mm_nt(qt, kco_s[...]) - bias(SEL_BLOCK * jrow + (SEL_BLOCK - 1))
        m = jnp.maximum(jnp.max(se, axis=-1, keepdims=True), jnp.max(so, axis=-1, keepdims=True))
        pe = jnp.exp(se - m)
        po = jnp.exp(so - m)
        den = jnp.sum(pe, axis=-1, keepdims=True) + jnp.sum(po, axis=-1, keepdims=True)
        pe = pe / den
        po = po / den
        oc_s[...] = _mm(pe, vce_s[...]) + _mm(po, vco_s[...])
        jp = _iota((npast, npast), 0)
        jj = _iota((npast, npast), 1)
        groups = []
        for kvh in range(NSA_KV_HEADS):
            base = kvh * NSA_GROUP * rq
            imp_e = pe[base:base + rq]
            imp_o = po[base:base + rq]
            for g in range(1, NSA_GROUP):
                imp_e = imp_e + pe[base + g * rq:base + (g + 1) * rq]
                imp_o = imp_o + po[base + g * rq:base + (g + 1) * rq]
            imp = imp_e + imp_o
            imp_t = jnp.concatenate([imp, jnp.zeros((128 - rq, npast), F32)], axis=0).T
            sel_rows = []
            for t in range(rq):
                colv = imp_t[:, t:t + 1]
                rowv = imp[t:t + 1, :]
                beats = ((colv > rowv) | ((colv == rowv) & (jp < jj))) & (jp >= 1)
                rank = jnp.sum(jnp.where(beats, 1.0, 0.0), axis=0, keepdims=True)
                sel_rows.append(jnp.where((jrow == 0) | (rank < N_SELECT - 2), 1.0, 0.0))
            groups += [jnp.concatenate(sel_rows, axis=0)] * NSA_GROUP
        groups.append(jnp.zeros((128 - ncol, npast), F32))
        sel_s[...] = jnp.concatenate(groups, axis=0).T[:, 0:ncol].astype(BF16)

    @pl.when(ph == 1)
    def _selected():
        @pl.when(gi == 0)
        def _():
            _flash_init(m_s, l_s, acc_s)

        k_t = keys_t(0)
        v_t = keys_t(128)
        pos_row = gi * KEYS_PER_STEP + _iota((1, KEYS_PER_STEP), 1)
        s = _mm(qt_s[...], k_t) - bias(pos_row)
        sel_rows = sel_s[pl.ds(pl.multiple_of(gi * SEL_PER_STEP, SEL_PER_STEP), SEL_PER_STEP), :]
        mask = _mm_tn(sel_rows, e0_ref[...]) > 0.5
        _flash_step_r(s, mask, lambda p: _mm_nt(p, v_t), m_s, l_s, acc_s)

    @pl.when((ph == 1) & (gi == N_PAGE_GROUPS - 1))
    def _finish():
        qt = qt_s[...]
        new = zn_ref[0, 0:rq, 0:768]
        npos_row = PAST_LEN + _iota((1, rq), 1)
        mask_n = npos_row <= t_col
        s = _mm_nt(qt, new[:, 256:384]) - bias(npos_row)
        _flash_step_r(s, mask_n, lambda p: _mm(p, new[:, 384:512]), m_s, l_s, acc_s)
        l = l_s[...]
        o_sel = acc_s[...] / jnp.where(l > 0, l, 1.0)
        wpos_row = PAST_LEN - WINDOW + _iota((1, WINDOW), 1)
        mask_w = (t_col - wpos_row) <= WINDOW
        s1 = jnp.where(mask_w, _mm(qt, win_ref[0, 0:128, :]) - bias(wpos_row), NEG)
        s2 = jnp.where(mask_n, _mm_nt(qt, new[:, 512:640]) - bias(npos_row), NEG)
        m = jnp.maximum(jnp.max(s1, axis=-1, keepdims=True), jnp.max(s2, axis=-1, keepdims=True))
        p1 = jnp.where(mask_w, jnp.exp(s1 - m), 0.0)
        p2 = jnp.where(mask_n, jnp.exp(s2 - m), 0.0)
        den = jnp.sum(p1, axis=-1, keepdims=True) + jnp.sum(p2, axis=-1, keepdims=True)
        o_win = (_mm_nt(p1, win_ref[0, 128:256, :]) + _mm(p2, new[:, 640:768])) / jnp.where(den > 0, den, 1.0)
        o_cmp = oc_s[...]
        for h in range(NSA_HEADS):
            rs = slice(h * rq, (h + 1) * rq)
            ls = slice(64 * (h // NSA_GROUP), 64 * (h // NSA_GROUP) + 64)
            gates = _sigmoid(zn_ref[0, 0:rq, 768 + 3 * h:771 + 3 * h])
            o = gates[:, 0:1] * o_cmp[rs, ls] + gates[:, 1:2] * o_sel[rs, ls] + gates[:, 2:3] * o_win[rs, ls]
            o_ref[0, 0:rq, 64 * h:64 * h + 64] = o * _silu(zn_ref[0, 0:rq, 1536 + 64 * h:1600 + 64 * h])
        o_ref[0, rq:, :] = jnp.zeros((SAMPLE_TPAD - rq, 512), F32)


def _nsa_sample(page_table, zn, win_t, cache_t, *, layer, n_pool, n_tok):
    b = zn.shape[0]
    rq = SAMPLE_TPAD
    assert n_tok & (n_tok - 1) == 0 and n_tok <= rq
    ncol = NSA_HEADS * n_tok
    tok = jnp.arange(KEYS_PER_STEP)
    blk = jnp.arange(SEL_PER_STEP)
    in_blk = tok[None, :] // SEL_BLOCK == blk[:, None]
    first_half = (tok[None, :] % SEL_BLOCK) < CMP_BLOCK
    pool = jnp.concatenate([in_blk & first_half, in_blk & ~first_half], axis=0).astype(BF16)
    e0 = in_blk.astype(BF16)

    def page_map(kidx):
        return lambda i, ph, gi, pt: (layer * n_pool + pt[i, gi * PAGES_PER_STEP + kidx], ph, 0)

    const = lambda a: pl.BlockSpec(a.shape, lambda i, ph, gi, pt: (0, 0))
    sq = lambda dt: pltpu.VMEM((ncol, 128), dt)
    grid_spec = pltpu.PrefetchScalarGridSpec(
        num_scalar_prefetch=1,
        grid=(b, 2, N_PAGE_GROUPS),
        in_specs=[pl.BlockSpec((1, rq, ZN_W), lambda i, ph, gi, pt: (i, 0, 0)),
                  pl.BlockSpec((1, HALF_PAGE_ROWS, WINDOW), lambda i, ph, gi, pt: (layer * b + i, 0, 0)),
                  const(pool), const(e0)]
        + [pl.BlockSpec((1, HALF_PAGE_ROWS, PAGE_SIZE), page_map(kidx)) for kidx in range(PAGES_PER_STEP)],
        out_specs=pl.BlockSpec((1, rq, 512), lambda i, ph, gi, pt: (i, 0, 0)),
        scratch_shapes=[pltpu.VMEM((N_PAST_SEL, 128), F32) for _ in range(4)]
        + [sq(BF16), pltpu.VMEM((N_PAST_SEL, ncol), BF16), sq(F32),
           pltpu.VMEM((ncol, 1), F32), pltpu.VMEM((ncol, 1), F32), sq(F32)],
    )
    return pl.pallas_call(
        functools.partial(_nsa_sample_kernel, n_tok=n_tok),
        grid_spec=grid_spec,
        out_shape=jax.ShapeDtypeStruct((b, rq, 512), F32),
        compiler_params=_cparams(("parallel", "arbitrary", "arbitrary")),
        name="nsa_sample",
    )(page_table, zn, win_t, pool, e0, *([cache_t] * PAGES_PER_STEP))


def _merge_kernel(x_ref, bg_ref, br_ref, bn_ref, mg_ref, wbr_ref, wout_ref, lng_ref, lnb_ref, o_ref):
    acc = _sigmoid(mg_ref[:, 0:1024]) * _mm(bg_ref[...], wbr_ref[0])
    acc = acc + _sigmoid(mg_ref[:, 1024:2048]) * _mm(br_ref[...], wbr_ref[1])
    acc = acc + _sigmoid(mg_ref[:, 2048:3072]) * _mm(bn_ref[...], wbr_ref[2])
    xf = DN_ALPHA * x_ref[...] + _mm(acc, wout_ref[...])
    mu = jnp.mean(xf, axis=-1, keepdims=True)
    d = xf - mu
    var = jnp.mean(d * d, axis=-1, keepdims=True)
    o_ref[...] = d * lax.rsqrt(var + LN_EPS) * lng_ref[...] + lnb_ref[...]


def _merge(x, o_gla, o_rwkv, o_nsa, zm, w_br, w_out, ln_g, ln_b, tm):
    m = x.shape[0]
    row = lambda n: pl.BlockSpec((tm, n), lambda i: (i, 0))
    return pl.pallas_call(
        _merge_kernel,
        grid=(m // tm,),
        in_specs=[row(D_MODEL), row(512), row(512), row(512), row(ZM_W),
                  pl.BlockSpec((3, 512, D_MODEL), lambda i: (0, 0, 0)),
                  pl.BlockSpec((D_MODEL, D_MODEL), lambda i: (0, 0)),
                  pl.BlockSpec((1, D_MODEL), lambda i: (0, 0)),
                  pl.BlockSpec((1, D_MODEL), lambda i: (0, 0))],
        out_specs=row(D_MODEL),
        out_shape=jax.ShapeDtypeStruct((m, D_MODEL), F32),
        compiler_params=_cparams(("parallel",)),
        name="merge",
    )(x, o_gla, o_rwkv, o_nsa, zm, w_br, w_out, ln_g, ln_b)


def _pack_weights(w_in, b_in):
    def pack(a):
        z = lambda n: jnp.zeros(a.shape[:-1] + (n,), a.dtype)
        gla = jnp.concatenate([a[..., 0:1024], a[..., 1040:1552], a[..., 1024:1040], z(ZG_W - 1552)], axis=-1)
        rwkv = a[..., 1552:3728]
        nsa = jnp.concatenate([a[..., 4240:5008], a[..., 5008:5032], z(1024 - 792), a[..., 3728:4240],
                               a[..., 5032:5544]], axis=-1)
        mg = a[..., 5544:8616]
        return gla, rwkv, nsa, mg
    ws = [w.astype(BF16) for w in pack(w_in)]
    bs = [b[:, None, :] for b in pack(b_in)]
    return ws, bs


def kernel(x_prompt, x_sample, cache_nsa_kv, state_nsa_win, state_gla, state_rwkv, state_rwkv_shift, page_table,
           w_in, b_in, gla_a_up, gla_a_bias, gla_norm, rwkv_mu, rwkv_w0, rwkv_w_up, rwkv_a0, rwkv_a_up, rwkv_k_k,
           rwkv_k_a, rwkv_r_k, rwkv_ln_w, rwkv_ln_b, w_br, w_out, ln_g, ln_b):
    bp, tp, _ = x_prompt.shape
    bs, ts, _ = x_sample.shape
    n_pool = cache_nsa_kv.shape[1]
    ws, bws = _pack_weights(w_in, b_in)
    w_br_b = w_br.astype(BF16)
    w_out_b = w_out.astype(BF16)
    seg = (jnp.arange(RWKV_W)[:, None] // RWKV_HD == jnp.arange(RWKV_W)[None, :] // RWKV_HD).astype(BF16)
    cache_t = jnp.transpose(cache_nsa_kv, (0, 1, 3, 4, 5, 2)).reshape(DEPTH * n_pool, 8 * NSA_HD, PAGE_SIZE)
    win_t = jnp.transpose(state_nsa_win, (0, 1, 3, 4, 5, 2)).reshape(DEPTH * bs, HALF_PAGE_ROWS, WINDOW)
    row2 = lambda a: a.reshape(DEPTH, 1, -1)
    gla_a_bias2, gla_norm2 = row2(gla_a_bias), row2(gla_norm)
    r_par = [row2(rwkv_mu), row2(rwkv_w0), rwkv_w_up, row2(rwkv_a0), rwkv_a_up, row2(rwkv_k_k), row2(rwkv_k_a),
             row2(rwkv_r_k), row2(rwkv_ln_w), row2(rwkv_ln_b)]
    ln_g2, ln_b2 = row2(ln_g), row2(ln_b)

    xp = x_prompt.reshape(bp * tp, D_MODEL)
    xs = jnp.pad(x_sample, ((0, 0), (0, SAMPLE_TPAD - ts), (0, 0))).reshape(bs * SAMPLE_TPAD, D_MODEL)
    zeros_gla = jnp.zeros((bp, GLA_HEADS, GLA_DK, GLA_DV), F32)
    zeros_rwkv = jnp.zeros((bp, RWKV_HEADS, RWKV_HD, RWKV_HD), F32)
    zeros_shift = jnp.zeros((bp, 1, RWKV_IN), F32)

    outs = {k: [] for k in ("kv_p", "kv_s", "win_p", "win_s", "gla_p", "gla_s", "rwkv_p", "rwkv_s", "sh_p", "sh_s")}
    for l in range(DEPTH):
        rp = [p[l] for p in r_par]
        zg, zr, zn, zm = (_proj(xp, ws[i][l], bws[i][l], 512) for i in range(4))
        zn3 = zn.reshape(bp, tp, ZN_W)
        o_gla, gla_st = _gla(zg.reshape(bp, tp, ZG_W), zeros_gla, gla_a_up[l], gla_a_bias2[l], gla_norm2[l],
                             tb_rows=512, chunk=GLA_CHUNK, t_valid=None)
        zr3 = zr.reshape(bp, tp, RWKV_IN)
        o_rwkv, rwkv_st = _rwkv(zr3, zeros_rwkv, zeros_shift, seg, rp, tb_rows=512, chunk=RWKV_CHUNK, t_valid=None)
        o_nsa = _nsa_prompt(zn3, tq=256)
        xp = _merge(xp, o_gla.reshape(bp * tp, 512), o_rwkv.reshape(bp * tp, 512), o_nsa.reshape(bp * tp, 512), zm,
                    w_br_b[l], w_out_b[l], ln_g2[l], ln_b2[l], 512)
        outs["kv_p"].append(zn3[:, :, 0:512].reshape(bp, tp, 4, NSA_KV_HEADS, NSA_HD))
        outs["win_p"].append(zn3[:, tp - WINDOW:, 512:768].reshape(bp, WINDOW, 2, NSA_KV_HEADS, NSA_HD))
        outs["gla_p"].append(gla_st)
        outs["rwkv_p"].append(rwkv_st)
        outs["sh_p"].append(zr3[:, tp - 1, :])
        rows_s = bs * SAMPLE_TPAD
        zg, zr, zn, zm = (_proj(xs, ws[i][l], bws[i][l], rows_s) for i in range(4))
        zn3 = zn.reshape(bs, SAMPLE_TPAD, ZN_W)
        o_gla, gla_st = _gla(zg.reshape(bs, SAMPLE_TPAD, ZG_W), state_gla[l], gla_a_up[l], gla_a_bias2[l],
                             gla_norm2[l], tb_rows=SAMPLE_TPAD, chunk=SAMPLE_TPAD, t_valid=ts)
        zr3 = zr.reshape(bs, SAMPLE_TPAD, RWKV_IN)
        o_rwkv, rwkv_st = _rwkv(zr3, state_rwkv[l], state_rwkv_shift[l][:, None, :], seg, rp,
                                tb_rows=SAMPLE_TPAD, chunk=SAMPLE_TPAD, t_valid=ts)
        o_nsa = _nsa_sample(page_table, zn3, win_t, cache_t, layer=l, n_pool=n_pool, n_tok=ts)
        xs = _merge(xs, o_gla.reshape(rows_s, 512), o_rwkv.reshape(rows_s, 512), o_nsa.reshape(rows_s, 512), zm,
                    w_br_b[l], w_out_b[l], ln_g2[l], ln_b2[l], rows_s)
        outs["kv_s"].append(zn3[:, 0:ts, 0:512].reshape(bs, ts, 4, NSA_KV_HEADS, NSA_HD))
        new_win = zn3[:, 0:ts, 512:768].reshape(bs, ts, 2, NSA_KV_HEADS, NSA_HD)
        outs["win_s"].append(jnp.concatenate([state_nsa_win[l][:, ts:], new_win], axis=1))
        outs["gla_s"].append(gla_st)
        outs["rwkv_s"].append(rwkv_st)
        outs["sh_s"].append(zr3[:, ts - 1, :])

    st = lambda k: jnp.stack(outs[k])
    y_prompt = xp.reshape(bp, tp, D_MODEL)
    y_sample = xs.reshape(bs, SAMPLE_TPAD, D_MODEL)[:, 0:ts]
    return (y_prompt, y_sample, st("kv_p"), st("kv_s"), st("win_p"), st("win_s"), st("gla_p"), st("gla_s"),
            st("rwkv_p"), st("rwkv_s"), st("sh_p"), st("sh_s"))
```

```python
import functools

import jax
import jax.numpy as jnp
from jax import lax
from jax.experimental import pallas as pl
from jax.experimental.pallas import tpu as pltpu

F32 = jnp.float32
BF16 = jnp.bfloat16

D_MODEL = 1024
DEPTH = 2
PAST_LEN = 16384
PAGE_SIZE = 128
N_PAGES = PAST_LEN // PAGE_SIZE

GLA_HEADS, GLA_DK, GLA_DV = 4, 64, 128
GLA_K, GLA_V, GLA_LORA = 256, 512, 16
GLA_GATE_NORM = 16.0
GLA_CHUNK = 64
GLA_SUB = 16

RWKV_HEADS, RWKV_HD, RWKV_W = 8, 64, 512
RWKV_IN = 2176
RWKV_LN_EPS = 64e-5
RWKV_CHUNK = 64

NSA_HEADS, NSA_KV_HEADS, NSA_GROUP, NSA_HD = 8, 2, 4, 64
CMP_BLOCK, SEL_BLOCK, N_SELECT, WINDOW = 32, 64, 16, 512
FORCE_SCORE = 1e9
NEG = -1e30

DN_ALPHA = (2 * DEPTH) ** 0.25
LN_EPS = 1e-5
NORM_EPS = 1e-6

ZG_W = 1664
ZN_W = 2048
ZM_W = 3072
SAMPLE_TPAD = 16
VMEM_LIMIT = 56 * 1024 * 1024


def _mm(a, b):
    return jnp.dot(a.astype(BF16), b.astype(BF16), preferred_element_type=F32)


def _mm_nt(a, b):
    return lax.dot_general(a.astype(BF16), b.astype(BF16), (((1,), (1,)), ((), ())), preferred_element_type=F32)


def _mm_tn(a, b):
    return lax.dot_general(a.astype(BF16), b.astype(BF16), (((0,), (0,)), ((), ())), preferred_element_type=F32)


def _mm_exact(a, b):
    return jnp.dot(a, b, preferred_element_type=F32, precision=lax.Precision.HIGHEST)


def _split2(a):
    hi = a.astype(BF16)
    return hi, (a - hi.astype(F32)).astype(BF16)


def _mm_split(a, b01):
    hi, lo = _split2(a)
    return jnp.dot(hi, b01, preferred_element_type=F32) + jnp.dot(lo, b01, preferred_element_type=F32)


def _softplus(x):
    return jnp.maximum(x, 0.0) + jnp.log(1.0 + jnp.exp(-jnp.abs(x)))


def _sigmoid(x):
    return 0.5 * jnp.tanh(0.5 * x) + 0.5


def _silu(x):
    return x * _sigmoid(x)


def _iota(shape, dim):
    return lax.broadcasted_iota(jnp.int32, shape, dim)


def _cparams(sem):
    return pltpu.CompilerParams(dimension_semantics=sem, vmem_limit_bytes=VMEM_LIMIT)


def _proj_kernel(x_ref, w_ref, b_ref, o_ref):
    o_ref[...] = _mm(x_ref[...], w_ref[...]) + b_ref[...]


def _proj(x, w, b, tm):
    m, k = x.shape
    n = w.shape[1]
    return pl.pallas_call(
        _proj_kernel,
        grid=(m // tm,),
        in_specs=[pl.BlockSpec((tm, k), lambda i: (i, 0)),
                  pl.BlockSpec((k, n), lambda i: (0, 0)),
                  pl.BlockSpec((1, n), lambda i: (0, 0))],
        out_specs=pl.BlockSpec((tm, n), lambda i: (i, 0)),
        out_shape=jax.ShapeDtypeStruct((m, n), F32),
        compiler_params=_cparams(("parallel",)),
        name="proj",
    )(x, w, b)


def _gla_kernel(zg_ref, s0_ref, aup_ref, abias_ref, norm_ref, o_ref, sout_ref, st_scr, *, tb_rows, chunk, t_valid):
    tb = pl.program_id(1)
    c_rows = chunk
    sub = min(GLA_SUB, c_rows)
    nsub = c_rows // sub

    @pl.when(tb == 0)
    def _():
        for h in range(GLA_HEADS):
            st_scr[h] = s0_ref[0, h].T

    tri = (_iota((c_rows, c_rows), 1) <= _iota((c_rows, c_rows), 0)).astype(F32)
    ones_red = jnp.ones((GLA_DK, 128), BF16)
    lane_s = _iota((sub, 128), 1)
    row_s = _iota((sub, 128), 0)
    col_c = _iota((sub, c_rows), 1)

    def chunk_body(c, carry):
        r0 = pl.multiple_of(c * c_rows, c_rows)
        z = zg_ref[0, pl.ds(r0, c_rows), :]
        q = z[:, 0:256] * (GLA_DK ** -0.5)
        k = z[:, 256:512]
        v = z[:, 512:1024]
        g = z[:, 1024:1536]
        ga = z[:, 1536:1552]
        la = -_softplus(-(_mm(ga, aup_ref[...]) + abias_ref[...])) * (1.0 / GLA_GATE_NORM)
        if t_valid is not None:
            ok = (tb * tb_rows + r0 + _iota((c_rows, 1), 0)) < t_valid
            la = jnp.where(ok, la, 0.0)
            k = jnp.where(ok, k, 0.0)
            v = jnp.where(ok, v, 0.0)
        cum = _mm_exact(tri, la)
        heads = range(GLA_HEADS)
        pairs = [(h, blk) for h in heads for blk in range(nsub)]
        qh = [q[:, 64 * h:64 * h + 64] for h in heads]
        kh = [k[:, 64 * h:64 * h + 64] for h in heads]
        ch = [cum[:, 64 * h:64 * h + 64] for h in heads]
        vh = [v[:, 128 * h:128 * h + 128] for h in heads]
        st = [st_scr[h] for h in heads]
        o_in = [_mm_nt(qh[h] * jnp.exp(ch[h]), st[h]) for h in heads]
        red, off = {}, {}
        for h, blk in pairs:
            sl = slice(blk * sub, (blk + 1) * sub)
            q_i, k_i, c_i = qh[h][sl], kh[h][sl], ch[h][sl]
            es = [q_i * k_i[j:j + 1] * jnp.exp(jnp.minimum(c_i - c_i[j:j + 1], 0.0)) for j in range(sub)]
            red[h, blk] = _mm(jnp.concatenate(es, axis=0), ones_red)
            if blk > 0:
                b_i = ch[h][blk * sub - 1:blk * sub]
                q_t = q_i * jnp.exp(c_i - b_i)
                k_t = kh[h] * jnp.exp(jnp.minimum(b_i - ch[h], 0.0))
                off[h, blk] = _mm_nt(q_t, k_t)
        att = []
        for h in heads:
            att_rows = []
            for blk in range(nsub):
                a_i = jnp.zeros((sub, 128), F32)
                for j in range(sub):
                    a_i = a_i + jnp.where((lane_s == blk * sub + j) & (row_s >= j),
                                          red[h, blk][j * sub:(j + 1) * sub], 0.0)
                a_i = a_i[:, 0:c_rows]
                if blk > 0:
                    a_i = a_i + jnp.where(col_c < blk * sub, off[h, blk], 0.0)
                att_rows.append(a_i)
            att.append(att_rows[0] if nsub == 1 else jnp.concatenate(att_rows, axis=0))
        o = [o_in[h] + _mm(att[h], vh[h]) for h in heads]
        last = [ch[h][c_rows - 1:c_rows] for h in heads]
        st_new = [st[h] * jnp.exp(last[h]) + _mm_tn(vh[h], kh[h] * jnp.exp(last[h] - ch[h])) for h in heads]
        for h in heads:
            st_scr[h] = st_new[h]
            oh = o[h] * lax.rsqrt(jnp.mean(o[h] * o[h], axis=-1, keepdims=True) + NORM_EPS)
            oh = oh * norm_ref[:, 128 * h:128 * h + 128] * _silu(g[:, 128 * h:128 * h + 128])
            o_ref[0, pl.ds(r0, c_rows), 128 * h:128 * h + 128] = oh
        return carry

    lax.fori_loop(0, tb_rows // c_rows, chunk_body, 0, unroll=min(2, tb_rows // c_rows))

    @pl.when(tb == pl.num_programs(1) - 1)
    def _():
        for h in range(GLA_HEADS):
            sout_ref[0, h] = st_scr[h].T


def _gla(zg, s0, a_up, a_bias, norm_g, *, tb_rows, chunk, t_valid):
    b, t, _ = zg.shape
    kern = functools.partial(_gla_kernel, tb_rows=tb_rows, chunk=chunk, t_valid=t_valid)
    return pl.pallas_call(
        kern,
        grid=(b, t // tb_rows),
        in_specs=[pl.BlockSpec((1, tb_rows, ZG_W), lambda i, j: (i, j, 0)),
                  pl.BlockSpec((1, GLA_HEADS, GLA_DK, GLA_DV), lambda i, j: (i, 0, 0, 0)),
                  pl.BlockSpec((GLA_LORA, GLA_K), lambda i, j: (0, 0)),
                  pl.BlockSpec((1, GLA_K), lambda i, j: (0, 0)),
                  pl.BlockSpec((1, GLA_V), lambda i, j: (0, 0))],
        out_specs=[pl.BlockSpec((1, tb_rows, GLA_V), lambda i, j: (i, j, 0)),
                   pl.BlockSpec((1, GLA_HEADS, GLA_DK, GLA_DV), lambda i, j: (i, 0, 0, 0))],
        out_shape=[jax.ShapeDtypeStruct((b, t, GLA_V), F32),
                   jax.ShapeDtypeStruct((b, GLA_HEADS, GLA_DK, GLA_DV), F32)],
        scratch_shapes=[pltpu.VMEM((GLA_HEADS, GLA_DV, GLA_DK), F32)],
        compiler_params=_cparams(("parallel", "arbitrary")),
        name="gla",
    )(zg, s0, a_up, a_bias, norm_g)


def _rwkv_kernel(zr_ref, s0_ref, sh0_ref, seg_ref, mu_ref, w0_ref, wup_ref, a0_ref, aup_ref, kk_ref, ka_ref, rk_ref,
                 lnw_ref, lnb_ref, y_ref, sout_ref,
                 s_scr, prev_scr, lw_s, kk_s, kka_s, k2_s, r_s, v_s, y_s, *, tb_rows, chunk, t_valid):
    tb = pl.program_id(1)
    c_rows = chunk
    nh = RWKV_HEADS

    @pl.when(tb == 0)
    def _():
        s_scr[...] = s0_ref[0]
        prev_scr[...] = sh0_ref[0]

    z = zr_ref[0]
    rows = _iota((tb_rows, 1), 0)
    zp = jnp.where(rows == 0, prev_scr[...], pltpu.roll(z, 1, axis=0))
    prev_scr[...] = z[tb_rows - 1:tb_rows]
    zs = z + (zp - z) * mu_ref[...]
    r = zs[:, 0:512]
    k = zs[:, 512:1024]
    v = zs[:, 1024:1536]
    wl = zs[:, 1536:1600]
    al = zs[:, 1600:1664]
    w = -_softplus(-(w0_ref[...] + _mm(jnp.tanh(wl), wup_ref[...]))) - 0.5
    lw = -jnp.exp(w)
    a = _sigmoid(a0_ref[...] + _mm(al, aup_ref[...]))
    kk = k * kk_ref[...]
    kk = kk * lax.rsqrt(_mm_split(kk * kk, seg_ref[...]) + NORM_EPS)
    k2 = k * (1.0 + (a - 1.0) * ka_ref[...])
    kka = kk * a
    if t_valid is not None:
        ok = (tb * tb_rows + rows) < t_valid
        lw = jnp.where(ok, lw, 0.0)
        kka = jnp.where(ok, kka, 0.0)
        k2 = jnp.where(ok, k2, 0.0)
    lw_s[...] = lw
    kk_s[...] = kk
    kka_s[...] = kka
    k2_s[...] = k2
    r_s[...] = r
    v_s[...] = v

    ri = _iota((c_rows, c_rows), 0)
    ci = _iota((c_rows, c_rows), 1)
    tri = (ci <= ri).astype(F32)
    strict = ci < ri
    incl = ci <= ri
    n_dbl = max(1, (c_rows - 1).bit_length())

    def chunk_body(c, carry):
        r0 = pl.multiple_of(c * c_rows, c_rows)
        ds = pl.ds(r0, c_rows)
        lwc = lw_s[ds, :]
        cl = _mm_exact(tri, lwc)
        e_inv = jnp.exp(-cl)
        e_fwd = jnp.exp(cl)
        e_prev = jnp.exp(cl - lwc)
        e_end = jnp.exp(cl[c_rows - 1:c_rows] - cl)
        g_end = jnp.exp(cl[c_rows - 1:c_rows])
        kkc, kkac, k2c, rc, vc = kk_s[ds, :], kka_s[ds, :], k2_s[ds, :], r_s[ds, :], v_s[ds, :]
        heads = range(nh)
        hsl = [slice(64 * h, 64 * h + 64) for h in heads]
        a_t = [-kkac[:, hs] * e_inv[:, hs] for hs in hsl]
        b_t = [kkc[:, hs] * e_prev[:, hs] for hs in hsl]
        k_t = [k2c[:, hs] * e_inv[:, hs] for hs in hsl]
        r_t = [rc[:, hs] * e_fwd[:, hs] for hs in hsl]
        vh = [vc[:, hs] for hs in hsl]
        cc = c_rows
        ak = [jnp.concatenate([a_t[h], k_t[h]], axis=0).astype(BF16) for h in heads]
        bra = [_mm_nt(jnp.concatenate([b_t[h], r_t[h]], axis=0), ak[h]) for h in heads]
        s0 = [s_scr[h] for h in heads]
        l_k = [jnp.where(strict, bra[h][0:cc, cc:2 * cc], 0.0) for h in heads]
        m_k = [jnp.where(incl, bra[h][cc:2 * cc, cc:2 * cc], 0.0) for h in heads]
        lmv = [_mm(jnp.concatenate([l_k[h], m_k[h]], axis=0), vh[h]) for h in heads]
        x = [jnp.concatenate([b_t[h], lmv[h][0:cc]], axis=1) for h in heads]
        lp = [jnp.where(strict, bra[h][0:cc, 0:cc], 0.0).astype(BF16) for h in heads]
        for step in range(n_dbl):
            if step + 1 < n_dbl:
                new = [_mm(lp[h], jnp.concatenate([x[h].astype(BF16), lp[h]], axis=1)) for h in heads]
                x = [x[h] + new[h][:, 0:128] for h in heads]
                lp = [new[h][:, 128:128 + cc].astype(BF16) for h in heads]
            else:
                x = [x[h] + _mm(lp[h], x[h]) for h in heads]
        prs = [_mm_nt(jnp.concatenate([x[h][:, 0:64], r_t[h]], axis=0), s0[h]) for h in heads]
        u = [prs[h][0:cc] + x[h][:, 64:128] for h in heads]
        m_a = [jnp.where(incl, bra[h][cc:2 * cc, 0:cc], 0.0) for h in heads]
        y = [prs[h][cc:2 * cc] + _mm(m_a[h], u[h]) + lmv[h][cc:2 * cc] for h in heads]
        akg = [jnp.concatenate([-kkac[:, hs] * e_end[:, hs], k2c[:, hs] * e_end[:, hs]], axis=0) for hs in hsl]
        s_new = [s0[h] * g_end[:, hsl[h]] + _mm_tn(jnp.concatenate([u[h], vh[h]], axis=0), akg[h]) for h in heads]
        for h in heads:
            s_scr[h] = s_new[h]
            y_s[ds, hsl[h]] = y[h]
        return carry

    lax.fori_loop(0, tb_rows // c_rows, chunk_body, 0, unroll=min(2, tb_rows // c_rows))

    y = y_s[...]
    seg = seg_ref[...]
    mean = _mm_split(y, seg) * (1.0 / RWKV_HD)
    d = y - mean
    var = _mm_split(d * d, seg) * (1.0 / RWKV_HD)
    yn = d * lax.rsqrt(var + RWKV_LN_EPS) * lnw_ref[...] + lnb_ref[...]
    bonus = _mm_split(r * k2 * rk_ref[...], seg) * v
    y_ref[0] = (yn + bonus) * _silu(zs[:, 1664:2176])

    @pl.when(tb == pl.num_programs(1) - 1)
    def _():
        sout_ref[0] = s_scr[...]


def _rwkv(zr, s0, sh0, seg, params, *, tb_rows, chunk, t_valid):
    b, t, _ = zr.shape
    kern = functools.partial(_rwkv_kernel, tb_rows=tb_rows, chunk=chunk, t_valid=t_valid)
    full = lambda shp: pl.BlockSpec(shp, lambda i, j: (0,) * len(shp))
    mu, w0, w_up, a0, a_up, k_k, k_a, r_k, ln_w, ln_b = params
    return pl.pallas_call(
        kern,
        grid=(b, t // tb_rows),
        in_specs=[pl.BlockSpec((1, tb_rows, RWKV_IN), lambda i, j: (i, j, 0)),
                  pl.BlockSpec((1, RWKV_HEADS, RWKV_HD, RWKV_HD), lambda i, j: (i, 0, 0, 0)),
                  pl.BlockSpec((1, 1, RWKV_IN), lambda i, j: (i, 0, 0)),
                  full((RWKV_W, RWKV_W)), full((1, RWKV_IN)), full((1, RWKV_W)), full((64, RWKV_W)),
                  full((1, RWKV_W)), full((64, RWKV_W)), full((1, RWKV_W)), full((1, RWKV_W)), full((1, RWKV_W)),
                  full((1, RWKV_W)), full((1, RWKV_W))],
        out_specs=[pl.BlockSpec((1, tb_rows, RWKV_W), lambda i, j: (i, j, 0)),
                   pl.BlockSpec((1, RWKV_HEADS, RWKV_HD, RWKV_HD), lambda i, j: (i, 0, 0, 0))],
        out_shape=[jax.ShapeDtypeStruct((b, t, RWKV_W), F32),
                   jax.ShapeDtypeStruct((b, RWKV_HEADS, RWKV_HD, RWKV_HD), F32)],
        scratch_shapes=[pltpu.VMEM((RWKV_HEADS, RWKV_HD, RWKV_HD), F32), pltpu.VMEM((1, RWKV_IN), F32)]
        + [pltpu.VMEM((tb_rows, RWKV_W), F32) for _ in range(7)],
        compiler_params=_cparams(("parallel", "arbitrary")),
        name="rwkv",
    )(zr, s0, sh0, seg, mu, w0, w_up, a0, a_up, k_k, k_a, r_k, ln_w, ln_b)


def _slope(h):
    return 2.0 ** (-8.0 * (h + 1) / NSA_HEADS)


def _flash_init(m_ref, l_ref, acc_ref):
    m_ref[...] = jnp.full(m_ref.shape, NEG, F32)
    l_ref[...] = jnp.zeros(l_ref.shape, F32)
    acc_ref[...] = jnp.zeros(acc_ref.shape, F32)


def _flash_step_t(s_t, v, m_ref, l_ref, acc_ref):
    m_old = m_ref[...]
    m_new = jnp.maximum(m_old, jnp.max(s_t, axis=0, keepdims=True))
    p = jnp.exp(s_t - m_new)
    alpha = jnp.exp(m_old - m_new)
    l_ref[...] = alpha * l_ref[...] + jnp.sum(p, axis=0, keepdims=True)
    acc_ref[...] = alpha * acc_ref[...] + _mm_tn(v, p)
    m_ref[...] = m_new


def _flash_steps_t(s_list, v_list, m_ref, l_ref, acc_ref):
    ks = range(len(s_list))
    m_old = [m_ref[k] for k in ks]
    m_new = [jnp.maximum(m_old[k], jnp.max(s_list[k], axis=0, keepdims=True)) for k in ks]
    p = [jnp.exp(s_list[k] - m_new[k]) for k in ks]
    alpha = [jnp.exp(m_old[k] - m_new[k]) for k in ks]
    pv = [_mm_tn(v_list[k], p[k]) for k in ks]
    for k in ks:
        l_ref[k] = alpha[k] * l_ref[k] + jnp.sum(p[k], axis=0, keepdims=True)
        acc_ref[k] = alpha[k] * acc_ref[k] + pv[k]
        m_ref[k] = m_new[k]


def _flash_result(l_ref, acc_ref, k):
    l = l_ref[k]
    return acc_ref[k] / jnp.where(l > 0, l, 1.0)


SEL_LANE0 = 66
BLOCK_PENALTY = -(2.0 ** 100)


def _nsa_prompt_kernel(q_ref, kv_ref, bg_ref, g_ref, o_ref, kc_s, vc_s, ks_s, vs_s, kw_s, vw_s, qa_s, qs_s, oc_s,
                       m_s, l_s, acc_s, m2_s, l2_s, acc2_s, *, t_len, tq):
    i = pl.program_id(1)
    nb = t_len // SEL_BLOCK
    tk = tq
    ncol = NSA_GROUP * tq
    win_tiles = WINDOW // tk

    def aug_lanes(lane, pos):
        return jnp.where(lane == 64, lax.shift_right_logical(pos, 7).astype(F32),
                         jnp.where(lane == 65, (pos & 127).astype(F32), 0.0))

    def head_lanes(x, kvh):
        return x if kvh == 0 else pltpu.roll(x, 64, axis=1)

    @pl.when(i == 0)
    def _():
        lane = _iota((t_len, 128), 1)
        pos = _iota((t_len, 128), 0)
        aug = aug_lanes(lane, pos)
        aug_sel = jnp.where(lane - SEL_LANE0 == lax.shift_right_logical(pos, 6), 1.0, aug)
        for kcol, vcol, k_dst, v_dst, k_aug in ((256, 384, ks_s, vs_s, aug_sel), (512, 640, kw_s, vw_s, aug)):
            kf = kv_ref[0, :, kcol:kcol + 128]
            vf = kv_ref[0, :, vcol:vcol + 128]
            for kvh in range(NSA_KV_HEADS):
                k_dst[kvh] = jnp.where(lane < 64, head_lanes(kf, kvh), k_aug).astype(BF16)
                v_dst[kvh] = jnp.where(lane < 64, head_lanes(vf, kvh), 0.0).astype(BF16)
        kcm = kv_ref[0, :, 0:128].reshape(nb, SEL_BLOCK, 128)
        vcm = kv_ref[0, :, 128:256].reshape(nb, SEL_BLOCK, 128)
        inv = 1.0 / CMP_BLOCK
        kc = jnp.concatenate([jnp.sum(kcm[:, 0:CMP_BLOCK], axis=1) * inv,
                              jnp.sum(kcm[:, CMP_BLOCK:SEL_BLOCK], axis=1) * inv], axis=0)
        vc = jnp.concatenate([jnp.sum(vcm[:, 0:CMP_BLOCK], axis=1) * inv,
                              jnp.sum(vcm[:, CMP_BLOCK:SEL_BLOCK], axis=1) * inv], axis=0)
        lane_c = _iota((2 * nb, 128), 1)
        r_c = _iota((2 * nb, 128), 0)
        cend = jnp.where(r_c < nb, SEL_BLOCK * r_c + (CMP_BLOCK - 1), SEL_BLOCK * (r_c - nb) + (SEL_BLOCK - 1))
        aug_c = aug_lanes(lane_c, cend)
        for kvh in range(NSA_KV_HEADS):
            kc_s[kvh] = jnp.where(lane_c < 64, head_lanes(kc, kvh), aug_c).astype(BF16)
            vc_s[kvh] = jnp.where(lane_c < 64, head_lanes(vc, kvh), 0.0).astype(BF16)

    lane_q = _iota((tq, 128), 1)
    for h in range(NSA_HEADS):
        qb = q_ref[0, :, 128 * (h // 2):128 * (h // 2) + 128]
        if h % 2:
            qb = pltpu.roll(qb, 64, axis=1)
        sl = _slope(h)
        qa = jnp.where(lane_q < 64, qb * (NSA_HD ** -0.5),
                       jnp.where(lane_q == 64, 128.0 * sl, jnp.where(lane_q == 65, sl, 0.0)))
        qa_s[h // NSA_GROUP, (h % NSA_GROUP) * tq:(h % NSA_GROUP + 1) * tq, :] = qa.astype(BF16)

    t_row = i * tq + (_iota((1, ncol), 1) & (tq - 1))
    t_row1 = i * tq + _iota((1, tq), 1)
    r_c1 = _iota((2 * nb, 1), 0)
    cend_col = jnp.where(r_c1 < nb, SEL_BLOCK * r_c1 + (CMP_BLOCK - 1), SEL_BLOCK * (r_c1 - nb) + (SEL_BLOCK - 1))
    mask_c = cend_col <= t_row
    jrow = _iota((nb, 1), 0)
    tblk = lax.shift_right_logical(t_row1, 6)
    forced = (jrow == tblk) | (jrow == 0)
    valid = jrow <= tblk
    bg_t = bg_ref[0].T

    for kvh in range(NSA_KV_HEADS):
        qa = qa_s[kvh]
        s_c = jnp.where(mask_c, _mm_nt(kc_s[kvh], qa), NEG)
        m = jnp.max(s_c, axis=0, keepdims=True)
        p = jnp.where(mask_c, jnp.exp(s_c - m), 0.0)
        den = jnp.sum(p, axis=0, keepdims=True)
        p = p / jnp.where(den > 0, den, 1.0)
        oc_s[kvh] = _mm_tn(vc_s[kvh], p)
        imp_e = p[0:nb, 0:tq]
        imp_o = p[nb:2 * nb, 0:tq]
        for g in range(1, NSA_GROUP):
            imp_e = imp_e + p[0:nb, g * tq:(g + 1) * tq]
            imp_o = imp_o + p[nb:2 * nb, g * tq:(g + 1) * tq]
        score = jnp.where(forced, FORCE_SCORE, jnp.where(valid, imp_e + imp_o, -FORCE_SCORE))
        rank = jnp.zeros((nb, tq), F32)
        for jp in range(nb):
            rj = score[jp:jp + 1, :]
            rank = rank + jnp.where((rj > score) | ((rj == score) & (jp < jrow)), 1.0, 0.0)
        sel = jnp.where((rank < N_SELECT) & valid, 1.0, 0.0)
        sel_q = jnp.concatenate([sel, jnp.zeros((128 - nb, tq), F32)], axis=0).T
        sel_q = pltpu.roll(sel_q, SEL_LANE0, axis=1)
        pen = jnp.where((lane_q >= SEL_LANE0) & (lane_q < SEL_LANE0 + nb) & (sel_q < 0.5), BLOCK_PENALTY, 0.0)
        for g in range(NSA_GROUP):
            cs = slice(g * tq, (g + 1) * tq)
            qs_s[kvh, cs, :] = (qa_s[kvh, cs, :].astype(F32) + pen).astype(BF16)

    _flash_init(m_s, l_s, acc_s)
    _flash_init(m2_s, l2_s, acc2_s)
    kvs = range(NSA_KV_HEADS)

    def sel_step(kt, diagonal):
        k0 = pl.multiple_of(kt * tk, tk)
        s_t = [_mm_nt(ks_s[kvh, pl.ds(k0, tk), :], qs_s[kvh]) for kvh in kvs]
        if diagonal:
            causal = kt * tk + _iota((tk, 1), 0) <= t_row
            s_t = [jnp.where(causal, s, NEG) for s in s_t]
        _flash_steps_t(s_t, [vs_s[kvh, pl.ds(k0, tk), :] for kvh in kvs], m_s, l_s, acc_s)

    def sel_body(kt, carry):
        sel_step(kt, False)
        return carry

    lax.fori_loop(0, i, sel_body, 0)
    sel_step(i, True)

    def win_step(kt, back):
        k0 = pl.multiple_of(kt * tk, tk)
        s_t = [_mm_nt(kw_s[kvh, pl.ds(k0, tk), :], qa_s[kvh]) for kvh in kvs]
        dist = t_row - (kt * tk + _iota((tk, 1), 0))
        if back == 0:
            s_t = [jnp.where(dist >= 0, s, NEG) for s in s_t]
        elif back == win_tiles:
            s_t = [jnp.where(dist <= WINDOW, s, NEG) for s in s_t]
        _flash_steps_t(s_t, [vw_s[kvh, pl.ds(k0, tk), :] for kvh in kvs], m2_s, l2_s, acc2_s)

    for back in range(win_tiles, 0, -1):
        @pl.when(i >= back)
        def _(back=back):
            win_step(i - back, back)
    win_step(i, 0)

    for kvh in kvs:
        o_sel = _flash_result(l_s, acc_s, kvh)
        o_win = _flash_result(l2_s, acc2_s, kvh)
        o_cmp = oc_s[kvh]
        for g in range(NSA_GROUP):
            h = NSA_GROUP * kvh + g
            cs = slice(g * tq, (g + 1) * tq)
            gates = _sigmoid(bg_t[3 * h:3 * h + 3, :])
            comb = gates[0:1] * o_cmp[:, cs] + gates[1:2] * o_sel[:, cs] + gates[2:3] * o_win[:, cs]
            o_ref[0, :, 64 * h:64 * h + 64] = comb.T[:, 0:64] * _silu(g_ref[0, :, 64 * h:64 * h + 64])


def _nsa_prompt(zn, *, tq):
    b, t, _ = zn.shape
    nb = t // SEL_BLOCK
    ncol = NSA_GROUP * tq
    kern = functools.partial(_nsa_prompt_kernel, t_len=t, tq=tq)
    kv_f32 = lambda n: pltpu.VMEM((NSA_KV_HEADS, n, ncol), F32)
    stat = lambda: [kv_f32(1), kv_f32(1), kv_f32(128)]
    kvbuf = lambda n: pltpu.VMEM((NSA_KV_HEADS, n, 128), BF16)
    return pl.pallas_call(
        kern,
        grid=(b, t // tq),
        in_specs=[pl.BlockSpec((1, tq, 512), lambda i, j: (i, j, 2)),
                  pl.BlockSpec((1, t, 768), lambda i, j: (i, 0, 0)),
                  pl.BlockSpec((1, tq, 256), lambda i, j: (i, j, 3)),
                  pl.BlockSpec((1, tq, 512), lambda i, j: (i, j, 3))],
        out_specs=pl.BlockSpec((1, tq, 512), lambda i, j: (i, j, 0)),
        out_shape=jax.ShapeDtypeStruct((b, t, 512), F32),
        scratch_shapes=[kvbuf(2 * nb), kvbuf(2 * nb), kvbuf(t), kvbuf(t), kvbuf(t), kvbuf(t), kvbuf(ncol),
                        kvbuf(ncol), kv_f32(128)] + stat() + stat(),
        compiler_params=_cparams(("parallel", "arbitrary")),
        name="nsa_prompt",
    )(zn, zn, zn, zn)


PAGES_PER_STEP = 32
N_PAGE_GROUPS = N_PAGES // PAGES_PER_STEP
KEYS_PER_STEP = PAGES_PER_STEP * PAGE_SIZE
SEL_PER_STEP = KEYS_PER_STEP // SEL_BLOCK
N_PAST_SEL = PAST_LEN // SEL_BLOCK
HALF_PAGE_ROWS = 2 * NSA_KV_HEADS * NSA_HD


def _flash_step_r(s, mask, pv, m_ref, l_ref, acc_ref):
    s = jnp.where(mask, s, NEG)
    m_old = m_ref[...]
    m_new = jnp.maximum(m_old, jnp.max(s, axis=-1, keepdims=True))
    p = jnp.exp(s - m_new)
    alpha = jnp.exp(m_old - m_new)
    l_ref[...] = alpha * l_ref[...] + jnp.sum(p, axis=-1, keepdims=True)
    acc_ref[...] = alpha * acc_ref[...] + pv(p)
    m_ref[...] = m_new


def _nsa_sample_kernel(pt_ref, zn_ref, win_ref, pool_ref, e0_ref, *rest, n_tok):
    pages = rest[:PAGES_PER_STEP]
    o_ref = rest[PAGES_PER_STEP]
    kce_s, kco_s, vce_s, vco_s, qt_s, sel_s, oc_s, m_s, l_s, acc_s = rest[PAGES_PER_STEP + 1:]
    ph = pl.program_id(1)
    gi = pl.program_id(2)
    rq = n_tok
    ncol = NSA_HEADS * rq
    npast = N_PAST_SEL
    col = _iota((ncol, 1), 0)
    t_col = PAST_LEN + (col & (rq - 1))
    hcol = lax.shift_right_logical(col, rq.bit_length() - 1)
    slope_col = jnp.zeros((ncol, 1), F32)
    for h in range(NSA_HEADS):
        slope_col = jnp.where(hcol == h, _slope(h), slope_col)

    def bias(pos_row):
        return slope_col * (t_col - pos_row).astype(F32)

    def keys_t(lo):
        return jnp.concatenate([pg[0, lo:lo + 128, :] for pg in pages], axis=1)

    @pl.when((ph == 0) & (gi == 0))
    def _queries():
        zero = jnp.zeros((rq, NSA_HD), F32)
        rows = []
        for h in range(NSA_HEADS):
            qh = zn_ref[0, 0:rq, 1024 + 64 * h:1088 + 64 * h] * (NSA_HD ** -0.5)
            rows.append(jnp.concatenate([qh, zero] if h < NSA_GROUP else [zero, qh], axis=1))
        qt_s[...] = jnp.concatenate(rows, axis=0).astype(BF16)

    @pl.when(ph == 0)
    def _pool():
        dn = (((1,), (1,)), ((), ()))
        dst_e = pl.ds(pl.multiple_of(gi * SEL_PER_STEP, SEL_PER_STEP), SEL_PER_STEP)
        for lo, even_s, odd_s in ((0, kce_s, kco_s), (128, vce_s, vco_s)):
            hi_part, lo_part = _split2(keys_t(lo))
            pooled = (lax.dot_general(pool_ref[...], hi_part, dn, preferred_element_type=F32)
                      + lax.dot_general(pool_ref[...], lo_part, dn, preferred_element_type=F32)) * (1.0 / CMP_BLOCK)
            even_s[dst_e, :] = pooled[0:SEL_PER_STEP]
            odd_s[dst_e, :] = pooled[SEL_PER_STEP:2 * SEL_PER_STEP]

    @pl.when((ph == 0) & (gi == N_PAGE_GROUPS - 1))
    def _compressed():
        jrow = _iota((1, npast), 1)
        qt = qt_s[...]
        se = _mm_nt(qt, kce_s[...]) - bias(SEL_BLOCK * jrow + (CMP_BLOCK - 1))
        so = _mm_nt(qt, kco_s[...]) - bias(SEL_BLOCK * jrow + (SEL_BLOCK - 1))
        m = jnp.maximum(jnp.max(se, axis=-1, keepdims=True), jnp.max(so, axis=-1, keepdims=True))
        pe = jnp.exp(se - m)
        po = jnp.exp(so - m)
        den = jnp.sum(pe, axis=-1, keepdims=True) + jnp.sum(po, axis=-1, keepdims=True)
        pe = pe / den
        po = po / den
        oc_s[...] = _mm(pe, vce_s[...]) + _mm(po, vco_s[...])
        jp = _iota((npast, npast), 0)
        jj = _iota((npast, npast), 1)
        groups = []
        for kvh in range(NSA_KV_HEADS):
            base = kvh * NSA_GROUP * rq
            imp_e = pe[base:base + rq]
            imp_o = po[base:base + rq]
            for g in range(1, NSA_GROUP):
                imp_e = imp_e + pe[base + g * rq:base + (g + 1) * rq]
                imp_o = imp_o + po[base + g * rq:base + (g + 1) * rq]
            imp = imp_e + imp_o
            imp_t = jnp.concatenate([imp, jnp.zeros((128 - rq, npast), F32)], axis=0).T
            sel_rows = []
            for t in range(rq):
                colv = imp_t[:, t:t + 1]
                rowv = imp[t:t + 1, :]
                beats = ((colv > rowv) | ((colv == rowv) & (jp < jj))) & (jp >= 1)
                rank = jnp.sum(jnp.where(beats, 1.0, 0.0), axis=0, keepdims=True)
                sel_rows.append(jnp.where((jrow == 0) | (rank < N_SELECT - 2), 1.0, 0.0))
            groups += [jnp.concatenate(sel_rows, axis=0)] * NSA_GROUP
        groups.append(jnp.zeros((128 - ncol, npast), F32))
        sel_s[...] = jnp.concatenate(groups, axis=0).T[:, 0:ncol].astype(BF16)

    @pl.when(ph == 1)
    def _selected():
        @pl.when(gi == 0)
        def _():
            _flash_init(m_s, l_s, acc_s)

        k_t = keys_t(0)
        v_t = keys_t(128)
        pos_row = gi * KEYS_PER_STEP + _iota((1, KEYS_PER_STEP), 1)
        s = _mm(qt_s[...], k_t) - bias(pos_row)
        sel_rows = sel_s[pl.ds(pl.multiple_of(gi * SEL_PER_STEP, SEL_PER_STEP), SEL_PER_STEP), :]
        mask = _mm_tn(sel_rows, e0_ref[...]) > 0.5
        _flash_step_r(s, mask, lambda p: _mm_nt(p, v_t), m_s, l_s, acc_s)

    @pl.when((ph == 1) & (gi == N_PAGE_GROUPS - 1))
    def _finish():
        qt = qt_s[...]
        new = zn_ref[0, 0:rq, 0:768]
        npos_row = PAST_LEN + _iota((1, rq), 1)
        mask_n = npos_row <= t_col
        s = _mm_nt(qt, new[:, 256:384]) - bias(npos_row)
        _flash_step_r(s, mask_n, lambda p: _mm(p, new[:, 384:512]), m_s, l_s, acc_s)
        l = l_s[...]
        o_sel = acc_s[...] / jnp.where(l > 0, l, 1.0)
        wpos_row = PAST_LEN - WINDOW + _iota((1, WINDOW), 1)
        mask_w = (t_col - wpos_row) <= WINDOW
        s1 = jnp.where(mask_w, _mm(qt, win_ref[0, 0:128, :]) - bias(wpos_row), NEG)
        s2 = jnp.where(mask_n, _mm_nt(qt, new[:, 512:640]) - bias(npos_row), NEG)
        m = jnp.maximum(jnp.max(s1, axis=-1, keepdims=True), jnp.max(s2, axis=-1, keepdims=True))
        p1 = jnp.where(mask_w, jnp.exp(s1 - m), 0.0)
        p2 = jnp.where(mask_n, jnp.exp(s2 - m), 0.0)
        den = jnp.sum(p1, axis=-1, keepdims=True) + jnp.sum(p2, axis=-1, keepdims=True)
        o_win = (_mm_nt(p1, win_ref[0, 128:256, :]) + _mm(p2, new[:, 640:768])) / jnp.where(den > 0, den, 1.0)
        o_cmp = oc_s[...]
        for h in range(NSA_HEADS):
            rs = slice(h * rq, (h + 1) * rq)
            ls = slice(64 * (h // NSA_GROUP), 64 * (h // NSA_GROUP) + 64)
            gates = _sigmoid(zn_ref[0, 0:rq, 768 + 3 * h:771 + 3 * h])
            o = gates[:, 0:1] * o_cmp[rs, ls] + gates[:, 1:2] * o_sel[rs, ls] + gates[:, 2:3] * o_win[rs, ls]
            o_ref[0, 0:rq, 64 * h:64 * h + 64] = o * _silu(zn_ref[0, 0:rq, 1536 + 64 * h:1600 + 64 * h])
        o_ref[0, rq:, :] = jnp.zeros((SAMPLE_TPAD - rq, 512), F32)


def _nsa_sample(page_table, zn, win_t, cache_t, *, layer, n_pool, n_tok):
    b = zn.shape[0]
    rq = SAMPLE_TPAD
    assert n_tok & (n_tok - 1) == 0 and n_tok <= rq
    ncol = NSA_HEADS * n_tok
    tok = jnp.arange(KEYS_PER_STEP)
    blk = jnp.arange(SEL_PER_STEP)
    in_blk = tok[None, :] // SEL_BLOCK == blk[:, None]
    first_half = (tok[None, :] % SEL_BLOCK) < CMP_BLOCK
    pool = jnp.concatenate([in_blk & first_half, in_blk & ~first_half], axis=0).astype(BF16)
    e0 = in_blk.astype(BF16)

    def page_map(kidx):
        return lambda i, ph, gi, pt: (layer * n_pool + pt[i, gi * PAGES_PER_STEP + kidx], ph, 0)

    const = lambda a: pl.BlockSpec(a.shape, lambda i, ph, gi, pt: (0, 0))
    sq = lambda dt: pltpu.VMEM((ncol, 128), dt)
    grid_spec = pltpu.PrefetchScalarGridSpec(
        num_scalar_prefetch=1,
        grid=(b, 2, N_PAGE_GROUPS),
        in_specs=[pl.BlockSpec((1, rq, ZN_W), lambda i, ph, gi, pt: (i, 0, 0)),
                  pl.BlockSpec((1, HALF_PAGE_ROWS, WINDOW), lambda i, ph, gi, pt: (layer * b + i, 0, 0)),
                  const(pool), const(e0)]
        + [pl.BlockSpec((1, HALF_PAGE_ROWS, PAGE_SIZE), page_map(kidx)) for kidx in range(PAGES_PER_STEP)],
        out_specs=pl.BlockSpec((1, rq, 512), lambda i, ph, gi, pt: (i, 0, 0)),
        scratch_shapes=[pltpu.VMEM((N_PAST_SEL, 128), F32) for _ in range(4)]
        + [sq(BF16), pltpu.VMEM((N_PAST_SEL, ncol), BF16), sq(F32),
           pltpu.VMEM((ncol, 1), F32), pltpu.VMEM((ncol, 1), F32), sq(F32)],
    )
    return pl.pallas_call(
        functools.partial(_nsa_sample_kernel, n_tok=n_tok),
        grid_spec=grid_spec,
        out_shape=jax.ShapeDtypeStruct((b, rq, 512), F32),
        compiler_params=_cparams(("parallel", "arbitrary", "arbitrary")),
        name="nsa_sample",
    )(page_table, zn, win_t, pool, e0, *([cache_t] * PAGES_PER_STEP))


def _merge_kernel(x_ref, bg_ref, br_ref, bn_ref, mg_ref, wbr_ref, wout_ref, lng_ref, lnb_ref, o_ref):
    acc = _sigmoid(mg_ref[:, 0:1024]) * _mm(bg_ref[...], wbr_ref[0])
    acc = acc + _sigmoid(mg_ref[:, 1024:2048]) * _mm(br_ref[...], wbr_ref[1])
    acc = acc + _sigmoid(mg_ref[:, 2048:3072]) * _mm(bn_ref[...], wbr_ref[2])
    xf = DN_ALPHA * x_ref[...] + _mm(acc, wout_ref[...])
    mu = jnp.mean(xf, axis=-1, keepdims=True)
    d = xf - mu
    var = jnp.mean(d * d, axis=-1, keepdims=True)
    o_ref[...] = d * lax.rsqrt(var + LN_EPS) * lng_ref[...] + lnb_ref[...]


def _merge(x, o_gla, o_rwkv, o_nsa, zm, w_br, w_out, ln_g, ln_b, tm):
    m = x.shape[0]
    row = lambda n: pl.BlockSpec((tm, n), lambda i: (i, 0))
    return pl.pallas_call(
        _merge_kernel,
        grid=(m // tm,),
        in_specs=[row(D_MODEL), row(512), row(512), row(512), row(ZM_W),
                  pl.BlockSpec((3, 512, D_MODEL), lambda i: (0, 0, 0)),
                  pl.BlockSpec((D_MODEL, D_MODEL), lambda i: (0, 0)),
                  pl.BlockSpec((1, D_MODEL), lambda i: (0, 0)),
                  pl.BlockSpec((1, D_MODEL), lambda i: (0, 0))],
        out_specs=row(D_MODEL),
        out_shape=jax.ShapeDtypeStruct((m, D_MODEL), F32),
        compiler_params=_cparams(("parallel",)),
        name="merge",
    )(x, o_gla, o_rwkv, o_nsa, zm, w_br, w_out, ln_g, ln_b)


def _pack_weights(w_in, b_in):
    def pack(a):
        z = lambda n: jnp.zeros(a.shape[:-1] + (n,), a.dtype)
        gla = jnp.concatenate([a[..., 0:1024], a[..., 1040:1552], a[..., 1024:1040], z(ZG_W - 1552)], axis=-1)
        rwkv = a[..., 1552:3728]
        nsa = jnp.concatenate([a[..., 4240:5008], a[..., 5008:5032], z(1024 - 792), a[..., 3728:4240],
                               a[..., 5032:5544]], axis=-1)
        mg = a[..., 5544:8616]
        return gla, rwkv, nsa, mg
    ws = [w.astype(BF16) for w in pack(w_in)]
    bs = [b[:, None, :] for b in pack(b_in)]
    return ws, bs


def kernel(x_prompt, x_sample, cache_nsa_kv, state_nsa_win, state_gla, state_rwkv, state_rwkv_shift, page_table,
           w_in, b_in, gla_a_up, gla_a_bias, gla_norm, rwkv_mu, rwkv_w0, rwkv_w_up, rwkv_a0, rwkv_a_up, rwkv_k_k,
           rwkv_k_a, rwkv_r_k, rwkv_ln_w, rwkv_ln_b, w_br, w_out, ln_g, ln_b):
    bp, tp, _ = x_prompt.shape
    bs, ts, _ = x_sample.shape
    n_pool = cache_nsa_kv.shape[1]
    ws, bws = _pack_weights(w_in, b_in)
    w_br_b = w_br.astype(BF16)
    w_out_b = w_out.astype(BF16)
    seg = (jnp.arange(RWKV_W)[:, None] // RWKV_HD == jnp.arange(RWKV_W)[None, :] // RWKV_HD).astype(BF16)
    cache_t = jnp.transpose(cache_nsa_kv, (0, 1, 3, 4, 5, 2)).reshape(DEPTH * n_pool, 8 * NSA_HD, PAGE_SIZE)
    win_t = jnp.transpose(state_nsa_win, (0, 1, 3, 4, 5, 2)).reshape(DEPTH * bs, HALF_PAGE_ROWS, WINDOW)
    row2 = lambda a: a.reshape(DEPTH, 1, -1)
    gla_a_bias2, gla_norm2 = row2(gla_a_bias), row2(gla_norm)
    r_par = [row2(rwkv_mu), row2(rwkv_w0), rwkv_w_up, row2(rwkv_a0), rwkv_a_up, row2(rwkv_k_k), row2(rwkv_k_a),
             row2(rwkv_r_k), row2(rwkv_ln_w), row2(rwkv_ln_b)]
    ln_g2, ln_b2 = row2(ln_g), row2(ln_b)

    xp = x_prompt.reshape(bp * tp, D_MODEL)
    xs = jnp.pad(x_sample, ((0, 0), (0, SAMPLE_TPAD - ts), (0, 0))).reshape(bs * SAMPLE_TPAD, D_MODEL)
    zeros_gla = jnp.zeros((bp, GLA_HEADS, GLA_DK, GLA_DV), F32)
    zeros_rwkv = jnp.zeros((bp, RWKV_HEADS, RWKV_HD, RWKV_HD), F32)
    zeros_shift = jnp.zeros((bp, 1, RWKV_IN), F32)

    outs = {k: [] for k in ("kv_p", "kv_s", "win_p", "win_s", "gla_p", "gla_s", "rwkv_p", "rwkv_s", "sh_p", "sh_s")}
    for l in range(DEPTH):
        rp = [p[l] for p in r_par]
        zg, zr, zn, zm = (_proj(xp, ws[i][l], bws[i][l], 512) for i in range(4))
        zn3 = zn.reshape(bp, tp, ZN_W)
        o_gla, gla_st = _gla(zg.reshape(bp, tp, ZG_W), zeros_gla, gla_a_up[l], gla_a_bias2[l], gla_norm2[l],
                             tb_rows=512, chunk=GLA_CHUNK, t_valid=None)
        zr3 = zr.reshape(bp, tp, RWKV_IN)
        o_rwkv, rwkv_st = _rwkv(zr3, zeros_rwkv, zeros_shift, seg, rp, tb_rows=512, chunk=RWKV_CHUNK, t_valid=None)
        o_nsa = _nsa_prompt(zn3, tq=256)
        xp = _merge(xp, o_gla.reshape(bp * tp, 512), o_rwkv.reshape(bp * tp, 512), o_nsa.reshape(bp * tp, 512), zm,
                    w_br_b[l], w_out_b[l], ln_g2[l], ln_b2[l], 512)
        outs["kv_p"].append(zn3[:, :, 0:512].reshape(bp, tp, 4, NSA_KV_HEADS, NSA_HD))
        outs["win_p"].append(zn3[:, tp - WINDOW:, 512:768].reshape(bp, WINDOW, 2, NSA_KV_HEADS, NSA_HD))
        outs["gla_p"].append(gla_st)
        outs["rwkv_p"].append(rwkv_st)
        outs["sh_p"].append(zr3[:, tp - 1, :])
        rows_s = bs * SAMPLE_TPAD
        zg, zr, zn, zm = (_proj(xs, ws[i][l], bws[i][l], rows_s) for i in range(4))
        zn3 = zn.reshape(bs, SAMPLE_TPAD, ZN_W)
        o_gla, gla_st = _gla(zg.reshape(bs, SAMPLE_TPAD, ZG_W), state_gla[l], gla_a_up[l], gla_a_bias2[l],
                             gla_norm2[l], tb_rows=SAMPLE_TPAD, chunk=SAMPLE_TPAD, t_valid=ts)
        zr3 = zr.reshape(bs, SAMPLE_TPAD, RWKV_IN)
        o_rwkv, rwkv_st = _rwkv(zr3, state_rwkv[l], state_rwkv_shift[l][:, None, :], seg, rp,
                                tb_rows=SAMPLE_TPAD, chunk=SAMPLE_TPAD, t_valid=ts)
        o_nsa = _nsa_sample(page_table, zn3, win_t, cache_t, layer=l, n_pool=n_pool, n_tok=ts)
        xs = _merge(xs, o_gla.reshape(rows_s, 512), o_rwkv.reshape(rows_s, 512), o_nsa.reshape(rows_s, 512), zm,
                    w_br_b[l], w_out_b[l], ln_g2[l], ln_b2[l], rows_s)
        outs["kv_s"].append(zn3[:, 0:ts, 0:512].reshape(bs, ts, 4, NSA_KV_HEADS, NSA_HD))
        new_win = zn3[:, 0:ts, 512:768].reshape(bs, ts, 2, NSA_KV_HEADS, NSA_HD)
        outs["win_s"].append(jnp.concatenate([state_nsa_win[l][:, ts:], new_win], axis=1))
        outs["gla_s"].append(gla_st)
        outs["rwkv_s"].append(rwkv_st)
        outs["sh_s"].append(zr3[:, ts - 1, :])

    st = lambda k: jnp.stack(outs[k])
    y_prompt = xp.reshape(bp, tp, D_MODEL)
    y_sample = xs.reshape(bs, SAMPLE_TPAD, D_MODEL)[:, 0:ts]
    return (y_prompt, y_sample, st("kv_p"), st("kv_s"), st("win_p"), st("win_s"), st("gla_p"), st("gla_s"),
            st("rwkv_p"), st("rwkv_s"), st("sh_p"), st("sh_s"))
```

```python
import functools

import jax
import jax.numpy as jnp
from jax import lax
from jax.experimental import pallas as pl
from jax.experimental.pallas import tpu as pltpu

F32 = jnp.float32
BF16 = jnp.bfloat16

D_MODEL = 1024
DEPTH = 2
PAST_LEN = 16384
PAGE_SIZE = 128
N_PAGES = PAST_LEN // PAGE_SIZE

GLA_HEADS, GLA_DK, GLA_DV = 4, 64, 128
GLA_K, GLA_V, GLA_LORA = 256, 512, 16
GLA_GATE_NORM = 16.0
GLA_CHUNK = 64
GLA_SUB = 16

RWKV_HEADS, RWKV_HD, RWKV_W = 8, 64, 512
RWKV_IN = 2176
RWKV_LN_EPS = 64e-5
RWKV_CHUNK = 64

NSA_HEADS, NSA_KV_HEADS, NSA_GROUP, NSA_HD = 8, 2, 4, 64
CMP_BLOCK, SEL_BLOCK, N_SELECT, WINDOW = 32, 64, 16, 512
FORCE_SCORE = 1e9
NEG = -1e30

DN_ALPHA = (2 * DEPTH) ** 0.25
LN_EPS = 1e-5
NORM_EPS = 1e-6

ZG_W = 1664
ZN_W = 2048
ZM_W = 3072
SAMPLE_TPAD = 16
VMEM_LIMIT = 56 * 1024 * 1024


def _mm(a, b):
    return jnp.dot(a.astype(BF16), b.astype(BF16), preferred_element_type=F32)


def _mm_nt(a, b):
    return lax.dot_general(a.astype(BF16), b.astype(BF16), (((1,), (1,)), ((), ())), preferred_element_type=F32)


def _mm_tn(a, b):
    return lax.dot_general(a.astype(BF16), b.astype(BF16), (((0,), (0,)), ((), ())), preferred_element_type=F32)


def _mm_exact(a, b):
    return jnp.dot(a, b, preferred_element_type=F32, precision=lax.Precision.HIGHEST)


def _split2(a):
    hi = a.astype(BF16)
    return hi, (a - hi.astype(F32)).astype(BF16)


def _mm_split(a, b01):
    hi, lo = _split2(a)
    return jnp.dot(hi, b01, preferred_element_type=F32) + jnp.dot(lo, b01, preferred_element_type=F32)


def _softplus(x):
    return jnp.maximum(x, 0.0) + jnp.log(1.0 + jnp.exp(-jnp.abs(x)))


def _sigmoid(x):
    return 0.5 * jnp.tanh(0.5 * x) + 0.5


def _silu(x):
    return x * _sigmoid(x)


def _iota(shape, dim):
    return lax.broadcasted_iota(jnp.int32, shape, dim)


def _cparams(sem):
    return pltpu.CompilerParams(dimension_semantics=sem, vmem_limit_bytes=VMEM_LIMIT)


def _proj_kernel(x_ref, w_ref, b_ref, o_ref):
    o_ref[...] = _mm(x_ref[...], w_ref[...]) + b_ref[...]


def _proj(x, w, b, tm):
    m, k = x.shape
    n = w.shape[1]
    return pl.pallas_call(
        _proj_kernel,
        grid=(m // tm,),
        in_specs=[pl.BlockSpec((tm, k), lambda i: (i, 0)),
                  pl.BlockSpec((k, n), lambda i: (0, 0)),
                  pl.BlockSpec((1, n), lambda i: (0, 0))],
        out_specs=pl.BlockSpec((tm, n), lambda i: (i, 0)),
        out_shape=jax.ShapeDtypeStruct((m, n), F32),
        compiler_params=_cparams(("parallel",)),
        name="proj",
    )(x, w, b)


def _gla_kernel(zg_ref, s0_ref, aup_ref, abias_ref, norm_ref, o_ref, sout_ref, st_scr, *, tb_rows, chunk, t_valid):
    tb = pl.program_id(1)
    c_rows = chunk
    sub = min(GLA_SUB, c_rows)
    nsub = c_rows // sub

    @pl.when(tb == 0)
    def _():
        for h in range(GLA_HEADS):
            st_scr[h] = s0_ref[0, h].T

    tri = (_iota((c_rows, c_rows), 1) <= _iota((c_rows, c_rows), 0)).astype(F32)
    ones_red = jnp.ones((GLA_DK, 128), BF16)
    lane_s = _iota((sub, 128), 1)
    row_s = _iota((sub, 128), 0)
    col_c = _iota((sub, c_rows), 1)

    def chunk_body(c, carry):
        r0 = pl.multiple_of(c * c_rows, c_rows)
        z = zg_ref[0, pl.ds(r0, c_rows), :]
        q = z[:, 0:256] * (GLA_DK ** -0.5)
        k = z[:, 256:512]
        v = z[:, 512:1024]
        g = z[:, 1024:1536]
        ga = z[:, 1536:1552]
        la = -_softplus(-(_mm(ga, aup_ref[...]) + abias_ref[...])) * (1.0 / GLA_GATE_NORM)
        if t_valid is not None:
            ok = (tb * tb_rows + r0 + _iota((c_rows, 1), 0)) < t_valid
            la = jnp.where(ok, la, 0.0)
            k = jnp.where(ok, k, 0.0)
            v = jnp.where(ok, v, 0.0)
        cum = _mm_exact(tri, la)
        heads = range(GLA_HEADS)
        pairs = [(h, blk) for h in heads for blk in range(nsub)]
        qh = [q[:, 64 * h:64 * h + 64] for h in heads]
        kh = [k[:, 64 * h:64 * h + 64] for h in heads]
        ch = [cum[:, 64 * h:64 * h + 64] for h in heads]
        vh = [v[:, 128 * h:128 * h + 128] for h in heads]
        st = [st_scr[h] for h in heads]
        o_in = [_mm_nt(qh[h] * jnp.exp(ch[h]), st[h]) for h in heads]
        red, off = {}, {}
        for h, blk in pairs:
            sl = slice(blk * sub, (blk + 1) * sub)
            q_i, k_i, c_i = qh[h][sl], kh[h][sl], ch[h][sl]
            es = [q_i * k_i[j:j + 1] * jnp.exp(jnp.minimum(c_i - c_i[j:j + 1], 0.0)) for j in range(sub)]
            red[h, blk] = _mm(jnp.concatenate(es, axis=0), ones_red)
            if blk > 0:
                b_i = ch[h][blk * sub - 1:blk * sub]
                q_t = q_i * jnp.exp(c_i - b_i)
                k_t = kh[h] * jnp.exp(jnp.minimum(b_i - ch[h], 0.0))
                off[h, blk] = _mm_nt(q_t, k_t)
        att = []
        for h in heads:
            att_rows = []
            for blk in range(nsub):
                a_i = jnp.zeros((sub, 128), F32)
                for j in range(sub):
                    a_i = a_i + jnp.where((lane_s == blk * sub + j) & (row_s >= j),
                                          red[h, blk][j * sub:(j + 1) * sub], 0.0)
                a_i = a_i[:, 0:c_rows]
                if blk > 0:
                    a_i = a_i + jnp.where(col_c < blk * sub, off[h, blk], 0.0)
                att_rows.append(a_i)
            att.append(att_rows[0] if nsub == 1 else jnp.concatenate(att_rows, axis=0))
        o = [o_in[h] + _mm(att[h], vh[h]) for h in heads]
        last = [ch[h][c_rows - 1:c_rows] for h in heads]
        st_new = [st[h] * jnp.exp(last[h]) + _mm_tn(vh[h], kh[h] * jnp.exp(last[h] - ch[h])) for h in heads]
        for h in heads:
            st_scr[h] = st_new[h]
            oh = o[h] * lax.rsqrt(jnp.mean(o[h] * o[h], axis=-1, keepdims=True) + NORM_EPS)
            oh = oh * norm_ref[:, 128 * h:128 * h + 128] * _silu(g[:, 128 * h:128 * h + 128])
            o_ref[0, pl.ds(r0, c_rows), 128 * h:128 * h + 128] = oh
        return carry

    lax.fori_loop(0, tb_rows // c_rows, chunk_body, 0, unroll=min(2, tb_rows // c_rows))

    @pl.when(tb == pl.num_programs(1) - 1)
    def _():
        for h in range(GLA_HEADS):
            sout_ref[0, h] = st_scr[h].T


def _gla(zg, s0, a_up, a_bias, norm_g, *, tb_rows, chunk, t_valid):
    b, t, _ = zg.shape
    kern = functools.partial(_gla_kernel, tb_rows=tb_rows, chunk=chunk, t_valid=t_valid)
    return pl.pallas_call(
        kern,
        grid=(b, t // tb_rows),
        in_specs=[pl.BlockSpec((1, tb_rows, ZG_W), lambda i, j: (i, j, 0)),
                  pl.BlockSpec((1, GLA_HEADS, GLA_DK, GLA_DV), lambda i, j: (i, 0, 0, 0)),
                  pl.BlockSpec((GLA_LORA, GLA_K), lambda i, j: (0, 0)),
                  pl.BlockSpec((1, GLA_K), lambda i, j: (0, 0)),
                  pl.BlockSpec((1, GLA_V), lambda i, j: (0, 0))],
        out_specs=[pl.BlockSpec((1, tb_rows, GLA_V), lambda i, j: (i, j, 0)),
                   pl.BlockSpec((1, GLA_HEADS, GLA_DK, GLA_DV), lambda i, j: (i, 0, 0, 0))],
        out_shape=[jax.ShapeDtypeStruct((b, t, GLA_V), F32),
                   jax.ShapeDtypeStruct((b, GLA_HEADS, GLA_DK, GLA_DV), F32)],
        scratch_shapes=[pltpu.VMEM((GLA_HEADS, GLA_DV, GLA_DK), F32)],
        compiler_params=_cparams(("parallel", "arbitrary")),
        name="gla",
    )(zg, s0, a_up, a_bias, norm_g)


def _rwkv_kernel(zr_ref, s0_ref, sh0_ref, seg_ref, mu_ref, w0_ref, wup_ref, a0_ref, aup_ref, kk_ref, ka_ref, rk_ref,
                 lnw_ref, lnb_ref, y_ref, sout_ref,
                 s_scr, prev_scr, lw_s, kk_s, kka_s, k2_s, r_s, v_s, y_s, *, tb_rows, chunk, t_valid):
    tb = pl.program_id(1)
    c_rows = chunk
    nh = RWKV_HEADS

    @pl.when(tb == 0)
    def _():
        s_scr[...] = s0_ref[0]
        prev_scr[...] = sh0_ref[0]

    z = zr_ref[0]
    rows = _iota((tb_rows, 1), 0)
    zp = pltpu.roll(z, 1, axis=0)
    zp = jnp.concatenate([jnp.where(rows[0:8] == 0, prev_scr[...], zp[0:8]), zp[8:]], axis=0)
    prev_scr[...] = z[tb_rows - 1:tb_rows]
    zs = z + (zp - z) * mu_ref[...]
    r = zs[:, 0:512]
    k = zs[:, 512:1024]
    v = zs[:, 1024:1536]
    wl = zs[:, 1536:1600]
    al = zs[:, 1600:1664]
    w = -_softplus(-(w0_ref[...] + _mm(jnp.tanh(wl), wup_ref[...]))) - 0.5
    lw = -jnp.exp(w)
    a = _sigmoid(a0_ref[...] + _mm(al, aup_ref[...]))
    kk = k * kk_ref[...]
    kk = kk * lax.rsqrt(_mm_split(kk * kk, seg_ref[...]) + NORM_EPS)
    k2 = k * (1.0 + (a - 1.0) * ka_ref[...])
    kka = kk * a
    if t_valid is not None:
        ok = (tb * tb_rows + rows) < t_valid
        lw = jnp.where(ok, lw, 0.0)
        kka = jnp.where(ok, kka, 0.0)
        k2 = jnp.where(ok, k2, 0.0)
    lw_s[...] = lw
    kk_s[...] = kk
    kka_s[...] = kka
    k2_s[...] = k2
    r_s[...] = r
    v_s[...] = v

    ri = _iota((c_rows, c_rows), 0)
    ci = _iota((c_rows, c_rows), 1)
    tri = (ci <= ri).astype(F32)
    strict = ci < ri
    incl = ci <= ri
    n_dbl = max(1, (c_rows - 1).bit_length())

    def chunk_body(c, carry):
        r0 = pl.multiple_of(c * c_rows, c_rows)
        ds = pl.ds(r0, c_rows)
        lwc = lw_s[ds, :]
        cl = _mm_exact(tri, lwc)
        e_inv = jnp.exp(-cl)
        e_fwd = jnp.exp(cl)
        e_prev = jnp.exp(cl - lwc)
        e_end = jnp.exp(cl[c_rows - 1:c_rows] - cl)
        g_end = jnp.exp(cl[c_rows - 1:c_rows])
        kkc, kkac, k2c, rc, vc = kk_s[ds, :], kka_s[ds, :], k2_s[ds, :], r_s[ds, :], v_s[ds, :]
        heads = range(nh)
        hsl = [slice(64 * h, 64 * h + 64) for h in heads]
        a_t = [-kkac[:, hs] * e_inv[:, hs] for hs in hsl]
        b_t = [kkc[:, hs] * e_prev[:, hs] for hs in hsl]
        k_t = [k2c[:, hs] * e_inv[:, hs] for hs in hsl]
        r_t = [rc[:, hs] * e_fwd[:, hs] for hs in hsl]
        vh = [vc[:, hs] for hs in hsl]
        cc = c_rows
        ak = [jnp.concatenate([a_t[h], k_t[h]], axis=0).astype(BF16) for h in heads]
        bra = [_mm_nt(jnp.concatenate([b_t[h], r_t[h]], axis=0), ak[h]) for h in heads]
        s0 = [s_scr[h] for h in heads]
        l_k = [jnp.where(strict, bra[h][0:cc, cc:2 * cc], 0.0) for h in heads]
        m_k = [jnp.where(incl, bra[h][cc:2 * cc, cc:2 * cc], 0.0) for h in heads]
        lmv = [_mm(jnp.concatenate([l_k[h], m_k[h]], axis=0), vh[h]) for h in heads]
        x = [jnp.concatenate([b_t[h], lmv[h][0:cc]], axis=1) for h in heads]
        lp = [jnp.where(strict, bra[h][0:cc, 0:cc], 0.0).astype(BF16) for h in heads]
        for step in range(n_dbl):
            if step + 1 < n_dbl:
                new = [_mm(lp[h], jnp.concatenate([x[h].astype(BF16), lp[h]], axis=1)) for h in heads]
                x = [x[h] + new[h][:, 0:128] for h in heads]
                lp = [new[h][:, 128:128 + cc].astype(BF16) for h in heads]
            else:
                x = [x[h] + _mm(lp[h], x[h]) for h in heads]
        prs = [_mm_nt(jnp.concatenate([x[h][:, 0:64], r_t[h]], axis=0), s0[h]) for h in heads]
        u = [prs[h][0:cc] + x[h][:, 64:128] for h in heads]
        m_a = [jnp.where(incl, bra[h][cc:2 * cc, 0:cc], 0.0) for h in heads]
        y = [prs[h][cc:2 * cc] + _mm(m_a[h], u[h]) + lmv[h][cc:2 * cc] for h in heads]
        akg = [jnp.concatenate([-kkac[:, hs] * e_end[:, hs], k2c[:, hs] * e_end[:, hs]], axis=0) for hs in hsl]
        s_new = [s0[h] * g_end[:, hsl[h]] + _mm_tn(jnp.concatenate([u[h], vh[h]], axis=0), akg[h]) for h in heads]
        for h in heads:
            s_scr[h] = s_new[h]
            y_s[ds, hsl[h]] = y[h]
        return carry

    lax.fori_loop(0, tb_rows // c_rows, chunk_body, 0, unroll=min(2, tb_rows // c_rows))

    y = y_s[...]
    seg = seg_ref[...]
    mean = _mm_split(y, seg) * (1.0 / RWKV_HD)
    d = y - mean
    var = _mm_split(d * d, seg) * (1.0 / RWKV_HD)
    yn = d * lax.rsqrt(var + RWKV_LN_EPS) * lnw_ref[...] + lnb_ref[...]
    bonus = _mm_split(r * k2 * rk_ref[...], seg) * v
    y_ref[0] = (yn + bonus) * _silu(zs[:, 1664:2176])

    @pl.when(tb == pl.num_programs(1) - 1)
    def _():
        sout_ref[0] = s_scr[...]


def _rwkv(zr, s0, sh0, seg, params, *, tb_rows, chunk, t_valid):
    b, t, _ = zr.shape
    kern = functools.partial(_rwkv_kernel, tb_rows=tb_rows, chunk=chunk, t_valid=t_valid)
    full = lambda shp: pl.BlockSpec(shp, lambda i, j: (0,) * len(shp))
    mu, w0, w_up, a0, a_up, k_k, k_a, r_k, ln_w, ln_b = params
    return pl.pallas_call(
        kern,
        grid=(b, t // tb_rows),
        in_specs=[pl.BlockSpec((1, tb_rows, RWKV_IN), lambda i, j: (i, j, 0)),
                  pl.BlockSpec((1, RWKV_HEADS, RWKV_HD, RWKV_HD), lambda i, j: (i, 0, 0, 0)),
                  pl.BlockSpec((1, 1, RWKV_IN), lambda i, j: (i, 0, 0)),
                  full((RWKV_W, RWKV_W)), full((1, RWKV_IN)), full((1, RWKV_W)), full((64, RWKV_W)),
                  full((1, RWKV_W)), full((64, RWKV_W)), full((1, RWKV_W)), full((1, RWKV_W)), full((1, RWKV_W)),
                  full((1, RWKV_W)), full((1, RWKV_W))],
        out_specs=[pl.BlockSpec((1, tb_rows, RWKV_W), lambda i, j: (i, j, 0)),
                   pl.BlockSpec((1, RWKV_HEADS, RWKV_HD, RWKV_HD), lambda i, j: (i, 0, 0, 0))],
        out_shape=[jax.ShapeDtypeStruct((b, t, RWKV_W), F32),
                   jax.ShapeDtypeStruct((b, RWKV_HEADS, RWKV_HD, RWKV_HD), F32)],
        scratch_shapes=[pltpu.VMEM((RWKV_HEADS, RWKV_HD, RWKV_HD), F32), pltpu.VMEM((1, RWKV_IN), F32)]
        + [pltpu.VMEM((tb_rows, RWKV_W), F32) for _ in range(7)],
        compiler_params=_cparams(("parallel", "arbitrary")),
        name="rwkv",
    )(zr, s0, sh0, seg, mu, w0, w_up, a0, a_up, k_k, k_a, r_k, ln_w, ln_b)


def _slope(h):
    return 2.0 ** (-8.0 * (h + 1) / NSA_HEADS)


def _flash_init(m_ref, l_ref, acc_ref):
    m_ref[...] = jnp.full(m_ref.shape, NEG, F32)
    l_ref[...] = jnp.zeros(l_ref.shape, F32)
    acc_ref[...] = jnp.zeros(acc_ref.shape, F32)


def _flash_step_t(s_t, v, m_ref, l_ref, acc_ref):
    m_old = m_ref[...]
    m_new = jnp.maximum(m_old, jnp.max(s_t, axis=0, keepdims=True))
    p = jnp.exp(s_t - m_new)
    alpha = jnp.exp(m_old - m_new)
    l_ref[...] = alpha * l_ref[...] + jnp.sum(p, axis=0, keepdims=True)
    acc_ref[...] = alpha * acc_ref[...] + _mm_tn(v, p)
    m_ref[...] = m_new


def _flash_steps_t(s_list, v_list, m_ref, l_ref, acc_ref):
    ks = range(len(s_list))
    m_old = [m_ref[k] for k in ks]
    m_new = [jnp.maximum(m_old[k], jnp.max(s_list[k], axis=0, keepdims=True)) for k in ks]
    p = [jnp.exp(s_list[k] - m_new[k]) for k in ks]
    alpha = [jnp.exp(m_old[k] - m_new[k]) for k in ks]
    pv = [_mm_tn(v_list[k], p[k]) for k in ks]
    for k in ks:
        l_ref[k] = alpha[k] * l_ref[k] + jnp.sum(p[k], axis=0, keepdims=True)
        acc_ref[k] = alpha[k] * acc_ref[k] + pv[k]
        m_ref[k] = m_new[k]


def _flash_result(l_ref, acc_ref, k):
    l = l_ref[k]
    return acc_ref[k] / jnp.where(l > 0, l, 1.0)


SEL_LANE0 = 66
BLOCK_PENALTY = -(2.0 ** 100)


def _nsa_prompt_kernel(q_ref, kv_ref, bg_ref, g_ref, o_ref, kc_s, vc_s, ks_s, vs_s, kw_s, vw_s, qa_s, qs_s, oc_s,
                       m_s, l_s, acc_s, m2_s, l2_s, acc2_s, *, t_len, tq):
    i = pl.program_id(1)
    nb = t_len // SEL_BLOCK
    tk = tq
    ncol = NSA_GROUP * tq
    win_tiles = WINDOW // tk

    def aug_lanes(lane, pos):
        return jnp.where(lane == 64, lax.shift_right_logical(pos, 7).astype(F32),
                         jnp.where(lane == 65, (pos & 127).astype(F32), 0.0))

    def head_lanes(x, kvh):
        return x if kvh == 0 else pltpu.roll(x, 64, axis=1)

    @pl.when(i == 0)
    def _():
        lane = _iota((t_len, 128), 1)
        pos = _iota((t_len, 128), 0)
        aug = aug_lanes(lane, pos)
        aug_sel = jnp.where(lane - SEL_LANE0 == lax.shift_right_logical(pos, 6), 1.0, aug)
        for kcol, vcol, k_dst, v_dst, k_aug in ((256, 384, ks_s, vs_s, aug_sel), (512, 640, kw_s, vw_s, aug)):
            kf = kv_ref[0, :, kcol:kcol + 128]
            vf = kv_ref[0, :, vcol:vcol + 128]
            for kvh in range(NSA_KV_HEADS):
                k_dst[kvh] = jnp.where(lane < 64, head_lanes(kf, kvh), k_aug).astype(BF16)
                v_dst[kvh] = jnp.where(lane < 64, head_lanes(vf, kvh), 0.0).astype(BF16)
        kcm = kv_ref[0, :, 0:128].reshape(nb, SEL_BLOCK, 128)
        vcm = kv_ref[0, :, 128:256].reshape(nb, SEL_BLOCK, 128)
        inv = 1.0 / CMP_BLOCK
        kc = jnp.concatenate([jnp.sum(kcm[:, 0:CMP_BLOCK], axis=1) * inv,
                              jnp.sum(kcm[:, CMP_BLOCK:SEL_BLOCK], axis=1) * inv], axis=0)
        vc = jnp.concatenate([jnp.sum(vcm[:, 0:CMP_BLOCK], axis=1) * inv,
                              jnp.sum(vcm[:, CMP_BLOCK:SEL_BLOCK], axis=1) * inv], axis=0)
        lane_c = _iota((2 * nb, 128), 1)
        r_c = _iota((2 * nb, 128), 0)
        cend = jnp.where(r_c < nb, SEL_BLOCK * r_c + (CMP_BLOCK - 1), SEL_BLOCK * (r_c - nb) + (SEL_BLOCK - 1))
        aug_c = aug_lanes(lane_c, cend)
        for kvh in range(NSA_KV_HEADS):
            kc_s[kvh] = jnp.where(lane_c < 64, head_lanes(kc, kvh), aug_c).astype(BF16)
            vc_s[kvh] = jnp.where(lane_c < 64, head_lanes(vc, kvh), 0.0).astype(BF16)

    lane_q = _iota((tq, 128), 1)
    for h in range(NSA_HEADS):
        qb = q_ref[0, :, 128 * (h // 2):128 * (h // 2) + 128]
        if h % 2:
            qb = pltpu.roll(qb, 64, axis=1)
        sl = _slope(h)
        qa = jnp.where(lane_q < 64, qb * (NSA_HD ** -0.5),
                       jnp.where(lane_q == 64, 128.0 * sl, jnp.where(lane_q == 65, sl, 0.0)))
        qa_s[h // NSA_GROUP, (h % NSA_GROUP) * tq:(h % NSA_GROUP + 1) * tq, :] = qa.astype(BF16)

    t_row = i * tq + (_iota((1, ncol), 1) & (tq - 1))
    t_row1 = i * tq + _iota((1, tq), 1)
    r_c1 = _iota((2 * nb, 1), 0)
    cend_col = jnp.where(r_c1 < nb, SEL_BLOCK * r_c1 + (CMP_BLOCK - 1), SEL_BLOCK * (r_c1 - nb) + (SEL_BLOCK - 1))
    mask_c = cend_col <= t_row
    jrow = _iota((nb, 1), 0)
    tblk = lax.shift_right_logical(t_row1, 6)
    forced = (jrow == tblk) | (jrow == 0)
    valid = jrow <= tblk
    bg_t = bg_ref[0].T

    for kvh in range(NSA_KV_HEADS):
        qa = qa_s[kvh]
        s_c = jnp.where(mask_c, _mm_nt(kc_s[kvh], qa), NEG)
        m = jnp.max(s_c, axis=0, keepdims=True)
        p = jnp.where(mask_c, jnp.exp(s_c - m), 0.0)
        den = jnp.sum(p, axis=0, keepdims=True)
        p = p / jnp.where(den > 0, den, 1.0)
        oc_s[kvh] = _mm_tn(vc_s[kvh], p)
        imp_e = p[0:nb, 0:tq]
        imp_o = p[nb:2 * nb, 0:tq]
        for g in range(1, NSA_GROUP):
            imp_e = imp_e + p[0:nb, g * tq:(g + 1) * tq]
            imp_o = imp_o + p[nb:2 * nb, g * tq:(g + 1) * tq]
        score = jnp.where(forced, FORCE_SCORE, jnp.where(valid, imp_e + imp_o, -FORCE_SCORE))
        rank = jnp.zeros((nb, tq), F32)
        for jp in range(nb):
            rj = score[jp:jp + 1, :]
            rank = rank + jnp.where((rj > score) | ((rj == score) & (jp < jrow)), 1.0, 0.0)
        sel = jnp.where((rank < N_SELECT) & valid, 1.0, 0.0)
        sel_q = jnp.concatenate([sel, jnp.zeros((128 - nb, tq), F32)], axis=0).T
        sel_q = pltpu.roll(sel_q, SEL_LANE0, axis=1)
        pen = jnp.where((lane_q >= SEL_LANE0) & (lane_q < SEL_LANE0 + nb) & (sel_q < 0.5), BLOCK_PENALTY, 0.0)
        for g in range(NSA_GROUP):
            cs = slice(g * tq, (g + 1) * tq)
            qs_s[kvh, cs, :] = (qa_s[kvh, cs, :].astype(F32) + pen).astype(BF16)

    _flash_init(m_s, l_s, acc_s)
    _flash_init(m2_s, l2_s, acc2_s)
    kvs = range(NSA_KV_HEADS)

    def sel_step(kt, diagonal):
        k0 = pl.multiple_of(kt * tk, tk)
        s_t = [_mm_nt(ks_s[kvh, pl.ds(k0, tk), :], qs_s[kvh]) for kvh in kvs]
        if diagonal:
            causal = kt * tk + _iota((tk, 1), 0) <= t_row
            s_t = [jnp.where(causal, s, NEG) for s in s_t]
        _flash_steps_t(s_t, [vs_s[kvh, pl.ds(k0, tk), :] for kvh in kvs], m_s, l_s, acc_s)

    def sel_body(kt, carry):
        sel_step(kt, False)
        return carry

    lax.fori_loop(0, i, sel_body, 0)
    sel_step(i, True)

    def win_step(kt, back):
        k0 = pl.multiple_of(kt * tk, tk)
        s_t = [_mm_nt(kw_s[kvh, pl.ds(k0, tk), :], qa_s[kvh]) for kvh in kvs]
        dist = t_row - (kt * tk + _iota((tk, 1), 0))
        if back == 0:
            s_t = [jnp.where(dist >= 0, s, NEG) for s in s_t]
        elif back == win_tiles:
            s_t = [jnp.where(dist <= WINDOW, s, NEG) for s in s_t]
        _flash_steps_t(s_t, [vw_s[kvh, pl.ds(k0, tk), :] for kvh in kvs], m2_s, l2_s, acc2_s)

    for back in range(win_tiles, 0, -1):
        @pl.when(i >= back)
        def _(back=back):
            win_step(i - back, back)
    win_step(i, 0)

    for kvh in kvs:
        o_sel = _flash_result(l_s, acc_s, kvh)
        o_win = _flash_result(l2_s, acc2_s, kvh)
        o_cmp = oc_s[kvh]
        for g in range(NSA_GROUP):
            h = NSA_GROUP * kvh + g
            cs = slice(g * tq, (g + 1) * tq)
            gates = _sigmoid(bg_t[3 * h:3 * h + 3, :])
            comb = gates[0:1] * o_cmp[:, cs] + gates[1:2] * o_sel[:, cs] + gates[2:3] * o_win[:, cs]
            o_ref[0, :, 64 * h:64 * h + 64] = comb.T[:, 0:64] * _silu(g_ref[0, :, 64 * h:64 * h + 64])


def _nsa_prompt(zn, *, tq):
    b, t, _ = zn.shape
    nb = t // SEL_BLOCK
    ncol = NSA_GROUP * tq
    kern = functools.partial(_nsa_prompt_kernel, t_len=t, tq=tq)
    kv_f32 = lambda n: pltpu.VMEM((NSA_KV_HEADS, n, ncol), F32)
    stat = lambda: [kv_f32(1), kv_f32(1), kv_f32(128)]
    kvbuf = lambda n: pltpu.VMEM((NSA_KV_HEADS, n, 128), BF16)
    return pl.pallas_call(
        kern,
        grid=(b, t // tq),
        in_specs=[pl.BlockSpec((1, tq, 512), lambda i, j: (i, j, 2)),
                  pl.BlockSpec((1, t, 768), lambda i, j: (i, 0, 0)),
                  pl.BlockSpec((1, tq, 256), lambda i, j: (i, j, 3)),
                  pl.BlockSpec((1, tq, 512), lambda i, j: (i, j, 3))],
        out_specs=pl.BlockSpec((1, tq, 512), lambda i, j: (i, j, 0)),
        out_shape=jax.ShapeDtypeStruct((b, t, 512), F32),
        scratch_shapes=[kvbuf(2 * nb), kvbuf(2 * nb), kvbuf(t), kvbuf(t), kvbuf(t), kvbuf(t), kvbuf(ncol),
                        kvbuf(ncol), kv_f32(128)] + stat() + stat(),
        compiler_params=_cparams(("parallel", "arbitrary")),
        name="nsa_prompt",
    )(zn, zn, zn, zn)


PAGES_PER_STEP = 64
N_PAGE_GROUPS = N_PAGES // PAGES_PER_STEP
KEYS_PER_STEP = PAGES_PER_STEP * PAGE_SIZE
SEL_PER_STEP = KEYS_PER_STEP // SEL_BLOCK
POOL_PAGES = 8
POOL_SEL = POOL_PAGES * PAGE_SIZE // SEL_BLOCK
N_PAST_SEL = PAST_LEN // SEL_BLOCK
HALF_PAGE_ROWS = 2 * NSA_KV_HEADS * NSA_HD


def _flash_step_r(s, mask, pv, m_ref, l_ref, acc_ref):
    s = jnp.where(mask, s, NEG)
    m_old = m_ref[...]
    m_new = jnp.maximum(m_old, jnp.max(s, axis=-1, keepdims=True))
    p = jnp.exp(s - m_new)
    alpha = jnp.exp(m_old - m_new)
    l_ref[...] = alpha * l_ref[...] + jnp.sum(p, axis=-1, keepdims=True)
    acc_ref[...] = alpha * acc_ref[...] + pv(p)
    m_ref[...] = m_new


def _nsa_sample_kernel(pt_ref, zn_ref, win_ref, pool_ref, e0_ref, *rest, n_tok):
    pages = rest[:PAGES_PER_STEP]
    o_ref = rest[PAGES_PER_STEP]
    kce_s, kco_s, vce_s, vco_s, qt_s, sel_s, oc_s, m_s, l_s, acc_s = rest[PAGES_PER_STEP + 1:]
    ph = pl.program_id(1)
    gi = pl.program_id(2)
    rq = n_tok
    ncol = NSA_HEADS * rq
    npast = N_PAST_SEL
    col = _iota((ncol, 1), 0)
    t_col = PAST_LEN + (col & (rq - 1))
    hcol = lax.shift_right_logical(col, rq.bit_length() - 1)
    slope_col = jnp.zeros((ncol, 1), F32)
    for h in range(NSA_HEADS):
        slope_col = jnp.where(hcol == h, _slope(h), slope_col)

    def bias(pos_row):
        return slope_col * (t_col - pos_row).astype(F32)

    def keys_t(lo):
        return jnp.concatenate([pg[0, lo:lo + 128, :] for pg in pages], axis=1)

    @pl.when((ph == 0) & (gi == 0))
    def _queries():
        zero = jnp.zeros((rq, NSA_HD), F32)
        rows = []
        for h in range(NSA_HEADS):
            qh = zn_ref[0, 0:rq, 1024 + 64 * h:1088 + 64 * h] * (NSA_HD ** -0.5)
            rows.append(jnp.concatenate([qh, zero] if h < NSA_GROUP else [zero, qh], axis=1))
        qt_s[...] = jnp.concatenate(rows, axis=0).astype(BF16)

    @pl.when(ph == 0)
    def _pool():
        dn = (((1,), (1,)), ((), ()))
        for sub in range(PAGES_PER_STEP // POOL_PAGES):
            grp = pages[sub * POOL_PAGES:(sub + 1) * POOL_PAGES]
            dst = pl.ds(pl.multiple_of(gi * SEL_PER_STEP + sub * POOL_SEL, POOL_SEL), POOL_SEL)
            for lo, even_s, odd_s in ((0, kce_s, kco_s), (128, vce_s, vco_s)):
                hi_part, lo_part = _split2(jnp.concatenate([pg[0, lo:lo + 128, :] for pg in grp], axis=1))
                pooled = (lax.dot_general(pool_ref[...], hi_part, dn, preferred_element_type=F32)
                          + lax.dot_general(pool_ref[...], lo_part, dn, preferred_element_type=F32)) * (1.0 / CMP_BLOCK)
                even_s[dst, :] = pooled[0:POOL_SEL]
                odd_s[dst, :] = pooled[POOL_SEL:2 * POOL_SEL]

    @pl.when((ph == 0) & (gi == N_PAGE_GROUPS - 1))
    def _compressed():
        jrow = _iota((1, npast), 1)
        qt = qt_s[...]
        se = _mm_nt(qt, kce_s[...]) - bias(SEL_BLOCK * jrow + (CMP_BLOCK - 1))
        so = _mm_nt(qt, kco_s[...]) - bias(SEL_BLOCK * jrow + (SEL_BLOCK - 1))
        m = jnp.maximum(jnp.max(se, axis=-1, keepdims=True), jnp.max(so, axis=-1, keepdims=True))
        pe = jnp.exp(se - m)
        po = jnp.exp(so - m)
        den = jnp.sum(pe, axis=-1, keepdims=True) + jnp.sum(po, axis=-1, keepdims=True)
        pe = pe / den
        po = po / den
        oc_s[...] = _mm(pe, vce_s[...]) + _mm(po, vco_s[...])
        jp = _iota((npast, npast), 0)
        jj = _iota((npast, npast), 1)
        groups = []
        for kvh in range(NSA_KV_HEADS):
            base = kvh * NSA_GROUP * rq
            imp_e = pe[base:base + rq]
            imp_o = po[base:base + rq]
            for g in range(1, NSA_GROUP):
                imp_e = imp_e + pe[base + g * rq:base + (g + 1) * rq]
                imp_o = imp_o + po[base + g * rq:base + (g + 1) * rq]
            imp = imp_e + imp_o
            imp_t = jnp.concatenate([imp, jnp.zeros((128 - rq, npast), F32)], axis=0).T
            sel_rows = []
            for t in range(rq):
                colv = imp_t[:, t:t + 1]
                rowv = imp[t:t + 1, :]
                beats = ((colv > rowv) | ((colv == rowv) & (jp < jj))) & (jp >= 1)
                rank = jnp.sum(jnp.where(beats, 1.0, 0.0), axis=0, keepdims=True)
                sel_rows.append(jnp.where((jrow == 0) | (rank < N_SELECT - 2), 1.0, 0.0))
            groups += [jnp.concatenate(sel_rows, axis=0)] * NSA_GROUP
        groups.append(jnp.zeros((128 - ncol, npast), F32))
        sel_s[...] = jnp.concatenate(groups, axis=0).T[:, 0:ncol].astype(BF16)

    @pl.when(ph == 1)
    def _selected():
        @pl.when(gi == 0)
        def _():
            _flash_init(m_s, l_s, acc_s)

        k_t = keys_t(0)
        v_t = keys_t(128)
        pos_row = gi * KEYS_PER_STEP + _iota((1, KEYS_PER_STEP), 1)
        s = _mm(qt_s[...], k_t) - bias(pos_row)
        sel_rows = sel_s[pl.ds(pl.multiple_of(gi * SEL_PER_STEP, SEL_PER_STEP), SEL_PER_STEP), :]
        mask = _mm_tn(sel_rows, e0_ref[...]) > 0.5
        _flash_step_r(s, mask, lambda p: _mm_nt(p, v_t), m_s, l_s, acc_s)

    @pl.when((ph == 1) & (gi == N_PAGE_GROUPS - 1))
    def _finish():
        qt = qt_s[...]
        new = zn_ref[0, 0:rq, 0:768]
        npos_row = PAST_LEN + _iota((1, rq), 1)
        mask_n = npos_row <= t_col
        s = _mm_nt(qt, new[:, 256:384]) - bias(npos_row)
        _flash_step_r(s, mask_n, lambda p: _mm(p, new[:, 384:512]), m_s, l_s, acc_s)
        l = l_s[...]
        o_sel = acc_s[...] / jnp.where(l > 0, l, 1.0)
        wpos_row = PAST_LEN - WINDOW + _iota((1, WINDOW), 1)
        mask_w = (t_col - wpos_row) <= WINDOW
        s1 = jnp.where(mask_w, _mm(qt, win_ref[0, 0:128, :]) - bias(wpos_row), NEG)
        s2 = jnp.where(mask_n, _mm_nt(qt, new[:, 512:640]) - bias(npos_row), NEG)
        m = jnp.maximum(jnp.max(s1, axis=-1, keepdims=True), jnp.max(s2, axis=-1, keepdims=True))
        p1 = jnp.where(mask_w, jnp.exp(s1 - m), 0.0)
        p2 = jnp.where(mask_n, jnp.exp(s2 - m), 0.0)
        den = jnp.sum(p1, axis=-1, keepdims=True) + jnp.sum(p2, axis=-1, keepdims=True)
        o_win = (_mm_nt(p1, win_ref[0, 128:256, :]) + _mm(p2, new[:, 640:768])) / jnp.where(den > 0, den, 1.0)
        o_cmp = oc_s[...]
        for h in range(NSA_HEADS):
            rs = slice(h * rq, (h + 1) * rq)
            ls = slice(64 * (h // NSA_GROUP), 64 * (h // NSA_GROUP) + 64)
            gates = _sigmoid(zn_ref[0, 0:rq, 768 + 3 * h:771 + 3 * h])
            o = gates[:, 0:1] * o_cmp[rs, ls] + gates[:, 1:2] * o_sel[rs, ls] + gates[:, 2:3] * o_win[rs, ls]
            o_ref[0, 0:rq, 64 * h:64 * h + 64] = o * _silu(zn_ref[0, 0:rq, 1536 + 64 * h:1600 + 64 * h])
        o_ref[0, rq:, :] = jnp.zeros((SAMPLE_TPAD - rq, 512), F32)


def _nsa_sample(page_table, zn, win_t, cache_t, *, layer, n_pool, n_tok):
    b = zn.shape[0]
    rq = SAMPLE_TPAD
    assert n_tok & (n_tok - 1) == 0 and n_tok <= rq
    ncol = NSA_HEADS * n_tok
    tok = jnp.arange(KEYS_PER_STEP)
    blk = jnp.arange(SEL_PER_STEP)
    in_blk = tok[None, :] // SEL_BLOCK == blk[:, None]
    first_half = (tok[None, :] % SEL_BLOCK) < CMP_BLOCK
    e0 = in_blk.astype(BF16)
    pk = POOL_PAGES * PAGE_SIZE
    pool = jnp.concatenate([(in_blk & first_half)[0:POOL_SEL, 0:pk], (in_blk & ~first_half)[0:POOL_SEL, 0:pk]],
                           axis=0).astype(BF16)

    def page_map(kidx):
        return lambda i, ph, gi, pt: (layer * n_pool + pt[i, gi * PAGES_PER_STEP + kidx], ph, 0)

    const = lambda a: pl.BlockSpec(a.shape, lambda i, ph, gi, pt: (0, 0))
    sq = lambda dt: pltpu.VMEM((ncol, 128), dt)
    grid_spec = pltpu.PrefetchScalarGridSpec(
        num_scalar_prefetch=1,
        grid=(b, 2, N_PAGE_GROUPS),
        in_specs=[pl.BlockSpec((1, rq, ZN_W), lambda i, ph, gi, pt: (i, 0, 0)),
                  pl.BlockSpec((1, HALF_PAGE_ROWS, WINDOW), lambda i, ph, gi, pt: (layer * b + i, 0, 0)),
                  const(pool), const(e0)]
        + [pl.BlockSpec((1, HALF_PAGE_ROWS, PAGE_SIZE), page_map(kidx)) for kidx in range(PAGES_PER_STEP)],
        out_specs=pl.BlockSpec((1, rq, 512), lambda i, ph, gi, pt: (i, 0, 0)),
        scratch_shapes=[pltpu.VMEM((N_PAST_SEL, 128), F32) for _ in range(4)]
        + [sq(BF16), pltpu.VMEM((N_PAST_SEL, ncol), BF16), sq(F32),
           pltpu.VMEM((ncol, 1), F32), pltpu.VMEM((ncol, 1), F32), sq(F32)],
    )
    return pl.pallas_call(
        functools.partial(_nsa_sample_kernel, n_tok=n_tok),
        grid_spec=grid_spec,
        out_shape=jax.ShapeDtypeStruct((b, rq, 512), F32),
        compiler_params=_cparams(("parallel", "arbitrary", "arbitrary")),
        name="nsa_sample",
    )(page_table, zn, win_t, pool, e0, *([cache_t] * PAGES_PER_STEP))


def _merge_kernel(x_ref, bg_ref, br_ref, bn_ref, mg_ref, wbr_ref, wout_ref, lng_ref, lnb_ref, o_ref):
    acc = _sigmoid(mg_ref[:, 0:1024]) * _mm(bg_ref[...], wbr_ref[0])
    acc = acc + _sigmoid(mg_ref[:, 1024:2048]) * _mm(br_ref[...], wbr_ref[1])
    acc = acc + _sigmoid(mg_ref[:, 2048:3072]) * _mm(bn_ref[...], wbr_ref[2])
    xf = DN_ALPHA * x_ref[...] + _mm(acc, wout_ref[...])
    mu = jnp.mean(xf, axis=-1, keepdims=True)
    d = xf - mu
    var = jnp.mean(d * d, axis=-1, keepdims=True)
    o_ref[...] = d * lax.rsqrt(var + LN_EPS) * lng_ref[...] + lnb_ref[...]


def _merge(x, o_gla, o_rwkv, o_nsa, zm, w_br, w_out, ln_g, ln_b, tm):
    m = x.shape[0]
    row = lambda n: pl.BlockSpec((tm, n), lambda i: (i, 0))
    return pl.pallas_call(
        _merge_kernel,
        grid=(m // tm,),
        in_specs=[row(D_MODEL), row(512), row(512), row(512), row(ZM_W),
                  pl.BlockSpec((3, 512, D_MODEL), lambda i: (0, 0, 0)),
                  pl.BlockSpec((D_MODEL, D_MODEL), lambda i: (0, 0)),
                  pl.BlockSpec((1, D_MODEL), lambda i: (0, 0)),
                  pl.BlockSpec((1, D_MODEL), lambda i: (0, 0))],
        out_specs=row(D_MODEL),
        out_shape=jax.ShapeDtypeStruct((m, D_MODEL), F32),
        compiler_params=_cparams(("parallel",)),
        name="merge",
    )(x, o_gla, o_rwkv, o_nsa, zm, w_br, w_out, ln_g, ln_b)


def _pack_weights(w_in, b_in):
    def pack(a):
        z = lambda n: jnp.zeros(a.shape[:-1] + (n,), a.dtype)
        gla = jnp.concatenate([a[..., 0:1024], a[..., 1040:1552], a[..., 1024:1040], z(ZG_W - 1552)], axis=-1)
        rwkv = a[..., 1552:3728]
        nsa = jnp.concatenate([a[..., 4240:5008], a[..., 5008:5032], z(1024 - 792), a[..., 3728:4240],
                               a[..., 5032:5544]], axis=-1)
        mg = a[..., 5544:8616]
        return gla, rwkv, nsa, mg
    ws = [w.astype(BF16) for w in pack(w_in)]
    bs = [b[:, None, :] for b in pack(b_in)]
    return ws, bs


def kernel(x_prompt, x_sample, cache_nsa_kv, state_nsa_win, state_gla, state_rwkv, state_rwkv_shift, page_table,
           w_in, b_in, gla_a_up, gla_a_bias, gla_norm, rwkv_mu, rwkv_w0, rwkv_w_up, rwkv_a0, rwkv_a_up, rwkv_k_k,
           rwkv_k_a, rwkv_r_k, rwkv_ln_w, rwkv_ln_b, w_br, w_out, ln_g, ln_b):
    bp, tp, _ = x_prompt.shape
    bs, ts, _ = x_sample.shape
    n_pool = cache_nsa_kv.shape[1]
    ws, bws = _pack_weights(w_in, b_in)
    w_br_b = w_br.astype(BF16)
    w_out_b = w_out.astype(BF16)
    seg = (jnp.arange(RWKV_W)[:, None] // RWKV_HD == jnp.arange(RWKV_W)[None, :] // RWKV_HD).astype(BF16)
    cache_t = jnp.transpose(cache_nsa_kv, (0, 1, 3, 4, 5, 2)).reshape(DEPTH * n_pool, 8 * NSA_HD, PAGE_SIZE)
    win_t = jnp.transpose(state_nsa_win, (0, 1, 3, 4, 5, 2)).reshape(DEPTH * bs, HALF_PAGE_ROWS, WINDOW)
    row2 = lambda a: a.reshape(DEPTH, 1, -1)
    gla_a_bias2, gla_norm2 = row2(gla_a_bias), row2(gla_norm)
    r_par = [row2(rwkv_mu), row2(rwkv_w0), rwkv_w_up, row2(rwkv_a0), rwkv_a_up, row2(rwkv_k_k), row2(rwkv_k_a),
             row2(rwkv_r_k), row2(rwkv_ln_w), row2(rwkv_ln_b)]
    ln_g2, ln_b2 = row2(ln_g), row2(ln_b)

    xp = x_prompt.reshape(bp * tp, D_MODEL)
    xs = jnp.pad(x_sample, ((0, 0), (0, SAMPLE_TPAD - ts), (0, 0))).reshape(bs * SAMPLE_TPAD, D_MODEL)
    zeros_gla = jnp.zeros((bp, GLA_HEADS, GLA_DK, GLA_DV), F32)
    zeros_rwkv = jnp.zeros((bp, RWKV_HEADS, RWKV_HD, RWKV_HD), F32)
    zeros_shift = jnp.zeros((bp, 1, RWKV_IN), F32)

    outs = {k: [] for k in ("kv_p", "kv_s", "win_p", "win_s", "gla_p", "gla_s", "rwkv_p", "rwkv_s", "sh_p", "sh_s")}
    for l in range(DEPTH):
        rp = [p[l] for p in r_par]
        zg, zr, zn, zm = (_proj(xp, ws[i][l], bws[i][l], 512) for i in range(4))
        zn3 = zn.reshape(bp, tp, ZN_W)
        o_gla, gla_st = _gla(zg.reshape(bp, tp, ZG_W), zeros_gla, gla_a_up[l], gla_a_bias2[l], gla_norm2[l],
                             tb_rows=512, chunk=GLA_CHUNK, t_valid=None)
        zr3 = zr.reshape(bp, tp, RWKV_IN)
        o_rwkv, rwkv_st = _rwkv(zr3, zeros_rwkv, zeros_shift, seg, rp, tb_rows=512, chunk=RWKV_CHUNK, t_valid=None)
        o_nsa = _nsa_prompt(zn3, tq=256)
        xp = _merge(xp, o_gla.reshape(bp * tp, 512), o_rwkv.reshape(bp * tp, 512), o_nsa.reshape(bp * tp, 512), zm,
                    w_br_b[l], w_out_b[l], ln_g2[l], ln_b2[l], 512)
        outs["kv_p"].append(zn3[:, :, 0:512].reshape(bp, tp, 4, NSA_KV_HEADS, NSA_HD))
        outs["win_p"].append(zn3[:, tp - WINDOW:, 512:768].reshape(bp, WINDOW, 2, NSA_KV_HEADS, NSA_HD))
        outs["gla_p"].append(gla_st)
        outs["rwkv_p"].append(rwkv_st)
        outs["sh_p"].append(zr3[:, tp - 1, :])
        rows_s = bs * SAMPLE_TPAD
        zg, zr, zn, zm = (_proj(xs, ws[i][l], bws[i][l], rows_s) for i in range(4))
        zn3 = zn.reshape(bs, SAMPLE_TPAD, ZN_W)
        o_gla, gla_st = _gla(zg.reshape(bs, SAMPLE_TPAD, ZG_W), state_gla[l], gla_a_up[l], gla_a_bias2[l],
                             gla_norm2[l], tb_rows=SAMPLE_TPAD, chunk=SAMPLE_TPAD, t_valid=ts)
        zr3 = zr.reshape(bs, SAMPLE_TPAD, RWKV_IN)
        o_rwkv, rwkv_st = _rwkv(zr3, state_rwkv[l], state_rwkv_shift[l][:, None, :], seg, rp,
                                tb_rows=SAMPLE_TPAD, chunk=SAMPLE_TPAD, t_valid=ts)
        o_nsa = _nsa_sample(page_table, zn3, win_t, cache_t, layer=l, n_pool=n_pool, n_tok=ts)
        xs = _merge(xs, o_gla.reshape(rows_s, 512), o_rwkv.reshape(rows_s, 512), o_nsa.reshape(rows_s, 512), zm,
                    w_br_b[l], w_out_b[l], ln_g2[l], ln_b2[l], rows_s)
        outs["kv_s"].append(zn3[:, 0:ts, 0:512].reshape(bs, ts, 4, NSA_KV_HEADS, NSA_HD))
        new_win = zn3[:, 0:ts, 512:768].reshape(bs, ts, 2, NSA_KV_HEADS, NSA_HD)
        outs["win_s"].append(jnp.concatenate([state_nsa_win[l][:, ts:], new_win], axis=1))
        outs["gla_s"].append(gla_st)
        outs["rwkv_s"].append(rwkv_st)
        outs["sh_s"].append(zr3[:, ts - 1, :])

    st = lambda k: jnp.stack(outs[k])
    y_prompt = xp.reshape(bp, tp, D_MODEL)
    y_sample = xs.reshape(bs, SAMPLE_TPAD, D_MODEL)[:, 0:ts]
    return (y_prompt, y_sample, st("kv_p"), st("kv_s"), st("win_p"), st("win_s"), st("gla_p"), st("gla_s"),
            st("rwkv_p"), st("rwkv_s"), st("sh_p"), st("sh_s"))
```

```python
import functools

import jax
import jax.numpy as jnp
from jax import lax
from jax.experimental import pallas as pl
from jax.experimental.pallas import tpu as pltpu

F32 = jnp.float32
BF16 = jnp.bfloat16

D_MODEL = 1024
DEPTH = 2
PAST_LEN = 16384
PAGE_SIZE = 128
N_PAGES = PAST_LEN // PAGE_SIZE

GLA_HEADS, GLA_DK, GLA_DV = 4, 64, 128
GLA_K, GLA_V, GLA_LORA = 256, 512, 16
GLA_GATE_NORM = 16.0
GLA_CHUNK = 64
GLA_SUB = 16

RWKV_HEADS, RWKV_HD, RWKV_W = 8, 64, 512
RWKV_IN = 2176
RWKV_LN_EPS = 64e-5
RWKV_CHUNK = 64

NSA_HEADS, NSA_KV_HEADS, NSA_GROUP, NSA_HD = 8, 2, 4, 64
CMP_BLOCK, SEL_BLOCK, N_SELECT, WINDOW = 32, 64, 16, 512
FORCE_SCORE = 1e9
NEG = -1e30

DN_ALPHA = (2 * DEPTH) ** 0.25
LN_EPS = 1e-5
NORM_EPS = 1e-6

ZG_W = 1664
ZN_W = 2048
ZM_W = 3072
SAMPLE_TPAD = 16
VMEM_LIMIT = 56 * 1024 * 1024


def _mm(a, b):
    return jnp.dot(a.astype(BF16), b.astype(BF16), preferred_element_type=F32)


def _mm_nt(a, b):
    return lax.dot_general(a.astype(BF16), b.astype(BF16), (((1,), (1,)), ((), ())), preferred_element_type=F32)


def _mm_tn(a, b):
    return lax.dot_general(a.astype(BF16), b.astype(BF16), (((0,), (0,)), ((), ())), preferred_element_type=F32)


def _mm_exact(a, b):
    return jnp.dot(a, b, preferred_element_type=F32, precision=lax.Precision.HIGHEST)


def _split2(a):
    hi = a.astype(BF16)
    return hi, (a - hi.astype(F32)).astype(BF16)


def _mm_split(a, b01):
    hi, lo = _split2(a)
    return jnp.dot(hi, b01, preferred_element_type=F32) + jnp.dot(lo, b01, preferred_element_type=F32)


def _softplus(x):
    return jnp.maximum(x, 0.0) + jnp.log(1.0 + jnp.exp(-jnp.abs(x)))


def _sigmoid(x):
    return 0.5 * jnp.tanh(0.5 * x) + 0.5


def _silu(x):
    return x * _sigmoid(x)


def _iota(shape, dim):
    return lax.broadcasted_iota(jnp.int32, shape, dim)


def _cparams(sem):
    return pltpu.CompilerParams(dimension_semantics=sem, vmem_limit_bytes=VMEM_LIMIT)


def _proj_kernel(x_ref, w_ref, b_ref, o_ref):
    o_ref[...] = _mm(x_ref[...], w_ref[...]) + b_ref[...]


def _proj(x, w, b, tm):
    m, k = x.shape
    n = w.shape[1]
    return pl.pallas_call(
        _proj_kernel,
        grid=(m // tm,),
        in_specs=[pl.BlockSpec((tm, k), lambda i: (i, 0)),
                  pl.BlockSpec((k, n), lambda i: (0, 0)),
                  pl.BlockSpec((1, n), lambda i: (0, 0))],
        out_specs=pl.BlockSpec((tm, n), lambda i: (i, 0)),
        out_shape=jax.ShapeDtypeStruct((m, n), F32),
        compiler_params=_cparams(("parallel",)),
        name="proj",
    )(x, w, b)


def _gla_kernel(zg_ref, s0_ref, aup_ref, abias_ref, norm_ref, o_ref, sout_ref, st_scr, *, tb_rows, chunk, t_valid):
    tb = pl.program_id(1)
    c_rows = chunk
    sub = min(GLA_SUB, c_rows)
    nsub = c_rows // sub

    @pl.when(tb == 0)
    def _():
        for h in range(GLA_HEADS):
            st_scr[h] = s0_ref[0, h].T

    tri = (_iota((c_rows, c_rows), 1) <= _iota((c_rows, c_rows), 0)).astype(F32)
    ones_red = jnp.ones((GLA_DK, 128), BF16)
    lane_s = _iota((sub, 128), 1)
    row_s = _iota((sub, 128), 0)
    col_c = _iota((sub, c_rows), 1)

    def chunk_body(c, carry):
        r0 = pl.multiple_of(c * c_rows, c_rows)
        z = zg_ref[0, pl.ds(r0, c_rows), :]
        q = z[:, 0:256] * (GLA_DK ** -0.5)
        k = z[:, 256:512]
        v = z[:, 512:1024]
        g = z[:, 1024:1536]
        ga = z[:, 1536:1552]
        la = -_softplus(-(_mm(ga, aup_ref[...]) + abias_ref[...])) * (1.0 / GLA_GATE_NORM)
        if t_valid is not None:
            ok = (tb * tb_rows + r0 + _iota((c_rows, 1), 0)) < t_valid
            la = jnp.where(ok, la, 0.0)
            k = jnp.where(ok, k, 0.0)
            v = jnp.where(ok, v, 0.0)
        cum = _mm_exact(tri, la)
        heads = range(GLA_HEADS)
        pairs = [(h, blk) for h in heads for blk in range(nsub)]
        qh = [q[:, 64 * h:64 * h + 64] for h in heads]
        kh = [k[:, 64 * h:64 * h + 64] for h in heads]
        ch = [cum[:, 64 * h:64 * h + 64] for h in heads]
        vh = [v[:, 128 * h:128 * h + 128] for h in heads]
        st = [st_scr[h] for h in heads]
        o_in = [_mm_nt(qh[h] * jnp.exp(ch[h]), st[h]) for h in heads]
        red, off = {}, {}
        for h, blk in pairs:
            sl = slice(blk * sub, (blk + 1) * sub)
            q_i, k_i, c_i = qh[h][sl], kh[h][sl], ch[h][sl]
            es = [q_i * k_i[j:j + 1] * jnp.exp(jnp.minimum(c_i - c_i[j:j + 1], 0.0)) for j in range(sub)]
            red[h, blk] = _mm(jnp.concatenate(es, axis=0), ones_red)
            if blk > 0:
                b_i = ch[h][blk * sub - 1:blk * sub]
                q_t = q_i * jnp.exp(c_i - b_i)
                k_t = kh[h] * jnp.exp(jnp.minimum(b_i - ch[h], 0.0))
                off[h, blk] = _mm_nt(q_t, k_t)
        att = []
        for h in heads:
            att_rows = []
            for blk in range(nsub):
                a_i = jnp.zeros((sub, 128), F32)
                for j in range(sub):
                    a_i = a_i + jnp.where((lane_s == blk * sub + j) & (row_s >= j),
                                          red[h, blk][j * sub:(j + 1) * sub], 0.0)
                a_i = a_i[:, 0:c_rows]
                if blk > 0:
                    a_i = a_i + jnp.where(col_c < blk * sub, off[h, blk], 0.0)
                att_rows.append(a_i)
            att.append(att_rows[0] if nsub == 1 else jnp.concatenate(att_rows, axis=0))
        o = [o_in[h] + _mm(att[h], vh[h]) for h in heads]
        last = [ch[h][c_rows - 1:c_rows] for h in heads]
        st_new = [st[h] * jnp.exp(last[h]) + _mm_tn(vh[h], kh[h] * jnp.exp(last[h] - ch[h])) for h in heads]
        for h in heads:
            st_scr[h] = st_new[h]
            oh = o[h] * lax.rsqrt(jnp.mean(o[h] * o[h], axis=-1, keepdims=True) + NORM_EPS)
            oh = oh * norm_ref[:, 128 * h:128 * h + 128] * _silu(g[:, 128 * h:128 * h + 128])
            o_ref[0, pl.ds(r0, c_rows), 128 * h:128 * h + 128] = oh
        return carry

    lax.fori_loop(0, tb_rows // c_rows, chunk_body, 0, unroll=min(2, tb_rows // c_rows))

    @pl.when(tb == pl.num_programs(1) - 1)
    def _():
        for h in range(GLA_HEADS):
            sout_ref[0, h] = st_scr[h].T


def _gla(zg, s0, a_up, a_bias, norm_g, *, tb_rows, chunk, t_valid):
    b, t, _ = zg.shape
    kern = functools.partial(_gla_kernel, tb_rows=tb_rows, chunk=chunk, t_valid=t_valid)
    return pl.pallas_call(
        kern,
        grid=(b, t // tb_rows),
        in_specs=[pl.BlockSpec((1, tb_rows, ZG_W), lambda i, j: (i, j, 0)),
                  pl.BlockSpec((1, GLA_HEADS, GLA_DK, GLA_DV), lambda i, j: (i, 0, 0, 0)),
                  pl.BlockSpec((GLA_LORA, GLA_K), lambda i, j: (0, 0)),
                  pl.BlockSpec((1, GLA_K), lambda i, j: (0, 0)),
                  pl.BlockSpec((1, GLA_V), lambda i, j: (0, 0))],
        out_specs=[pl.BlockSpec((1, tb_rows, GLA_V), lambda i, j: (i, j, 0)),
                   pl.BlockSpec((1, GLA_HEADS, GLA_DK, GLA_DV), lambda i, j: (i, 0, 0, 0))],
        out_shape=[jax.ShapeDtypeStruct((b, t, GLA_V), F32),
                   jax.ShapeDtypeStruct((b, GLA_HEADS, GLA_DK, GLA_DV), F32)],
        scratch_shapes=[pltpu.VMEM((GLA_HEADS, GLA_DV, GLA_DK), F32)],
        compiler_params=_cparams(("parallel", "arbitrary")),
        name="gla",
    )(zg, s0, a_up, a_bias, norm_g)


def _rwkv_kernel(zr_ref, s0_ref, sh0_ref, seg_ref, mu_ref, w0_ref, wup_ref, a0_ref, aup_ref, kk_ref, ka_ref, rk_ref,
                 lnw_ref, lnb_ref, y_ref, sout_ref,
                 s_scr, prev_scr, lw_s, kk_s, kka_s, k2_s, r_s, v_s, y_s, *, tb_rows, chunk, t_valid):
    tb = pl.program_id(1)
    c_rows = chunk
    nh = RWKV_HEADS

    @pl.when(tb == 0)
    def _():
        s_scr[...] = s0_ref[0]
        prev_scr[...] = sh0_ref[0]

    z = zr_ref[0]
    rows = _iota((tb_rows, 1), 0)
    zp = pltpu.roll(z, 1, axis=0)
    zp = jnp.concatenate([jnp.where(rows[0:8] == 0, prev_scr[...], zp[0:8]), zp[8:]], axis=0)
    prev_scr[...] = z[tb_rows - 1:tb_rows]
    zs = z + (zp - z) * mu_ref[...]
    r = zs[:, 0:512]
    k = zs[:, 512:1024]
    v = zs[:, 1024:1536]
    wl = zs[:, 1536:1600]
    al = zs[:, 1600:1664]
    w = -_softplus(-(w0_ref[...] + _mm(jnp.tanh(wl), wup_ref[...]))) - 0.5
    lw = -jnp.exp(w)
    a = _sigmoid(a0_ref[...] + _mm(al, aup_ref[...]))
    kk = k * kk_ref[...]
    kk = kk * lax.rsqrt(_mm_split(kk * kk, seg_ref[...]) + NORM_EPS)
    k2 = k * (1.0 + (a - 1.0) * ka_ref[...])
    kka = kk * a
    if t_valid is not None:
        ok = (tb * tb_rows + rows) < t_valid
        lw = jnp.where(ok, lw, 0.0)
        kka = jnp.where(ok, kka, 0.0)
        k2 = jnp.where(ok, k2, 0.0)
    lw_s[...] = lw
    kk_s[...] = kk
    kka_s[...] = kka
    k2_s[...] = k2
    r_s[...] = r
    v_s[...] = v

    ri = _iota((c_rows, c_rows), 0)
    ci = _iota((c_rows, c_rows), 1)
    tri = (ci <= ri).astype(F32)
    strict = ci < ri
    incl = ci <= ri
    n_dbl = max(1, (c_rows - 1).bit_length())

    def chunk_body(c, carry):
        r0 = pl.multiple_of(c * c_rows, c_rows)
        ds = pl.ds(r0, c_rows)
        lwc = lw_s[ds, :]
        cl = _mm_exact(tri, lwc)
        e_inv = jnp.exp(-cl)
        e_fwd = jnp.exp(cl)
        e_prev = jnp.exp(cl - lwc)
        e_end = jnp.exp(cl[c_rows - 1:c_rows] - cl)
        g_end = jnp.exp(cl[c_rows - 1:c_rows])
        kkc, kkac, k2c, rc, vc = kk_s[ds, :], kka_s[ds, :], k2_s[ds, :], r_s[ds, :], v_s[ds, :]
        heads = range(nh)
        hsl = [slice(64 * h, 64 * h + 64) for h in heads]
        a_t = [-kkac[:, hs] * e_inv[:, hs] for hs in hsl]
        b_t = [kkc[:, hs] * e_prev[:, hs] for hs in hsl]
        k_t = [k2c[:, hs] * e_inv[:, hs] for hs in hsl]
        r_t = [rc[:, hs] * e_fwd[:, hs] for hs in hsl]
        vh = [vc[:, hs] for hs in hsl]
        cc = c_rows
        ak = [jnp.concatenate([a_t[h], k_t[h]], axis=0).astype(BF16) for h in heads]
        bra = [_mm_nt(jnp.concatenate([b_t[h], r_t[h]], axis=0), ak[h]) for h in heads]
        s0 = [s_scr[h] for h in heads]
        l_k = [jnp.where(strict, bra[h][0:cc, cc:2 * cc], 0.0) for h in heads]
        m_k = [jnp.where(incl, bra[h][cc:2 * cc, cc:2 * cc], 0.0) for h in heads]
        lmv = [_mm(jnp.concatenate([l_k[h], m_k[h]], axis=0), vh[h]) for h in heads]
        x = [jnp.concatenate([b_t[h], lmv[h][0:cc]], axis=1) for h in heads]
        lp = [jnp.where(strict, bra[h][0:cc, 0:cc], 0.0).astype(BF16) for h in heads]
        for step in range(n_dbl):
            if step + 1 < n_dbl:
                new = [_mm(lp[h], jnp.concatenate([x[h].astype(BF16), lp[h]], axis=1)) for h in heads]
                x = [x[h] + new[h][:, 0:128] for h in heads]
                lp = [new[h][:, 128:128 + cc].astype(BF16) for h in heads]
            else:
                x = [x[h] + _mm(lp[h], x[h]) for h in heads]
        prs = [_mm_nt(jnp.concatenate([x[h][:, 0:64], r_t[h]], axis=0), s0[h]) for h in heads]
        u = [prs[h][0:cc] + x[h][:, 64:128] for h in heads]
        m_a = [jnp.where(incl, bra[h][cc:2 * cc, 0:cc], 0.0) for h in heads]
        y = [prs[h][cc:2 * cc] + _mm(m_a[h], u[h]) + lmv[h][cc:2 * cc] for h in heads]
        akg = [jnp.concatenate([-kkac[:, hs] * e_end[:, hs], k2c[:, hs] * e_end[:, hs]], axis=0) for hs in hsl]
        s_new = [s0[h] * g_end[:, hsl[h]] + _mm_tn(jnp.concatenate([u[h], vh[h]], axis=0), akg[h]) for h in heads]
        for h in heads:
            s_scr[h] = s_new[h]
            y_s[ds, hsl[h]] = y[h]
        return carry

    lax.fori_loop(0, tb_rows // c_rows, chunk_body, 0, unroll=min(2, tb_rows // c_rows))

    y = y_s[...]
    seg = seg_ref[...]
    mean = _mm_split(y, seg) * (1.0 / RWKV_HD)
    d = y - mean
    var = _mm_split(d * d, seg) * (1.0 / RWKV_HD)
    yn = d * lax.rsqrt(var + RWKV_LN_EPS) * lnw_ref[...] + lnb_ref[...]
    bonus = _mm_split(r * k2 * rk_ref[...], seg) * v
    y_ref[0] = (yn + bonus) * _silu(zs[:, 1664:2176])

    @pl.when(tb == pl.num_programs(1) - 1)
    def _():
        sout_ref[0] = s_scr[...]


def _rwkv(zr, s0, sh0, seg, params, *, tb_rows, chunk, t_valid):
    b, t, _ = zr.shape
    kern = functools.partial(_rwkv_kernel, tb_rows=tb_rows, chunk=chunk, t_valid=t_valid)
    full = lambda shp: pl.BlockSpec(shp, lambda i, j: (0,) * len(shp))
    mu, w0, w_up, a0, a_up, k_k, k_a, r_k, ln_w, ln_b = params
    return pl.pallas_call(
        kern,
        grid=(b, t // tb_rows),
        in_specs=[pl.BlockSpec((1, tb_rows, RWKV_IN), lambda i, j: (i, j, 0)),
                  pl.BlockSpec((1, RWKV_HEADS, RWKV_HD, RWKV_HD), lambda i, j: (i, 0, 0, 0)),
                  pl.BlockSpec((1, 1, RWKV_IN), lambda i, j: (i, 0, 0)),
                  full((RWKV_W, RWKV_W)), full((1, RWKV_IN)), full((1, RWKV_W)), full((64, RWKV_W)),
                  full((1, RWKV_W)), full((64, RWKV_W)), full((1, RWKV_W)), full((1, RWKV_W)), full((1, RWKV_W)),
                  full((1, RWKV_W)), full((1, RWKV_W))],
        out_specs=[pl.BlockSpec((1, tb_rows, RWKV_W), lambda i, j: (i, j, 0)),
                   pl.BlockSpec((1, RWKV_HEADS, RWKV_HD, RWKV_HD), lambda i, j: (i, 0, 0, 0))],
        out_shape=[jax.ShapeDtypeStruct((b, t, RWKV_W), F32),
                   jax.ShapeDtypeStruct((b, RWKV_HEADS, RWKV_HD, RWKV_HD), F32)],
        scratch_shapes=[pltpu.VMEM((RWKV_HEADS, RWKV_HD, RWKV_HD), F32), pltpu.VMEM((1, RWKV_IN), F32)]
        + [pltpu.VMEM((tb_rows, RWKV_W), F32) for _ in range(7)],
        compiler_params=_cparams(("parallel", "arbitrary")),
        name="rwkv",
    )(zr, s0, sh0, seg, mu, w0, w_up, a0, a_up, k_k, k_a, r_k, ln_w, ln_b)


def _slope(h):
    return 2.0 ** (-8.0 * (h + 1) / NSA_HEADS)


def _flash_init(m_ref, l_ref, acc_ref):
    m_ref[...] = jnp.full(m_ref.shape, NEG, F32)
    l_ref[...] = jnp.zeros(l_ref.shape, F32)
    acc_ref[...] = jnp.zeros(acc_ref.shape, F32)


def _flash_step_t(s_t, v, m_ref, l_ref, acc_ref):
    m_old = m_ref[...]
    m_new = jnp.maximum(m_old, jnp.max(s_t, axis=0, keepdims=True))
    p = jnp.exp(s_t - m_new)
    alpha = jnp.exp(m_old - m_new)
    l_ref[...] = alpha * l_ref[...] + jnp.sum(p, axis=0, keepdims=True)
    acc_ref[...] = alpha * acc_ref[...] + _mm_tn(v, p)
    m_ref[...] = m_new


def _flash_steps_t(s_list, v_list, m_ref, l_ref, acc_ref):
    ks = range(len(s_list))
    m_old = [m_ref[k] for k in ks]
    m_new = [jnp.maximum(m_old[k], jnp.max(s_list[k], axis=0, keepdims=True)) for k in ks]
    p = [jnp.exp(s_list[k] - m_new[k]) for k in ks]
    alpha = [jnp.exp(m_old[k] - m_new[k]) for k in ks]
    pv = [_mm_tn(v_list[k], p[k]) for k in ks]
    for k in ks:
        l_ref[k] = alpha[k] * l_ref[k] + jnp.sum(p[k], axis=0, keepdims=True)
        acc_ref[k] = alpha[k] * acc_ref[k] + pv[k]
        m_ref[k] = m_new[k]


def _flash_result(l_ref, acc_ref, k):
    l = l_ref[k]
    return acc_ref[k] / jnp.where(l > 0, l, 1.0)


SEL_LANE0 = 66
BLOCK_PENALTY = -(2.0 ** 100)


def _nsa_prompt_kernel(q_ref, kv_ref, bg_ref, g_ref, o_ref, kvt_ref, kc_s, vc_s, ks_s, vs_s, kw_s, vw_s, qa_s, qs_s, oc_s,
                       m_s, l_s, acc_s, m2_s, l2_s, acc2_s, *, t_len, tq):
    i = pl.program_id(1)
    nb = t_len // SEL_BLOCK
    tk = tq
    ncol = NSA_GROUP * tq
    win_tiles = WINDOW // tk

    def aug_lanes(lane, pos):
        return jnp.where(lane == 64, lax.shift_right_logical(pos, 7).astype(F32),
                         jnp.where(lane == 65, (pos & 127).astype(F32), 0.0))

    def head_lanes(x, kvh):
        return x if kvh == 0 else pltpu.roll(x, 64, axis=1)

    @pl.when(i == 0)
    def _():
        kvt_ref[0] = kv_ref[0].T
        lane = _iota((t_len, 128), 1)
        pos = _iota((t_len, 128), 0)
        aug = aug_lanes(lane, pos)
        aug_sel = jnp.where(lane - SEL_LANE0 == lax.shift_right_logical(pos, 6), 1.0, aug)
        for kcol, vcol, k_dst, v_dst, k_aug in ((256, 384, ks_s, vs_s, aug_sel), (512, 640, kw_s, vw_s, aug)):
            kf = kv_ref[0, :, kcol:kcol + 128]
            vf = kv_ref[0, :, vcol:vcol + 128]
            for kvh in range(NSA_KV_HEADS):
                k_dst[kvh] = jnp.where(lane < 64, head_lanes(kf, kvh), k_aug).astype(BF16)
                v_dst[kvh] = jnp.where(lane < 64, head_lanes(vf, kvh), 0.0).astype(BF16)
        kcm = kv_ref[0, :, 0:128].reshape(nb, SEL_BLOCK, 128)
        vcm = kv_ref[0, :, 128:256].reshape(nb, SEL_BLOCK, 128)
        inv = 1.0 / CMP_BLOCK
        kc = jnp.concatenate([jnp.sum(kcm[:, 0:CMP_BLOCK], axis=1) * inv,
                              jnp.sum(kcm[:, CMP_BLOCK:SEL_BLOCK], axis=1) * inv], axis=0)
        vc = jnp.concatenate([jnp.sum(vcm[:, 0:CMP_BLOCK], axis=1) * inv,
                              jnp.sum(vcm[:, CMP_BLOCK:SEL_BLOCK], axis=1) * inv], axis=0)
        lane_c = _iota((2 * nb, 128), 1)
        r_c = _iota((2 * nb, 128), 0)
        cend = jnp.where(r_c < nb, SEL_BLOCK * r_c + (CMP_BLOCK - 1), SEL_BLOCK * (r_c - nb) + (SEL_BLOCK - 1))
        aug_c = aug_lanes(lane_c, cend)
        for kvh in range(NSA_KV_HEADS):
            kc_s[kvh] = jnp.where(lane_c < 64, head_lanes(kc, kvh), aug_c).astype(BF16)
            vc_s[kvh] = jnp.where(lane_c < 64, head_lanes(vc, kvh), 0.0).astype(BF16)

    lane_q = _iota((tq, 128), 1)
    for h in range(NSA_HEADS):
        qb = q_ref[0, :, 128 * (h // 2):128 * (h // 2) + 128]
        if h % 2:
            qb = pltpu.roll(qb, 64, axis=1)
        sl = _slope(h)
        qa = jnp.where(lane_q < 64, qb * (NSA_HD ** -0.5),
                       jnp.where(lane_q == 64, 128.0 * sl, jnp.where(lane_q == 65, sl, 0.0)))
        qa_s[h // NSA_GROUP, (h % NSA_GROUP) * tq:(h % NSA_GROUP + 1) * tq, :] = qa.astype(BF16)

    t_row = i * tq + (_iota((1, ncol), 1) & (tq - 1))
    t_row1 = i * tq + _iota((1, tq), 1)
    r_c1 = _iota((2 * nb, 1), 0)
    cend_col = jnp.where(r_c1 < nb, SEL_BLOCK * r_c1 + (CMP_BLOCK - 1), SEL_BLOCK * (r_c1 - nb) + (SEL_BLOCK - 1))
    mask_c = cend_col <= t_row
    jrow = _iota((nb, 1), 0)
    tblk = lax.shift_right_logical(t_row1, 6)
    forced = (jrow == tblk) | (jrow == 0)
    valid = jrow <= tblk
    bg_t = bg_ref[0].T

    for kvh in range(NSA_KV_HEADS):
        qa = qa_s[kvh]
        s_c = jnp.where(mask_c, _mm_nt(kc_s[kvh], qa), NEG)
        m = jnp.max(s_c, axis=0, keepdims=True)
        p = jnp.where(mask_c, jnp.exp(s_c - m), 0.0)
        den = jnp.sum(p, axis=0, keepdims=True)
        p = p / jnp.where(den > 0, den, 1.0)
        oc_s[kvh] = _mm_tn(vc_s[kvh], p)
        imp_e = p[0:nb, 0:tq]
        imp_o = p[nb:2 * nb, 0:tq]
        for g in range(1, NSA_GROUP):
            imp_e = imp_e + p[0:nb, g * tq:(g + 1) * tq]
            imp_o = imp_o + p[nb:2 * nb, g * tq:(g + 1) * tq]
        score = jnp.where(forced, FORCE_SCORE, jnp.where(valid, imp_e + imp_o, -FORCE_SCORE))
        rank = jnp.zeros((nb, tq), F32)
        for jp in range(nb):
            rj = score[jp:jp + 1, :]
            rank = rank + jnp.where((rj > score) | ((rj == score) & (jp < jrow)), 1.0, 0.0)
        sel = jnp.where((rank < N_SELECT) & valid, 1.0, 0.0)
        sel_q = jnp.concatenate([sel, jnp.zeros((128 - nb, tq), F32)], axis=0).T
        sel_q = pltpu.roll(sel_q, SEL_LANE0, axis=1)
        pen = jnp.where((lane_q >= SEL_LANE0) & (lane_q < SEL_LANE0 + nb) & (sel_q < 0.5), BLOCK_PENALTY, 0.0)
        for g in range(NSA_GROUP):
            cs = slice(g * tq, (g + 1) * tq)
            qs_s[kvh, cs, :] = (qa_s[kvh, cs, :].astype(F32) + pen).astype(BF16)

    _flash_init(m_s, l_s, acc_s)
    _flash_init(m2_s, l2_s, acc2_s)
    kvs = range(NSA_KV_HEADS)

    def sel_step(kt, diagonal):
        k0 = pl.multiple_of(kt * tk, tk)
        s_t = [_mm_nt(ks_s[kvh, pl.ds(k0, tk), :], qs_s[kvh]) for kvh in kvs]
        if diagonal:
            causal = kt * tk + _iota((tk, 1), 0) <= t_row
            s_t = [jnp.where(causal, s, NEG) for s in s_t]
        _flash_steps_t(s_t, [vs_s[kvh, pl.ds(k0, tk), :] for kvh in kvs], m_s, l_s, acc_s)

    def sel_body(kt, carry):
        sel_step(kt, False)
        return carry

    lax.fori_loop(0, i, sel_body, 0)
    sel_step(i, True)

    def win_step(kt, back):
        k0 = pl.multiple_of(kt * tk, tk)
        s_t = [_mm_nt(kw_s[kvh, pl.ds(k0, tk), :], qa_s[kvh]) for kvh in kvs]
        dist = t_row - (kt * tk + _iota((tk, 1), 0))
        if back == 0:
            s_t = [jnp.where(dist >= 0, s, NEG) for s in s_t]
        elif back == win_tiles:
            s_t = [jnp.where(dist <= WINDOW, s, NEG) for s in s_t]
        _flash_steps_t(s_t, [vw_s[kvh, pl.ds(k0, tk), :] for kvh in kvs], m2_s, l2_s, acc2_s)

    for back in range(win_tiles, 0, -1):
        @pl.when(i >= back)
        def _(back=back):
            win_step(i - back, back)
    win_step(i, 0)

    for kvh in kvs:
        o_sel = _flash_result(l_s, acc_s, kvh)
        o_win = _flash_result(l2_s, acc2_s, kvh)
        o_cmp = oc_s[kvh]
        for g in range(NSA_GROUP):
            h = NSA_GROUP * kvh + g
            cs = slice(g * tq, (g + 1) * tq)
            gates = _sigmoid(bg_t[3 * h:3 * h + 3, :])
            comb = gates[0:1] * o_cmp[:, cs] + gates[1:2] * o_sel[:, cs] + gates[2:3] * o_win[:, cs]
            o_ref[0, :, 64 * h:64 * h + 64] = comb.T[:, 0:64] * _silu(g_ref[0, :, 64 * h:64 * h + 64])


def _nsa_prompt(zn, *, tq):
    b, t, _ = zn.shape
    nb = t // SEL_BLOCK
    ncol = NSA_GROUP * tq
    kern = functools.partial(_nsa_prompt_kernel, t_len=t, tq=tq)
    kv_f32 = lambda n: pltpu.VMEM((NSA_KV_HEADS, n, ncol), F32)
    stat = lambda: [kv_f32(1), kv_f32(1), kv_f32(128)]
    kvbuf = lambda n: pltpu.VMEM((NSA_KV_HEADS, n, 128), BF16)
    return pl.pallas_call(
        kern,
        grid=(b, t // tq),
        in_specs=[pl.BlockSpec((1, tq, 512), lambda i, j: (i, j, 2)),
                  pl.BlockSpec((1, t, 768), lambda i, j: (i, 0, 0)),
                  pl.BlockSpec((1, tq, 256), lambda i, j: (i, j, 3)),
                  pl.BlockSpec((1, tq, 512), lambda i, j: (i, j, 3))],
        out_specs=[pl.BlockSpec((1, tq, 512), lambda i, j: (i, j, 0)),
                   pl.BlockSpec((1, 768, t), lambda i, j: (i, 0, 0))],
        out_shape=[jax.ShapeDtypeStruct((b, t, 512), F32), jax.ShapeDtypeStruct((b, 768, t), F32)],
        scratch_shapes=[kvbuf(2 * nb), kvbuf(2 * nb), kvbuf(t), kvbuf(t), kvbuf(t), kvbuf(t), kvbuf(ncol),
                        kvbuf(ncol), kv_f32(128)] + stat() + stat(),
        compiler_params=_cparams(("parallel", "arbitrary")),
        name="nsa_prompt",
    )(zn, zn, zn, zn)


PAGES_PER_STEP = 64
N_PAGE_GROUPS = N_PAGES // PAGES_PER_STEP
KEYS_PER_STEP = PAGES_PER_STEP * PAGE_SIZE
SEL_PER_STEP = KEYS_PER_STEP // SEL_BLOCK
POOL_PAGES = 8
POOL_SEL = POOL_PAGES * PAGE_SIZE // SEL_BLOCK
N_PAST_SEL = PAST_LEN // SEL_BLOCK
HALF_PAGE_ROWS = 2 * NSA_KV_HEADS * NSA_HD


def _flash_step_r(s, mask, pv, m_ref, l_ref, acc_ref):
    s = jnp.where(mask, s, NEG)
    m_old = m_ref[...]
    m_new = jnp.maximum(m_old, jnp.max(s, axis=-1, keepdims=True))
    p = jnp.exp(s - m_new)
    alpha = jnp.exp(m_old - m_new)
    l_ref[...] = alpha * l_ref[...] + jnp.sum(p, axis=-1, keepdims=True)
    acc_ref[...] = alpha * acc_ref[...] + pv(p)
    m_ref[...] = m_new


def _nsa_sample_kernel(pt_ref, zn_ref, win_ref, pool_ref, e0_ref, *rest, n_tok):
    pages = rest[:PAGES_PER_STEP]
    o_ref = rest[PAGES_PER_STEP]
    kce_s, kco_s, vce_s, vco_s, qt_s, sel_s, oc_s, m_s, l_s, acc_s = rest[PAGES_PER_STEP + 1:]
    ph = pl.program_id(1)
    gi = pl.program_id(2)
    rq = n_tok
    ncol = NSA_HEADS * rq
    npast = N_PAST_SEL
    col = _iota((ncol, 1), 0)
    t_col = PAST_LEN + (col & (rq - 1))
    hcol = lax.shift_right_logical(col, rq.bit_length() - 1)
    slope_col = jnp.zeros((ncol, 1), F32)
    for h in range(NSA_HEADS):
        slope_col = jnp.where(hcol == h, _slope(h), slope_col)

    def bias(pos_row):
        return slope_col * (t_col - pos_row).astype(F32)

    def keys_t(lo):
        return jnp.concatenate([pg[0, lo:lo + 128, :] for pg in pages], axis=1)

    @pl.when((ph == 0) & (gi == 0))
    def _queries():
        zero = jnp.zeros((rq, NSA_HD), F32)
        rows = []
        for h in range(NSA_HEADS):
            qh = zn_ref[0, 0:rq, 1024 + 64 * h:1088 + 64 * h] * (NSA_HD ** -0.5)
            rows.append(jnp.concatenate([qh, zero] if h < NSA_GROUP else [zero, qh], axis=1))
        qt_s[...] = jnp.concatenate(rows, axis=0).astype(BF16)

    @pl.when(ph == 0)
    def _pool():
        dn = (((1,), (1,)), ((), ()))
        for sub in range(PAGES_PER_STEP // POOL_PAGES):
            grp = pages[sub * POOL_PAGES:(sub + 1) * POOL_PAGES]
            dst = pl.ds(pl.multiple_of(gi * SEL_PER_STEP + sub * POOL_SEL, POOL_SEL), POOL_SEL)
            for lo, even_s, odd_s in ((0, kce_s, kco_s), (128, vce_s, vco_s)):
                hi_part, lo_part = _split2(jnp.concatenate([pg[0, lo:lo + 128, :] for pg in grp], axis=1))
                pooled = (lax.dot_general(pool_ref[...], hi_part, dn, preferred_element_type=F32)
                          + lax.dot_general(pool_ref[...], lo_part, dn, preferred_element_type=F32)) * (1.0 / CMP_BLOCK)
                even_s[dst, :] = pooled[0:POOL_SEL]
                odd_s[dst, :] = pooled[POOL_SEL:2 * POOL_SEL]

    @pl.when((ph == 0) & (gi == N_PAGE_GROUPS - 1))
    def _compressed():
        jrow = _iota((1, npast), 1)
        qt = qt_s[...]
        se = _mm_nt(qt, kce_s[...]) - bias(SEL_BLOCK * jrow + (CMP_BLOCK - 1))
        so = _mm_nt(qt, kco_s[...]) - bias(SEL_BLOCK * jrow + (SEL_BLOCK - 1))
        m = jnp.maximum(jnp.max(se, axis=-1, keepdims=True), jnp.max(so, axis=-1, keepdims=True))
        pe = jnp.exp(se - m)
        po = jnp.exp(so - m)
        den = jnp.sum(pe, axis=-1, keepdims=True) + jnp.sum(po, axis=-1, keepdims=True)
        pe = pe / den
        po = po / den
        oc_s[...] = _mm(pe, vce_s[...]) + _mm(po, vco_s[...])
        jp = _iota((npast, npast), 0)
        jj = _iota((npast, npast), 1)
        groups = []
        for kvh in range(NSA_KV_HEADS):
            base = kvh * NSA_GROUP * rq
            imp_e = pe[base:base + rq]
            imp_o = po[base:base + rq]
            for g in range(1, NSA_GROUP):
                imp_e = imp_e + pe[base + g * rq:base + (g + 1) * rq]
                imp_o = imp_o + po[base + g * rq:base + (g + 1) * rq]
            imp = imp_e + imp_o
            imp_t = jnp.concatenate([imp, jnp.zeros((128 - rq, npast), F32)], axis=0).T
            sel_rows = []
            for t in range(rq):
                colv = imp_t[:, t:t + 1]
                rowv = imp[t:t + 1, :]
                beats = ((colv > rowv) | ((colv == rowv) & (jp < jj))) & (jp >= 1)
                rank = jnp.sum(jnp.where(beats, 1.0, 0.0), axis=0, keepdims=True)
                sel_rows.append(jnp.where((jrow == 0) | (rank < N_SELECT - 2), 1.0, 0.0))
            groups += [jnp.concatenate(sel_rows, axis=0)] * NSA_GROUP
        groups.append(jnp.zeros((128 - ncol, npast), F32))
        sel_s[...] = jnp.concatenate(groups, axis=0).T[:, 0:ncol].astype(BF16)

    @pl.when(ph == 1)
    def _selected():
        @pl.when(gi == 0)
        def _():
            _flash_init(m_s, l_s, acc_s)

        k_t = keys_t(0)
        v_t = keys_t(128)
        pos_row = gi * KEYS_PER_STEP + _iota((1, KEYS_PER_STEP), 1)
        s = _mm(qt_s[...], k_t) - bias(pos_row)
        sel_rows = sel_s[pl.ds(pl.multiple_of(gi * SEL_PER_STEP, SEL_PER_STEP), SEL_PER_STEP), :]
        mask = _mm_tn(sel_rows, e0_ref[...]) > 0.5
        _flash_step_r(s, mask, lambda p: _mm_nt(p, v_t), m_s, l_s, acc_s)

    @pl.when((ph == 1) & (gi == N_PAGE_GROUPS - 1))
    def _finish():
        qt = qt_s[...]
        new = zn_ref[0, 0:rq, 0:768]
        npos_row = PAST_LEN + _iota((1, rq), 1)
        mask_n = npos_row <= t_col
        s = _mm_nt(qt, new[:, 256:384]) - bias(npos_row)
        _flash_step_r(s, mask_n, lambda p: _mm(p, new[:, 384:512]), m_s, l_s, acc_s)
        l = l_s[...]
        o_sel = acc_s[...] / jnp.where(l > 0, l, 1.0)
        wpos_row = PAST_LEN - WINDOW + _iota((1, WINDOW), 1)
        mask_w = (t_col - wpos_row) <= WINDOW
        s1 = jnp.where(mask_w, _mm(qt, win_ref[0, 0:128, :]) - bias(wpos_row), NEG)
        s2 = jnp.where(mask_n, _mm_nt(qt, new[:, 512:640]) - bias(npos_row), NEG)
        m = jnp.maximum(jnp.max(s1, axis=-1, keepdims=True), jnp.max(s2, axis=-1, keepdims=True))
        p1 = jnp.where(mask_w, jnp.exp(s1 - m), 0.0)
        p2 = jnp.where(mask_n, jnp.exp(s2 - m), 0.0)
        den = jnp.sum(p1, axis=-1, keepdims=True) + jnp.sum(p2, axis=-1, keepdims=True)
        o_win = (_mm_nt(p1, win_ref[0, 128:256, :]) + _mm(p2, new[:, 640:768])) / jnp.where(den > 0, den, 1.0)
        o_cmp = oc_s[...]
        for h in range(NSA_HEADS):
            rs = slice(h * rq, (h + 1) * rq)
            ls = slice(64 * (h // NSA_GROUP), 64 * (h // NSA_GROUP) + 64)
            gates = _sigmoid(zn_ref[0, 0:rq, 768 + 3 * h:771 + 3 * h])
            o = gates[:, 0:1] * o_cmp[rs, ls] + gates[:, 1:2] * o_sel[rs, ls] + gates[:, 2:3] * o_win[rs, ls]
            o_ref[0, 0:rq, 64 * h:64 * h + 64] = o * _silu(zn_ref[0, 0:rq, 1536 + 64 * h:1600 + 64 * h])
        o_ref[0, rq:, :] = jnp.zeros((SAMPLE_TPAD - rq, 512), F32)


def _nsa_sample(page_table, zn, win_t, cache_t, *, layer, n_pool, n_tok):
    b = zn.shape[0]
    rq = SAMPLE_TPAD
    assert n_tok & (n_tok - 1) == 0 and n_tok <= rq
    ncol = NSA_HEADS * n_tok
    tok = jnp.arange(KEYS_PER_STEP)
    blk = jnp.arange(SEL_PER_STEP)
    in_blk = tok[None, :] // SEL_BLOCK == blk[:, None]
    first_half = (tok[None, :] % SEL_BLOCK) < CMP_BLOCK
    e0 = in_blk.astype(BF16)
    pk = POOL_PAGES * PAGE_SIZE
    pool = jnp.concatenate([(in_blk & first_half)[0:POOL_SEL, 0:pk], (in_blk & ~first_half)[0:POOL_SEL, 0:pk]],
                           axis=0).astype(BF16)

    def page_map(kidx):
        return lambda i, ph, gi, pt: (layer * n_pool + pt[i, gi * PAGES_PER_STEP + kidx], ph, 0)

    const = lambda a: pl.BlockSpec(a.shape, lambda i, ph, gi, pt: (0, 0))
    sq = lambda dt: pltpu.VMEM((ncol, 128), dt)
    grid_spec = pltpu.PrefetchScalarGridSpec(
        num_scalar_prefetch=1,
        grid=(b, 2, N_PAGE_GROUPS),
        in_specs=[pl.BlockSpec((1, rq, ZN_W), lambda i, ph, gi, pt: (i, 0, 0)),
                  pl.BlockSpec((1, HALF_PAGE_ROWS, WINDOW), lambda i, ph, gi, pt: (layer * b + i, 0, 0)),
                  const(pool), const(e0)]
        + [pl.BlockSpec((1, HALF_PAGE_ROWS, PAGE_SIZE), page_map(kidx)) for kidx in range(PAGES_PER_STEP)],
        out_specs=pl.BlockSpec((1, rq, 512), lambda i, ph, gi, pt: (i, 0, 0)),
        scratch_shapes=[pltpu.VMEM((N_PAST_SEL, 128), F32) for _ in range(4)]
        + [sq(BF16), pltpu.VMEM((N_PAST_SEL, ncol), BF16), sq(F32),
           pltpu.VMEM((ncol, 1), F32), pltpu.VMEM((ncol, 1), F32), sq(F32)],
    )
    return pl.pallas_call(
        functools.partial(_nsa_sample_kernel, n_tok=n_tok),
        grid_spec=grid_spec,
        out_shape=jax.ShapeDtypeStruct((b, rq, 512), F32),
        compiler_params=_cparams(("parallel", "arbitrary", "arbitrary")),
        name="nsa_sample",
    )(page_table, zn, win_t, pool, e0, *([cache_t] * PAGES_PER_STEP))


def _merge_kernel(x_ref, bg_ref, br_ref, bn_ref, mg_ref, wbr_ref, wout_ref, lng_ref, lnb_ref, o_ref):
    acc = _sigmoid(mg_ref[:, 0:1024]) * _mm(bg_ref[...], wbr_ref[0])
    acc = acc + _sigmoid(mg_ref[:, 1024:2048]) * _mm(br_ref[...], wbr_ref[1])
    acc = acc + _sigmoid(mg_ref[:, 2048:3072]) * _mm(bn_ref[...], wbr_ref[2])
    xf = DN_ALPHA * x_ref[...] + _mm(acc, wout_ref[...])
    mu = jnp.mean(xf, axis=-1, keepdims=True)
    d = xf - mu
    var = jnp.mean(d * d, axis=-1, keepdims=True)
    o_ref[...] = d * lax.rsqrt(var + LN_EPS) * lng_ref[...] + lnb_ref[...]


def _merge(x, o_gla, o_rwkv, o_nsa, zm, w_br, w_out, ln_g, ln_b, tm):
    m = x.shape[0]
    row = lambda n: pl.BlockSpec((tm, n), lambda i: (i, 0))
    return pl.pallas_call(
        _merge_kernel,
        grid=(m // tm,),
        in_specs=[row(D_MODEL), row(512), row(512), row(512), row(ZM_W),
                  pl.BlockSpec((3, 512, D_MODEL), lambda i: (0, 0, 0)),
                  pl.BlockSpec((D_MODEL, D_MODEL), lambda i: (0, 0)),
                  pl.BlockSpec((1, D_MODEL), lambda i: (0, 0)),
                  pl.BlockSpec((1, D_MODEL), lambda i: (0, 0))],
        out_specs=row(D_MODEL),
        out_shape=jax.ShapeDtypeStruct((m, D_MODEL), F32),
        compiler_params=_cparams(("parallel",)),
        name="merge",
    )(x, o_gla, o_rwkv, o_nsa, zm, w_br, w_out, ln_g, ln_b)


def _pack_weights(w_in, b_in):
    def pack(a):
        z = lambda n: jnp.zeros(a.shape[:-1] + (n,), a.dtype)
        gla = jnp.concatenate([a[..., 0:1024], a[..., 1040:1552], a[..., 1024:1040], z(ZG_W - 1552)], axis=-1)
        rwkv = a[..., 1552:3728]
        nsa = jnp.concatenate([a[..., 4240:5008], a[..., 5008:5032], z(1024 - 792), a[..., 3728:4240],
                               a[..., 5032:5544]], axis=-1)
        mg = a[..., 5544:8616]
        return gla, rwkv, nsa, mg
    ws = [w.astype(BF16) for w in pack(w_in)]
    bs = [b[:, None, :] for b in pack(b_in)]
    return ws, bs


def kernel(x_prompt, x_sample, cache_nsa_kv, state_nsa_win, state_gla, state_rwkv, state_rwkv_shift, page_table,
           w_in, b_in, gla_a_up, gla_a_bias, gla_norm, rwkv_mu, rwkv_w0, rwkv_w_up, rwkv_a0, rwkv_a_up, rwkv_k_k,
           rwkv_k_a, rwkv_r_k, rwkv_ln_w, rwkv_ln_b, w_br, w_out, ln_g, ln_b):
    bp, tp, _ = x_prompt.shape
    bs, ts, _ = x_sample.shape
    n_pool = cache_nsa_kv.shape[1]
    ws, bws = _pack_weights(w_in, b_in)
    w_br_b = w_br.astype(BF16)
    w_out_b = w_out.astype(BF16)
    seg = (jnp.arange(RWKV_W)[:, None] // RWKV_HD == jnp.arange(RWKV_W)[None, :] // RWKV_HD).astype(BF16)
    cache_t = jnp.transpose(cache_nsa_kv, (0, 1, 3, 4, 5, 2)).reshape(DEPTH * n_pool, 8 * NSA_HD, PAGE_SIZE)
    win_t = jnp.transpose(state_nsa_win, (0, 1, 3, 4, 5, 2)).reshape(DEPTH * bs, HALF_PAGE_ROWS, WINDOW)
    row2 = lambda a: a.reshape(DEPTH, 1, -1)
    gla_a_bias2, gla_norm2 = row2(gla_a_bias), row2(gla_norm)
    r_par = [row2(rwkv_mu), row2(rwkv_w0), rwkv_w_up, row2(rwkv_a0), rwkv_a_up, row2(rwkv_k_k), row2(rwkv_k_a),
             row2(rwkv_r_k), row2(rwkv_ln_w), row2(rwkv_ln_b)]
    ln_g2, ln_b2 = row2(ln_g), row2(ln_b)

    xp = x_prompt.reshape(bp * tp, D_MODEL)
    xs = jnp.pad(x_sample, ((0, 0), (0, SAMPLE_TPAD - ts), (0, 0))).reshape(bs * SAMPLE_TPAD, D_MODEL)
    zeros_gla = jnp.zeros((bp, GLA_HEADS, GLA_DK, GLA_DV), F32)
    zeros_rwkv = jnp.zeros((bp, RWKV_HEADS, RWKV_HD, RWKV_HD), F32)
    zeros_shift = jnp.zeros((bp, 1, RWKV_IN), F32)

    outs = {k: [] for k in ("kv_p", "kv_s", "win_p", "win_s", "gla_p", "gla_s", "rwkv_p", "rwkv_s", "sh_p", "sh_s")}
    for l in range(DEPTH):
        rp = [p[l] for p in r_par]
        zg, zr, zn, zm = (_proj(xp, ws[i][l], bws[i][l], 512) for i in range(4))
        zn3 = zn.reshape(bp, tp, ZN_W)
        o_gla, gla_st = _gla(zg.reshape(bp, tp, ZG_W), zeros_gla, gla_a_up[l], gla_a_bias2[l], gla_norm2[l],
                             tb_rows=512, chunk=GLA_CHUNK, t_valid=None)
        zr3 = zr.reshape(bp, tp, RWKV_IN)
        o_rwkv, rwkv_st = _rwkv(zr3, zeros_rwkv, zeros_shift, seg, rp, tb_rows=512, chunk=RWKV_CHUNK, t_valid=None)
        o_nsa, kvt = _nsa_prompt(zn3, tq=256)
        xp = _merge(xp, o_gla.reshape(bp * tp, 512), o_rwkv.reshape(bp * tp, 512), o_nsa.reshape(bp * tp, 512), zm,
                    w_br_b[l], w_out_b[l], ln_g2[l], ln_b2[l], 512)
        outs["kv_p"].append(jnp.transpose(kvt[:, 0:512, :].reshape(bp, 4, NSA_KV_HEADS, NSA_HD, tp), (0, 4, 1, 2, 3)))
        outs["win_p"].append(jnp.transpose(kvt[:, 512:768, tp - WINDOW:].reshape(bp, 2, NSA_KV_HEADS, NSA_HD, WINDOW),
                                           (0, 4, 1, 2, 3)))
        outs["gla_p"].append(gla_st)
        outs["rwkv_p"].append(rwkv_st)
        outs["sh_p"].append(zr3[:, tp - 1, :])
        rows_s = bs * SAMPLE_TPAD
        zg, zr, zn, zm = (_proj(xs, ws[i][l], bws[i][l], rows_s) for i in range(4))
        zn3 = zn.reshape(bs, SAMPLE_TPAD, ZN_W)
        o_gla, gla_st = _gla(zg.reshape(bs, SAMPLE_TPAD, ZG_W), state_gla[l], gla_a_up[l], gla_a_bias2[l],
                             gla_norm2[l], tb_rows=SAMPLE_TPAD, chunk=SAMPLE_TPAD, t_valid=ts)
        zr3 = zr.reshape(bs, SAMPLE_TPAD, RWKV_IN)
        o_rwkv, rwkv_st = _rwkv(zr3, state_rwkv[l], state_rwkv_shift[l][:, None, :], seg, rp,
                                tb_rows=SAMPLE_TPAD, chunk=SAMPLE_TPAD, t_valid=ts)
        o_nsa = _nsa_sample(page_table, zn3, win_t, cache_t, layer=l, n_pool=n_pool, n_tok=ts)
        xs = _merge(xs, o_gla.reshape(rows_s, 512), o_rwkv.reshape(rows_s, 512), o_nsa.reshape(rows_s, 512), zm,
                    w_br_b[l], w_out_b[l], ln_g2[l], ln_b2[l], rows_s)
        outs["kv_s"].append(zn3[:, 0:ts, 0:512].reshape(bs, ts, 4, NSA_KV_HEADS, NSA_HD))
        new_win = zn3[:, 0:ts, 512:768].reshape(bs, ts, 2, NSA_KV_HEADS, NSA_HD)
        outs["win_s"].append(jnp.concatenate([state_nsa_win[l][:, ts:], new_win], axis=1))
        outs["gla_s"].append(gla_st)
        outs["rwkv_s"].append(rwkv_st)
        outs["sh_s"].append(zr3[:, ts - 1, :])

    st = lambda k: jnp.stack(outs[k])
    y_prompt = xp.reshape(bp, tp, D_MODEL)
    y_sample = xs.reshape(bs, SAMPLE_TPAD, D_MODEL)[:, 0:ts]
    return (y_prompt, y_sample, st("kv_p"), st("kv_s"), st("win_p"), st("win_s"), st("gla_p"), st("gla_s"),
            st("rwkv_p"), st("rwkv_s"), st("sh_p"), st("sh_s"))
```

```python
import functools

import jax
import jax.numpy as jnp
from jax import lax
from jax.experimental import pallas as pl
from jax.experimental.pallas import tpu as pltpu

F32 = jnp.float32
BF16 = jnp.bfloat16

D_MODEL = 1024
DEPTH = 2
PAST_LEN = 16384
PAGE_SIZE = 128
N_PAGES = PAST_LEN // PAGE_SIZE

GLA_HEADS, GLA_DK, GLA_DV = 4, 64, 128
GLA_K, GLA_V, GLA_LORA = 256, 512, 16
GLA_GATE_NORM = 16.0
GLA_CHUNK = 64
GLA_SUB = 16

RWKV_HEADS, RWKV_HD, RWKV_W = 8, 64, 512
RWKV_IN = 2176
RWKV_LN_EPS = 64e-5
RWKV_CHUNK = 64

NSA_HEADS, NSA_KV_HEADS, NSA_GROUP, NSA_HD = 8, 2, 4, 64
CMP_BLOCK, SEL_BLOCK, N_SELECT, WINDOW = 32, 64, 16, 512
FORCE_SCORE = 1e9
NEG = -1e30

DN_ALPHA = (2 * DEPTH) ** 0.25
LN_EPS = 1e-5
NORM_EPS = 1e-6

ZG_W = 1664
ZN_W = 2048
ZM_W = 3072
SAMPLE_TPAD = 16
VMEM_LIMIT = 56 * 1024 * 1024
BRANCH_DTYPE = BF16
GROUP_DTYPES = (F32, F32, F32, BF16)


def _mm(a, b):
    return jnp.dot(a.astype(BF16), b.astype(BF16), preferred_element_type=F32)


def _mm_nt(a, b):
    return lax.dot_general(a.astype(BF16), b.astype(BF16), (((1,), (1,)), ((), ())), preferred_element_type=F32)


def _mm_tn(a, b):
    return lax.dot_general(a.astype(BF16), b.astype(BF16), (((0,), (0,)), ((), ())), preferred_element_type=F32)


def _mm_exact(a, b):
    return jnp.dot(a, b, preferred_element_type=F32, precision=lax.Precision.HIGHEST)


def _split2(a):
    hi = a.astype(BF16)
    return hi, (a - hi.astype(F32)).astype(BF16)


def _mm_split(a, b01):
    hi, lo = _split2(a)
    return jnp.dot(hi, b01, preferred_element_type=F32) + jnp.dot(lo, b01, preferred_element_type=F32)


def _softplus(x):
    return jnp.maximum(x, 0.0) + jnp.log(1.0 + jnp.exp(-jnp.abs(x)))


def _sigmoid(x):
    return 0.5 * jnp.tanh(0.5 * x) + 0.5


def _silu(x):
    return x * _sigmoid(x)


def _iota(shape, dim):
    return lax.broadcasted_iota(jnp.int32, shape, dim)


def _cparams(sem):
    return pltpu.CompilerParams(dimension_semantics=sem, vmem_limit_bytes=VMEM_LIMIT)


def _proj_kernel(x_ref, w_ref, b_ref, o_ref):
    o_ref[...] = (_mm(x_ref[...], w_ref[...]) + b_ref[...]).astype(o_ref.dtype)


def _proj(x, w, b, tm, out_dtype=F32):
    m, k = x.shape
    n = w.shape[1]
    return pl.pallas_call(
        _proj_kernel,
        grid=(m // tm,),
        in_specs=[pl.BlockSpec((tm, k), lambda i: (i, 0)),
                  pl.BlockSpec((k, n), lambda i: (0, 0)),
                  pl.BlockSpec((1, n), lambda i: (0, 0))],
        out_specs=pl.BlockSpec((tm, n), lambda i: (i, 0)),
        out_shape=jax.ShapeDtypeStruct((m, n), out_dtype),
        compiler_params=_cparams(("parallel",)),
        name="proj",
    )(x, w, b)


def _gla_kernel(zg_ref, s0_ref, aup_ref, abias_ref, norm_ref, o_ref, sout_ref, st_scr, *, tb_rows, chunk, t_valid):
    tb = pl.program_id(1)
    c_rows = chunk
    sub = min(GLA_SUB, c_rows)
    nsub = c_rows // sub

    @pl.when(tb == 0)
    def _():
        for h in range(GLA_HEADS):
            st_scr[h] = s0_ref[0, h].T

    tri = (_iota((c_rows, c_rows), 1) <= _iota((c_rows, c_rows), 0)).astype(F32)
    ones_red = jnp.ones((GLA_DK, 128), BF16)
    lane_s = _iota((sub, 128), 1)
    row_s = _iota((sub, 128), 0)
    col_c = _iota((sub, c_rows), 1)

    def chunk_body(c, carry):
        r0 = pl.multiple_of(c * c_rows, c_rows)
        z = zg_ref[0, pl.ds(r0, c_rows), :]
        q = z[:, 0:256] * (GLA_DK ** -0.5)
        k = z[:, 256:512]
        v = z[:, 512:1024]
        g = z[:, 1024:1536]
        ga = z[:, 1536:1552]
        la = -_softplus(-(_mm(ga, aup_ref[...]) + abias_ref[...])) * (1.0 / GLA_GATE_NORM)
        if t_valid is not None:
            ok = (tb * tb_rows + r0 + _iota((c_rows, 1), 0)) < t_valid
            la = jnp.where(ok, la, 0.0)
            k = jnp.where(ok, k, 0.0)
            v = jnp.where(ok, v, 0.0)
        cum = _mm_exact(tri, la)
        heads = range(GLA_HEADS)
        pairs = [(h, blk) for h in heads for blk in range(nsub)]
        qh = [q[:, 64 * h:64 * h + 64] for h in heads]
        kh = [k[:, 64 * h:64 * h + 64] for h in heads]
        ch = [cum[:, 64 * h:64 * h + 64] for h in heads]
        vh = [v[:, 128 * h:128 * h + 128] for h in heads]
        st = [st_scr[h] for h in heads]
        o_in = [_mm_nt(qh[h] * jnp.exp(ch[h]), st[h]) for h in heads]
        red, off = {}, {}
        for h, blk in pairs:
            sl = slice(blk * sub, (blk + 1) * sub)
            q_i, k_i, c_i = qh[h][sl], kh[h][sl], ch[h][sl]
            es = [q_i * k_i[j:j + 1] * jnp.exp(jnp.minimum(c_i - c_i[j:j + 1], 0.0)) for j in range(sub)]
            red[h, blk] = _mm(jnp.concatenate(es, axis=0), ones_red)
            if blk > 0:
                b_i = ch[h][blk * sub - 1:blk * sub]
                q_t = q_i * jnp.exp(c_i - b_i)
                k_t = kh[h] * jnp.exp(jnp.minimum(b_i - ch[h], 0.0))
                off[h, blk] = _mm_nt(q_t, k_t)
        att = []
        for h in heads:
            att_rows = []
            for blk in range(nsub):
                a_i = jnp.zeros((sub, 128), F32)
                for j in range(sub):
                    a_i = a_i + jnp.where((lane_s == blk * sub + j) & (row_s >= j),
                                          red[h, blk][j * sub:(j + 1) * sub], 0.0)
                a_i = a_i[:, 0:c_rows]
                if blk > 0:
                    a_i = a_i + jnp.where(col_c < blk * sub, off[h, blk], 0.0)
                att_rows.append(a_i)
            att.append(att_rows[0] if nsub == 1 else jnp.concatenate(att_rows, axis=0))
        o = [o_in[h] + _mm(att[h], vh[h]) for h in heads]
        last = [ch[h][c_rows - 1:c_rows] for h in heads]
        st_new = [st[h] * jnp.exp(last[h]) + _mm_tn(vh[h], kh[h] * jnp.exp(last[h] - ch[h])) for h in heads]
        for h in heads:
            st_scr[h] = st_new[h]
            oh = o[h] * lax.rsqrt(jnp.mean(o[h] * o[h], axis=-1, keepdims=True) + NORM_EPS)
            oh = oh * norm_ref[:, 128 * h:128 * h + 128] * _silu(g[:, 128 * h:128 * h + 128])
            o_ref[0, pl.ds(r0, c_rows), 128 * h:128 * h + 128] = oh.astype(o_ref.dtype)
        return carry

    lax.fori_loop(0, tb_rows // c_rows, chunk_body, 0, unroll=min(2, tb_rows // c_rows))

    @pl.when(tb == pl.num_programs(1) - 1)
    def _():
        for h in range(GLA_HEADS):
            sout_ref[0, h] = st_scr[h].T


def _gla(zg, s0, a_up, a_bias, norm_g, *, tb_rows, chunk, t_valid):
    b, t, _ = zg.shape
    kern = functools.partial(_gla_kernel, tb_rows=tb_rows, chunk=chunk, t_valid=t_valid)
    return pl.pallas_call(
        kern,
        grid=(b, t // tb_rows),
        in_specs=[pl.BlockSpec((1, tb_rows, ZG_W), lambda i, j: (i, j, 0)),
                  pl.BlockSpec((1, GLA_HEADS, GLA_DK, GLA_DV), lambda i, j: (i, 0, 0, 0)),
                  pl.BlockSpec((GLA_LORA, GLA_K), lambda i, j: (0, 0)),
                  pl.BlockSpec((1, GLA_K), lambda i, j: (0, 0)),
                  pl.BlockSpec((1, GLA_V), lambda i, j: (0, 0))],
        out_specs=[pl.BlockSpec((1, tb_rows, GLA_V), lambda i, j: (i, j, 0)),
                   pl.BlockSpec((1, GLA_HEADS, GLA_DK, GLA_DV), lambda i, j: (i, 0, 0, 0))],
        out_shape=[jax.ShapeDtypeStruct((b, t, GLA_V), BRANCH_DTYPE),
                   jax.ShapeDtypeStruct((b, GLA_HEADS, GLA_DK, GLA_DV), F32)],
        scratch_shapes=[pltpu.VMEM((GLA_HEADS, GLA_DV, GLA_DK), F32)],
        compiler_params=_cparams(("parallel", "arbitrary")),
        name="gla",
    )(zg, s0, a_up, a_bias, norm_g)


def _rwkv_kernel(zr_ref, s0_ref, sh0_ref, seg_ref, mu_ref, w0_ref, wup_ref, a0_ref, aup_ref, kk_ref, ka_ref, rk_ref,
                 lnw_ref, lnb_ref, y_ref, sout_ref,
                 s_scr, prev_scr, lw_s, kk_s, kka_s, k2_s, r_s, v_s, y_s, *, tb_rows, chunk, t_valid):
    tb = pl.program_id(1)
    c_rows = chunk
    nh = RWKV_HEADS

    @pl.when(tb == 0)
    def _():
        s_scr[...] = s0_ref[0]
        prev_scr[...] = sh0_ref[0]

    z = zr_ref[0]
    rows = _iota((tb_rows, 1), 0)
    zp = pltpu.roll(z, 1, axis=0)
    zp = jnp.concatenate([jnp.where(rows[0:8] == 0, prev_scr[...], zp[0:8]), zp[8:]], axis=0)
    prev_scr[...] = z[tb_rows - 1:tb_rows]
    zs = z + (zp - z) * mu_ref[...]
    r = zs[:, 0:512]
    k = zs[:, 512:1024]
    v = zs[:, 1024:1536]
    wl = zs[:, 1536:1600]
    al = zs[:, 1600:1664]
    w = -_softplus(-(w0_ref[...] + _mm(jnp.tanh(wl), wup_ref[...]))) - 0.5
    lw = -jnp.exp(w)
    a = _sigmoid(a0_ref[...] + _mm(al, aup_ref[...]))
    kk = k * kk_ref[...]
    kk = kk * lax.rsqrt(_mm_split(kk * kk, seg_ref[...]) + NORM_EPS)
    k2 = k * (1.0 + (a - 1.0) * ka_ref[...])
    kka = kk * a
    if t_valid is not None:
        ok = (tb * tb_rows + rows) < t_valid
        lw = jnp.where(ok, lw, 0.0)
        kka = jnp.where(ok, kka, 0.0)
        k2 = jnp.where(ok, k2, 0.0)
    lw_s[...] = lw
    kk_s[...] = kk
    kka_s[...] = kka
    k2_s[...] = k2
    r_s[...] = r
    v_s[...] = v

    ri = _iota((c_rows, c_rows), 0)
    ci = _iota((c_rows, c_rows), 1)
    tri = (ci <= ri).astype(F32)
    strict = ci < ri
    incl = ci <= ri
    n_dbl = max(1, (c_rows - 1).bit_length())

    def chunk_body(c, carry):
        r0 = pl.multiple_of(c * c_rows, c_rows)
        ds = pl.ds(r0, c_rows)
        lwc = lw_s[ds, :]
        cl = _mm_exact(tri, lwc)
        e_inv = jnp.exp(-cl)
        e_fwd = jnp.exp(cl)
        e_prev = jnp.exp(cl - lwc)
        e_end = jnp.exp(cl[c_rows - 1:c_rows] - cl)
        g_end = jnp.exp(cl[c_rows - 1:c_rows])
        kkc, kkac, k2c, rc, vc = kk_s[ds, :], kka_s[ds, :], k2_s[ds, :], r_s[ds, :], v_s[ds, :]
        heads = range(nh)
        hsl = [slice(64 * h, 64 * h + 64) for h in heads]
        a_t = [-kkac[:, hs] * e_inv[:, hs] for hs in hsl]
        b_t = [kkc[:, hs] * e_prev[:, hs] for hs in hsl]
        k_t = [k2c[:, hs] * e_inv[:, hs] for hs in hsl]
        r_t = [rc[:, hs] * e_fwd[:, hs] for hs in hsl]
        vh = [vc[:, hs] for hs in hsl]
        cc = c_rows
        ak = [jnp.concatenate([a_t[h], k_t[h]], axis=0).astype(BF16) for h in heads]
        bra = [_mm_nt(jnp.concatenate([b_t[h], r_t[h]], axis=0), ak[h]) for h in heads]
        s0 = [s_scr[h] for h in heads]
        l_k = [jnp.where(strict, bra[h][0:cc, cc:2 * cc], 0.0) for h in heads]
        m_k = [jnp.where(incl, bra[h][cc:2 * cc, cc:2 * cc], 0.0) for h in heads]
        lmv = [_mm(jnp.concatenate([l_k[h], m_k[h]], axis=0), vh[h]) for h in heads]
        x = [jnp.concatenate([b_t[h], lmv[h][0:cc]], axis=1) for h in heads]
        lp = [jnp.where(strict, bra[h][0:cc, 0:cc], 0.0).astype(BF16) for h in heads]
        for step in range(n_dbl):
            if step + 1 < n_dbl:
                new = [_mm(lp[h], jnp.concatenate([x[h].astype(BF16), lp[h]], axis=1)) for h in heads]
                x = [x[h] + new[h][:, 0:128] for h in heads]
                lp = [new[h][:, 128:128 + cc].astype(BF16) for h in heads]
            else:
                x = [x[h] + _mm(lp[h], x[h]) for h in heads]
        prs = [_mm_nt(jnp.concatenate([x[h][:, 0:64], r_t[h]], axis=0), s0[h]) for h in heads]
        u = [prs[h][0:cc] + x[h][:, 64:128] for h in heads]
        m_a = [jnp.where(incl, bra[h][cc:2 * cc, 0:cc], 0.0) for h in heads]
        y = [prs[h][cc:2 * cc] + _mm(m_a[h], u[h]) + lmv[h][cc:2 * cc] for h in heads]
        akg = [jnp.concatenate([-kkac[:, hs] * e_end[:, hs], k2c[:, hs] * e_end[:, hs]], axis=0) for hs in hsl]
        s_new = [s0[h] * g_end[:, hsl[h]] + _mm_tn(jnp.concatenate([u[h], vh[h]], axis=0), akg[h]) for h in heads]
        for h in heads:
            s_scr[h] = s_new[h]
            y_s[ds, hsl[h]] = y[h]
        return carry

    lax.fori_loop(0, tb_rows // c_rows, chunk_body, 0, unroll=min(2, tb_rows // c_rows))

    y = y_s[...]
    seg = seg_ref[...]
    mean = _mm_split(y, seg) * (1.0 / RWKV_HD)
    d = y - mean
    var = _mm_split(d * d, seg) * (1.0 / RWKV_HD)
    yn = d * lax.rsqrt(var + RWKV_LN_EPS) * lnw_ref[...] + lnb_ref[...]
    bonus = _mm_split(r * k2 * rk_ref[...], seg) * v
    y_ref[0] = ((yn + bonus) * _silu(zs[:, 1664:2176])).astype(y_ref.dtype)

    @pl.when(tb == pl.num_programs(1) - 1)
    def _():
        sout_ref[0] = s_scr[...]


def _rwkv(zr, s0, sh0, seg, params, *, tb_rows, chunk, t_valid):
    b, t, _ = zr.shape
    kern = functools.partial(_rwkv_kernel, tb_rows=tb_rows, chunk=chunk, t_valid=t_valid)
    full = lambda shp: pl.BlockSpec(shp, lambda i, j: (0,) * len(shp))
    mu, w0, w_up, a0, a_up, k_k, k_a, r_k, ln_w, ln_b = params
    return pl.pallas_call(
        kern,
        grid=(b, t // tb_rows),
        in_specs=[pl.BlockSpec((1, tb_rows, RWKV_IN), lambda i, j: (i, j, 0)),
                  pl.BlockSpec((1, RWKV_HEADS, RWKV_HD, RWKV_HD), lambda i, j: (i, 0, 0, 0)),
                  pl.BlockSpec((1, 1, RWKV_IN), lambda i, j: (i, 0, 0)),
                  full((RWKV_W, RWKV_W)), full((1, RWKV_IN)), full((1, RWKV_W)), full((64, RWKV_W)),
                  full((1, RWKV_W)), full((64, RWKV_W)), full((1, RWKV_W)), full((1, RWKV_W)), full((1, RWKV_W)),
                  full((1, RWKV_W)), full((1, RWKV_W))],
        out_specs=[pl.BlockSpec((1, tb_rows, RWKV_W), lambda i, j: (i, j, 0)),
                   pl.BlockSpec((1, RWKV_HEADS, RWKV_HD, RWKV_HD), lambda i, j: (i, 0, 0, 0))],
        out_shape=[jax.ShapeDtypeStruct((b, t, RWKV_W), BRANCH_DTYPE),
                   jax.ShapeDtypeStruct((b, RWKV_HEADS, RWKV_HD, RWKV_HD), F32)],
        scratch_shapes=[pltpu.VMEM((RWKV_HEADS, RWKV_HD, RWKV_HD), F32), pltpu.VMEM((1, RWKV_IN), F32)]
        + [pltpu.VMEM((tb_rows, RWKV_W), F32) for _ in range(7)],
        compiler_params=_cparams(("parallel", "arbitrary")),
        name="rwkv",
    )(zr, s0, sh0, seg, mu, w0, w_up, a0, a_up, k_k, k_a, r_k, ln_w, ln_b)


def _slope(h):
    return 2.0 ** (-8.0 * (h + 1) / NSA_HEADS)


def _flash_init(m_ref, l_ref, acc_ref):
    m_ref[...] = jnp.full(m_ref.shape, NEG, F32)
    l_ref[...] = jnp.zeros(l_ref.shape, F32)
    acc_ref[...] = jnp.zeros(acc_ref.shape, F32)


def _flash_step_t(s_t, v, m_ref, l_ref, acc_ref):
    m_old = m_ref[...]
    m_new = jnp.maximum(m_old, jnp.max(s_t, axis=0, keepdims=True))
    p = jnp.exp(s_t - m_new)
    alpha = jnp.exp(m_old - m_new)
    l_ref[...] = alpha * l_ref[...] + jnp.sum(p, axis=0, keepdims=True)
    acc_ref[...] = alpha * acc_ref[...] + _mm_tn(v, p)
    m_ref[...] = m_new


def _flash_steps_t(s_list, v_list, m_ref, l_ref, acc_ref):
    ks = range(len(s_list))
    m_old = [m_ref[k] for k in ks]
    m_new = [jnp.maximum(m_old[k], jnp.max(s_list[k], axis=0, keepdims=True)) for k in ks]
    p = [jnp.exp(s_list[k] - m_new[k]) for k in ks]
    alpha = [jnp.exp(m_old[k] - m_new[k]) for k in ks]
    pv = [_mm_tn(v_list[k], p[k]) for k in ks]
    for k in ks:
        l_ref[k] = alpha[k] * l_ref[k] + jnp.sum(p[k], axis=0, keepdims=True)
        acc_ref[k] = alpha[k] * acc_ref[k] + pv[k]
        m_ref[k] = m_new[k]


def _flash_result(l_ref, acc_ref, k):
    l = l_ref[k]
    return acc_ref[k] / jnp.where(l > 0, l, 1.0)


SEL_LANE0 = 66
BLOCK_PENALTY = -(2.0 ** 100)


def _nsa_prompt_kernel(q_ref, kv_ref, bg_ref, g_ref, o_ref, kc_s, vc_s, ks_s, vs_s, kw_s, vw_s, qa_s, qs_s, oc_s,
                       m_s, l_s, acc_s, m2_s, l2_s, acc2_s, *, t_len, tq):
    i = pl.program_id(1)
    nb = t_len // SEL_BLOCK
    tk = tq
    ncol = NSA_GROUP * tq
    win_tiles = WINDOW // tk

    def aug_lanes(lane, pos):
        return jnp.where(lane == 64, lax.shift_right_logical(pos, 7).astype(F32),
                         jnp.where(lane == 65, (pos & 127).astype(F32), 0.0))

    def head_lanes(x, kvh):
        return x if kvh == 0 else pltpu.roll(x, 64, axis=1)

    @pl.when(i == 0)
    def _():
        lane = _iota((t_len, 128), 1)
        pos = _iota((t_len, 128), 0)
        aug = aug_lanes(lane, pos)
        aug_sel = jnp.where(lane - SEL_LANE0 == lax.shift_right_logical(pos, 6), 1.0, aug)
        for kcol, vcol, k_dst, v_dst, k_aug in ((256, 384, ks_s, vs_s, aug_sel), (512, 640, kw_s, vw_s, aug)):
            kf = kv_ref[0, :, kcol:kcol + 128]
            vf = kv_ref[0, :, vcol:vcol + 128]
            for kvh in range(NSA_KV_HEADS):
                k_dst[kvh] = jnp.where(lane < 64, head_lanes(kf, kvh), k_aug).astype(BF16)
                v_dst[kvh] = jnp.where(lane < 64, head_lanes(vf, kvh), 0.0).astype(BF16)
        kcm = kv_ref[0, :, 0:128].reshape(nb, SEL_BLOCK, 128)
        vcm = kv_ref[0, :, 128:256].reshape(nb, SEL_BLOCK, 128)
        inv = 1.0 / CMP_BLOCK
        kc = jnp.concatenate([jnp.sum(kcm[:, 0:CMP_BLOCK], axis=1) * inv,
                              jnp.sum(kcm[:, CMP_BLOCK:SEL_BLOCK], axis=1) * inv], axis=0)
        vc = jnp.concatenate([jnp.sum(vcm[:, 0:CMP_BLOCK], axis=1) * inv,
                              jnp.sum(vcm[:, CMP_BLOCK:SEL_BLOCK], axis=1) * inv], axis=0)
        lane_c = _iota((2 * nb, 128), 1)
        r_c = _iota((2 * nb, 128), 0)
        cend = jnp.where(r_c < nb, SEL_BLOCK * r_c + (CMP_BLOCK - 1), SEL_BLOCK * (r_c - nb) + (SEL_BLOCK - 1))
        aug_c = aug_lanes(lane_c, cend)
        for kvh in range(NSA_KV_HEADS):
            kc_s[kvh] = jnp.where(lane_c < 64, head_lanes(kc, kvh), aug_c).astype(BF16)
            vc_s[kvh] = jnp.where(lane_c < 64, head_lanes(vc, kvh), 0.0).astype(BF16)

    lane_q = _iota((tq, 128), 1)
    for h in range(NSA_HEADS):
        qb = q_ref[0, :, 128 * (h // 2):128 * (h // 2) + 128]
        if h % 2:
            qb = pltpu.roll(qb, 64, axis=1)
        sl = _slope(h)
        qa = jnp.where(lane_q < 64, qb * (NSA_HD ** -0.5),
                       jnp.where(lane_q == 64, 128.0 * sl, jnp.where(lane_q == 65, sl, 0.0)))
        qa_s[h // NSA_GROUP, (h % NSA_GROUP) * tq:(h % NSA_GROUP + 1) * tq, :] = qa.astype(BF16)

    t_row = i * tq + (_iota((1, ncol), 1) & (tq - 1))
    t_row1 = i * tq + _iota((1, tq), 1)
    r_c1 = _iota((2 * nb, 1), 0)
    cend_col = jnp.where(r_c1 < nb, SEL_BLOCK * r_c1 + (CMP_BLOCK - 1), SEL_BLOCK * (r_c1 - nb) + (SEL_BLOCK - 1))
    mask_c = cend_col <= t_row
    jrow = _iota((nb, 1), 0)
    tblk = lax.shift_right_logical(t_row1, 6)
    forced = (jrow == tblk) | (jrow == 0)
    valid = jrow <= tblk
    bg_t = bg_ref[0].T

    for kvh in range(NSA_KV_HEADS):
        qa = qa_s[kvh]
        s_c = jnp.where(mask_c, _mm_nt(kc_s[kvh], qa), NEG)
        m = jnp.max(s_c, axis=0, keepdims=True)
        p = jnp.where(mask_c, jnp.exp(s_c - m), 0.0)
        den = jnp.sum(p, axis=0, keepdims=True)
        p = p / jnp.where(den > 0, den, 1.0)
        oc_s[kvh] = _mm_tn(vc_s[kvh], p)
        imp_e = p[0:nb, 0:tq]
        imp_o = p[nb:2 * nb, 0:tq]
        for g in range(1, NSA_GROUP):
            imp_e = imp_e + p[0:nb, g * tq:(g + 1) * tq]
            imp_o = imp_o + p[nb:2 * nb, g * tq:(g + 1) * tq]
        score = jnp.where(forced, FORCE_SCORE, jnp.where(valid, imp_e + imp_o, -FORCE_SCORE))
        rank = jnp.zeros((nb, tq), F32)
        for jp in range(nb):
            rj = score[jp:jp + 1, :]
            rank = rank + jnp.where((rj > score) | ((rj == score) & (jp < jrow)), 1.0, 0.0)
        sel = jnp.where((rank < N_SELECT) & valid, 1.0, 0.0)
        sel_q = jnp.concatenate([sel, jnp.zeros((128 - nb, tq), F32)], axis=0).T
        sel_q = pltpu.roll(sel_q, SEL_LANE0, axis=1)
        pen = jnp.where((lane_q >= SEL_LANE0) & (lane_q < SEL_LANE0 + nb) & (sel_q < 0.5), BLOCK_PENALTY, 0.0)
        for g in range(NSA_GROUP):
            cs = slice(g * tq, (g + 1) * tq)
            qs_s[kvh, cs, :] = (qa_s[kvh, cs, :].astype(F32) + pen).astype(BF16)

    _flash_init(m_s, l_s, acc_s)
    _flash_init(m2_s, l2_s, acc2_s)
    kvs = range(NSA_KV_HEADS)

    def sel_step(kt, diagonal):
        k0 = pl.multiple_of(kt * tk, tk)
        s_t = [_mm_nt(ks_s[kvh, pl.ds(k0, tk), :], qs_s[kvh]) for kvh in kvs]
        if diagonal:
            causal = kt * tk + _iota((tk, 1), 0) <= t_row
            s_t = [jnp.where(causal, s, NEG) for s in s_t]
        _flash_steps_t(s_t, [vs_s[kvh, pl.ds(k0, tk), :] for kvh in kvs], m_s, l_s, acc_s)

    def sel_body(kt, carry):
        sel_step(kt, False)
        return carry

    lax.fori_loop(0, i, sel_body, 0)
    sel_step(i, True)

    def win_step(kt, back):
        k0 = pl.multiple_of(kt * tk, tk)
        s_t = [_mm_nt(kw_s[kvh, pl.ds(k0, tk), :], qa_s[kvh]) for kvh in kvs]
        dist = t_row - (kt * tk + _iota((tk, 1), 0))
        if back == 0:
            s_t = [jnp.where(dist >= 0, s, NEG) for s in s_t]
        elif back == win_tiles:
            s_t = [jnp.where(dist <= WINDOW, s, NEG) for s in s_t]
        _flash_steps_t(s_t, [vw_s[kvh, pl.ds(k0, tk), :] for kvh in kvs], m2_s, l2_s, acc2_s)

    for back in range(win_tiles, 0, -1):
        @pl.when(i >= back)
        def _(back=back):
            win_step(i - back, back)
    win_step(i, 0)

    for kvh in kvs:
        o_sel = _flash_result(l_s, acc_s, kvh)
        o_win = _flash_result(l2_s, acc2_s, kvh)
        o_cmp = oc_s[kvh]
        for g in range(NSA_GROUP):
            h = NSA_GROUP * kvh + g
            cs = slice(g * tq, (g + 1) * tq)
            gates = _sigmoid(bg_t[3 * h:3 * h + 3, :])
            comb = gates[0:1] * o_cmp[:, cs] + gates[1:2] * o_sel[:, cs] + gates[2:3] * o_win[:, cs]
            o_h = comb.T[:, 0:64] * _silu(g_ref[0, :, 64 * h:64 * h + 64])
            o_ref[0, :, 64 * h:64 * h + 64] = o_h.astype(o_ref.dtype)


def _nsa_prompt(zn, *, tq):
    b, t, _ = zn.shape
    nb = t // SEL_BLOCK
    ncol = NSA_GROUP * tq
    kern = functools.partial(_nsa_prompt_kernel, t_len=t, tq=tq)
    kv_f32 = lambda n: pltpu.VMEM((NSA_KV_HEADS, n, ncol), F32)
    stat = lambda: [kv_f32(1), kv_f32(1), kv_f32(128)]
    kvbuf = lambda n: pltpu.VMEM((NSA_KV_HEADS, n, 128), BF16)
    return pl.pallas_call(
        kern,
        grid=(b, t // tq),
        in_specs=[pl.BlockSpec((1, tq, 512), lambda i, j: (i, j, 2)),
                  pl.BlockSpec((1, t, 768), lambda i, j: (i, 0, 0)),
                  pl.BlockSpec((1, tq, 256), lambda i, j: (i, j, 3)),
                  pl.BlockSpec((1, tq, 512), lambda i, j: (i, j, 3))],
        out_specs=pl.BlockSpec((1, tq, 512), lambda i, j: (i, j, 0)),
        out_shape=jax.ShapeDtypeStruct((b, t, 512), BRANCH_DTYPE),
        scratch_shapes=[kvbuf(2 * nb), kvbuf(2 * nb), kvbuf(t), kvbuf(t), kvbuf(t), kvbuf(t), kvbuf(ncol),
                        kvbuf(ncol), kv_f32(128)] + stat() + stat(),
        compiler_params=_cparams(("parallel", "arbitrary")),
        name="nsa_prompt",
    )(zn, zn, zn, zn)


PAGES_PER_STEP = 64
N_PAGE_GROUPS = N_PAGES // PAGES_PER_STEP
KEYS_PER_STEP = PAGES_PER_STEP * PAGE_SIZE
SEL_PER_STEP = KEYS_PER_STEP // SEL_BLOCK
POOL_PAGES = 8
POOL_SEL = POOL_PAGES * PAGE_SIZE // SEL_BLOCK
N_PAST_SEL = PAST_LEN // SEL_BLOCK
HALF_PAGE_ROWS = 2 * NSA_KV_HEADS * NSA_HD


def _flash_step_r(s, mask, pv, m_ref, l_ref, acc_ref):
    s = jnp.where(mask, s, NEG)
    m_old = m_ref[...]
    m_new = jnp.maximum(m_old, jnp.max(s, axis=-1, keepdims=True))
    p = jnp.exp(s - m_new)
    alpha = jnp.exp(m_old - m_new)
    l_ref[...] = alpha * l_ref[...] + jnp.sum(p, axis=-1, keepdims=True)
    acc_ref[...] = alpha * acc_ref[...] + pv(p)
    m_ref[...] = m_new


def _nsa_sample_kernel(pt_ref, zn_ref, win_ref, pool_ref, e0_ref, *rest, n_tok):
    pages = rest[:PAGES_PER_STEP]
    o_ref = rest[PAGES_PER_STEP]
    kce_s, kco_s, vce_s, vco_s, qt_s, sel_s, oc_s, m_s, l_s, acc_s = rest[PAGES_PER_STEP + 1:]
    ph = pl.program_id(1)
    gi = pl.program_id(2)
    rq = n_tok
    ncol = NSA_HEADS * rq
    npast = N_PAST_SEL
    col = _iota((ncol, 1), 0)
    t_col = PAST_LEN + (col & (rq - 1))
    hcol = lax.shift_right_logical(col, rq.bit_length() - 1)
    slope_col = jnp.zeros((ncol, 1), F32)
    for h in range(NSA_HEADS):
        slope_col = jnp.where(hcol == h, _slope(h), slope_col)

    def bias(pos_row):
        return slope_col * (t_col - pos_row).astype(F32)

    def keys_t(lo):
        return jnp.concatenate([pg[0, lo:lo + 128, :] for pg in pages], axis=1)

    @pl.when((ph == 0) & (gi == 0))
    def _queries():
        zero = jnp.zeros((rq, NSA_HD), F32)
        rows = []
        for h in range(NSA_HEADS):
            qh = zn_ref[0, 0:rq, 1024 + 64 * h:1088 + 64 * h] * (NSA_HD ** -0.5)
            rows.append(jnp.concatenate([qh, zero] if h < NSA_GROUP else [zero, qh], axis=1))
        qt_s[...] = jnp.concatenate(rows, axis=0).astype(BF16)

    @pl.when(ph == 0)
    def _pool():
        dn = (((1,), (1,)), ((), ()))
        for sub in range(PAGES_PER_STEP // POOL_PAGES):
            grp = pages[sub * POOL_PAGES:(sub + 1) * POOL_PAGES]
            dst = pl.ds(pl.multiple_of(gi * SEL_PER_STEP + sub * POOL_SEL, POOL_SEL), POOL_SEL)
            for lo, even_s, odd_s in ((0, kce_s, kco_s), (128, vce_s, vco_s)):
                hi_part, lo_part = _split2(jnp.concatenate([pg[0, lo:lo + 128, :] for pg in grp], axis=1))
                pooled = (lax.dot_general(pool_ref[...], hi_part, dn, preferred_element_type=F32)
                          + lax.dot_general(pool_ref[...], lo_part, dn, preferred_element_type=F32)) * (1.0 / CMP_BLOCK)
                even_s[dst, :] = pooled[0:POOL_SEL]
                odd_s[dst, :] = pooled[POOL_SEL:2 * POOL_SEL]

    @pl.when((ph == 0) & (gi == N_PAGE_GROUPS - 1))
    def _compressed():
        jrow = _iota((1, npast), 1)
        qt = qt_s[...]
        se = _mm_nt(qt, kce_s[...]) - bias(SEL_BLOCK * jrow + (CMP_BLOCK - 1))
        so = _mm_nt(qt, kco_s[...]) - bias(SEL_BLOCK * jrow + (SEL_BLOCK - 1))
        m = jnp.maximum(jnp.max(se, axis=-1, keepdims=True), jnp.max(so, axis=-1, keepdims=True))
        pe = jnp.exp(se - m)
        po = jnp.exp(so - m)
        den = jnp.sum(pe, axis=-1, keepdims=True) + jnp.sum(po, axis=-1, keepdims=True)
        pe = pe / den
        po = po / den
        oc_s[...] = _mm(pe, vce_s[...]) + _mm(po, vco_s[...])
        jp = _iota((npast, npast), 0)
        jj = _iota((npast, npast), 1)
        groups = []
        for kvh in range(NSA_KV_HEADS):
            base = kvh * NSA_GROUP * rq
            imp_e = pe[base:base + rq]
            imp_o = po[base:base + rq]
            for g in range(1, NSA_GROUP):
                imp_e = imp_e + pe[base + g * rq:base + (g + 1) * rq]
                imp_o = imp_o + po[base + g * rq:base + (g + 1) * rq]
            imp = imp_e + imp_o
            imp_t = jnp.concatenate([imp, jnp.zeros((128 - rq, npast), F32)], axis=0).T
            sel_rows = []
            for t in range(rq):
                colv = imp_t[:, t:t + 1]
                rowv = imp[t:t + 1, :]
                beats = ((colv > rowv) | ((colv == rowv) & (jp < jj))) & (jp >= 1)
                rank = jnp.sum(jnp.where(beats, 1.0, 0.0), axis=0, keepdims=True)
                sel_rows.append(jnp.where((jrow == 0) | (rank < N_SELECT - 2), 1.0, 0.0))
            groups += [jnp.concatenate(sel_rows, axis=0)] * NSA_GROUP
        groups.append(jnp.zeros((128 - ncol, npast), F32))
        sel_s[...] = jnp.concatenate(groups, axis=0).T[:, 0:ncol].astype(BF16)

    @pl.when(ph == 1)
    def _selected():
        @pl.when(gi == 0)
        def _():
            _flash_init(m_s, l_s, acc_s)

        k_t = keys_t(0)
        v_t = keys_t(128)
        pos_row = gi * KEYS_PER_STEP + _iota((1, KEYS_PER_STEP), 1)
        s = _mm(qt_s[...], k_t) - bias(pos_row)
        sel_rows = sel_s[pl.ds(pl.multiple_of(gi * SEL_PER_STEP, SEL_PER_STEP), SEL_PER_STEP), :]
        mask = _mm_tn(sel_rows, e0_ref[...]) > 0.5
        _flash_step_r(s, mask, lambda p: _mm_nt(p, v_t), m_s, l_s, acc_s)

    @pl.when((ph == 1) & (gi == N_PAGE_GROUPS - 1))
    def _finish():
        qt = qt_s[...]
        new = zn_ref[0, 0:rq, 0:768]
        npos_row = PAST_LEN + _iota((1, rq), 1)
        mask_n = npos_row <= t_col
        s = _mm_nt(qt, new[:, 256:384]) - bias(npos_row)
        _flash_step_r(s, mask_n, lambda p: _mm(p, new[:, 384:512]), m_s, l_s, acc_s)
        l = l_s[...]
        o_sel = acc_s[...] / jnp.where(l > 0, l, 1.0)
        wpos_row = PAST_LEN - WINDOW + _iota((1, WINDOW), 1)
        mask_w = (t_col - wpos_row) <= WINDOW
        s1 = jnp.where(mask_w, _mm(qt, win_ref[0, 0:128, :]) - bias(wpos_row), NEG)
        s2 = jnp.where(mask_n, _mm_nt(qt, new[:, 512:640]) - bias(npos_row), NEG)
        m = jnp.maximum(jnp.max(s1, axis=-1, keepdims=True), jnp.max(s2, axis=-1, keepdims=True))
        p1 = jnp.where(mask_w, jnp.exp(s1 - m), 0.0)
        p2 = jnp.where(mask_n, jnp.exp(s2 - m), 0.0)
        den = jnp.sum(p1, axis=-1, keepdims=True) + jnp.sum(p2, axis=-1, keepdims=True)
        o_win = (_mm_nt(p1, win_ref[0, 128:256, :]) + _mm(p2, new[:, 640:768])) / jnp.where(den > 0, den, 1.0)
        o_cmp = oc_s[...]
        for h in range(NSA_HEADS):
            rs = slice(h * rq, (h + 1) * rq)
            ls = slice(64 * (h // NSA_GROUP), 64 * (h // NSA_GROUP) + 64)
            gates = _sigmoid(zn_ref[0, 0:rq, 768 + 3 * h:771 + 3 * h])
            o = gates[:, 0:1] * o_cmp[rs, ls] + gates[:, 1:2] * o_sel[rs, ls] + gates[:, 2:3] * o_win[rs, ls]
            o_ref[0, 0:rq, 64 * h:64 * h + 64] = o * _silu(zn_ref[0, 0:rq, 1536 + 64 * h:1600 + 64 * h])
        o_ref[0, rq:, :] = jnp.zeros((SAMPLE_TPAD - rq, 512), F32)


def _nsa_sample(page_table, zn, win_t, cache_t, *, layer, n_pool, n_tok):
    b = zn.shape[0]
    rq = SAMPLE_TPAD
    assert n_tok & (n_tok - 1) == 0 and n_tok <= rq
    ncol = NSA_HEADS * n_tok
    tok = jnp.arange(KEYS_PER_STEP)
    blk = jnp.arange(SEL_PER_STEP)
    in_blk = tok[None, :] // SEL_BLOCK == blk[:, None]
    first_half = (tok[None, :] % SEL_BLOCK) < CMP_BLOCK
    e0 = in_blk.astype(BF16)
    pk = POOL_PAGES * PAGE_SIZE
    pool = jnp.concatenate([(in_blk & first_half)[0:POOL_SEL, 0:pk], (in_blk & ~first_half)[0:POOL_SEL, 0:pk]],
                           axis=0).astype(BF16)

    def page_map(kidx):
        return lambda i, ph, gi, pt: (layer * n_pool + pt[i, gi * PAGES_PER_STEP + kidx], ph, 0)

    const = lambda a: pl.BlockSpec(a.shape, lambda i, ph, gi, pt: (0, 0))
    sq = lambda dt: pltpu.VMEM((ncol, 128), dt)
    grid_spec = pltpu.PrefetchScalarGridSpec(
        num_scalar_prefetch=1,
        grid=(b, 2, N_PAGE_GROUPS),
        in_specs=[pl.BlockSpec((1, rq, ZN_W), lambda i, ph, gi, pt: (i, 0, 0)),
                  pl.BlockSpec((1, HALF_PAGE_ROWS, WINDOW), lambda i, ph, gi, pt: (layer * b + i, 0, 0)),
                  const(pool), const(e0)]
        + [pl.BlockSpec((1, HALF_PAGE_ROWS, PAGE_SIZE), page_map(kidx)) for kidx in range(PAGES_PER_STEP)],
        out_specs=pl.BlockSpec((1, rq, 512), lambda i, ph, gi, pt: (i, 0, 0)),
        scratch_shapes=[pltpu.VMEM((N_PAST_SEL, 128), F32) for _ in range(4)]
        + [sq(BF16), pltpu.VMEM((N_PAST_SEL, ncol), BF16), sq(F32),
           pltpu.VMEM((ncol, 1), F32), pltpu.VMEM((ncol, 1), F32), sq(F32)],
    )
    return pl.pallas_call(
        functools.partial(_nsa_sample_kernel, n_tok=n_tok),
        grid_spec=grid_spec,
        out_shape=jax.ShapeDtypeStruct((b, rq, 512), F32),
        compiler_params=_cparams(("parallel", "arbitrary", "arbitrary")),
        name="nsa_sample",
    )(page_table, zn, win_t, pool, e0, *([cache_t] * PAGES_PER_STEP))


def _merge_kernel(x_ref, bg_ref, br_ref, bn_ref, mg_ref, wbr_ref, wout_ref, lng_ref, lnb_ref, o_ref):
    gate = lambda lo: _sigmoid(mg_ref[:, lo:lo + 1024].astype(F32))
    acc = gate(0) * _mm(bg_ref[...], wbr_ref[0])
    acc = acc + gate(1024) * _mm(br_ref[...], wbr_ref[1])
    acc = acc + gate(2048) * _mm(bn_ref[...], wbr_ref[2])
    xf = DN_ALPHA * x_ref[...] + _mm(acc, wout_ref[...])
    mu = jnp.mean(xf, axis=-1, keepdims=True)
    d = xf - mu
    var = jnp.mean(d * d, axis=-1, keepdims=True)
    o_ref[...] = d * lax.rsqrt(var + LN_EPS) * lng_ref[...] + lnb_ref[...]


def _merge(x, o_gla, o_rwkv, o_nsa, zm, w_br, w_out, ln_g, ln_b, tm):
    m = x.shape[0]
    row = lambda n: pl.BlockSpec((tm, n), lambda i: (i, 0))
    return pl.pallas_call(
        _merge_kernel,
        grid=(m // tm,),
        in_specs=[row(D_MODEL), row(512), row(512), row(512), row(ZM_W),
                  pl.BlockSpec((3, 512, D_MODEL), lambda i: (0, 0, 0)),
                  pl.BlockSpec((D_MODEL, D_MODEL), lambda i: (0, 0)),
                  pl.BlockSpec((1, D_MODEL), lambda i: (0, 0)),
                  pl.BlockSpec((1, D_MODEL), lambda i: (0, 0))],
        out_specs=row(D_MODEL),
        out_shape=jax.ShapeDtypeStruct((m, D_MODEL), F32),
        compiler_params=_cparams(("parallel",)),
        name="merge",
    )(x, o_gla, o_rwkv, o_nsa, zm, w_br, w_out, ln_g, ln_b)


def _pack_weights(w_in, b_in):
    def pack(a):
        z = lambda n: jnp.zeros(a.shape[:-1] + (n,), a.dtype)
        gla = jnp.concatenate([a[..., 0:1024], a[..., 1040:1552], a[..., 1024:1040], z(ZG_W - 1552)], axis=-1)
        rwkv = a[..., 1552:3728]
        nsa = jnp.concatenate([a[..., 4240:5008], a[..., 5008:5032], z(1024 - 792), a[..., 3728:4240],
                               a[..., 5032:5544]], axis=-1)
        mg = a[..., 5544:8616]
        return gla, rwkv, nsa, mg
    ws = [w.astype(BF16) for w in pack(w_in)]
    bs = [b[:, None, :] for b in pack(b_in)]
    return ws, bs


def kernel(x_prompt, x_sample, cache_nsa_kv, state_nsa_win, state_gla, state_rwkv, state_rwkv_shift, page_table,
           w_in, b_in, gla_a_up, gla_a_bias, gla_norm, rwkv_mu, rwkv_w0, rwkv_w_up, rwkv_a0, rwkv_a_up, rwkv_k_k,
           rwkv_k_a, rwkv_r_k, rwkv_ln_w, rwkv_ln_b, w_br, w_out, ln_g, ln_b):
    bp, tp, _ = x_prompt.shape
    bs, ts, _ = x_sample.shape
    n_pool = cache_nsa_kv.shape[1]
    ws, bws = _pack_weights(w_in, b_in)
    w_br_b = w_br.astype(BF16)
    w_out_b = w_out.astype(BF16)
    seg = (jnp.arange(RWKV_W)[:, None] // RWKV_HD == jnp.arange(RWKV_W)[None, :] // RWKV_HD).astype(BF16)
    cache_t = jnp.transpose(cache_nsa_kv, (0, 1, 3, 4, 5, 2)).reshape(DEPTH * n_pool, 8 * NSA_HD, PAGE_SIZE)
    win_t = jnp.transpose(state_nsa_win, (0, 1, 3, 4, 5, 2)).reshape(DEPTH * bs, HALF_PAGE_ROWS, WINDOW)
    row2 = lambda a: a.reshape(DEPTH, 1, -1)
    gla_a_bias2, gla_norm2 = row2(gla_a_bias), row2(gla_norm)
    r_par = [row2(rwkv_mu), row2(rwkv_w0), rwkv_w_up, row2(rwkv_a0), rwkv_a_up, row2(rwkv_k_k), row2(rwkv_k_a),
             row2(rwkv_r_k), row2(rwkv_ln_w), row2(rwkv_ln_b)]
    ln_g2, ln_b2 = row2(ln_g), row2(ln_b)

    xp = x_prompt.reshape(bp * tp, D_MODEL)
    xs = jnp.pad(x_sample, ((0, 0), (0, SAMPLE_TPAD - ts), (0, 0))).reshape(bs * SAMPLE_TPAD, D_MODEL)
    zeros_gla = jnp.zeros((bp, GLA_HEADS, GLA_DK, GLA_DV), F32)
    zeros_rwkv = jnp.zeros((bp, RWKV_HEADS, RWKV_HD, RWKV_HD), F32)
    zeros_shift = jnp.zeros((bp, 1, RWKV_IN), F32)

    outs = {k: [] for k in ("kv_p", "kv_s", "win_p", "win_s", "gla_p", "gla_s", "rwkv_p", "rwkv_s", "sh_p", "sh_s")}
    for l in range(DEPTH):
        rp = [p[l] for p in r_par]
        zg, zr, zn, zm = (_proj(xp, ws[i][l], bws[i][l], 512, GROUP_DTYPES[i]) for i in range(4))
        zn3 = zn.reshape(bp, tp, ZN_W)
        o_gla, gla_st = _gla(zg.reshape(bp, tp, ZG_W), zeros_gla, gla_a_up[l], gla_a_bias2[l], gla_norm2[l],
                             tb_rows=512, chunk=GLA_CHUNK, t_valid=None)
        zr3 = zr.reshape(bp, tp, RWKV_IN)
        o_rwkv, rwkv_st = _rwkv(zr3, zeros_rwkv, zeros_shift, seg, rp, tb_rows=512, chunk=RWKV_CHUNK, t_valid=None)
        o_nsa = _nsa_prompt(zn3, tq=256)
        xp = _merge(xp, o_gla.reshape(bp * tp, 512), o_rwkv.reshape(bp * tp, 512), o_nsa.reshape(bp * tp, 512), zm,
                    w_br_b[l], w_out_b[l], ln_g2[l], ln_b2[l], 512)
        outs["kv_p"].append(zn3[:, :, 0:512].reshape(bp, tp, 4, NSA_KV_HEADS, NSA_HD))
        outs["win_p"].append(zn3[:, tp - WINDOW:, 512:768].reshape(bp, WINDOW, 2, NSA_KV_HEADS, NSA_HD))
        outs["gla_p"].append(gla_st)
        outs["rwkv_p"].append(rwkv_st)
        outs["sh_p"].append(zr3[:, tp - 1, :])
        rows_s = bs * SAMPLE_TPAD
        zg, zr, zn, zm = (_proj(xs, ws[i][l], bws[i][l], rows_s, GROUP_DTYPES[i]) for i in range(4))
        zn3 = zn.reshape(bs, SAMPLE_TPAD, ZN_W)
        o_gla, gla_st = _gla(zg.reshape(bs, SAMPLE_TPAD, ZG_W), state_gla[l], gla_a_up[l], gla_a_bias2[l],
                             gla_norm2[l], tb_rows=SAMPLE_TPAD, chunk=SAMPLE_TPAD, t_valid=ts)
        zr3 = zr.reshape(bs, SAMPLE_TPAD, RWKV_IN)
        o_rwkv, rwkv_st = _rwkv(zr3, state_rwkv[l], state_rwkv_shift[l][:, None, :], seg, rp,
                                tb_rows=SAMPLE_TPAD, chunk=SAMPLE_TPAD, t_valid=ts)
        o_nsa = _nsa_sample(page_table, zn3, win_t, cache_t, layer=l, n_pool=n_pool, n_tok=ts)
        xs = _merge(xs, o_gla.reshape(rows_s, 512), o_rwkv.reshape(rows_s, 512), o_nsa.reshape(rows_s, 512), zm,
                    w_br_b[l], w_out_b[l], ln_g2[l], ln_b2[l], rows_s)
        outs["kv_s"].append(zn3[:, 0:ts, 0:512].reshape(bs, ts, 4, NSA_KV_HEADS, NSA_HD))
        new_win = zn3[:, 0:ts, 512:768].reshape(bs, ts, 2, NSA_KV_HEADS, NSA_HD)
        outs["win_s"].append(jnp.concatenate([state_nsa_win[l][:, ts:], new_win], axis=1))
        outs["gla_s"].append(gla_st)
        outs["rwkv_s"].append(rwkv_st)
        outs["sh_s"].append(zr3[:, ts - 1, :])

    st = lambda k: jnp.stack(outs[k])
    y_prompt = xp.reshape(bp, tp, D_MODEL)
    y_sample = xs.reshape(bs, SAMPLE_TPAD, D_MODEL)[:, 0:ts]
    return (y_prompt, y_sample, st("kv_p"), st("kv_s"), st("win_p"), st("win_s"), st("gla_p"), st("gla_s"),
            st("rwkv_p"), st("rwkv_s"), st("sh_p"), st("sh_s"))
```

```python
import functools

import jax
import jax.numpy as jnp
from jax import lax
from jax.experimental import pallas as pl
from jax.experimental.pallas import tpu as pltpu

F32 = jnp.float32
BF16 = jnp.bfloat16

D_MODEL = 1024
DEPTH = 2
PAST_LEN = 16384
PAGE_SIZE = 128
N_PAGES = PAST_LEN // PAGE_SIZE

GLA_HEADS, GLA_DK, GLA_DV = 4, 64, 128
GLA_K, GLA_V, GLA_LORA = 256, 512, 16
GLA_GATE_NORM = 16.0
GLA_CHUNK = 64
GLA_SUB = 16

RWKV_HEADS, RWKV_HD, RWKV_W = 8, 64, 512
RWKV_IN = 2176
RWKV_LN_EPS = 64e-5
RWKV_CHUNK = 64

NSA_HEADS, NSA_KV_HEADS, NSA_GROUP, NSA_HD = 8, 2, 4, 64
CMP_BLOCK, SEL_BLOCK, N_SELECT, WINDOW = 32, 64, 16, 512
FORCE_SCORE = 1e9
NEG = -1e30

DN_ALPHA = (2 * DEPTH) ** 0.25
LN_EPS = 1e-5
NORM_EPS = 1e-6

ZG_W = 1664
ZN_W = 2048
ZM_W = 3072
SAMPLE_TPAD = 16
VMEM_LIMIT = 56 * 1024 * 1024
BRANCH_DTYPE = BF16
GROUP_DTYPES = (F32, F32, F32, BF16)


def _mm(a, b):
    return jnp.dot(a.astype(BF16), b.astype(BF16), preferred_element_type=F32)


def _mm_nt(a, b):
    return lax.dot_general(a.astype(BF16), b.astype(BF16), (((1,), (1,)), ((), ())), preferred_element_type=F32)


def _mm_tn(a, b):
    return lax.dot_general(a.astype(BF16), b.astype(BF16), (((0,), (0,)), ((), ())), preferred_element_type=F32)


def _mm_exact(a, b):
    return jnp.dot(a, b, preferred_element_type=F32, precision=lax.Precision.HIGHEST)


def _split2(a):
    hi = a.astype(BF16)
    return hi, (a - hi.astype(F32)).astype(BF16)


def _mm_split(a, b01):
    hi, lo = _split2(a)
    return jnp.dot(hi, b01, preferred_element_type=F32) + jnp.dot(lo, b01, preferred_element_type=F32)


def _softplus(x):
    return jnp.maximum(x, 0.0) + jnp.log(1.0 + jnp.exp(-jnp.abs(x)))


def _sigmoid(x):
    return 0.5 * jnp.tanh(0.5 * x) + 0.5


def _silu(x):
    return x * _sigmoid(x)


def _iota(shape, dim):
    return lax.broadcasted_iota(jnp.int32, shape, dim)


def _cparams(sem):
    return pltpu.CompilerParams(dimension_semantics=sem, vmem_limit_bytes=VMEM_LIMIT)


def _proj_kernel(x_ref, w_ref, b_ref, o_ref):
    o_ref[...] = (_mm(x_ref[...], w_ref[...]) + b_ref[...]).astype(o_ref.dtype)


def _proj(x, w, b, tm, out_dtype=F32):
    m, k = x.shape
    n = w.shape[1]
    return pl.pallas_call(
        _proj_kernel,
        grid=(m // tm,),
        in_specs=[pl.BlockSpec((tm, k), lambda i: (i, 0)),
                  pl.BlockSpec((k, n), lambda i: (0, 0)),
                  pl.BlockSpec((1, n), lambda i: (0, 0))],
        out_specs=pl.BlockSpec((tm, n), lambda i: (i, 0)),
        out_shape=jax.ShapeDtypeStruct((m, n), out_dtype),
        compiler_params=_cparams(("parallel",)),
        name="proj",
    )(x, w, b)


def _gla_kernel(zg_ref, s0_ref, aup_ref, abias_ref, norm_ref, o_ref, sout_ref, st_scr, *, tb_rows, chunk, t_valid):
    tb = pl.program_id(1)
    c_rows = chunk
    sub = min(GLA_SUB, c_rows)
    nsub = c_rows // sub

    @pl.when(tb == 0)
    def _():
        for h in range(GLA_HEADS):
            st_scr[h] = s0_ref[0, h].T

    tri = (_iota((c_rows, c_rows), 1) <= _iota((c_rows, c_rows), 0)).astype(F32)
    ones_red = jnp.ones((GLA_DK, 128), BF16)
    lane_s = _iota((sub, 128), 1)
    row_s = _iota((sub, 128), 0)
    col_c = _iota((sub, c_rows), 1)

    def chunk_body(c, carry):
        r0 = pl.multiple_of(c * c_rows, c_rows)
        z = zg_ref[0, pl.ds(r0, c_rows), :]
        q = z[:, 0:256] * (GLA_DK ** -0.5)
        k = z[:, 256:512]
        v = z[:, 512:1024]
        g = z[:, 1024:1536]
        ga = z[:, 1536:1552]
        la = -_softplus(-(_mm(ga, aup_ref[...]) + abias_ref[...])) * (1.0 / GLA_GATE_NORM)
        if t_valid is not None:
            ok = (tb * tb_rows + r0 + _iota((c_rows, 1), 0)) < t_valid
            la = jnp.where(ok, la, 0.0)
            k = jnp.where(ok, k, 0.0)
            v = jnp.where(ok, v, 0.0)
        cum = _mm_exact(tri, la)
        heads = range(GLA_HEADS)
        pairs = [(h, blk) for h in heads for blk in range(nsub)]
        qh = [q[:, 64 * h:64 * h + 64] for h in heads]
        kh = [k[:, 64 * h:64 * h + 64] for h in heads]
        ch = [cum[:, 64 * h:64 * h + 64] for h in heads]
        vh = [v[:, 128 * h:128 * h + 128] for h in heads]
        st = [st_scr[h] for h in heads]
        o_in = [_mm_nt(qh[h] * jnp.exp(ch[h]), st[h]) for h in heads]
        red, off = {}, {}
        for h, blk in pairs:
            sl = slice(blk * sub, (blk + 1) * sub)
            q_i, k_i, c_i = qh[h][sl], kh[h][sl], ch[h][sl]
            es = [q_i * k_i[j:j + 1] * jnp.exp(jnp.minimum(c_i - c_i[j:j + 1], 0.0)) for j in range(sub)]
            red[h, blk] = _mm(jnp.concatenate(es, axis=0), ones_red)
            if blk > 0:
                b_i = ch[h][blk * sub - 1:blk * sub]
                q_t = q_i * jnp.exp(c_i - b_i)
                k_t = kh[h] * jnp.exp(jnp.minimum(b_i - ch[h], 0.0))
                off[h, blk] = _mm_nt(q_t, k_t)
        att = []
        for h in heads:
            att_rows = []
            for blk in range(nsub):
                a_i = jnp.zeros((sub, 128), F32)
                for j in range(sub):
                    a_i = a_i + jnp.where((lane_s == blk * sub + j) & (row_s >= j),
                                          red[h, blk][j * sub:(j + 1) * sub], 0.0)
                a_i = a_i[:, 0:c_rows]
                if blk > 0:
                    a_i = a_i + jnp.where(col_c < blk * sub, off[h, blk], 0.0)
                att_rows.append(a_i)
            att.append(att_rows[0] if nsub == 1 else jnp.concatenate(att_rows, axis=0))
        o = [o_in[h] + _mm(att[h], vh[h]) for h in heads]
        last = [ch[h][c_rows - 1:c_rows] for h in heads]
        st_new = [st[h] * jnp.exp(last[h]) + _mm_tn(vh[h], kh[h] * jnp.exp(last[h] - ch[h])) for h in heads]
        for h in heads:
            st_scr[h] = st_new[h]
            oh = o[h] * lax.rsqrt(jnp.mean(o[h] * o[h], axis=-1, keepdims=True) + NORM_EPS)
            oh = oh * norm_ref[:, 128 * h:128 * h + 128] * _silu(g[:, 128 * h:128 * h + 128])
            o_ref[0, pl.ds(r0, c_rows), 128 * h:128 * h + 128] = oh.astype(o_ref.dtype)
        return carry

    lax.fori_loop(0, tb_rows // c_rows, chunk_body, 0, unroll=min(2, tb_rows // c_rows))

    @pl.when(tb == pl.num_programs(1) - 1)
    def _():
        for h in range(GLA_HEADS):
            sout_ref[0, h] = st_scr[h].T


def _gla(zg, s0, a_up, a_bias, norm_g, *, tb_rows, chunk, t_valid):
    b, t, _ = zg.shape
    kern = functools.partial(_gla_kernel, tb_rows=tb_rows, chunk=chunk, t_valid=t_valid)
    return pl.pallas_call(
        kern,
        grid=(b, t // tb_rows),
        in_specs=[pl.BlockSpec((1, tb_rows, ZG_W), lambda i, j: (i, j, 0)),
                  pl.BlockSpec((1, GLA_HEADS, GLA_DK, GLA_DV), lambda i, j: (i, 0, 0, 0)),
                  pl.BlockSpec((GLA_LORA, GLA_K), lambda i, j: (0, 0)),
                  pl.BlockSpec((1, GLA_K), lambda i, j: (0, 0)),
                  pl.BlockSpec((1, GLA_V), lambda i, j: (0, 0))],
        out_specs=[pl.BlockSpec((1, tb_rows, GLA_V), lambda i, j: (i, j, 0)),
                   pl.BlockSpec((1, GLA_HEADS, GLA_DK, GLA_DV), lambda i, j: (i, 0, 0, 0))],
        out_shape=[jax.ShapeDtypeStruct((b, t, GLA_V), BRANCH_DTYPE),
                   jax.ShapeDtypeStruct((b, GLA_HEADS, GLA_DK, GLA_DV), F32)],
        scratch_shapes=[pltpu.VMEM((GLA_HEADS, GLA_DV, GLA_DK), F32)],
        compiler_params=_cparams(("parallel", "arbitrary")),
        name="gla",
    )(zg, s0, a_up, a_bias, norm_g)


def _rwkv_kernel(zr_ref, s0_ref, sh0_ref, seg_ref, mu_ref, w0_ref, wup_ref, a0_ref, aup_ref, kk_ref, ka_ref, rk_ref,
                 lnw_ref, lnb_ref, y_ref, sout_ref,
                 s_scr, prev_scr, lw_s, kk_s, kka_s, k2_s, r_s, v_s, y_s, *, tb_rows, chunk, t_valid):
    tb = pl.program_id(1)
    c_rows = chunk
    nh = RWKV_HEADS

    @pl.when(tb == 0)
    def _():
        s_scr[...] = s0_ref[0]
        prev_scr[...] = sh0_ref[0]

    z = zr_ref[0]
    rows = _iota((tb_rows, 1), 0)
    zp = pltpu.roll(z, 1, axis=0)
    zp = jnp.concatenate([jnp.where(rows[0:8] == 0, prev_scr[...], zp[0:8]), zp[8:]], axis=0)
    prev_scr[...] = z[tb_rows - 1:tb_rows]
    zs = z + (zp - z) * mu_ref[...]
    r = zs[:, 0:512]
    k = zs[:, 512:1024]
    v = zs[:, 1024:1536]
    wl = zs[:, 1536:1600]
    al = zs[:, 1600:1664]
    w = -_softplus(-(w0_ref[...] + _mm(jnp.tanh(wl), wup_ref[...]))) - 0.5
    lw = -jnp.exp(w)
    a = _sigmoid(a0_ref[...] + _mm(al, aup_ref[...]))
    kk = k * kk_ref[...]
    kk = kk * lax.rsqrt(_mm_split(kk * kk, seg_ref[...]) + NORM_EPS)
    k2 = k * (1.0 + (a - 1.0) * ka_ref[...])
    kka = kk * a
    if t_valid is not None:
        ok = (tb * tb_rows + rows) < t_valid
        lw = jnp.where(ok, lw, 0.0)
        kka = jnp.where(ok, kka, 0.0)
        k2 = jnp.where(ok, k2, 0.0)
    lw_s[...] = lw
    kk_s[...] = kk
    kka_s[...] = kka
    k2_s[...] = k2
    r_s[...] = r
    v_s[...] = v

    ri = _iota((c_rows, c_rows), 0)
    ci = _iota((c_rows, c_rows), 1)
    tri = (ci <= ri).astype(F32)
    strict = ci < ri
    incl = ci <= ri
    n_dbl = max(1, (c_rows - 1).bit_length())

    def chunk_body(c, carry):
        r0 = pl.multiple_of(c * c_rows, c_rows)
        ds = pl.ds(r0, c_rows)
        lwc = lw_s[ds, :]
        cl = _mm_exact(tri, lwc)
        e_inv = jnp.exp(-cl)
        e_fwd = jnp.exp(cl)
        e_prev = jnp.exp(cl - lwc)
        e_end = jnp.exp(cl[c_rows - 1:c_rows] - cl)
        g_end = jnp.exp(cl[c_rows - 1:c_rows])
        kkc, kkac, k2c, rc, vc = kk_s[ds, :], kka_s[ds, :], k2_s[ds, :], r_s[ds, :], v_s[ds, :]
        heads = range(nh)
        hsl = [slice(64 * h, 64 * h + 64) for h in heads]
        a_t = [-kkac[:, hs] * e_inv[:, hs] for hs in hsl]
        b_t = [kkc[:, hs] * e_prev[:, hs] for hs in hsl]
        k_t = [k2c[:, hs] * e_inv[:, hs] for hs in hsl]
        r_t = [rc[:, hs] * e_fwd[:, hs] for hs in hsl]
        vh = [vc[:, hs] for hs in hsl]
        cc = c_rows
        ak = [jnp.concatenate([a_t[h], k_t[h]], axis=0).astype(BF16) for h in heads]
        bra = [_mm_nt(jnp.concatenate([b_t[h], r_t[h]], axis=0), ak[h]) for h in heads]
        s0 = [s_scr[h] for h in heads]
        l_k = [jnp.where(strict, bra[h][0:cc, cc:2 * cc], 0.0) for h in heads]
        m_k = [jnp.where(incl, bra[h][cc:2 * cc, cc:2 * cc], 0.0) for h in heads]
        lmv = [_mm(jnp.concatenate([l_k[h], m_k[h]], axis=0), vh[h]) for h in heads]
        x = [jnp.concatenate([b_t[h], lmv[h][0:cc]], axis=1) for h in heads]
        lp = [jnp.where(strict, bra[h][0:cc, 0:cc], 0.0).astype(BF16) for h in heads]
        for step in range(n_dbl):
            if step + 1 < n_dbl:
                new = [_mm(lp[h], jnp.concatenate([x[h].astype(BF16), lp[h]], axis=1)) for h in heads]
                x = [x[h] + new[h][:, 0:128] for h in heads]
                lp = [new[h][:, 128:128 + cc].astype(BF16) for h in heads]
            else:
                x = [x[h] + _mm(lp[h], x[h]) for h in heads]
        prs = [_mm_nt(jnp.concatenate([x[h][:, 0:64], r_t[h]], axis=0), s0[h]) for h in heads]
        u = [prs[h][0:cc] + x[h][:, 64:128] for h in heads]
        m_a = [jnp.where(incl, bra[h][cc:2 * cc, 0:cc], 0.0) for h in heads]
        y = [prs[h][cc:2 * cc] + _mm(m_a[h], u[h]) + lmv[h][cc:2 * cc] for h in heads]
        akg = [jnp.concatenate([-kkac[:, hs] * e_end[:, hs], k2c[:, hs] * e_end[:, hs]], axis=0) for hs in hsl]
        s_new = [s0[h] * g_end[:, hsl[h]] + _mm_tn(jnp.concatenate([u[h], vh[h]], axis=0), akg[h]) for h in heads]
        for h in heads:
            s_scr[h] = s_new[h]
            y_s[ds, hsl[h]] = y[h]
        return carry

    lax.fori_loop(0, tb_rows // c_rows, chunk_body, 0, unroll=min(2, tb_rows // c_rows))

    y = y_s[...]
    seg = seg_ref[...]
    mean = _mm_split(y, seg) * (1.0 / RWKV_HD)
    d = y - mean
    var = _mm_split(d * d, seg) * (1.0 / RWKV_HD)
    yn = d * lax.rsqrt(var + RWKV_LN_EPS) * lnw_ref[...] + lnb_ref[...]
    bonus = _mm_split(r * k2 * rk_ref[...], seg) * v
    y_ref[0] = ((yn + bonus) * _silu(zs[:, 1664:2176])).astype(y_ref.dtype)

    @pl.when(tb == pl.num_programs(1) - 1)
    def _():
        sout_ref[0] = s_scr[...]


def _rwkv(zr, s0, sh0, seg, params, *, tb_rows, chunk, t_valid):
    b, t, _ = zr.shape
    kern = functools.partial(_rwkv_kernel, tb_rows=tb_rows, chunk=chunk, t_valid=t_valid)
    full = lambda shp: pl.BlockSpec(shp, lambda i, j: (0,) * len(shp))
    mu, w0, w_up, a0, a_up, k_k, k_a, r_k, ln_w, ln_b = params
    return pl.pallas_call(
        kern,
        grid=(b, t // tb_rows),
        in_specs=[pl.BlockSpec((1, tb_rows, RWKV_IN), lambda i, j: (i, j, 0)),
                  pl.BlockSpec((1, RWKV_HEADS, RWKV_HD, RWKV_HD), lambda i, j: (i, 0, 0, 0)),
                  pl.BlockSpec((1, 1, RWKV_IN), lambda i, j: (i, 0, 0)),
                  full((RWKV_W, RWKV_W)), full((1, RWKV_IN)), full((1, RWKV_W)), full((64, RWKV_W)),
                  full((1, RWKV_W)), full((64, RWKV_W)), full((1, RWKV_W)), full((1, RWKV_W)), full((1, RWKV_W)),
                  full((1, RWKV_W)), full((1, RWKV_W))],
        out_specs=[pl.BlockSpec((1, tb_rows, RWKV_W), lambda i, j: (i, j, 0)),
                   pl.BlockSpec((1, RWKV_HEADS, RWKV_HD, RWKV_HD), lambda i, j: (i, 0, 0, 0))],
        out_shape=[jax.ShapeDtypeStruct((b, t, RWKV_W), BRANCH_DTYPE),
                   jax.ShapeDtypeStruct((b, RWKV_HEADS, RWKV_HD, RWKV_HD), F32)],
        scratch_shapes=[pltpu.VMEM((RWKV_HEADS, RWKV_HD, RWKV_HD), F32), pltpu.VMEM((1, RWKV_IN), F32)]
        + [pltpu.VMEM((tb_rows, RWKV_W), F32) for _ in range(7)],
        compiler_params=_cparams(("parallel", "arbitrary")),
        name="rwkv",
    )(zr, s0, sh0, seg, mu, w0, w_up, a0, a_up, k_k, k_a, r_k, ln_w, ln_b)


def _slope(h):
    return 2.0 ** (-8.0 * (h + 1) / NSA_HEADS)


def _flash_init(m_ref, l_ref, acc_ref):
    m_ref[...] = jnp.full(m_ref.shape, NEG, F32)
    l_ref[...] = jnp.zeros(l_ref.shape, F32)
    acc_ref[...] = jnp.zeros(acc_ref.shape, F32)


def _flash_step_t(s_t, v, m_ref, l_ref, acc_ref):
    m_old = m_ref[...]
    m_new = jnp.maximum(m_old, jnp.max(s_t, axis=0, keepdims=True))
    p = jnp.exp(s_t - m_new)
    alpha = jnp.exp(m_old - m_new)
    l_ref[...] = alpha * l_ref[...] + jnp.sum(p, axis=0, keepdims=True)
    acc_ref[...] = alpha * acc_ref[...] + _mm_tn(v, p)
    m_ref[...] = m_new


def _flash_steps_t(s_list, v_list, m_ref, l_ref, acc_ref):
    ks = range(len(s_list))
    m_old = [m_ref[k] for k in ks]
    m_new = [jnp.maximum(m_old[k], jnp.max(s_list[k], axis=0, keepdims=True)) for k in ks]
    p = [jnp.exp(s_list[k] - m_new[k]) for k in ks]
    alpha = [jnp.exp(m_old[k] - m_new[k]) for k in ks]
    pv = [_mm_tn(v_list[k], p[k]) for k in ks]
    for k in ks:
        l_ref[k] = alpha[k] * l_ref[k] + jnp.sum(p[k], axis=0, keepdims=True)
        acc_ref[k] = alpha[k] * acc_ref[k] + pv[k]
        m_ref[k] = m_new[k]


def _flash_result(l_ref, acc_ref, k):
    l = l_ref[k]
    return acc_ref[k] / jnp.where(l > 0, l, 1.0)


SEL_LANE0 = 66
BLOCK_PENALTY = -(2.0 ** 100)


def _nsa_prompt_kernel(q_ref, kv_ref, bg_ref, g_ref, o_ref, kc_s, vc_s, ks_s, vs_s, kw_s, vw_s, qa_s, qs_s, oc_s,
                       m_s, l_s, acc_s, m2_s, l2_s, acc2_s, *, t_len, tq):
    i = pl.program_id(1)
    nb = t_len // SEL_BLOCK
    tk = tq
    ncol = NSA_GROUP * tq
    win_tiles = WINDOW // tk

    def aug_lanes(lane, pos):
        return jnp.where(lane == 64, lax.shift_right_logical(pos, 7).astype(F32),
                         jnp.where(lane == 65, (pos & 127).astype(F32), 0.0))

    def head_lanes(x, kvh):
        return x if kvh == 0 else pltpu.roll(x, 64, axis=1)

    @pl.when(i == 0)
    def _():
        lane = _iota((t_len, 128), 1)
        pos = _iota((t_len, 128), 0)
        aug = aug_lanes(lane, pos)
        aug_sel = jnp.where(lane - SEL_LANE0 == lax.shift_right_logical(pos, 6), 1.0, aug)
        for kcol, vcol, k_dst, v_dst, k_aug in ((256, 384, ks_s, vs_s, aug_sel), (512, 640, kw_s, vw_s, aug)):
            kf = kv_ref[0, :, kcol:kcol + 128]
            vf = kv_ref[0, :, vcol:vcol + 128]
            for kvh in range(NSA_KV_HEADS):
                k_dst[kvh] = jnp.where(lane < 64, head_lanes(kf, kvh), k_aug).astype(BF16)
                v_dst[kvh] = jnp.where(lane < 64, head_lanes(vf, kvh), 0.0).astype(BF16)
        kcm = kv_ref[0, :, 0:128].reshape(nb, SEL_BLOCK, 128)
        vcm = kv_ref[0, :, 128:256].reshape(nb, SEL_BLOCK, 128)
        inv = 1.0 / CMP_BLOCK
        kc = jnp.concatenate([jnp.sum(kcm[:, 0:CMP_BLOCK], axis=1) * inv,
                              jnp.sum(kcm[:, CMP_BLOCK:SEL_BLOCK], axis=1) * inv], axis=0)
        vc = jnp.concatenate([jnp.sum(vcm[:, 0:CMP_BLOCK], axis=1) * inv,
                              jnp.sum(vcm[:, CMP_BLOCK:SEL_BLOCK], axis=1) * inv], axis=0)
        lane_c = _iota((2 * nb, 128), 1)
        r_c = _iota((2 * nb, 128), 0)
        cend = jnp.where(r_c < nb, SEL_BLOCK * r_c + (CMP_BLOCK - 1), SEL_BLOCK * (r_c - nb) + (SEL_BLOCK - 1))
        aug_c = aug_lanes(lane_c, cend)
        for kvh in range(NSA_KV_HEADS):
            kc_s[kvh] = jnp.where(lane_c < 64, head_lanes(kc, kvh), aug_c).astype(BF16)
            vc_s[kvh] = jnp.where(lane_c < 64, head_lanes(vc, kvh), 0.0).astype(BF16)

    lane_q = _iota((tq, 128), 1)
    for h in range(NSA_HEADS):
        qb = q_ref[0, :, 128 * (h // 2):128 * (h // 2) + 128]
        if h % 2:
            qb = pltpu.roll(qb, 64, axis=1)
        sl = _slope(h)
        qa = jnp.where(lane_q < 64, qb * (NSA_HD ** -0.5),
                       jnp.where(lane_q == 64, 128.0 * sl, jnp.where(lane_q == 65, sl, 0.0)))
        qa_s[h // NSA_GROUP, (h % NSA_GROUP) * tq:(h % NSA_GROUP + 1) * tq, :] = qa.astype(BF16)

    t_row = i * tq + (_iota((1, ncol), 1) & (tq - 1))
    t_row1 = i * tq + _iota((1, tq), 1)
    r_c1 = _iota((2 * nb, 1), 0)
    cend_col = jnp.where(r_c1 < nb, SEL_BLOCK * r_c1 + (CMP_BLOCK - 1), SEL_BLOCK * (r_c1 - nb) + (SEL_BLOCK - 1))
    mask_c = cend_col <= t_row
    jrow = _iota((nb, 1), 0)
    tblk = lax.shift_right_logical(t_row1, 6)
    forced = (jrow == tblk) | (jrow == 0)
    valid = jrow <= tblk
    bg_t = bg_ref[0].T

    for kvh in range(NSA_KV_HEADS):
        qa = qa_s[kvh]
        s_c = jnp.where(mask_c, _mm_nt(kc_s[kvh], qa), NEG)
        m = jnp.max(s_c, axis=0, keepdims=True)
        p = jnp.where(mask_c, jnp.exp(s_c - m), 0.0)
        den = jnp.sum(p, axis=0, keepdims=True)
        p = p / jnp.where(den > 0, den, 1.0)
        oc_s[kvh] = _mm_tn(vc_s[kvh], p)
        imp_e = p[0:nb, 0:tq]
        imp_o = p[nb:2 * nb, 0:tq]
        for g in range(1, NSA_GROUP):
            imp_e = imp_e + p[0:nb, g * tq:(g + 1) * tq]
            imp_o = imp_o + p[nb:2 * nb, g * tq:(g + 1) * tq]
        score = jnp.where(forced, FORCE_SCORE, jnp.where(valid, imp_e + imp_o, -FORCE_SCORE))
        rank = jnp.zeros((nb, tq), F32)
        for jp in range(nb):
            rj = score[jp:jp + 1, :]
            rank = rank + jnp.where((rj > score) | ((rj == score) & (jp < jrow)), 1.0, 0.0)
        sel = jnp.where((rank < N_SELECT) & valid, 1.0, 0.0)
        sel_q = jnp.concatenate([sel, jnp.zeros((128 - nb, tq), F32)], axis=0).T
        sel_q = pltpu.roll(sel_q, SEL_LANE0, axis=1)
        pen = jnp.where((lane_q >= SEL_LANE0) & (lane_q < SEL_LANE0 + nb) & (sel_q < 0.5), BLOCK_PENALTY, 0.0)
        for g in range(NSA_GROUP):
            cs = slice(g * tq, (g + 1) * tq)
            qs_s[kvh, cs, :] = (qa_s[kvh, cs, :].astype(F32) + pen).astype(BF16)

    _flash_init(m_s, l_s, acc_s)
    _flash_init(m2_s, l2_s, acc2_s)
    kvs = range(NSA_KV_HEADS)

    def sel_step(kt, diagonal):
        k0 = pl.multiple_of(kt * tk, tk)
        s_t = [_mm_nt(ks_s[kvh, pl.ds(k0, tk), :], qs_s[kvh]) for kvh in kvs]
        if diagonal:
            causal = kt * tk + _iota((tk, 1), 0) <= t_row
            s_t = [jnp.where(causal, s, NEG) for s in s_t]
        _flash_steps_t(s_t, [vs_s[kvh, pl.ds(k0, tk), :] for kvh in kvs], m_s, l_s, acc_s)

    def sel_body(kt, carry):
        sel_step(kt, False)
        return carry

    lax.fori_loop(0, i, sel_body, 0)
    sel_step(i, True)

    def win_step(kt, back):
        k0 = pl.multiple_of(kt * tk, tk)
        s_t = [_mm_nt(kw_s[kvh, pl.ds(k0, tk), :], qa_s[kvh]) for kvh in kvs]
        dist = t_row - (kt * tk + _iota((tk, 1), 0))
        if back == 0:
            s_t = [jnp.where(dist >= 0, s, NEG) for s in s_t]
        elif back == win_tiles:
            s_t = [jnp.where(dist <= WINDOW, s, NEG) for s in s_t]
        _flash_steps_t(s_t, [vw_s[kvh, pl.ds(k0, tk), :] for kvh in kvs], m2_s, l2_s, acc2_s)

    for back in range(win_tiles, 0, -1):
        @pl.when(i >= back)
        def _(back=back):
            win_step(i - back, back)
    win_step(i, 0)

    for kvh in kvs:
        o_sel = _flash_result(l_s, acc_s, kvh)
        o_win = _flash_result(l2_s, acc2_s, kvh)
        o_cmp = oc_s[kvh]
        for g in range(NSA_GROUP):
            h = NSA_GROUP * kvh + g
            cs = slice(g * tq, (g + 1) * tq)
            gates = _sigmoid(bg_t[3 * h:3 * h + 3, :])
            comb = gates[0:1] * o_cmp[:, cs] + gates[1:2] * o_sel[:, cs] + gates[2:3] * o_win[:, cs]
            o_h = comb.T[:, 0:64] * _silu(g_ref[0, :, 64 * h:64 * h + 64])
            o_ref[0, :, 64 * h:64 * h + 64] = o_h.astype(o_ref.dtype)


def _nsa_prompt(zn, *, tq):
    b, t, _ = zn.shape
    nb = t // SEL_BLOCK
    ncol = NSA_GROUP * tq
    kern = functools.partial(_nsa_prompt_kernel, t_len=t, tq=tq)
    kv_f32 = lambda n: pltpu.VMEM((NSA_KV_HEADS, n, ncol), F32)
    stat = lambda: [kv_f32(1), kv_f32(1), kv_f32(128)]
    kvbuf = lambda n: pltpu.VMEM((NSA_KV_HEADS, n, 128), BF16)
    return pl.pallas_call(
        kern,
        grid=(b, t // tq),
        in_specs=[pl.BlockSpec((1, tq, 512), lambda i, j: (i, j, 2)),
                  pl.BlockSpec((1, t, 768), lambda i, j: (i, 0, 0)),
                  pl.BlockSpec((1, tq, 256), lambda i, j: (i, j, 3)),
                  pl.BlockSpec((1, tq, 512), lambda i, j: (i, j, 3))],
        out_specs=pl.BlockSpec((1, tq, 512), lambda i, j: (i, j, 0)),
        out_shape=jax.ShapeDtypeStruct((b, t, 512), BRANCH_DTYPE),
        scratch_shapes=[kvbuf(2 * nb), kvbuf(2 * nb), kvbuf(t), kvbuf(t), kvbuf(t), kvbuf(t), kvbuf(ncol),
                        kvbuf(ncol), kv_f32(128)] + stat() + stat(),
        compiler_params=_cparams(("parallel", "arbitrary")),
        name="nsa_prompt",
    )(zn, zn, zn, zn)


PAGES_PER_STEP = 64
N_PAGE_GROUPS = N_PAGES // PAGES_PER_STEP
KEYS_PER_STEP = PAGES_PER_STEP * PAGE_SIZE
SEL_PER_STEP = KEYS_PER_STEP // SEL_BLOCK
POOL_PAGES = 8
POOL_SEL = POOL_PAGES * PAGE_SIZE // SEL_BLOCK
N_PAST_SEL = PAST_LEN // SEL_BLOCK
HALF_PAGE_ROWS = 2 * NSA_KV_HEADS * NSA_HD


def _flash_step_r(s, mask, pv, m_ref, l_ref, acc_ref):
    s = jnp.where(mask, s, NEG)
    m_old = m_ref[...]
    m_new = jnp.maximum(m_old, jnp.max(s, axis=-1, keepdims=True))
    p = jnp.exp(s - m_new)
    alpha = jnp.exp(m_old - m_new)
    l_ref[...] = alpha * l_ref[...] + jnp.sum(p, axis=-1, keepdims=True)
    acc_ref[...] = alpha * acc_ref[...] + pv(p)
    m_ref[...] = m_new


def _nsa_sample_kernel(pt_ref, zn_ref, win_ref, pool_ref, e0_ref, *rest, n_tok):
    pages = rest[:PAGES_PER_STEP]
    o_ref = rest[PAGES_PER_STEP]
    kce_s, kco_s, vce_s, vco_s, qt_s, sel_s, oc_s, m_s, l_s, acc_s = rest[PAGES_PER_STEP + 1:]
    ph = pl.program_id(1)
    gi = pl.program_id(2)
    rq = n_tok
    ncol = NSA_HEADS * rq
    npast = N_PAST_SEL
    col = _iota((ncol, 1), 0)
    t_col = PAST_LEN + (col & (rq - 1))
    hcol = lax.shift_right_logical(col, rq.bit_length() - 1)
    slope_col = jnp.zeros((ncol, 1), F32)
    for h in range(NSA_HEADS):
        slope_col = jnp.where(hcol == h, _slope(h), slope_col)

    def bias(pos_row):
        return slope_col * (t_col - pos_row).astype(F32)

    def keys_t(lo):
        return jnp.concatenate([pg[0, lo:lo + 128, :] for pg in pages], axis=1)

    @pl.when((ph == 0) & (gi == 0))
    def _queries():
        zero = jnp.zeros((rq, NSA_HD), F32)
        rows = []
        for h in range(NSA_HEADS):
            qh = zn_ref[0, 0:rq, 1024 + 64 * h:1088 + 64 * h] * (NSA_HD ** -0.5)
            rows.append(jnp.concatenate([qh, zero] if h < NSA_GROUP else [zero, qh], axis=1))
        qt_s[...] = jnp.concatenate(rows, axis=0).astype(BF16)

    @pl.when(ph == 0)
    def _pool():
        dn = (((1,), (1,)), ((), ()))
        for sub in range(PAGES_PER_STEP // POOL_PAGES):
            grp = pages[sub * POOL_PAGES:(sub + 1) * POOL_PAGES]
            dst = pl.ds(pl.multiple_of(gi * SEL_PER_STEP + sub * POOL_SEL, POOL_SEL), POOL_SEL)
            for lo, even_s, odd_s in ((0, kce_s, kco_s), (128, vce_s, vco_s)):
                hi_part, lo_part = _split2(jnp.concatenate([pg[0, lo:lo + 128, :] for pg in grp], axis=1))
                pooled = (lax.dot_general(pool_ref[...], hi_part, dn, preferred_element_type=F32)
                          + lax.dot_general(pool_ref[...], lo_part, dn, preferred_element_type=F32)) * (1.0 / CMP_BLOCK)
                even_s[dst, :] = pooled[0:POOL_SEL]
                odd_s[dst, :] = pooled[POOL_SEL:2 * POOL_SEL]

    @pl.when((ph == 0) & (gi == N_PAGE_GROUPS - 1))
    def _compressed():
        jrow = _iota((1, npast), 1)
        qt = qt_s[...]
        se = _mm_nt(qt, kce_s[...]) - bias(SEL_BLOCK * jrow + (CMP_BLOCK - 1))
        so = _mm_nt(qt, kco_s[...]) - bias(SEL_BLOCK * jrow + (SEL_BLOCK - 1))
        m = jnp.maximum(jnp.max(se, axis=-1, keepdims=True), jnp.max(so, axis=-1, keepdims=True))
        pe = jnp.exp(se - m)
        po = jnp.exp(so - m)
        den = jnp.sum(pe, axis=-1, keepdims=True) + jnp.sum(po, axis=-1, keepdims=True)
        pe = pe / den
        po = po / den
        oc_s[...] = _mm(pe, vce_s[...]) + _mm(po, vco_s[...])
        jp = _iota((npast, npast), 0)
        jj = _iota((npast, npast), 1)
        groups = []
        for kvh in range(NSA_KV_HEADS):
            base = kvh * NSA_GROUP * rq
            imp_e = pe[base:base + rq]
            imp_o = po[base:base + rq]
            for g in range(1, NSA_GROUP):
                imp_e = imp_e + pe[base + g * rq:base + (g + 1) * rq]
                imp_o = imp_o + po[base + g * rq:base + (g + 1) * rq]
            imp = imp_e + imp_o
            imp_t = jnp.concatenate([imp, jnp.zeros((128 - rq, npast), F32)], axis=0).T
            sel_rows = []
            for t in range(rq):
                colv = imp_t[:, t:t + 1]
                rowv = imp[t:t + 1, :]
                beats = ((colv > rowv) | ((colv == rowv) & (jp < jj))) & (jp >= 1)
                rank = jnp.sum(jnp.where(beats, 1.0, 0.0), axis=0, keepdims=True)
                sel_rows.append(jnp.where((jrow == 0) | (rank < N_SELECT - 2), 1.0, 0.0))
            groups += [jnp.concatenate(sel_rows, axis=0)] * NSA_GROUP
        groups.append(jnp.zeros((128 - ncol, npast), F32))
        sel_s[...] = jnp.concatenate(groups, axis=0).T[:, 0:ncol].astype(BF16)

    @pl.when(ph == 1)
    def _selected():
        @pl.when(gi == 0)
        def _():
            _flash_init(m_s, l_s, acc_s)

        k_t = keys_t(0)
        v_t = keys_t(128)
        pos_row = gi * KEYS_PER_STEP + _iota((1, KEYS_PER_STEP), 1)
        s = _mm(qt_s[...], k_t) - bias(pos_row)
        sel_rows = sel_s[pl.ds(pl.multiple_of(gi * SEL_PER_STEP, SEL_PER_STEP), SEL_PER_STEP), :]
        mask = _mm_tn(sel_rows, e0_ref[...]) > 0.5
        _flash_step_r(s, mask, lambda p: _mm_nt(p, v_t), m_s, l_s, acc_s)

    @pl.when((ph == 1) & (gi == N_PAGE_GROUPS - 1))
    def _finish():
        qt = qt_s[...]
        new = zn_ref[0, 0:rq, 0:768]
        npos_row = PAST_LEN + _iota((1, rq), 1)
        mask_n = npos_row <= t_col
        s = _mm_nt(qt, new[:, 256:384]) - bias(npos_row)
        _flash_step_r(s, mask_n, lambda p: _mm(p, new[:, 384:512]), m_s, l_s, acc_s)
        l = l_s[...]
        o_sel = acc_s[...] / jnp.where(l > 0, l, 1.0)
        wpos_row = PAST_LEN - WINDOW + _iota((1, WINDOW), 1)
        mask_w = (t_col - wpos_row) <= WINDOW
        s1 = jnp.where(mask_w, _mm(qt, win_ref[0, 0:128, :]) - bias(wpos_row), NEG)
        s2 = jnp.where(mask_n, _mm_nt(qt, new[:, 512:640]) - bias(npos_row), NEG)
        m = jnp.maximum(jnp.max(s1, axis=-1, keepdims=True), jnp.max(s2, axis=-1, keepdims=True))
        p1 = jnp.where(mask_w, jnp.exp(s1 - m), 0.0)
        p2 = jnp.where(mask_n, jnp.exp(s2 - m), 0.0)
        den = jnp.sum(p1, axis=-1, keepdims=True) + jnp.sum(p2, axis=-1, keepdims=True)
        o_win = (_mm_nt(p1, win_ref[0, 128:256, :]) + _mm(p2, new[:, 640:768])) / jnp.where(den > 0, den, 1.0)
        o_cmp = oc_s[...]
        for h in range(NSA_HEADS):
            rs = slice(h * rq, (h + 1) * rq)
            ls = slice(64 * (h // NSA_GROUP), 64 * (h // NSA_GROUP) + 64)
            gates = _sigmoid(zn_ref[0, 0:rq, 768 + 3 * h:771 + 3 * h])
            o = gates[:, 0:1] * o_cmp[rs, ls] + gates[:, 1:2] * o_sel[rs, ls] + gates[:, 2:3] * o_win[rs, ls]
            o_ref[0, 0:rq, 64 * h:64 * h + 64] = o * _silu(zn_ref[0, 0:rq, 1536 + 64 * h:1600 + 64 * h])
        o_ref[0, rq:, :] = jnp.zeros((SAMPLE_TPAD - rq, 512), F32)


def _nsa_sample(page_table, zn, win_t, cache_t, *, layer, n_pool, n_tok):
    b = zn.shape[0]
    rq = SAMPLE_TPAD
    assert n_tok & (n_tok - 1) == 0 and n_tok <= rq
    ncol = NSA_HEADS * n_tok
    tok = jnp.arange(KEYS_PER_STEP)
    blk = jnp.arange(SEL_PER_STEP)
    in_blk = tok[None, :] // SEL_BLOCK == blk[:, None]
    first_half = (tok[None, :] % SEL_BLOCK) < CMP_BLOCK
    e0 = in_blk.astype(BF16)
    pk = POOL_PAGES * PAGE_SIZE
    pool = jnp.concatenate([(in_blk & first_half)[0:POOL_SEL, 0:pk], (in_blk & ~first_half)[0:POOL_SEL, 0:pk]],
                           axis=0).astype(BF16)

    def page_map(kidx):
        return lambda i, ph, gi, pt: (layer * n_pool + pt[i, gi * PAGES_PER_STEP + kidx], ph, 0)

    const = lambda a: pl.BlockSpec(a.shape, lambda i, ph, gi, pt: (0, 0))
    sq = lambda dt: pltpu.VMEM((ncol, 128), dt)
    grid_spec = pltpu.PrefetchScalarGridSpec(
        num_scalar_prefetch=1,
        grid=(b, 2, N_PAGE_GROUPS),
        in_specs=[pl.BlockSpec((1, rq, ZN_W), lambda i, ph, gi, pt: (i, 0, 0)),
                  pl.BlockSpec((1, HALF_PAGE_ROWS, WINDOW), lambda i, ph, gi, pt: (layer * b + i, 0, 0)),
                  const(pool), const(e0)]
        + [pl.BlockSpec((1, HALF_PAGE_ROWS, PAGE_SIZE), page_map(kidx)) for kidx in range(PAGES_PER_STEP)],
        out_specs=pl.BlockSpec((1, rq, 512), lambda i, ph, gi, pt: (i, 0, 0)),
        scratch_shapes=[pltpu.VMEM((N_PAST_SEL, 128), F32) for _ in range(4)]
        + [sq(BF16), pltpu.VMEM((N_PAST_SEL, ncol), BF16), sq(F32),
           pltpu.VMEM((ncol, 1), F32), pltpu.VMEM((ncol, 1), F32), sq(F32)],
    )
    return pl.pallas_call(
        functools.partial(_nsa_sample_kernel, n_tok=n_tok),
        grid_spec=grid_spec,
        out_shape=jax.ShapeDtypeStruct((b, rq, 512), F32),
        compiler_params=_cparams(("parallel", "arbitrary", "arbitrary")),
        name="nsa_sample",
    )(page_table, zn, win_t, pool, e0, *([cache_t] * PAGES_PER_STEP))


def _merge_kernel(x_ref, bg_ref, br_ref, bn_ref, mg_ref, wbr_ref, wout_ref, lng_ref, lnb_ref, o_ref, o16_ref):
    gate = lambda lo: _sigmoid(mg_ref[:, lo:lo + 1024].astype(F32))
    acc = gate(0) * _mm(bg_ref[...], wbr_ref[0])
    acc = acc + gate(1024) * _mm(br_ref[...], wbr_ref[1])
    acc = acc + gate(2048) * _mm(bn_ref[...], wbr_ref[2])
    xf = DN_ALPHA * x_ref[...] + _mm(acc, wout_ref[...])
    mu = jnp.mean(xf, axis=-1, keepdims=True)
    d = xf - mu
    var = jnp.mean(d * d, axis=-1, keepdims=True)
    xn = d * lax.rsqrt(var + LN_EPS) * lng_ref[...] + lnb_ref[...]
    o_ref[...] = xn
    o16_ref[...] = xn.astype(BF16)


def _merge(x, o_gla, o_rwkv, o_nsa, zm, w_br, w_out, ln_g, ln_b, tm):
    m = x.shape[0]
    row = lambda n: pl.BlockSpec((tm, n), lambda i: (i, 0))
    return pl.pallas_call(
        _merge_kernel,
        grid=(m // tm,),
        in_specs=[row(D_MODEL), row(512), row(512), row(512), row(ZM_W),
                  pl.BlockSpec((3, 512, D_MODEL), lambda i: (0, 0, 0)),
                  pl.BlockSpec((D_MODEL, D_MODEL), lambda i: (0, 0)),
                  pl.BlockSpec((1, D_MODEL), lambda i: (0, 0)),
                  pl.BlockSpec((1, D_MODEL), lambda i: (0, 0))],
        out_specs=[row(D_MODEL), row(D_MODEL)],
        out_shape=[jax.ShapeDtypeStruct((m, D_MODEL), F32), jax.ShapeDtypeStruct((m, D_MODEL), BF16)],
        compiler_params=_cparams(("parallel",)),
        name="merge",
    )(x, o_gla, o_rwkv, o_nsa, zm, w_br, w_out, ln_g, ln_b)


def _pack_weights(w_in, b_in):
    def pack(a):
        z = lambda n: jnp.zeros(a.shape[:-1] + (n,), a.dtype)
        gla = jnp.concatenate([a[..., 0:1024], a[..., 1040:1552], a[..., 1024:1040], z(ZG_W - 1552)], axis=-1)
        rwkv = a[..., 1552:3728]
        nsa = jnp.concatenate([a[..., 4240:5008], a[..., 5008:5032], z(1024 - 792), a[..., 3728:4240],
                               a[..., 5032:5544]], axis=-1)
        mg = a[..., 5544:8616]
        return gla, rwkv, nsa, mg
    ws = [w.astype(BF16) for w in pack(w_in)]
    bs = [b[:, None, :] for b in pack(b_in)]
    return ws, bs


def kernel(x_prompt, x_sample, cache_nsa_kv, state_nsa_win, state_gla, state_rwkv, state_rwkv_shift, page_table,
           w_in, b_in, gla_a_up, gla_a_bias, gla_norm, rwkv_mu, rwkv_w0, rwkv_w_up, rwkv_a0, rwkv_a_up, rwkv_k_k,
           rwkv_k_a, rwkv_r_k, rwkv_ln_w, rwkv_ln_b, w_br, w_out, ln_g, ln_b):
    bp, tp, _ = x_prompt.shape
    bs, ts, _ = x_sample.shape
    n_pool = cache_nsa_kv.shape[1]
    ws, bws = _pack_weights(w_in, b_in)
    w_br_b = w_br.astype(BF16)
    w_out_b = w_out.astype(BF16)
    seg = (jnp.arange(RWKV_W)[:, None] // RWKV_HD == jnp.arange(RWKV_W)[None, :] // RWKV_HD).astype(BF16)
    cache_t = jnp.transpose(cache_nsa_kv, (0, 1, 3, 4, 5, 2)).reshape(DEPTH * n_pool, 8 * NSA_HD, PAGE_SIZE)
    win_t = jnp.transpose(state_nsa_win, (0, 1, 3, 4, 5, 2)).reshape(DEPTH * bs, HALF_PAGE_ROWS, WINDOW)
    row2 = lambda a: a.reshape(DEPTH, 1, -1)
    gla_a_bias2, gla_norm2 = row2(gla_a_bias), row2(gla_norm)
    r_par = [row2(rwkv_mu), row2(rwkv_w0), rwkv_w_up, row2(rwkv_a0), rwkv_a_up, row2(rwkv_k_k), row2(rwkv_k_a),
             row2(rwkv_r_k), row2(rwkv_ln_w), row2(rwkv_ln_b)]
    ln_g2, ln_b2 = row2(ln_g), row2(ln_b)

    xp = x_prompt.reshape(bp * tp, D_MODEL)
    xs = jnp.pad(x_sample, ((0, 0), (0, SAMPLE_TPAD - ts), (0, 0))).reshape(bs * SAMPLE_TPAD, D_MODEL)
    xp16, xs16 = xp.astype(BF16), xs.astype(BF16)
    zeros_gla = jnp.zeros((bp, GLA_HEADS, GLA_DK, GLA_DV), F32)
    zeros_rwkv = jnp.zeros((bp, RWKV_HEADS, RWKV_HD, RWKV_HD), F32)
    zeros_shift = jnp.zeros((bp, 1, RWKV_IN), F32)

    outs = {k: [] for k in ("kv_p", "kv_s", "win_p", "win_s", "gla_p", "gla_s", "rwkv_p", "rwkv_s", "sh_p", "sh_s")}
    for l in range(DEPTH):
        rp = [p[l] for p in r_par]
        zg, zr, zn, zm = (_proj(xp16, ws[i][l], bws[i][l], 512, GROUP_DTYPES[i]) for i in range(4))
        zn3 = zn.reshape(bp, tp, ZN_W)
        o_gla, gla_st = _gla(zg.reshape(bp, tp, ZG_W), zeros_gla, gla_a_up[l], gla_a_bias2[l], gla_norm2[l],
                             tb_rows=512, chunk=GLA_CHUNK, t_valid=None)
        zr3 = zr.reshape(bp, tp, RWKV_IN)
        o_rwkv, rwkv_st = _rwkv(zr3, zeros_rwkv, zeros_shift, seg, rp, tb_rows=512, chunk=RWKV_CHUNK, t_valid=None)
        o_nsa = _nsa_prompt(zn3, tq=256)
        xp, xp16 = _merge(xp, o_gla.reshape(bp * tp, 512), o_rwkv.reshape(bp * tp, 512), o_nsa.reshape(bp * tp, 512), zm,
                    w_br_b[l], w_out_b[l], ln_g2[l], ln_b2[l], 512)
        outs["kv_p"].append(zn3[:, :, 0:512].reshape(bp, tp, 4, NSA_KV_HEADS, NSA_HD))
        outs["win_p"].append(zn3[:, tp - WINDOW:, 512:768].reshape(bp, WINDOW, 2, NSA_KV_HEADS, NSA_HD))
        outs["gla_p"].append(gla_st)
        outs["rwkv_p"].append(rwkv_st)
        outs["sh_p"].append(zr3[:, tp - 1, :])
        rows_s = bs * SAMPLE_TPAD
        zg, zr, zn, zm = (_proj(xs16, ws[i][l], bws[i][l], rows_s, GROUP_DTYPES[i]) for i in range(4))
        zn3 = zn.reshape(bs, SAMPLE_TPAD, ZN_W)
        o_gla, gla_st = _gla(zg.reshape(bs, SAMPLE_TPAD, ZG_W), state_gla[l], gla_a_up[l], gla_a_bias2[l],
                             gla_norm2[l], tb_rows=SAMPLE_TPAD, chunk=SAMPLE_TPAD, t_valid=ts)
        zr3 = zr.reshape(bs, SAMPLE_TPAD, RWKV_IN)
        o_rwkv, rwkv_st = _rwkv(zr3, state_rwkv[l], state_rwkv_shift[l][:, None, :], seg, rp,
                                tb_rows=SAMPLE_TPAD, chunk=SAMPLE_TPAD, t_valid=ts)
        o_nsa = _nsa_sample(page_table, zn3, win_t, cache_t, layer=l, n_pool=n_pool, n_tok=ts)
        xs, xs16 = _merge(xs, o_gla.reshape(rows_s, 512), o_rwkv.reshape(rows_s, 512), o_nsa.reshape(rows_s, 512), zm,
                    w_br_b[l], w_out_b[l], ln_g2[l], ln_b2[l], rows_s)
        outs["kv_s"].append(zn3[:, 0:ts, 0:512].reshape(bs, ts, 4, NSA_KV_HEADS, NSA_HD))
        new_win = zn3[:, 0:ts, 512:768].reshape(bs, ts, 2, NSA_KV_HEADS, NSA_HD)
        outs["win_s"].append(jnp.concatenate([state_nsa_win[l][:, ts:], new_win], axis=1))
        outs["gla_s"].append(gla_st)
        outs["rwkv_s"].append(rwkv_st)
        outs["sh_s"].append(zr3[:, ts - 1, :])

    st = lambda k: jnp.stack(outs[k])
    y_prompt = xp.reshape(bp, tp, D_MODEL)
    y_sample = xs.reshape(bs, SAMPLE_TPAD, D_MODEL)[:, 0:ts]
    return (y_prompt, y_sample, st("kv_p"), st("kv_s"), st("win_p"), st("win_s"), st("gla_p"), st("gla_s"),
            st("rwkv_p"), st("rwkv_s"), st("sh_p"), st("sh_s"))
```
